```python
import math
import jax, jax.numpy as jnp
from jax import lax
import numpy as np

D_MODEL = 1024
BATCH = 8
SEQ = 2048
DEPTH = 1
DEC_BATCH = 128
DEC_SEQ = 1
PAST_LEN = 16384
PAGE_SIZE = 128

N_META = 16
D_MIX = D_MODEL
S5_DIM = D_MIX // 2
S5_GROUP = 16
S5_GROUPS = S5_DIM // S5_GROUP
S5_STATE = 64
HG_DIM = D_MIX - S5_DIM
HG_HEAD_DIM = 128
HG_HEADS = HG_DIM // HG_HEAD_DIM
HG_CHUNK = 64
D_FF = ((8 * D_MODEL // 3 + 127) // 128) * 128
CONV_W = 3
IN_COLS = S5_DIM + 4 * HG_DIM
EPS = 1e-6

kernel_name = "hymba_s5_hgrn2_convffn_step"


def rms_norm(x, g):
    xf = x.astype(jnp.float32)
    y = xf * lax.rsqrt(jnp.mean(xf * xf, axis=-1, keepdims=True) + EPS)
    return (y * g.astype(jnp.float32)).astype(x.dtype)


def s5_discretize(lam_re, lam_im, log_dt, b_re, b_im):
    lam_re = lam_re.astype(jnp.float32)
    lam_im = lam_im.astype(jnp.float32)
    dt = jnp.exp(log_dt.astype(jnp.float32))[:, None]
    mag = jnp.exp(lam_re * dt)
    ar = mag * jnp.cos(lam_im * dt)
    ai = mag * jnp.sin(lam_im * dt)
    nr = ar - 1.0
    den = lam_re * lam_re + lam_im * lam_im
    cr = (nr * lam_re + ai * lam_im) / den
    ci = (ai * lam_re - nr * lam_im) / den
    b_re = b_re.astype(jnp.float32)
    b_im = b_im.astype(jnp.float32)
    bb_re = cr[..., None] * b_re - ci[..., None] * b_im
    bb_im = cr[..., None] * b_im + ci[..., None] * b_re
    return ar, ai, bb_re, bb_im


def _complex_scan_op(e1, e2):
    a1r, a1i, b1r, b1i = e1
    a2r, a2i, b2r, b2i = e2
    return (a2r * a1r - a2i * a1i,
            a2r * a1i + a2i * a1r,
            a2r * b1r - a2i * b1i + b2r,
            a2r * b1i + a2i * b1r + b2i)


def s5_mix(u, h0_re, h0_im, lam_re, lam_im, log_dt, b_re, b_im, c_re, c_im, d, w_glu, b_glu):
    n, l, _ = u.shape
    uf = u.astype(jnp.float32)
    ug = uf.reshape(n, l, S5_GROUPS, S5_GROUP)
    ar, ai, bbr, bbi = s5_discretize(lam_re, lam_im, log_dt, b_re, b_im)
    xr = jnp.einsum('nlgc,gpc->nlgp', ug, bbr)
    xi = jnp.einsum('nlgc,gpc->nlgp', ug, bbi)
    h0r = h0_re.astype(jnp.float32)
    h0i = h0_im.astype(jnp.float32)
    xr = xr.at[:, 0].add(ar * h0r - ai * h0i)
    xi = xi.at[:, 0].add(ar * h0i + ai * h0r)
    a_r = jnp.broadcast_to(ar, xr.shape)
    a_i = jnp.broadcast_to(ai, xi.shape)
    _, _, hr, hi = lax.associative_scan(_complex_scan_op, (a_r, a_i, xr, xi), axis=1)
    y = (jnp.einsum('nlgp,gcp->nlgc', hr, c_re.astype(jnp.float32))
         - jnp.einsum('nlgp,gcp->nlgc', hi, c_im.astype(jnp.float32)))
    y = y.reshape(n, l, S5_DIM) + d.astype(jnp.float32) * uf
    y = jax.nn.gelu(y)
    y = y * jax.nn.sigmoid(y @ w_glu.astype(jnp.float32) + b_glu.astype(jnp.float32))
    return y.astype(u.dtype), hr[:, -1], hi[:, -1]


def hgrn_chunks(q, k, logf, v, s0, chunk):
    n, l, h, _ = q.shape
    nc = l // chunk

    def blk(t):
        return t.reshape(n, nc, chunk, h, t.shape[-1]).transpose(1, 0, 3, 2, 4)

    mask = jnp.tril(jnp.ones((chunk, chunk), dtype=bool))

    def step(S, inp):
        qc, kc, lc, vc = inp
        b = jnp.cumsum(lc, axis=2)
        qd = qc * jnp.exp(b)
        kd = kc * jnp.exp(-b)
        att = jnp.where(mask, jnp.einsum('nhck,nhsk->nhcs', qd, kd), 0.0)
        o = jnp.einsum('nhck,nhkv->nhcv', qd, S) + jnp.einsum('nhcs,nhsv->nhcv', att, vc)
        bl = b[:, :, -1:, :]
        S = (jnp.exp(bl[:, :, 0, :])[..., None] * S
             + jnp.einsum('nhsk,nhsv->nhkv', kc * jnp.exp(bl - b), vc))
        return S, o

    S, o = lax.scan(step, s0, (blk(q), blk(k), blk(logf), blk(v)))
    o = o.transpose(1, 0, 3, 2, 4).reshape(n, l, h, v.shape[-1])
    return o, S


def hgrn_mix(q, f_logit, i_in, g, lb, norm_g, s0, n_lead):
    n, l, _ = q.shape
    heads = lambda t: t.astype(jnp.float32).reshape(n, l, HG_HEADS, HG_HEAD_DIM)
    f = lb + (1.0 - lb) * jax.nn.sigmoid(f_logit.astype(jnp.float32))
    logf = heads(jnp.log(f))
    k = heads(1.0 - f)
    qh = heads(q)
    vh = heads(i_in)
    S = s0.astype(jnp.float32)
    if n_lead > 0:
        o1, S = hgrn_chunks(qh[:, :n_lead], k[:, :n_lead], logf[:, :n_lead], vh[:, :n_lead], S, n_lead)
        rest = l - n_lead
        o2, S = hgrn_chunks(qh[:, n_lead:], k[:, n_lead:], logf[:, n_lead:], vh[:, n_lead:], S,
                            math.gcd(rest, HG_CHUNK))
        o = jnp.concatenate([o1, o2], axis=1)
    else:
        o, S = hgrn_chunks(qh, k, logf, vh, S, math.gcd(l, HG_CHUNK))
    o = o * lax.rsqrt(jnp.mean(o * o, axis=-1, keepdims=True) + EPS)
    o = o.reshape(n, l, HG_DIM) * norm_g.astype(jnp.float32) * jax.nn.silu(g.astype(jnp.float32))
    return o.astype(q.dtype), S


def conv_ffn(x, w_up, conv_w, conv_b, w_down, buf):
    l = x.shape[1]
    hid = x @ w_up
    a, v = hid[..., :D_FF], hid[..., D_FF:]
    ext = jnp.concatenate([buf.astype(a.dtype), a], axis=1)
    c = conv_b + sum(ext[:, j:j + l] * conv_w[j] for j in range(CONV_W))
    y = (jax.nn.silu(c) * v) @ w_down
    return y, ext[:, -(CONV_W - 1):]


def block(h, n_lead, s5r0, s5i0, hg0, conv0, lb,
          norm_mix_g, w_in, s5_lambda_re, s5_lambda_im, s5_log_dt, s5_b_re, s5_b_im,
          s5_c_re, s5_c_im, s5_d, s5_w_glu, s5_b_glu, hg_norm_g, w_out,
          norm_ffn_g, ffn_w_up, ffn_conv_w, ffn_conv_b, ffn_w_down):
    hn = rms_norm(h, norm_mix_g)
    z = hn @ w_in
    o = S5_DIM
    u = z[..., :o]
    q = z[..., o:o + HG_DIM]
    fl = z[..., o + HG_DIM:o + 2 * HG_DIM]
    iv = z[..., o + 2 * HG_DIM:o + 3 * HG_DIM]
    g = z[..., o + 3 * HG_DIM:]
    y5, s5r, s5i = s5_mix(u, s5r0, s5i0, s5_lambda_re, s5_lambda_im, s5_log_dt, s5_b_re, s5_b_im,
                          s5_c_re, s5_c_im, s5_d, s5_w_glu, s5_b_glu)
    yh, hg = hgrn_mix(q, fl, iv, g, lb, hg_norm_g, hg0, n_lead)
    h = h + jnp.concatenate([y5, yh], axis=-1) @ w_out
    yf, conv = conv_ffn(rms_norm(h, norm_ffn_g), ffn_w_up, ffn_conv_w, ffn_conv_b, ffn_w_down, conv0)
    return h + yf, s5r, s5i, hg, conv


def setup_inputs(seed: int = 0) -> dict:
    key = jax.random.key(seed)
    ks = jax.random.split(key, 32)
    f32 = jnp.float32
    nrm = lambda k, shape, s: jax.random.normal(k, shape, f32) * s
    n_arange = jnp.arange(S5_STATE, dtype=f32)
    return {
        "x_prompt": nrm(ks[0], (BATCH, SEQ, D_MODEL), 1.0),
        "x_sample": nrm(ks[1], (DEC_BATCH, DEC_SEQ, D_MODEL), 1.0),
        "state_s5_re": nrm(ks[2], (DEPTH, DEC_BATCH, S5_GROUPS, S5_STATE), 0.1),
        "state_s5_im": nrm(ks[3], (DEPTH, DEC_BATCH, S5_GROUPS, S5_STATE), 0.1),
        "state_hgrn": nrm(ks[4], (DEPTH, DEC_BATCH, HG_HEADS, HG_HEAD_DIM, HG_HEAD_DIM), 0.5),
        "state_ffn_conv": nrm(ks[5], (DEPTH, DEC_BATCH, CONV_W - 1, D_FF), 1.0),
        "meta_tokens": nrm(ks[6], (N_META, D_MODEL), 1.0),
        "norm_mix_g": 1.0 + nrm(ks[7], (DEPTH, D_MODEL), 0.02),
        "w_in": nrm(ks[8], (DEPTH, D_MODEL, IN_COLS), D_MODEL ** -0.5),
        "s5_lambda_re": -0.5 + nrm(ks[9], (DEPTH, S5_GROUPS, S5_STATE), 0.01),
        "s5_lambda_im": jnp.pi * n_arange + nrm(ks[10], (DEPTH, S5_GROUPS, S5_STATE), 0.01),
        "s5_log_dt": jax.random.uniform(ks[11], (DEPTH, S5_GROUPS), f32,
                                        minval=math.log(1e-3), maxval=math.log(1e-1)),
        "s5_b_re": nrm(ks[12], (DEPTH, S5_GROUPS, S5_STATE, S5_GROUP), (2 * S5_GROUP) ** -0.5),
        "s5_b_im": nrm(ks[13], (DEPTH, S5_GROUPS, S5_STATE, S5_GROUP), (2 * S5_GROUP) ** -0.5),
        "s5_c_re": nrm(ks[14], (DEPTH, S5_GROUPS, S5_GROUP, S5_STATE), S5_STATE ** -0.5),
        "s5_c_im": nrm(ks[15], (DEPTH, S5_GROUPS, S5_GROUP, S5_STATE), S5_STATE ** -0.5),
        "s5_d": nrm(ks[16], (DEPTH, S5_DIM), 1.0),
        "s5_w_glu": nrm(ks[17], (DEPTH, S5_DIM, S5_DIM), S5_DIM ** -0.5),
        "s5_b_glu": nrm(ks[18], (DEPTH, S5_DIM), 0.01),
        "hg_lower_bounds": nrm(ks[19], (DEPTH + 1, HG_DIM), 0.1),
        "hg_norm_g": 1.0 + nrm(ks[20], (DEPTH, HG_DIM), 0.02),
        "w_out": nrm(ks[21], (DEPTH, D_MIX, D_MODEL), D_MIX ** -0.5),
        "norm_ffn_g": 1.0 + nrm(ks[22], (DEPTH, D_MODEL), 0.02),
        "ffn_w_up": nrm(ks[23], (DEPTH, D_MODEL, 2 * D_FF), D_MODEL ** -0.5),
        "ffn_conv_w": nrm(ks[24], (DEPTH, CONV_W, D_FF), CONV_W ** -0.5),
        "ffn_conv_b": nrm(ks[25], (DEPTH, D_FF), 0.01),
        "ffn_w_down": nrm(ks[26], (DEPTH, D_FF, D_MODEL), D_FF ** -0.5),
        "final_norm_g": 1.0 + nrm(ks[27], (D_MODEL,), 0.02),
    }


def reference(x_prompt, x_sample, state_s5_re, state_s5_im, state_hgrn, state_ffn_conv,
              meta_tokens, norm_mix_g, w_in, s5_lambda_re, s5_lambda_im, s5_log_dt,
              s5_b_re, s5_b_im, s5_c_re, s5_c_im, s5_d, s5_w_glu, s5_b_glu,
              hg_lower_bounds, hg_norm_g, w_out, norm_ffn_g, ffn_w_up, ffn_conv_w,
              ffn_conv_b, ffn_w_down, final_norm_g):
    nb = x_prompt.shape[0]
    meta = jnp.broadcast_to(meta_tokens[None].astype(x_prompt.dtype), (nb, N_META, D_MODEL))
    hp = jnp.concatenate([meta, x_prompt], axis=1)
    hs = x_sample
    lbs = jnp.cumsum(jax.nn.softmax(hg_lower_bounds.astype(jnp.float32), axis=0), axis=0)
    p_s5r, p_s5i, p_hg, p_cv = [], [], [], []
    s_s5r, s_s5i, s_hg, s_cv = [], [], [], []
    for li in range(DEPTH):
        wts = (norm_mix_g[li], w_in[li], s5_lambda_re[li], s5_lambda_im[li], s5_log_dt[li],
               s5_b_re[li], s5_b_im[li], s5_c_re[li], s5_c_im[li], s5_d[li], s5_w_glu[li],
               s5_b_glu[li], hg_norm_g[li], w_out[li], norm_ffn_g[li], ffn_w_up[li],
               ffn_conv_w[li], ffn_conv_b[li], ffn_w_down[li])
        z5 = jnp.zeros((nb, S5_GROUPS, S5_STATE), jnp.float32)
        zh = jnp.zeros((nb, HG_HEADS, HG_HEAD_DIM, HG_HEAD_DIM), jnp.float32)
        zc = jnp.zeros((nb, CONV_W - 1, D_FF), hp.dtype)
        hp, a, b, c, d = block(hp, N_META, z5, z5, zh, zc, lbs[li], *wts)
        p_s5r.append(a); p_s5i.append(b); p_hg.append(c); p_cv.append(d)
        hs, a, b, c, d = block(hs, 0, state_s5_re[li], state_s5_im[li], state_hgrn[li],
                               state_ffn_conv[li], lbs[li], *wts)
        s_s5r.append(a); s_s5i.append(b); s_hg.append(c); s_cv.append(d)
    y_prompt = rms_norm(hp, final_norm_g)[:, N_META:]
    y_sample = rms_norm(hs, final_norm_g)
    return (y_prompt, y_sample,
            jnp.stack(p_s5r), jnp.stack(p_s5i), jnp.stack(p_hg), jnp.stack(p_cv),
            jnp.stack(s_s5r), jnp.stack(s_s5i), jnp.stack(s_hg), jnp.stack(s_cv))
```

```python
import functools

import jax
import jax.numpy as jnp
from jax import lax
from jax.experimental import pallas as pl
from jax.experimental.pallas import tpu as pltpu

F32 = jnp.float32
BF16 = jnp.bfloat16

D_MODEL = 1024
N_META = 16
S5_DIM = 512
S5_GROUP = 16
S5_GROUPS = 32
S5_STATE = 64
S5_LANES = S5_GROUPS * S5_STATE
HG_DIM = 512
HG_HEAD_DIM = 128
HG_HEADS = 4
HG_CHUNK = 64
D_FF = 2816
CONV_W = 3
EPS = 1e-6

S5_HALVES = 2
S5_HALF_CH = S5_DIM // S5_HALVES
S5_HALF_ST = S5_LANES // S5_HALVES
SCAN_LANES = 512
FF_CHUNK = 256
N_FF_CHUNKS = D_FF // FF_CHUNK
SUBLANES = 8
VMEM_LIMIT = 56 * 1024 * 1024


def _sigmoid(x):
    return 1.0 / (1.0 + jnp.exp(-x))


def _rms_norm(x, g):
    ms = jnp.mean(x * x, axis=-1, keepdims=True)
    return x * lax.rsqrt(ms + EPS) * g


def _dot(a, b):
    return jnp.dot(a, b, preferred_element_type=F32)


def _params(*sem):
    return pltpu.CompilerParams(dimension_semantics=sem, vmem_limit_bytes=VMEM_LIMIT)


def _prep_kernel(lam_re_ref, lam_im_ref, dt_ref, bt_re_ref, bt_im_ref, hlb_ref,
                 ar_ref, ai_ref, bb_re_ref, bb_im_ref, lb_ref):
    lam_re = lam_re_ref[...]
    lam_im = lam_im_ref[...]
    dt = jnp.exp(dt_ref[...])
    mag = jnp.exp(lam_re * dt)
    ar = mag * jnp.cos(lam_im * dt)
    ai = mag * jnp.sin(lam_im * dt)
    nr = ar - 1.0
    den = lam_re * lam_re + lam_im * lam_im
    cr = (nr * lam_re + ai * lam_im) / den
    ci = (ai * lam_re - nr * lam_im) / den
    ar_ref[...] = ar
    ai_ref[...] = ai
    bt_re = bt_re_ref[...]
    bt_im = bt_im_ref[...]
    bb_re_ref[...] = cr * bt_re - ci * bt_im
    bb_im_ref[...] = cr * bt_im + ci * bt_re
    hlb = hlb_ref[...]
    e = jnp.exp(hlb - jnp.max(hlb, axis=0, keepdims=True))
    lb_ref[...] = e[0:1] / jnp.sum(e, axis=0, keepdims=True)


def _prep(lam_re, lam_im, log_dt, b_re, b_im, hlb):
    g, p = lam_re.shape
    sds = jax.ShapeDtypeStruct
    return pl.pallas_call(
        _prep_kernel,
        out_shape=(sds((g, 1, p), F32), sds((g, 1, p), F32),
                   sds((g, S5_GROUP, p), F32), sds((g, S5_GROUP, p), F32),
                   sds((1, HG_DIM), F32)),
        name="param_prep",
    )(lam_re.reshape(g, 1, p), lam_im.reshape(g, 1, p),
      jnp.broadcast_to(log_dt.reshape(g, 1, 1), (g, 1, p)),
      b_re.transpose(0, 2, 1), b_im.transpose(0, 2, 1), hlb)


def _block_diag(blocks):
    h, g, r, c = blocks.shape
    eye = jnp.eye(g, dtype=blocks.dtype)
    return jnp.einsum("jgrc,gh->jgrhc", blocks, eye).reshape(h, g * r, g * c)


def _inproj_kernel(x_ref, g_ref, w_ref, u_ref, z_ref):
    hn = _rms_norm(x_ref[0], g_ref[...])
    z = _dot(hn.astype(BF16), w_ref[...])
    u_ref[...] = z[:, :S5_DIM]
    z_ref[0] = z[:, S5_DIM:]


def _inproj(x, g, w, tm, n_tiles):
    n, l, d = x.shape
    cols = w.shape[1]
    rows = tm * n_tiles
    return pl.pallas_call(
        _inproj_kernel,
        grid=(n, n_tiles),
        in_specs=[pl.BlockSpec((1, tm, d), lambda b, i: (b, i, 0)),
                  pl.BlockSpec((1, d), lambda b, i: (0, 0)),
                  pl.BlockSpec((d, cols), lambda b, i: (0, 0))],
        out_specs=[pl.BlockSpec((tm, S5_DIM), lambda b, i: (i, b)),
                   pl.BlockSpec((1, tm, cols - S5_DIM), lambda b, i: (b, i, 0))],
        out_shape=(jax.ShapeDtypeStruct((rows, n * S5_DIM), F32),
                   jax.ShapeDtypeStruct((n, rows, cols - S5_DIM), F32)),
        compiler_params=_params("arbitrary", "arbitrary"),
        name="inproj",
    )(x, g, w)


def _gelu_tanh(y):
    return 0.5 * y * (1.0 + jnp.tanh(0.7978845608028654 * (y + 0.044715 * (y * y * y))))


def _s5_kernel(u_ref, h0r_ref, h0i_ref, ar_ref, ai_ref, wbr_ref, wbi_ref, wcr_ref, wci_ref, d_ref,
               y_ref, hr_out_ref, hi_out_ref, xr_s, xi_s, hr_s, hi_s, *, n, tt):
    @pl.when(pl.program_id(0) == 0)
    def _():
        hr_s[...] = h0r_ref[...]
        hi_s[...] = h0i_ref[...]

    u = u_ref[...]
    ub = u.astype(BF16)
    for j in range(S5_HALVES):
        ch = slice(j * S5_HALF_CH, (j + 1) * S5_HALF_CH)
        xr_s[...] = _dot(ub[:, ch], wbr_ref[j])
        xi_s[...] = _dot(ub[:, ch], wbi_ref[j])
        for c in range(S5_HALF_ST // SCAN_LANES):
            loc = slice(c * SCAN_LANES, (c + 1) * SCAN_LANES)
            glob = slice(j * S5_HALF_ST + c * SCAN_LANES, j * S5_HALF_ST + (c + 1) * SCAN_LANES)
            ar = ar_ref[:, glob]
            ai = ai_ref[:, glob]

            def step(t, carry, ar=ar, ai=ai, loc=loc):
                hr, hi = carry
                r = 0 if tt == 1 else pl.multiple_of(t * n, n)
                nhr = ar * hr - ai * hi + xr_s[pl.ds(r, n), loc]
                nhi = ar * hi + ai * hr + xi_s[pl.ds(r, n), loc]
                xr_s[pl.ds(r, n), loc] = nhr
                xi_s[pl.ds(r, n), loc] = nhi
                return nhr, nhi

            carry = (hr_s[:, glob], hi_s[:, glob])
            if tt == 1:
                carry = step(0, carry)
            else:
                carry = lax.fori_loop(0, tt, step, carry, unroll=8)
            hr_s[:, glob] = carry[0]
            hi_s[:, glob] = carry[1]
        y = (_dot(xr_s[...].astype(BF16), wcr_ref[j]) + _dot(xi_s[...].astype(BF16), wci_ref[j])
             + d_ref[:, ch] * u[:, ch])
        y_ref[:, ch] = _gelu_tanh(y)
    hr_out_ref[...] = hr_s[...]
    hi_out_ref[...] = hi_s[...]


def _s5(u_tm, h0r, h0i, ar, ai, wbr, wbi, wcr, wci, d, n, tt, n_tiles):
    rows = tt * n
    const = lambda shape: pl.BlockSpec(shape, lambda i: (0,) * len(shape))
    ar_b = jnp.broadcast_to(ar, (n, S5_LANES))
    ai_b = jnp.broadcast_to(ai, (n, S5_LANES))
    return pl.pallas_call(
        functools.partial(_s5_kernel, n=n, tt=tt),
        grid=(n_tiles,),
        in_specs=[pl.BlockSpec((rows, S5_DIM), lambda i: (i, 0)),
                  const((n, S5_LANES)), const((n, S5_LANES)),
                  const((n, S5_LANES)), const((n, S5_LANES)),
                  const(wbr.shape), const(wbi.shape), const(wcr.shape), const(wci.shape),
                  const((1, S5_DIM))],
        out_specs=[pl.BlockSpec((rows, S5_DIM), lambda i: (i, 0)),
                   const((n, S5_LANES)), const((n, S5_LANES))],
        out_shape=(jax.ShapeDtypeStruct((rows * n_tiles, S5_DIM), F32),
                   jax.ShapeDtypeStruct((n, S5_LANES), F32),
                   jax.ShapeDtypeStruct((n, S5_LANES), F32)),
        scratch_shapes=[pltpu.VMEM((rows, S5_HALF_ST), F32), pltpu.VMEM((rows, S5_HALF_ST), F32),
                        pltpu.VMEM((n, S5_LANES), F32), pltpu.VMEM((n, S5_LANES), F32)],
        compiler_params=_params("arbitrary"),
        name="s5_scan",
    )(u_tm, h0r, h0i, ar_b, ai_b, wbr, wbi, wcr, wci, d)


def _hgrn_gate_out(o, g, ng):
    parts = []
    for h in range(HG_HEADS):
        oh = o[:, h * HG_HEAD_DIM:(h + 1) * HG_HEAD_DIM]
        ms = jnp.mean(oh * oh, axis=-1, keepdims=True)
        parts.append(oh * lax.rsqrt(ms + EPS))
    return jnp.concatenate(parts, axis=-1) * ng * (g * _sigmoid(g))


def _hgrn_kernel(q_ref, f_ref, i_ref, g_ref, lb_ref, ng_ref, s0_ref,
                 y_ref, s_out_ref, st_s, o_s, *, th, valid):
    i = pl.program_id(1)
    c = HG_CHUNK

    @pl.when(i == 0)
    def _():
        for h in range(HG_HEADS):
            st_s[h] = s0_ref[0, h].T

    lb = lb_ref[...]
    f = lb + (1.0 - lb) * _sigmoid(f_ref[0])
    lc = jnp.log(f)
    k = 1.0 - f
    q = q_ref[0]
    if valid < th:
        live = lax.broadcasted_iota(jnp.int32, (th, 1), 0) < valid
        lc = jnp.where(live, lc, 0.0)
        k = jnp.where(live, k, 0.0)
        q = jnp.where(live, q, 0.0)
    row = lax.broadcasted_iota(jnp.int32, (th, th), 0)
    col = lax.broadcasted_iota(jnp.int32, (th, th), 1)
    same_chunk = jnp.bitwise_xor(row, col) < c
    tri = jnp.where(same_chunk & (col <= row), 1.0, 0.0).astype(BF16)
    lc_hi = lc.astype(BF16)
    lc_lo = (lc - lc_hi.astype(F32)).astype(BF16)
    b = _dot(tri, lc_hi) + _dot(tri, lc_lo)
    qd = (q * jnp.exp(b)).astype(BF16)
    kd = (k * jnp.exp(-b)).astype(BF16)
    v = i_ref[0]
    vb = v.astype(BF16)
    causal = (lax.broadcasted_iota(jnp.int32, (c, c), 1) <= lax.broadcasted_iota(jnp.int32, (c, c), 0))
    nt = (((1,), (1,)), ((), ()))
    for cc in range(th // c):
        rows = slice(cc * c, (cc + 1) * c)
        bl = b[cc * c + c - 1:cc * c + c, :]
        kdec = (k[rows] * jnp.exp(bl - b[rows])).astype(BF16)
        dec = jnp.exp(bl)
        for h in range(HG_HEADS):
            ls = slice(h * HG_HEAD_DIM, (h + 1) * HG_HEAD_DIM)
            st = st_s[h]
            att = lax.dot_general(qd[rows, ls], kd[rows, ls], nt, preferred_element_type=F32)
            att = jnp.where(causal, att, 0.0).astype(BF16)
            o = (lax.dot_general(qd[rows, ls], st.astype(BF16), nt, preferred_element_type=F32)
                 + _dot(att, vb[rows, ls]))
            o_s[rows, ls] = o
            vt = v[rows, ls].T.astype(BF16)
            st_s[h] = dec[:, ls] * st + _dot(vt, kdec[:, ls])
    y_ref[0] = _hgrn_gate_out(o_s[...], g_ref[0], ng_ref[...])

    @pl.when(i == pl.num_programs(1) - 1)
    def _():
        for h in range(HG_HEADS):
            s_out_ref[0, h] = st_s[h].T


def _hgrn(z, lb, ng, s0, th, n_tiles, valid):
    n = z.shape[0]
    col = lambda j: pl.BlockSpec((1, th, HG_DIM), lambda b, i, j=j: (b, i, j))
    vec = pl.BlockSpec((1, HG_DIM), lambda b, i: (0, 0))
    st = pl.BlockSpec((1, HG_HEADS, HG_HEAD_DIM, HG_HEAD_DIM), lambda b, i: (b, 0, 0, 0))
    return pl.pallas_call(
        functools.partial(_hgrn_kernel, th=th, valid=valid),
        grid=(n, n_tiles),
        in_specs=[col(0), col(1), col(2), col(3), vec, vec, st],
        out_specs=[pl.BlockSpec((1, th, HG_DIM), lambda b, i: (b, i, 0)), st],
        out_shape=(jax.ShapeDtypeStruct((n, n_tiles * th, HG_DIM), F32),
                   jax.ShapeDtypeStruct((n, HG_HEADS, HG_HEAD_DIM, HG_HEAD_DIM), F32)),
        scratch_shapes=[pltpu.VMEM((HG_HEADS, HG_HEAD_DIM, HG_HEAD_DIM), F32),
                        pltpu.VMEM((th, HG_DIM), F32)],
        compiler_params=_params("arbitrary", "arbitrary"),
        name="hgrn_chunks",
    )(z, z, z, z, lb, ng, s0)


def _hgrn_step_kernel(q_ref, f_ref, i_ref, g_ref, lb_ref, ng_ref, s0_ref,
                      y_ref, s_out_ref, o_s, *, sb):
    lb = lb_ref[...]
    f = lb + (1.0 - lb) * _sigmoid(f_ref[0])
    k = 1.0 - f
    q = q_ref[0]
    v = i_ref[0]
    pad = jnp.zeros((HG_HEAD_DIM - 3 * sb, HG_HEAD_DIM), F32)
    for h in range(HG_HEADS):
        ls = slice(h * HG_HEAD_DIM, (h + 1) * HG_HEAD_DIM)
        cols = jnp.concatenate([q[:, ls], f[:, ls], k[:, ls], pad], axis=0).T
        for s in range(sb):
            qc = cols[:, s:s + 1]
            fc = cols[:, sb + s:sb + s + 1]
            kc = cols[:, 2 * sb + s:2 * sb + s + 1]
            sn = fc * s0_ref[s, h] + kc * v[s:s + 1, ls]
            s_out_ref[s, h] = sn
            o_s[s:s + 1, ls] = jnp.sum(qc * sn, axis=0, keepdims=True)
    y_ref[0] = _hgrn_gate_out(o_s[...], g_ref[0], ng_ref[...])


def _hgrn_step(z, lb, ng, s0, sb):
    r = z.shape[1]
    col = lambda j: pl.BlockSpec((1, sb, HG_DIM), lambda i, j=j: (0, i, j))
    vec = pl.BlockSpec((1, HG_DIM), lambda i: (0, 0))
    st = pl.BlockSpec((sb, HG_HEADS, HG_HEAD_DIM, HG_HEAD_DIM), lambda i: (i, 0, 0, 0))
    return pl.pallas_call(
        functools.partial(_hgrn_step_kernel, sb=sb),
        grid=(r // sb,),
        in_specs=[col(0), col(1), col(2), col(3), vec, vec, st],
        out_specs=[pl.BlockSpec((1, sb, HG_DIM), lambda i: (0, i, 0)), st],
        out_shape=(jax.ShapeDtypeStruct((1, r, HG_DIM), F32),
                   jax.ShapeDtypeStruct(s0.shape, F32)),
        scratch_shapes=[pltpu.VMEM((sb, HG_DIM), F32)],
        compiler_params=_params("arbitrary"),
        name="hgrn_step",
    )(z, z, z, z, lb, ng, s0)


def _ffn_kernel(y5_ref, yh_ref, h_ref, wglu_ref, bglu_ref, wout_ref, g2_ref,
                wa_ref, wv_ref, wd_ref, cw_ref, cin_a_ref, cin_b_ref, gf_ref,
                out_ref, cout_ref, carry_s, hn_s, h1_s, acc_s, *, tm, per_row):
    y5p = y5_ref[...]
    y5 = y5p * _sigmoid(_dot(y5p.astype(BF16), wglu_ref[...]) + bglu_ref[...])
    ymix = jnp.concatenate([y5, yh_ref[0]], axis=-1).astype(BF16)
    h1 = h_ref[0] + _dot(ymix, wout_ref[...])
    h1_s[...] = h1
    hn_s[...] = _rms_norm(h1, g2_ref[...]).astype(BF16)
    acc_s[...] = jnp.zeros_like(acc_s)

    if not per_row:
        @pl.when(pl.program_id(1) == 0)
        def _():
            carry_s[...] = cin_a_ref[0]

    def chunk(j, _):
        hn = hn_s[...]
        a = _dot(hn, wa_ref[j])
        v = _dot(hn, wv_ref[j])
        cw = cw_ref[j]
        if per_row:
            am2 = cin_a_ref[j]
            am1 = cin_b_ref[j]
            cout_ref[j] = a
        else:
            ext = jnp.concatenate([carry_s[j], a], axis=0)
            am1 = ext[SUBLANES - 1:SUBLANES - 1 + tm]
            am2 = ext[SUBLANES - 2:SUBLANES - 2 + tm]
            carry_s[j] = a[tm - SUBLANES:tm]
        cv = cw[3:4] + am2 * cw[0:1] + am1 * cw[1:2] + a * cw[2:3]
        s = (cv * _sigmoid(cv)) * v
        acc_s[...] += _dot(s.astype(BF16), wd_ref[j])
        return 0

    lax.fori_loop(0, N_FF_CHUNKS, chunk, 0)
    out_ref[0] = _rms_norm(h1_s[...] + acc_s[...], gf_ref[...])
    if not per_row:
        cout_ref[0] = carry_s[...]


def _ffn(y5, yh, h, wglu, bglu, wout, g2, wa, wv, wd, cw, cin_a, cin_b, gf, tm, n_tiles, per_row):
    n = h.shape[0]
    const = lambda shape: pl.BlockSpec(shape, lambda b, i: (0,) * len(shape))
    slab = (1, N_FF_CHUNKS, SUBLANES, FF_CHUNK)
    if per_row:
        cin_spec = const(cin_a.shape)
        cout_spec, cout_shape = const(cin_a.shape), cin_a.shape
    else:
        cin_spec = pl.BlockSpec(slab, lambda b, i: (b, 0, 0, 0))
        cout_spec, cout_shape = cin_spec, (n,) + slab[1:]
    return pl.pallas_call(
        functools.partial(_ffn_kernel, tm=tm, per_row=per_row),
        grid=(n, n_tiles),
        in_specs=[pl.BlockSpec((tm, S5_DIM), lambda b, i: (i, b)),
                  pl.BlockSpec((1, tm, HG_DIM), lambda b, i: (b, i, 0)),
                  pl.BlockSpec((1, tm, D_MODEL), lambda b, i: (b, i, 0)),
                  const(wglu.shape), const(bglu.shape), const(wout.shape), const(g2.shape),
                  const(wa.shape), const(wv.shape), const(wd.shape), const(cw.shape),
                  cin_spec, cin_spec, const(gf.shape)],
        out_specs=[pl.BlockSpec((1, tm, D_MODEL), lambda b, i: (b, i, 0)), cout_spec],
        out_shape=(jax.ShapeDtypeStruct((n, n_tiles * tm, D_MODEL), F32),
                   jax.ShapeDtypeStruct(cout_shape, F32)),
        scratch_shapes=[pltpu.VMEM((N_FF_CHUNKS, SUBLANES, FF_CHUNK), F32),
                        pltpu.VMEM((tm, D_MODEL), BF16),
                        pltpu.VMEM((tm, D_MODEL), F32),
                        pltpu.VMEM((tm, D_MODEL), F32)],
        compiler_params=_params("arbitrary", "arbitrary"),
        name="mix_out_ffn",
    )(y5, yh, h, wglu, bglu, wout, g2, wa, wv, wd, cw, cin_a, cin_b, gf)


def _ff_chunks(x):
    y = x.reshape(x.shape[:-1] + (N_FF_CHUNKS, FF_CHUNK))
    return jnp.moveaxis(y, -2, 0)


def _ff_unchunk(x):
    y = jnp.moveaxis(x, 0, -2)
    return y.reshape(y.shape[:-2] + (D_FF,))


def kernel(x_prompt, x_sample, state_s5_re, state_s5_im, state_hgrn, state_ffn_conv, meta_tokens, norm_mix_g, w_in, s5_lambda_re, s5_lambda_im, s5_log_dt, s5_b_re, s5_b_im, s5_c_re, s5_c_im, s5_d, s5_w_glu, s5_b_glu, hg_lower_bounds, hg_norm_g, w_out, norm_ffn_g, ffn_w_up, ffn_conv_w, ffn_conv_b, ffn_w_down, final_norm_g):
    nb, seq, _ = x_prompt.shape
    ns = x_sample.shape[0]
    li = 0

    ar, ai, bb_re, bb_im, lb = _prep(s5_lambda_re[li], s5_lambda_im[li], s5_log_dt[li],
                                     s5_b_re[li], s5_b_im[li], hg_lower_bounds)
    ar = ar.reshape(1, S5_LANES)
    ai = ai.reshape(1, S5_LANES)
    halves = lambda t: t.reshape((S5_HALVES, S5_GROUPS // S5_HALVES) + t.shape[1:])
    wbr = _block_diag(halves(bb_re)).astype(BF16)
    wbi = _block_diag(halves(bb_im)).astype(BF16)
    wcr = _block_diag(halves(s5_c_re[li].transpose(0, 2, 1))).astype(BF16)
    wci = _block_diag(halves(-s5_c_im[li].transpose(0, 2, 1))).astype(BF16)
    d5 = s5_d[li].reshape(1, S5_DIM)
    g1 = norm_mix_g[li].reshape(1, D_MODEL)
    g2 = norm_ffn_g[li].reshape(1, D_MODEL)
    gf = final_norm_g.reshape(1, D_MODEL)
    ng = hg_norm_g[li].reshape(1, HG_DIM)
    bglu = s5_b_glu[li].reshape(1, S5_DIM)
    w_in_b = w_in[li].astype(BF16)
    wglu = s5_w_glu[li].astype(BF16)
    wout = w_out[li].astype(BF16)
    wa = _ff_chunks(ffn_w_up[li][:, :D_FF]).astype(BF16)
    wv = _ff_chunks(ffn_w_up[li][:, D_FF:]).astype(BF16)
    wd = ffn_w_down[li].reshape(N_FF_CHUNKS, FF_CHUNK, D_MODEL).astype(BF16)
    cw = jnp.concatenate([ffn_conv_w[li], ffn_conv_b[li][None],
                          jnp.zeros((SUBLANES - CONV_W - 1, D_FF), F32)], axis=0)
    cw = _ff_chunks(cw)
    s5w = (ar, ai, wbr, wbi, wcr, wci, d5)
    ffw = (wglu, bglu, wout, g2, wa, wv, wd, cw)

    def seq_run(x, s5r0, s5i0, hg0, conv0, tm, n_tiles, tt, s5_tiles, th, hg_tiles, valid):
        n = x.shape[0]
        u_tm, z = _inproj(x, g1, w_in_b, tm, n_tiles)
        y5, s5r, s5i = _s5(u_tm.reshape(-1, S5_DIM), s5r0, s5i0, *s5w, n, tt, s5_tiles)
        yh, hg = _hgrn(z, lb, ng, hg0, th, hg_tiles, valid)
        y, conv = _ffn(y5.reshape(-1, n * S5_DIM), yh, x, *ffw, conv0, conv0, gf, tm, n_tiles, False)
        return y, s5r, s5i, hg, conv

    meta = jnp.zeros((1, HG_CHUNK, D_MODEL), F32).at[0, :N_META].set(meta_tokens)
    z5 = jnp.zeros((1, S5_LANES), F32)
    zh = jnp.zeros((1, HG_HEADS, HG_HEAD_DIM, HG_HEAD_DIM), F32)
    zc = jnp.zeros((1, N_FF_CHUNKS, SUBLANES, FF_CHUNK), F32)
    u_tm, z = _inproj(meta, g1, w_in_b, HG_CHUNK, 1)
    y5, m5r, m5i = _s5(u_tm, z5, z5, *s5w, 1, N_META, 1)
    yh, mhg = _hgrn(z, lb, ng, zh, HG_CHUNK, 1, N_META)
    _, mconv = _ffn(y5, yh, meta, *ffw, zc, zc, gf, N_META, 1, False)

    bc = lambda t: jnp.broadcast_to(t, (nb,) + t.shape[1:])
    tm, tt, th = 256, 128, 256
    y_prompt, p5r, p5i, phg, pconv = seq_run(
        x_prompt, bc(m5r), bc(m5i), bc(mhg), bc(mconv),
        tm, seq // tm, tt, seq // tt, th, seq // th, th)

    xs = x_sample.reshape(1, ns, D_MODEL)
    u_tm, z = _inproj(xs, g1, w_in_b, ns, 1)
    y5, s5r, s5i = _s5(u_tm, state_s5_re[li].reshape(ns, S5_LANES), state_s5_im[li].reshape(ns, S5_LANES),
                       *s5w, ns, 1, 1)
    yh, shg = _hgrn_step(z, lb, ng, state_hgrn[li], 16)
    buf = state_ffn_conv[li]
    y_sample, a_new = _ffn(y5, yh, xs, *ffw, _ff_chunks(buf[:, 0]), _ff_chunks(buf[:, 1]), gf, ns, 1, True)
    sconv = jnp.stack([buf[:, 1], _ff_unchunk(a_new)], axis=1)

    st5 = lambda t: t.reshape(1, -1, S5_GROUPS, S5_STATE)
    pconv = jnp.moveaxis(pconv[:, :, SUBLANES - (CONV_W - 1):, :], 1, 2).reshape(nb, CONV_W - 1, D_FF)
    return (y_prompt, y_sample.reshape(ns, 1, D_MODEL),
            st5(p5r), st5(p5i), phg[None], pconv[None],
            st5(s5r), st5(s5i), shg[None], sconv[None])
```

```python
import functools

import jax
import jax.numpy as jnp
from jax import lax
from jax.experimental import pallas as pl
from jax.experimental.pallas import tpu as pltpu

F32 = jnp.float32
BF16 = jnp.bfloat16

D_MODEL = 1024
N_META = 16
S5_DIM = 512
S5_GROUP = 16
S5_GROUPS = 32
S5_STATE = 64
S5_LANES = S5_GROUPS * S5_STATE
HG_DIM = 512
HG_HEAD_DIM = 128
HG_HEADS = 4
HG_CHUNK = 64
D_FF = 2816
CONV_W = 3
EPS = 1e-6

S5_HALVES = 2
S5_HALF_CH = S5_DIM // S5_HALVES
S5_HALF_ST = S5_LANES // S5_HALVES
SCAN_LANES = 512
FF_CHUNK = 256
N_FF_CHUNKS = D_FF // FF_CHUNK
SUBLANES = 8
VMEM_LIMIT = 56 * 1024 * 1024


def _sigmoid(x):
    return 1.0 / (1.0 + jnp.exp(-x))


def _rms_norm(x, g):
    ms = jnp.mean(x * x, axis=-1, keepdims=True)
    return x * lax.rsqrt(ms + EPS) * g


def _dot(a, b):
    return jnp.dot(a, b, preferred_element_type=F32)


def _params(*sem):
    return pltpu.CompilerParams(dimension_semantics=sem, vmem_limit_bytes=VMEM_LIMIT)


def _prep_kernel(lam_re_ref, lam_im_ref, dt_ref, bt_re_ref, bt_im_ref, hlb_ref,
                 ar_ref, ai_ref, bb_re_ref, bb_im_ref, lb_ref):
    lam_re = lam_re_ref[...]
    lam_im = lam_im_ref[...]
    dt = jnp.exp(dt_ref[...])
    mag = jnp.exp(lam_re * dt)
    ar = mag * jnp.cos(lam_im * dt)
    ai = mag * jnp.sin(lam_im * dt)
    nr = ar - 1.0
    den = lam_re * lam_re + lam_im * lam_im
    cr = (nr * lam_re + ai * lam_im) / den
    ci = (ai * lam_re - nr * lam_im) / den
    ar_ref[...] = ar
    ai_ref[...] = ai
    bt_re = bt_re_ref[...]
    bt_im = bt_im_ref[...]
    bb_re_ref[...] = cr * bt_re - ci * bt_im
    bb_im_ref[...] = cr * bt_im + ci * bt_re
    hlb = hlb_ref[...]
    e = jnp.exp(hlb - jnp.max(hlb, axis=0, keepdims=True))
    lb_ref[...] = e[0:1] / jnp.sum(e, axis=0, keepdims=True)


def _prep(lam_re, lam_im, log_dt, b_re, b_im, hlb):
    g, p = lam_re.shape
    sds = jax.ShapeDtypeStruct
    return pl.pallas_call(
        _prep_kernel,
        out_shape=(sds((g, 1, p), F32), sds((g, 1, p), F32),
                   sds((g, S5_GROUP, p), F32), sds((g, S5_GROUP, p), F32),
                   sds((1, HG_DIM), F32)),
        name="param_prep",
    )(lam_re.reshape(g, 1, p), lam_im.reshape(g, 1, p),
      jnp.broadcast_to(log_dt.reshape(g, 1, 1), (g, 1, p)),
      b_re.transpose(0, 2, 1), b_im.transpose(0, 2, 1), hlb)


def _block_diag(blocks):
    h, g, r, c = blocks.shape
    eye = jnp.eye(g, dtype=blocks.dtype)
    return jnp.einsum("jgrc,gh->jgrhc", blocks, eye).reshape(h, g * r, g * c)


def _inproj_kernel(x_ref, g_ref, w_ref, u_ref, z_ref):
    hn = _rms_norm(x_ref[0], g_ref[...])
    z = _dot(hn.astype(BF16), w_ref[...])
    u_ref[...] = z[:, :S5_DIM]
    z_ref[0] = z[:, S5_DIM:]


def _inproj(x, g, w, tm, n_tiles):
    n, l, d = x.shape
    cols = w.shape[1]
    rows = tm * n_tiles
    return pl.pallas_call(
        _inproj_kernel,
        grid=(n, n_tiles),
        in_specs=[pl.BlockSpec((1, tm, d), lambda b, i: (b, i, 0)),
                  pl.BlockSpec((1, d), lambda b, i: (0, 0)),
                  pl.BlockSpec((d, cols), lambda b, i: (0, 0))],
        out_specs=[pl.BlockSpec((tm, S5_DIM), lambda b, i: (i, b)),
                   pl.BlockSpec((1, tm, cols - S5_DIM), lambda b, i: (b, i, 0))],
        out_shape=(jax.ShapeDtypeStruct((rows, n * S5_DIM), F32),
                   jax.ShapeDtypeStruct((n, rows, cols - S5_DIM), F32)),
        compiler_params=_params("arbitrary", "arbitrary"),
        name="inproj",
    )(x, g, w)


def _gelu_tanh(y):
    return 0.5 * y * (1.0 + jnp.tanh(0.7978845608028654 * (y + 0.044715 * (y * y * y))))


def _s5_kernel(u_ref, h0r_ref, h0i_ref, ar_ref, ai_ref, wbr_ref, wbi_ref, wcr_ref, wci_ref, d_ref,
               y_ref, hr_out_ref, hi_out_ref, xr_s, xi_s, hr_s, hi_s, *, n, tt):
    @pl.when(pl.program_id(0) == 0)
    def _():
        hr_s[...] = h0r_ref[...]
        hi_s[...] = h0i_ref[...]

    u = u_ref[...]
    ub = u.astype(BF16)
    for j in range(S5_HALVES):
        ch = slice(j * S5_HALF_CH, (j + 1) * S5_HALF_CH)
        xr_s[...] = _dot(ub[:, ch], wbr_ref[j])
        xi_s[...] = _dot(ub[:, ch], wbi_ref[j])
        for c in range(S5_HALF_ST // SCAN_LANES):
            loc = slice(c * SCAN_LANES, (c + 1) * SCAN_LANES)
            glob = slice(j * S5_HALF_ST + c * SCAN_LANES, j * S5_HALF_ST + (c + 1) * SCAN_LANES)
            ar = ar_ref[:, glob]
            ai = ai_ref[:, glob]

            def step(t, carry, ar=ar, ai=ai, loc=loc):
                hr, hi = carry
                r = 0 if tt == 1 else pl.multiple_of(t * n, n)
                nhr = ar * hr - ai * hi + xr_s[pl.ds(r, n), loc]
                nhi = ar * hi + ai * hr + xi_s[pl.ds(r, n), loc]
                xr_s[pl.ds(r, n), loc] = nhr
                xi_s[pl.ds(r, n), loc] = nhi
                return nhr, nhi

            carry = (hr_s[:, glob], hi_s[:, glob])
            if tt == 1:
                carry = step(0, carry)
            else:
                carry = lax.fori_loop(0, tt, step, carry, unroll=8)
            hr_s[:, glob] = carry[0]
            hi_s[:, glob] = carry[1]
        y = (_dot(xr_s[...].astype(BF16), wcr_ref[j]) + _dot(xi_s[...].astype(BF16), wci_ref[j])
             + d_ref[:, ch] * u[:, ch])
        y_ref[:, ch] = _gelu_tanh(y)
    hr_out_ref[...] = hr_s[...]
    hi_out_ref[...] = hi_s[...]


def _s5(u_tm, h0r, h0i, ar, ai, wbr, wbi, wcr, wci, d, n, tt, n_tiles):
    rows = tt * n
    const = lambda shape: pl.BlockSpec(shape, lambda i: (0,) * len(shape))
    ar_b = jnp.broadcast_to(ar, (n, S5_LANES))
    ai_b = jnp.broadcast_to(ai, (n, S5_LANES))
    return pl.pallas_call(
        functools.partial(_s5_kernel, n=n, tt=tt),
        grid=(n_tiles,),
        in_specs=[pl.BlockSpec((rows, S5_DIM), lambda i: (i, 0)),
                  const((n, S5_LANES)), const((n, S5_LANES)),
                  const((n, S5_LANES)), const((n, S5_LANES)),
                  const(wbr.shape), const(wbi.shape), const(wcr.shape), const(wci.shape),
                  const((1, S5_DIM))],
        out_specs=[pl.BlockSpec((rows, S5_DIM), lambda i: (i, 0)),
                   const((n, S5_LANES)), const((n, S5_LANES))],
        out_shape=(jax.ShapeDtypeStruct((rows * n_tiles, S5_DIM), F32),
                   jax.ShapeDtypeStruct((n, S5_LANES), F32),
                   jax.ShapeDtypeStruct((n, S5_LANES), F32)),
        scratch_shapes=[pltpu.VMEM((rows, S5_HALF_ST), F32), pltpu.VMEM((rows, S5_HALF_ST), F32),
                        pltpu.VMEM((n, S5_LANES), F32), pltpu.VMEM((n, S5_LANES), F32)],
        compiler_params=_params("arbitrary"),
        name="s5_scan",
    )(u_tm, h0r, h0i, ar_b, ai_b, wbr, wbi, wcr, wci, d)


def _hgrn_gate_out(o, g, ng):
    parts = []
    for h in range(HG_HEADS):
        oh = o[:, h * HG_HEAD_DIM:(h + 1) * HG_HEAD_DIM]
        ms = jnp.mean(oh * oh, axis=-1, keepdims=True)
        parts.append(oh * lax.rsqrt(ms + EPS))
    return jnp.concatenate(parts, axis=-1) * ng * (g * _sigmoid(g))


def _hgrn_kernel(q_ref, f_ref, i_ref, g_ref, lb_ref, ng_ref, s0_ref,
                 y_ref, s_out_ref, st_s, o_s, *, th, valid):
    i = pl.program_id(1)
    c = HG_CHUNK

    @pl.when(i == 0)
    def _():
        for h in range(HG_HEADS):
            st_s[h] = s0_ref[0, h].T

    lb = lb_ref[...]
    f = lb + (1.0 - lb) * _sigmoid(f_ref[0])
    lc = jnp.log(f)
    k = 1.0 - f
    q = q_ref[0]
    if valid < th:
        live = lax.broadcasted_iota(jnp.int32, (th, 1), 0) < valid
        lc = jnp.where(live, lc, 0.0)
        k = jnp.where(live, k, 0.0)
        q = jnp.where(live, q, 0.0)
    row = lax.broadcasted_iota(jnp.int32, (th, th), 0)
    col = lax.broadcasted_iota(jnp.int32, (th, th), 1)
    same_chunk = jnp.bitwise_xor(row, col) < c
    tri = jnp.where(same_chunk & (col <= row), 1.0, 0.0).astype(BF16)
    lc_hi = lc.astype(BF16)
    lc_lo = (lc - lc_hi.astype(F32)).astype(BF16)
    b = _dot(tri, lc_hi) + _dot(tri, lc_lo)
    qd = (q * jnp.exp(b)).astype(BF16)
    kd = (k * jnp.exp(-b)).astype(BF16)
    v = i_ref[0]
    vb = v.astype(BF16)
    causal = (lax.broadcasted_iota(jnp.int32, (c, c), 1) <= lax.broadcasted_iota(jnp.int32, (c, c), 0))
    nt = (((1,), (1,)), ((), ()))
    for cc in range(th // c):
        rows = slice(cc * c, (cc + 1) * c)
        bl = b[cc * c + c - 1:cc * c + c, :]
        kdec = (k[rows] * jnp.exp(bl - b[rows])).astype(BF16)
        dec = jnp.exp(bl)
        for h in range(HG_HEADS):
            ls = slice(h * HG_HEAD_DIM, (h + 1) * HG_HEAD_DIM)
            st = st_s[h]
            att = lax.dot_general(qd[rows, ls], kd[rows, ls], nt, preferred_element_type=F32)
            att = jnp.where(causal, att, 0.0).astype(BF16)
            o = (lax.dot_general(qd[rows, ls], st.astype(BF16), nt, preferred_element_type=F32)
                 + _dot(att, vb[rows, ls]))
            o_s[rows, ls] = o
            vt = v[rows, ls].T.astype(BF16)
            st_s[h] = dec[:, ls] * st + _dot(vt, kdec[:, ls])
    y_ref[0] = _hgrn_gate_out(o_s[...], g_ref[0], ng_ref[...])

    @pl.when(i == pl.num_programs(1) - 1)
    def _():
        for h in range(HG_HEADS):
            s_out_ref[0, h] = st_s[h].T


def _hgrn(z, lb, ng, s0, th, n_tiles, valid):
    n = z.shape[0]
    col = lambda j: pl.BlockSpec((1, th, HG_DIM), lambda b, i, j=j: (b, i, j))
    vec = pl.BlockSpec((1, HG_DIM), lambda b, i: (0, 0))
    st = pl.BlockSpec((1, HG_HEADS, HG_HEAD_DIM, HG_HEAD_DIM), lambda b, i: (b, 0, 0, 0))
    return pl.pallas_call(
        functools.partial(_hgrn_kernel, th=th, valid=valid),
        grid=(n, n_tiles),
        in_specs=[col(0), col(1), col(2), col(3), vec, vec, st],
        out_specs=[pl.BlockSpec((1, th, HG_DIM), lambda b, i: (b, i, 0)), st],
        out_shape=(jax.ShapeDtypeStruct((n, n_tiles * th, HG_DIM), F32),
                   jax.ShapeDtypeStruct((n, HG_HEADS, HG_HEAD_DIM, HG_HEAD_DIM), F32)),
        scratch_shapes=[pltpu.VMEM((HG_HEADS, HG_HEAD_DIM, HG_HEAD_DIM), F32),
                        pltpu.VMEM((th, HG_DIM), F32)],
        compiler_params=_params("arbitrary", "arbitrary"),
        name="hgrn_chunks",
    )(z, z, z, z, lb, ng, s0)


def _hgrn_step_kernel(q_ref, f_ref, i_ref, g_ref, lb_ref, ng_ref, s0_ref,
                      y_ref, s_out_ref, o_s, *, sb):
    lb = lb_ref[...]
    f = lb + (1.0 - lb) * _sigmoid(f_ref[0])
    k = 1.0 - f
    q = q_ref[0]
    v = i_ref[0]
    pad = jnp.zeros((HG_HEAD_DIM - 3 * sb, HG_HEAD_DIM), F32)
    for h in range(HG_HEADS):
        ls = slice(h * HG_HEAD_DIM, (h + 1) * HG_HEAD_DIM)
        cols = jnp.concatenate([q[:, ls], f[:, ls], k[:, ls], pad], axis=0).T
        for s in range(sb):
            qc = cols[:, s:s + 1]
            fc = cols[:, sb + s:sb + s + 1]
            kc = cols[:, 2 * sb + s:2 * sb + s + 1]
            sn = fc * s0_ref[s, h] + kc * v[s:s + 1, ls]
            s_out_ref[s, h] = sn
            o_s[s:s + 1, ls] = jnp.sum(qc * sn, axis=0, keepdims=True)
    y_ref[0] = _hgrn_gate_out(o_s[...], g_ref[0], ng_ref[...])


def _hgrn_step(z, lb, ng, s0, sb):
    r = z.shape[1]
    col = lambda j: pl.BlockSpec((1, sb, HG_DIM), lambda i, j=j: (0, i, j))
    vec = pl.BlockSpec((1, HG_DIM), lambda i: (0, 0))
    st = pl.BlockSpec((sb, HG_HEADS, HG_HEAD_DIM, HG_HEAD_DIM), lambda i: (i, 0, 0, 0))
    return pl.pallas_call(
        functools.partial(_hgrn_step_kernel, sb=sb),
        grid=(r // sb,),
        in_specs=[col(0), col(1), col(2), col(3), vec, vec, st],
        out_specs=[pl.BlockSpec((1, sb, HG_DIM), lambda i: (0, i, 0)), st],
        out_shape=(jax.ShapeDtypeStruct((1, r, HG_DIM), F32),
                   jax.ShapeDtypeStruct(s0.shape, F32)),
        scratch_shapes=[pltpu.VMEM((sb, HG_DIM), F32)],
        compiler_params=_params("arbitrary"),
        name="hgrn_step",
    )(z, z, z, z, lb, ng, s0)


def _ffn_kernel(y5_ref, yh_ref, h_ref, wglu_ref, bglu_ref, wout_ref, g2_ref,
                wup_ref, wd_ref, cw_ref, cin_a_ref, cin_b_ref, gf_ref,
                out_ref, cout_ref, carry_s, hn_s, h1_s, s_s, *, tm, per_row):
    y5p = y5_ref[...]
    y5 = y5p * _sigmoid(_dot(y5p.astype(BF16), wglu_ref[...]) + bglu_ref[...])
    ymix = jnp.concatenate([y5, yh_ref[0]], axis=-1).astype(BF16)
    h1 = h_ref[0] + _dot(ymix, wout_ref[...])
    h1_s[...] = h1
    hn_s[...] = _rms_norm(h1, g2_ref[...]).astype(BF16)

    if not per_row:
        @pl.when(pl.program_id(1) == 0)
        def _():
            carry_s[...] = cin_a_ref[0]
        rid = lax.broadcasted_iota(jnp.int32, (SUBLANES, 1), 0)

    for j in range(N_FF_CHUNKS):
        cs = slice(j * FF_CHUNK, (j + 1) * FF_CHUNK)
        hn = hn_s[...]
        a = _dot(hn, wup_ref[:, cs])
        v = _dot(hn, wup_ref[:, D_FF + j * FF_CHUNK:D_FF + (j + 1) * FF_CHUNK])
        cw = cw_ref[:, cs]
        if per_row:
            cout_ref[:, cs] = a
            cv = cw[3:4] + cin_a_ref[:, cs] * cw[0:1] + cin_b_ref[:, cs] * cw[1:2] + a * cw[2:3]
        else:
            cv = (cw[3:4] + pltpu.roll(a, 2, 0) * cw[0:1] + pltpu.roll(a, 1, 0) * cw[1:2]
                  + a * cw[2:3])
            last = a[tm - SUBLANES:tm]
            d = carry_s[:, cs] - last
            fix = (jnp.where(rid < 1, pltpu.roll(d, 1, 0), 0.0) * cw[1:2]
                   + jnp.where(rid < 2, pltpu.roll(d, 2, 0), 0.0) * cw[0:1])
            cv = jnp.concatenate([cv[:SUBLANES] + fix, cv[SUBLANES:]], axis=0)
            carry_s[:, cs] = last
        s_s[:, cs] = ((cv * _sigmoid(cv)) * v).astype(BF16)

    out_ref[0] = _rms_norm(h1_s[...] + _dot(s_s[...], wd_ref[...]), gf_ref[...])
    if not per_row:
        cout_ref[0] = carry_s[...]


def _ffn(y5, yh, h, wglu, bglu, wout, g2, wup, wd, cw, cin_a, cin_b, gf, tm, n_tiles, per_row):
    n = h.shape[0]
    const = lambda shape: pl.BlockSpec(shape, lambda b, i: (0,) * len(shape),
                                       pipeline_mode=pl.Buffered(1))
    if per_row:
        cin_spec = const(cin_a.shape)
        cout_spec, cout_shape = pl.BlockSpec(cin_a.shape, lambda b, i: (0, 0)), cin_a.shape
    else:
        cin_spec = pl.BlockSpec((1, SUBLANES, D_FF), lambda b, i: (b, 0, 0))
        cout_spec, cout_shape = cin_spec, (n, SUBLANES, D_FF)
    return pl.pallas_call(
        functools.partial(_ffn_kernel, tm=tm, per_row=per_row),
        grid=(n, n_tiles),
        in_specs=[pl.BlockSpec((tm, S5_DIM), lambda b, i: (i, b)),
                  pl.BlockSpec((1, tm, HG_DIM), lambda b, i: (b, i, 0)),
                  pl.BlockSpec((1, tm, D_MODEL), lambda b, i: (b, i, 0)),
                  const(wglu.shape), const(bglu.shape), const(wout.shape), const(g2.shape),
                  const(wup.shape), const(wd.shape), const(cw.shape),
                  cin_spec, cin_spec, const(gf.shape)],
        out_specs=[pl.BlockSpec((1, tm, D_MODEL), lambda b, i: (b, i, 0)), cout_spec],
        out_shape=(jax.ShapeDtypeStruct((n, n_tiles * tm, D_MODEL), F32),
                   jax.ShapeDtypeStruct(cout_shape, F32)),
        scratch_shapes=[pltpu.VMEM((SUBLANES, D_FF), F32),
                        pltpu.VMEM((tm, D_MODEL), BF16),
                        pltpu.VMEM((tm, D_MODEL), F32),
                        pltpu.VMEM((tm, D_FF), BF16)],
        compiler_params=_params("arbitrary", "arbitrary"),
        name="mix_out_ffn",
    )(y5, yh, h, wglu, bglu, wout, g2, wup, wd, cw, cin_a, cin_b, gf)


def kernel(x_prompt, x_sample, state_s5_re, state_s5_im, state_hgrn, state_ffn_conv, meta_tokens, norm_mix_g, w_in, s5_lambda_re, s5_lambda_im, s5_log_dt, s5_b_re, s5_b_im, s5_c_re, s5_c_im, s5_d, s5_w_glu, s5_b_glu, hg_lower_bounds, hg_norm_g, w_out, norm_ffn_g, ffn_w_up, ffn_conv_w, ffn_conv_b, ffn_w_down, final_norm_g):
    nb, seq, _ = x_prompt.shape
    ns = x_sample.shape[0]
    li = 0

    ar, ai, bb_re, bb_im, lb = _prep(s5_lambda_re[li], s5_lambda_im[li], s5_log_dt[li],
                                     s5_b_re[li], s5_b_im[li], hg_lower_bounds)
    ar = ar.reshape(1, S5_LANES)
    ai = ai.reshape(1, S5_LANES)
    halves = lambda t: t.reshape((S5_HALVES, S5_GROUPS // S5_HALVES) + t.shape[1:])
    wbr = _block_diag(halves(bb_re)).astype(BF16)
    wbi = _block_diag(halves(bb_im)).astype(BF16)
    wcr = _block_diag(halves(s5_c_re[li].transpose(0, 2, 1))).astype(BF16)
    wci = _block_diag(halves(-s5_c_im[li].transpose(0, 2, 1))).astype(BF16)
    d5 = s5_d[li].reshape(1, S5_DIM)
    g1 = norm_mix_g[li].reshape(1, D_MODEL)
    g2 = norm_ffn_g[li].reshape(1, D_MODEL)
    gf = final_norm_g.reshape(1, D_MODEL)
    ng = hg_norm_g[li].reshape(1, HG_DIM)
    bglu = s5_b_glu[li].reshape(1, S5_DIM)
    w_in_b = w_in[li].astype(BF16)
    wglu = s5_w_glu[li].astype(BF16)
    wout = w_out[li].astype(BF16)
    wup = ffn_w_up[li].astype(BF16)
    wd = ffn_w_down[li].astype(BF16)
    cw = jnp.concatenate([ffn_conv_w[li], ffn_conv_b[li][None],
                          jnp.zeros((SUBLANES - CONV_W - 1, D_FF), F32)], axis=0)
    s5w = (ar, ai, wbr, wbi, wcr, wci, d5)
    ffw = (wglu, bglu, wout, g2, wup, wd, cw)

    def seq_run(x, s5r0, s5i0, hg0, conv0, tm, n_tiles, tt, s5_tiles, th, hg_tiles, valid):
        n = x.shape[0]
        u_tm, z = _inproj(x, g1, w_in_b, tm, n_tiles)
        y5, s5r, s5i = _s5(u_tm.reshape(-1, S5_DIM), s5r0, s5i0, *s5w, n, tt, s5_tiles)
        yh, hg = _hgrn(z, lb, ng, hg0, th, hg_tiles, valid)
        y, conv = _ffn(y5.reshape(-1, n * S5_DIM), yh, x, *ffw, conv0, conv0, gf, tm, n_tiles, False)
        return y, s5r, s5i, hg, conv

    meta = jnp.zeros((1, HG_CHUNK, D_MODEL), F32).at[0, :N_META].set(meta_tokens)
    z5 = jnp.zeros((1, S5_LANES), F32)
    zh = jnp.zeros((1, HG_HEADS, HG_HEAD_DIM, HG_HEAD_DIM), F32)
    zc = jnp.zeros((1, SUBLANES, D_FF), F32)
    u_tm, z = _inproj(meta, g1, w_in_b, HG_CHUNK, 1)
    y5, m5r, m5i = _s5(u_tm, z5, z5, *s5w, 1, N_META, 1)
    yh, mhg = _hgrn(z, lb, ng, zh, HG_CHUNK, 1, N_META)
    _, mconv = _ffn(y5, yh, meta, *ffw, zc, zc, gf, N_META, 1, False)

    bc = lambda t: jnp.broadcast_to(t, (nb,) + t.shape[1:])
    tm, tt, th = 512, 128, 256
    y_prompt, p5r, p5i, phg, pconv = seq_run(
        x_prompt, bc(m5r), bc(m5i), bc(mhg), bc(mconv),
        tm, seq // tm, tt, seq // tt, th, seq // th, th)

    xs = x_sample.reshape(1, ns, D_MODEL)
    u_tm, z = _inproj(xs, g1, w_in_b, ns, 1)
    y5, s5r, s5i = _s5(u_tm, state_s5_re[li].reshape(ns, S5_LANES), state_s5_im[li].reshape(ns, S5_LANES),
                       *s5w, ns, 1, 1)
    yh, shg = _hgrn_step(z, lb, ng, state_hgrn[li], 16)
    buf = state_ffn_conv[li]
    y_sample, a_new = _ffn(y5, yh, xs, *ffw, buf[:, 0], buf[:, 1], gf, ns, 1, True)
    sconv = jnp.stack([buf[:, 1], a_new], axis=1)

    st5 = lambda t: t.reshape(1, -1, S5_GROUPS, S5_STATE)
    pconv = pconv[:, SUBLANES - (CONV_W - 1):, :]
    return (y_prompt, y_sample.reshape(ns, 1, D_MODEL),
            st5(p5r), st5(p5i), phg[None], pconv[None],
            st5(s5r), st5(s5i), shg[None], sconv[None])
```

```python
import functools

import jax
import jax.numpy as jnp
from jax import lax
from jax.experimental import pallas as pl
from jax.experimental.pallas import tpu as pltpu

F32 = jnp.float32
BF16 = jnp.bfloat16

D_MODEL = 1024
N_META = 16
S5_DIM = 512
S5_GROUP = 16
S5_GROUPS = 32
S5_STATE = 64
S5_LANES = S5_GROUPS * S5_STATE
HG_DIM = 512
HG_HEAD_DIM = 128
HG_HEADS = 4
HG_CHUNK = 64
D_FF = 2816
CONV_W = 3
EPS = 1e-6

S5_HALVES = 2
S5_HALF_CH = S5_DIM // S5_HALVES
S5_HALF_ST = S5_LANES // S5_HALVES
SCAN_LANES = 512
FF_CHUNK = 256
N_FF_CHUNKS = D_FF // FF_CHUNK
SUBLANES = 8
LANES = 128
VMEM_LIMIT = 56 * 1024 * 1024


def _sigmoid(x):
    return 1.0 / (1.0 + jnp.exp(-x))


def _rms_norm(x, g):
    ms = jnp.mean(x * x, axis=-1, keepdims=True)
    return x * lax.rsqrt(ms + EPS) * g


def _dot(a, b):
    return jnp.dot(a, b, preferred_element_type=F32)


def _params(*sem):
    return pltpu.CompilerParams(dimension_semantics=sem, vmem_limit_bytes=VMEM_LIMIT)


def _prep_kernel(lam_re_ref, lam_im_ref, dt_ref, bt_re_ref, bt_im_ref, hlb_ref,
                 ar_ref, ai_ref, bb_re_ref, bb_im_ref, lb_ref):
    lam_re = lam_re_ref[...]
    lam_im = lam_im_ref[...]
    dt = jnp.exp(dt_ref[...])
    mag = jnp.exp(lam_re * dt)
    ar = mag * jnp.cos(lam_im * dt)
    ai = mag * jnp.sin(lam_im * dt)
    nr = ar - 1.0
    den = lam_re * lam_re + lam_im * lam_im
    cr = (nr * lam_re + ai * lam_im) / den
    ci = (ai * lam_re - nr * lam_im) / den
    ar_ref[...] = ar
    ai_ref[...] = ai
    bt_re = bt_re_ref[...]
    bt_im = bt_im_ref[...]
    bb_re_ref[...] = cr * bt_re - ci * bt_im
    bb_im_ref[...] = cr * bt_im + ci * bt_re
    hlb = hlb_ref[...]
    e = jnp.exp(hlb - jnp.max(hlb, axis=0, keepdims=True))
    lb_ref[...] = e[0:1] / jnp.sum(e, axis=0, keepdims=True)


def _prep(lam_re, lam_im, log_dt, b_re, b_im, hlb):
    g, p = lam_re.shape
    sds = jax.ShapeDtypeStruct
    return pl.pallas_call(
        _prep_kernel,
        out_shape=(sds((g, 1, p), F32), sds((g, 1, p), F32),
                   sds((g, S5_GROUP, p), F32), sds((g, S5_GROUP, p), F32),
                   sds((1, HG_DIM), F32)),
        name="param_prep",
    )(lam_re.reshape(g, 1, p), lam_im.reshape(g, 1, p),
      jnp.broadcast_to(log_dt.reshape(g, 1, 1), (g, 1, p)),
      b_re.transpose(0, 2, 1), b_im.transpose(0, 2, 1), hlb)


def _block_diag(blocks):
    h, g, r, c = blocks.shape
    eye = jnp.eye(g, dtype=blocks.dtype)
    return jnp.einsum("jgrc,gh->jgrhc", blocks, eye).reshape(h, g * r, g * c)


def _inproj_kernel(x_ref, g_ref, w_ref, u_ref, z_ref, *, n, tm):
    hn = _rms_norm(x_ref[0], g_ref[...])
    z = _dot(hn.astype(BF16), w_ref[...])
    b = pl.program_id(1)
    for l in range(S5_DIM // LANES):
        zl = z[:, l * LANES:(l + 1) * LANES]
        if n == 1:
            u_ref[l] = zl
        else:
            u_ref[l, pl.ds(b, tm, stride=n), :] = zl
    z_ref[0] = z[:, S5_DIM:]


def _inproj(x, g, w, tm, n_tiles):
    n, l, d = x.shape
    cols = w.shape[1]
    rows = tm * n_tiles
    const = lambda shape: pl.BlockSpec(shape, lambda i, b: (0,) * len(shape),
                                       pipeline_mode=pl.Buffered(1))
    return pl.pallas_call(
        functools.partial(_inproj_kernel, n=n, tm=tm),
        grid=(n_tiles, n),
        in_specs=[pl.BlockSpec((1, tm, d), lambda i, b: (b, i, 0)),
                  const((1, d)), const((d, cols))],
        out_specs=[pl.BlockSpec((S5_DIM // LANES, tm * n, LANES), lambda i, b: (0, i, 0)),
                   pl.BlockSpec((1, tm, cols - S5_DIM), lambda i, b: (b, i, 0))],
        out_shape=(jax.ShapeDtypeStruct((S5_DIM // LANES, rows * n, LANES), F32),
                   jax.ShapeDtypeStruct((n, rows, cols - S5_DIM), F32)),
        compiler_params=_params("arbitrary", "arbitrary"),
        name="inproj",
    )(x, g, w)


def _gelu_tanh(y):
    return 0.5 * y * (1.0 + jnp.tanh(0.7978845608028654 * (y + 0.044715 * (y * y * y))))


def _s5_kernel(u_ref, h0r_ref, h0i_ref, ar_ref, ai_ref, wbr_ref, wbi_ref, wcr_ref, wci_ref, d_ref,
               y_ref, hr_out_ref, hi_out_ref, xr_s, xi_s, hr_s, hi_s, y_s, *, n, tt, interleaved):
    @pl.when(pl.program_id(0) == 0)
    def _():
        hr_s[...] = h0r_ref[...]
        hi_s[...] = h0i_ref[...]

    slabs = S5_HALF_CH // LANES
    for j in range(S5_HALVES):
        ch = slice(j * S5_HALF_CH, (j + 1) * S5_HALF_CH)
        u = jnp.concatenate([u_ref[j * slabs + l] for l in range(slabs)], axis=1)
        ub = u.astype(BF16)
        xr_s[...] = _dot(ub, wbr_ref[j])
        xi_s[...] = _dot(ub, wbi_ref[j])
        for c in range(S5_HALF_ST // SCAN_LANES):
            loc = slice(c * SCAN_LANES, (c + 1) * SCAN_LANES)
            glob = slice(j * S5_HALF_ST + c * SCAN_LANES, j * S5_HALF_ST + (c + 1) * SCAN_LANES)
            ar = ar_ref[:, glob]
            ai = ai_ref[:, glob]

            def step(t, carry, ar=ar, ai=ai, loc=loc):
                hr, hi = carry
                r = 0 if tt == 1 else pl.multiple_of(t * n, n)
                nhr = ar * hr - ai * hi + xr_s[pl.ds(r, n), loc]
                nhi = ar * hi + ai * hr + xi_s[pl.ds(r, n), loc]
                xr_s[pl.ds(r, n), loc] = nhr
                xi_s[pl.ds(r, n), loc] = nhi
                return nhr, nhi

            carry = (hr_s[:, glob], hi_s[:, glob])
            if tt == 1:
                carry = step(0, carry)
            else:
                carry = lax.fori_loop(0, tt, step, carry, unroll=8)
            hr_s[:, glob] = carry[0]
            hi_s[:, glob] = carry[1]
        y = _gelu_tanh(_dot(xr_s[...].astype(BF16), wcr_ref[j]) + _dot(xi_s[...].astype(BF16), wci_ref[j])
                       + d_ref[:, ch] * u)
        if interleaved:
            for l in range(slabs):
                y_s[j * slabs + l] = y[:, l * LANES:(l + 1) * LANES]
        else:
            y_ref[0, :, ch] = y
    if interleaved:
        for b in range(n):
            for l in range(S5_DIM // LANES):
                y_ref[b, :, l * LANES:(l + 1) * LANES] = y_s[l, pl.ds(b, tt, stride=n), :]
    hr_out_ref[...] = hr_s[...]
    hi_out_ref[...] = hi_s[...]


def _s5(u4, h0r, h0i, ar, ai, wbr, wbi, wcr, wci, d, n, tt, n_tiles):
    rows = tt * n
    interleaved = n > 1 and tt > 1
    const = lambda shape: pl.BlockSpec(shape, lambda i: (0,) * len(shape),
                                       pipeline_mode=pl.Buffered(1))
    state = pl.BlockSpec((n, S5_LANES), lambda i: (0, 0))
    ar_b = jnp.broadcast_to(ar, (n, S5_LANES))
    ai_b = jnp.broadcast_to(ai, (n, S5_LANES))
    y_block = (n, tt, S5_DIM) if interleaved else (1, rows, S5_DIM)
    y_shape = (n, tt * n_tiles, S5_DIM) if interleaved else (1, rows * n_tiles, S5_DIM)
    return pl.pallas_call(
        functools.partial(_s5_kernel, n=n, tt=tt, interleaved=interleaved),
        grid=(n_tiles,),
        in_specs=[pl.BlockSpec((S5_DIM // LANES, rows, LANES), lambda i: (0, i, 0)),
                  const((n, S5_LANES)), const((n, S5_LANES)),
                  const((n, S5_LANES)), const((n, S5_LANES)),
                  const(wbr.shape), const(wbi.shape), const(wcr.shape), const(wci.shape),
                  const((1, S5_DIM))],
        out_specs=[pl.BlockSpec(y_block, lambda i: (0, i, 0)), state, state],
        out_shape=(jax.ShapeDtypeStruct(y_shape, F32),
                   jax.ShapeDtypeStruct((n, S5_LANES), F32),
                   jax.ShapeDtypeStruct((n, S5_LANES), F32)),
        scratch_shapes=[pltpu.VMEM((rows, S5_HALF_ST), F32), pltpu.VMEM((rows, S5_HALF_ST), F32),
                        pltpu.VMEM((n, S5_LANES), F32), pltpu.VMEM((n, S5_LANES), F32),
                        pltpu.VMEM((S5_DIM // LANES, rows if interleaved else SUBLANES, LANES), F32)],
        compiler_params=_params("arbitrary"),
        name="s5_scan",
    )(u4, h0r, h0i, ar_b, ai_b, wbr, wbi, wcr, wci, d)


def _hgrn_gate_out(o, g, ng):
    parts = []
    for h in range(HG_HEADS):
        oh = o[:, h * HG_HEAD_DIM:(h + 1) * HG_HEAD_DIM]
        ms = jnp.mean(oh * oh, axis=-1, keepdims=True)
        parts.append(oh * lax.rsqrt(ms + EPS))
    return jnp.concatenate(parts, axis=-1) * ng * (g * _sigmoid(g))


def _hgrn_kernel(q_ref, f_ref, i_ref, g_ref, lb_ref, ng_ref, s0_ref,
                 y_ref, s_out_ref, st_s, o_s, *, th, valid):
    i = pl.program_id(1)
    c = HG_CHUNK

    @pl.when(i == 0)
    def _():
        for h in range(HG_HEADS):
            st_s[h] = s0_ref[0, h].T

    lb = lb_ref[...]
    f = lb + (1.0 - lb) * _sigmoid(f_ref[0])
    lc = jnp.log(f)
    k = 1.0 - f
    q = q_ref[0]
    if valid < th:
        live = lax.broadcasted_iota(jnp.int32, (th, 1), 0) < valid
        lc = jnp.where(live, lc, 0.0)
        k = jnp.where(live, k, 0.0)
        q = jnp.where(live, q, 0.0)
    row = lax.broadcasted_iota(jnp.int32, (th, th), 0)
    col = lax.broadcasted_iota(jnp.int32, (th, th), 1)
    same_chunk = jnp.bitwise_xor(row, col) < c
    tri = jnp.where(same_chunk & (col <= row), 1.0, 0.0).astype(BF16)
    lc_hi = lc.astype(BF16)
    lc_lo = (lc - lc_hi.astype(F32)).astype(BF16)
    b = _dot(tri, lc_hi) + _dot(tri, lc_lo)
    qd = (q * jnp.exp(b)).astype(BF16)
    kd = (k * jnp.exp(-b)).astype(BF16)
    v = i_ref[0]
    vb = v.astype(BF16)
    causal = (lax.broadcasted_iota(jnp.int32, (c, c), 1) <= lax.broadcasted_iota(jnp.int32, (c, c), 0))
    nt = (((1,), (1,)), ((), ()))
    for cc in range(th // c):
        rows = slice(cc * c, (cc + 1) * c)
        bl = b[cc * c + c - 1:cc * c + c, :]
        kdec = (k[rows] * jnp.exp(bl - b[rows])).astype(BF16)
        dec = jnp.exp(bl)
        for h in range(HG_HEADS):
            ls = slice(h * HG_HEAD_DIM, (h + 1) * HG_HEAD_DIM)
            st = st_s[h]
            att = lax.dot_general(qd[rows, ls], kd[rows, ls], nt, preferred_element_type=F32)
            att = jnp.where(causal, att, 0.0).astype(BF16)
            o = (lax.dot_general(qd[rows, ls], st.astype(BF16), nt, preferred_element_type=F32)
                 + _dot(att, vb[rows, ls]))
            o_s[rows, ls] = o
            vt = v[rows, ls].T.astype(BF16)
            st_s[h] = dec[:, ls] * st + _dot(vt, kdec[:, ls])
    y_ref[0] = _hgrn_gate_out(o_s[...], g_ref[0], ng_ref[...])

    @pl.when(i == pl.num_programs(1) - 1)
    def _():
        for h in range(HG_HEADS):
            s_out_ref[0, h] = st_s[h].T


def _hgrn(z, lb, ng, s0, th, n_tiles, valid):
    n = z.shape[0]
    col = lambda j: pl.BlockSpec((1, th, HG_DIM), lambda b, i, j=j: (b, i, j))
    vec = pl.BlockSpec((1, HG_DIM), lambda b, i: (0, 0))
    st = pl.BlockSpec((1, HG_HEADS, HG_HEAD_DIM, HG_HEAD_DIM), lambda b, i: (b, 0, 0, 0))
    return pl.pallas_call(
        functools.partial(_hgrn_kernel, th=th, valid=valid),
        grid=(n, n_tiles),
        in_specs=[col(0), col(1), col(2), col(3), vec, vec, st],
        out_specs=[pl.BlockSpec((1, th, HG_DIM), lambda b, i: (b, i, 0)), st],
        out_shape=(jax.ShapeDtypeStruct((n, n_tiles * th, HG_DIM), F32),
                   jax.ShapeDtypeStruct((n, HG_HEADS, HG_HEAD_DIM, HG_HEAD_DIM), F32)),
        scratch_shapes=[pltpu.VMEM((HG_HEADS, HG_HEAD_DIM, HG_HEAD_DIM), F32),
                        pltpu.VMEM((th, HG_DIM), F32)],
        compiler_params=_params("arbitrary", "arbitrary"),
        name="hgrn_chunks",
    )(z, z, z, z, lb, ng, s0)


def _hgrn_step_kernel(q_ref, f_ref, i_ref, g_ref, lb_ref, ng_ref, s0_ref,
                      y_ref, s_out_ref, o_s, *, sb):
    lb = lb_ref[...]
    f = lb + (1.0 - lb) * _sigmoid(f_ref[0])
    k = 1.0 - f
    q = q_ref[0]
    v = i_ref[0]
    pad = jnp.zeros((HG_HEAD_DIM - 3 * sb, HG_HEAD_DIM), F32)
    for h in range(HG_HEADS):
        ls = slice(h * HG_HEAD_DIM, (h + 1) * HG_HEAD_DIM)
        cols = jnp.concatenate([q[:, ls], f[:, ls], k[:, ls], pad], axis=0).T
        for s in range(sb):
            qc = cols[:, s:s + 1]
            fc = cols[:, sb + s:sb + s + 1]
            kc = cols[:, 2 * sb + s:2 * sb + s + 1]
            sn = fc * s0_ref[s, h] + kc * v[s:s + 1, ls]
            s_out_ref[s, h] = sn
            o_s[s:s + 1, ls] = jnp.sum(qc * sn, axis=0, keepdims=True)
    y_ref[0] = _hgrn_gate_out(o_s[...], g_ref[0], ng_ref[...])


def _hgrn_step(z, lb, ng, s0, sb):
    r = z.shape[1]
    col = lambda j: pl.BlockSpec((1, sb, HG_DIM), lambda i, j=j: (0, i, j))
    vec = pl.BlockSpec((1, HG_DIM), lambda i: (0, 0))
    st = pl.BlockSpec((sb, HG_HEADS, HG_HEAD_DIM, HG_HEAD_DIM), lambda i: (i, 0, 0, 0))
    return pl.pallas_call(
        functools.partial(_hgrn_step_kernel, sb=sb),
        grid=(r // sb,),
        in_specs=[col(0), col(1), col(2), col(3), vec, vec, st],
        out_specs=[pl.BlockSpec((1, sb, HG_DIM), lambda i: (0, i, 0)), st],
        out_shape=(jax.ShapeDtypeStruct((1, r, HG_DIM), F32),
                   jax.ShapeDtypeStruct(s0.shape, F32)),
        scratch_shapes=[pltpu.VMEM((sb, HG_DIM), F32)],
        compiler_params=_params("arbitrary"),
        name="hgrn_step",
    )(z, z, z, z, lb, ng, s0)


def _ffn_kernel(y5_ref, yh_ref, h_ref, wglu_ref, bglu_ref, wout_ref, g2_ref,
                wup_ref, wd_ref, cw_ref, cin_a_ref, cin_b_ref, gf_ref,
                out_ref, cout_ref, carry_s, hn_s, h1_s, s_s, *, tm, per_row):
    y5p = y5_ref[0]
    y5 = y5p * _sigmoid(_dot(y5p.astype(BF16), wglu_ref[...]) + bglu_ref[...])
    ymix = jnp.concatenate([y5, yh_ref[0]], axis=-1).astype(BF16)
    h1 = h_ref[0] + _dot(ymix, wout_ref[...])
    h1_s[...] = h1
    hn_s[...] = _rms_norm(h1, g2_ref[...]).astype(BF16)

    if not per_row:
        @pl.when(pl.program_id(1) == 0)
        def _():
            carry_s[...] = cin_a_ref[0]
        rid = lax.broadcasted_iota(jnp.int32, (SUBLANES, 1), 0)

    for j in range(N_FF_CHUNKS):
        cs = slice(j * FF_CHUNK, (j + 1) * FF_CHUNK)
        hn = hn_s[...]
        a = _dot(hn, wup_ref[:, cs])
        v = _dot(hn, wup_ref[:, D_FF + j * FF_CHUNK:D_FF + (j + 1) * FF_CHUNK])
        cw = cw_ref[:, cs]
        if per_row:
            cout_ref[:, cs] = a
            cv = cw[3:4] + cin_a_ref[:, cs] * cw[0:1] + cin_b_ref[:, cs] * cw[1:2] + a * cw[2:3]
        else:
            cv = (cw[3:4] + pltpu.roll(a, 2, 0) * cw[0:1] + pltpu.roll(a, 1, 0) * cw[1:2]
                  + a * cw[2:3])
            last = a[tm - SUBLANES:tm]
            d = carry_s[:, cs] - last
            fix = (jnp.where(rid < 1, pltpu.roll(d, 1, 0), 0.0) * cw[1:2]
                   + jnp.where(rid < 2, pltpu.roll(d, 2, 0), 0.0) * cw[0:1])
            cv = jnp.concatenate([cv[:SUBLANES] + fix, cv[SUBLANES:]], axis=0)
            carry_s[:, cs] = last
        s_s[:, cs] = ((cv * _sigmoid(cv)) * v).astype(BF16)

    out_ref[0] = _rms_norm(h1_s[...] + _dot(s_s[...], wd_ref[...]), gf_ref[...])
    if not per_row:
        cout_ref[0] = carry_s[...]


def _ffn(y5, yh, h, wglu, bglu, wout, g2, wup, wd, cw, cin_a, cin_b, gf, tm, n_tiles, per_row):
    n = h.shape[0]
    const = lambda shape: pl.BlockSpec(shape, lambda b, i: (0,) * len(shape),
                                       pipeline_mode=pl.Buffered(1))
    if per_row:
        cin_spec = const(cin_a.shape)
        cout_spec, cout_shape = pl.BlockSpec(cin_a.shape, lambda b, i: (0, 0)), cin_a.shape
    else:
        cin_spec = pl.BlockSpec((1, SUBLANES, D_FF), lambda b, i: (b, 0, 0))
        cout_spec, cout_shape = cin_spec, (n, SUBLANES, D_FF)
    return pl.pallas_call(
        functools.partial(_ffn_kernel, tm=tm, per_row=per_row),
        grid=(n, n_tiles),
        in_specs=[pl.BlockSpec((1, tm, S5_DIM), lambda b, i: (b, i, 0)),
                  pl.BlockSpec((1, tm, HG_DIM), lambda b, i: (b, i, 0)),
                  pl.BlockSpec((1, tm, D_MODEL), lambda b, i: (b, i, 0)),
                  const(wglu.shape), const(bglu.shape), const(wout.shape), const(g2.shape),
                  const(wup.shape), const(wd.shape), const(cw.shape),
                  cin_spec, cin_spec, const(gf.shape)],
        out_specs=[pl.BlockSpec((1, tm, D_MODEL), lambda b, i: (b, i, 0)), cout_spec],
        out_shape=(jax.ShapeDtypeStruct((n, n_tiles * tm, D_MODEL), F32),
                   jax.ShapeDtypeStruct(cout_shape, F32)),
        scratch_shapes=[pltpu.VMEM((SUBLANES, D_FF), F32),
                        pltpu.VMEM((tm, D_MODEL), BF16),
                        pltpu.VMEM((tm, D_MODEL), F32),
                        pltpu.VMEM((tm, D_FF), BF16)],
        compiler_params=_params("arbitrary", "arbitrary"),
        name="mix_out_ffn",
    )(y5, yh, h, wglu, bglu, wout, g2, wup, wd, cw, cin_a, cin_b, gf)


def kernel(x_prompt, x_sample, state_s5_re, state_s5_im, state_hgrn, state_ffn_conv, meta_tokens, norm_mix_g, w_in, s5_lambda_re, s5_lambda_im, s5_log_dt, s5_b_re, s5_b_im, s5_c_re, s5_c_im, s5_d, s5_w_glu, s5_b_glu, hg_lower_bounds, hg_norm_g, w_out, norm_ffn_g, ffn_w_up, ffn_conv_w, ffn_conv_b, ffn_w_down, final_norm_g):
    nb, seq, _ = x_prompt.shape
    ns = x_sample.shape[0]
    li = 0

    ar, ai, bb_re, bb_im, lb = _prep(s5_lambda_re[li], s5_lambda_im[li], s5_log_dt[li],
                                     s5_b_re[li], s5_b_im[li], hg_lower_bounds)
    ar = ar.reshape(1, S5_LANES)
    ai = ai.reshape(1, S5_LANES)
    halves = lambda t: t.reshape((S5_HALVES, S5_GROUPS // S5_HALVES) + t.shape[1:])
    wbr = _block_diag(halves(bb_re)).astype(BF16)
    wbi = _block_diag(halves(bb_im)).astype(BF16)
    wcr = _block_diag(halves(s5_c_re[li].transpose(0, 2, 1))).astype(BF16)
    wci = _block_diag(halves(-s5_c_im[li].transpose(0, 2, 1))).astype(BF16)
    d5 = s5_d[li].reshape(1, S5_DIM)
    g1 = norm_mix_g[li].reshape(1, D_MODEL)
    g2 = norm_ffn_g[li].reshape(1, D_MODEL)
    gf = final_norm_g.reshape(1, D_MODEL)
    ng = hg_norm_g[li].reshape(1, HG_DIM)
    bglu = s5_b_glu[li].reshape(1, S5_DIM)
    w_in_b = w_in[li].astype(BF16)
    wglu = s5_w_glu[li].astype(BF16)
    wout = w_out[li].astype(BF16)
    wup = ffn_w_up[li].astype(BF16)
    wd = ffn_w_down[li].astype(BF16)
    cw = jnp.concatenate([ffn_conv_w[li], ffn_conv_b[li][None],
                          jnp.zeros((SUBLANES - CONV_W - 1, D_FF), F32)], axis=0)
    s5w = (ar, ai, wbr, wbi, wcr, wci, d5)
    ffw = (wglu, bglu, wout, g2, wup, wd, cw)

    def seq_run(x, s5r0, s5i0, hg0, conv0, tm, n_tiles, tt, s5_tiles, th, hg_tiles, valid):
        n = x.shape[0]
        u_tm, z = _inproj(x, g1, w_in_b, tm, n_tiles)
        y5, s5r, s5i = _s5(u_tm, s5r0, s5i0, *s5w, n, tt, s5_tiles)
        yh, hg = _hgrn(z, lb, ng, hg0, th, hg_tiles, valid)
        y, conv = _ffn(y5, yh, x, *ffw, conv0, conv0, gf, tm, n_tiles, False)
        return y, s5r, s5i, hg, conv

    meta = jnp.zeros((1, HG_CHUNK, D_MODEL), F32).at[0, :N_META].set(meta_tokens)
    z5 = jnp.zeros((1, S5_LANES), F32)
    zh = jnp.zeros((1, HG_HEADS, HG_HEAD_DIM, HG_HEAD_DIM), F32)
    zc = jnp.zeros((1, SUBLANES, D_FF), F32)
    u_tm, z = _inproj(meta, g1, w_in_b, HG_CHUNK, 1)
    y5, m5r, m5i = _s5(u_tm, z5, z5, *s5w, 1, N_META, 1)
    yh, mhg = _hgrn(z, lb, ng, zh, HG_CHUNK, 1, N_META)
    _, mconv = _ffn(y5, yh, meta, *ffw, zc, zc, gf, N_META, 1, False)

    bc = lambda t: jnp.broadcast_to(t, (nb,) + t.shape[1:])
    tm, tt, th = 512, 128, 256
    y_prompt, p5r, p5i, phg, pconv = seq_run(
        x_prompt, bc(m5r), bc(m5i), bc(mhg), bc(mconv),
        tm, seq // tm, tt, seq // tt, th, seq // th, th)

    xs = x_sample.reshape(1, ns, D_MODEL)
    u_tm, z = _inproj(xs, g1, w_in_b, ns, 1)
    y5, s5r, s5i = _s5(u_tm, state_s5_re[li].reshape(ns, S5_LANES), state_s5_im[li].reshape(ns, S5_LANES),
                       *s5w, ns, 1, 1)
    yh, shg = _hgrn_step(z, lb, ng, state_hgrn[li], 16)
    buf = state_ffn_conv[li]
    y_sample, a_new = _ffn(y5, yh, xs, *ffw, buf[:, 0], buf[:, 1], gf, ns, 1, True)
    sconv = jnp.stack([buf[:, 1], a_new], axis=1)

    st5 = lambda t: t.reshape(1, -1, S5_GROUPS, S5_STATE)
    pconv = pconv[:, SUBLANES - (CONV_W - 1):, :]
    return (y_prompt, y_sample.reshape(ns, 1, D_MODEL),
            st5(p5r), st5(p5i), phg[None], pconv[None],
            st5(s5r), st5(s5i), shg[None], sconv[None])
```

```python
import functools

import jax
import jax.numpy as jnp
from jax import lax
from jax.experimental import pallas as pl
from jax.experimental.pallas import tpu as pltpu

F32 = jnp.float32
BF16 = jnp.bfloat16

D_MODEL = 1024
N_META = 16
S5_DIM = 512
S5_GROUP = 16
S5_GROUPS = 32
S5_STATE = 64
S5_LANES = S5_GROUPS * S5_STATE
HG_DIM = 512
HG_HEAD_DIM = 128
HG_HEADS = 4
HG_CHUNK = 64
D_FF = 2816
CONV_W = 3
EPS = 1e-6

S5_HALVES = 2
S5_HALF_CH = S5_DIM // S5_HALVES
S5_HALF_ST = S5_LANES // S5_HALVES
SCAN_LANES = 512
FF_CHUNK = 256
N_FF_CHUNKS = D_FF // FF_CHUNK
SUBLANES = 8
LANES = 128
VMEM_LIMIT = 56 * 1024 * 1024


def _sigmoid(x):
    return 1.0 / (1.0 + jnp.exp(-x))


def _rms_norm(x, g):
    ms = jnp.mean(x * x, axis=-1, keepdims=True)
    return x * lax.rsqrt(ms + EPS) * g


def _dot(a, b):
    return jnp.dot(a, b, preferred_element_type=F32)


def _params(*sem):
    return pltpu.CompilerParams(dimension_semantics=sem, vmem_limit_bytes=VMEM_LIMIT)


def _prep_kernel(lam_re_ref, lam_im_ref, dt_ref, bt_re_ref, bt_im_ref, hlb_ref,
                 ar_ref, ai_ref, bb_re_ref, bb_im_ref, lb_ref):
    lam_re = lam_re_ref[...]
    lam_im = lam_im_ref[...]
    dt = jnp.exp(dt_ref[...])
    mag = jnp.exp(lam_re * dt)
    ar = mag * jnp.cos(lam_im * dt)
    ai = mag * jnp.sin(lam_im * dt)
    nr = ar - 1.0
    den = lam_re * lam_re + lam_im * lam_im
    cr = (nr * lam_re + ai * lam_im) / den
    ci = (ai * lam_re - nr * lam_im) / den
    ar_ref[...] = ar
    ai_ref[...] = ai
    bt_re = bt_re_ref[...]
    bt_im = bt_im_ref[...]
    bb_re_ref[...] = cr * bt_re - ci * bt_im
    bb_im_ref[...] = cr * bt_im + ci * bt_re
    hlb = hlb_ref[...]
    e = jnp.exp(hlb - jnp.max(hlb, axis=0, keepdims=True))
    lb_ref[...] = e[0:1] / jnp.sum(e, axis=0, keepdims=True)


def _prep(lam_re, lam_im, log_dt, b_re, b_im, hlb):
    g, p = lam_re.shape
    sds = jax.ShapeDtypeStruct
    return pl.pallas_call(
        _prep_kernel,
        out_shape=(sds((g, 1, p), F32), sds((g, 1, p), F32),
                   sds((g, S5_GROUP, p), F32), sds((g, S5_GROUP, p), F32),
                   sds((1, HG_DIM), F32)),
        name="param_prep",
    )(lam_re.reshape(g, 1, p), lam_im.reshape(g, 1, p),
      jnp.broadcast_to(log_dt.reshape(g, 1, 1), (g, 1, p)),
      b_re.transpose(0, 2, 1), b_im.transpose(0, 2, 1), hlb)


def _block_diag(blocks):
    h, g, r, c = blocks.shape
    eye = jnp.eye(g, dtype=blocks.dtype)
    return jnp.einsum("jgrc,gh->jgrhc", blocks, eye).reshape(h, g * r, g * c)


def _inproj_kernel(x_ref, g_ref, w_ref, u_ref, z_ref, *, n, tm):
    hn = _rms_norm(x_ref[0], g_ref[...])
    z = _dot(hn.astype(BF16), w_ref[...])
    b = pl.program_id(1)
    for l in range(S5_DIM // LANES):
        zl = z[:, l * LANES:(l + 1) * LANES]
        if n == 1:
            u_ref[l] = zl
        else:
            u_ref[l, pl.ds(b, tm, stride=n), :] = zl
    z_ref[0] = z[:, S5_DIM:]


def _inproj(x, g, w, tm, n_tiles):
    n, l, d = x.shape
    cols = w.shape[1]
    rows = tm * n_tiles
    const = lambda shape: pl.BlockSpec(shape, lambda i, b: (0,) * len(shape),
                                       pipeline_mode=pl.Buffered(1))
    return pl.pallas_call(
        functools.partial(_inproj_kernel, n=n, tm=tm),
        grid=(n_tiles, n),
        in_specs=[pl.BlockSpec((1, tm, d), lambda i, b: (b, i, 0)),
                  const((1, d)), const((d, cols))],
        out_specs=[pl.BlockSpec((S5_DIM // LANES, tm * n, LANES), lambda i, b: (0, i, 0)),
                   pl.BlockSpec((1, tm, cols - S5_DIM), lambda i, b: (b, i, 0))],
        out_shape=(jax.ShapeDtypeStruct((S5_DIM // LANES, rows * n, LANES), F32),
                   jax.ShapeDtypeStruct((n, rows, cols - S5_DIM), F32)),
        compiler_params=_params("arbitrary", "arbitrary"),
        name="inproj",
    )(x, g, w)


def _gelu_tanh(y):
    return 0.5 * y * (1.0 + jnp.tanh(0.7978845608028654 * (y + 0.044715 * (y * y * y))))


def _s5_kernel(u_ref, h0r_ref, h0i_ref, ar_ref, ai_ref, wbr_ref, wbi_ref, wcr_ref, wci_ref, d_ref,
               y_ref, hr_out_ref, hi_out_ref, xr_s, xi_s, hr_s, hi_s, y_s, *, n, tt, interleaved):
    @pl.when(pl.program_id(0) == 0)
    def _():
        hr_s[...] = h0r_ref[...]
        hi_s[...] = h0i_ref[...]

    slabs = S5_HALF_CH // LANES
    load_u = lambda j: jnp.concatenate([u_ref[j * slabs + l] for l in range(slabs)], axis=1)
    for j in range(S5_HALVES):
        st = slice(j * S5_HALF_ST, (j + 1) * S5_HALF_ST)
        ub = load_u(j).astype(BF16)
        xr_s[:, st] = _dot(ub, wbr_ref[j])
        xi_s[:, st] = _dot(ub, wbi_ref[j])
    for c in range(S5_LANES // SCAN_LANES):
        loc = slice(c * SCAN_LANES, (c + 1) * SCAN_LANES)
        ar = ar_ref[:, loc]
        ai = ai_ref[:, loc]

        def step(t, carry, ar=ar, ai=ai, loc=loc):
            hr, hi = carry
            r = 0 if tt == 1 else pl.multiple_of(t * n, n)
            nhr = ar * hr - ai * hi + xr_s[pl.ds(r, n), loc]
            nhi = ar * hi + ai * hr + xi_s[pl.ds(r, n), loc]
            xr_s[pl.ds(r, n), loc] = nhr
            xi_s[pl.ds(r, n), loc] = nhi
            return nhr, nhi

        carry = (hr_s[:, loc], hi_s[:, loc])
        if tt == 1:
            carry = step(0, carry)
        else:
            carry = lax.fori_loop(0, tt, step, carry, unroll=True)
        hr_s[:, loc] = carry[0]
        hi_s[:, loc] = carry[1]
    for j in range(S5_HALVES):
        ch = slice(j * S5_HALF_CH, (j + 1) * S5_HALF_CH)
        st = slice(j * S5_HALF_ST, (j + 1) * S5_HALF_ST)
        y = _gelu_tanh(_dot(xr_s[:, st].astype(BF16), wcr_ref[j]) + _dot(xi_s[:, st].astype(BF16), wci_ref[j])
                       + d_ref[:, ch] * load_u(j))
        if interleaved:
            for l in range(slabs):
                y_s[j * slabs + l] = y[:, l * LANES:(l + 1) * LANES]
        else:
            y_ref[0, :, ch] = y
    if interleaved:
        for b in range(n):
            for l in range(S5_DIM // LANES):
                y_ref[b, :, l * LANES:(l + 1) * LANES] = y_s[l, pl.ds(b, tt, stride=n), :]
    hr_out_ref[...] = hr_s[...]
    hi_out_ref[...] = hi_s[...]


def _s5(u4, h0r, h0i, ar, ai, wbr, wbi, wcr, wci, d, n, tt, n_tiles):
    rows = tt * n
    interleaved = n > 1 and tt > 1
    const = lambda shape: pl.BlockSpec(shape, lambda i: (0,) * len(shape),
                                       pipeline_mode=pl.Buffered(1))
    state = pl.BlockSpec((n, S5_LANES), lambda i: (0, 0))
    ar_b = jnp.broadcast_to(ar, (n, S5_LANES))
    ai_b = jnp.broadcast_to(ai, (n, S5_LANES))
    y_block = (n, tt, S5_DIM) if interleaved else (1, rows, S5_DIM)
    y_shape = (n, tt * n_tiles, S5_DIM) if interleaved else (1, rows * n_tiles, S5_DIM)
    return pl.pallas_call(
        functools.partial(_s5_kernel, n=n, tt=tt, interleaved=interleaved),
        grid=(n_tiles,),
        in_specs=[pl.BlockSpec((S5_DIM // LANES, rows, LANES), lambda i: (0, i, 0)),
                  const((n, S5_LANES)), const((n, S5_LANES)),
                  const((n, S5_LANES)), const((n, S5_LANES)),
                  const(wbr.shape), const(wbi.shape), const(wcr.shape), const(wci.shape),
                  const((1, S5_DIM))],
        out_specs=[pl.BlockSpec(y_block, lambda i: (0, i, 0)), state, state],
        out_shape=(jax.ShapeDtypeStruct(y_shape, F32),
                   jax.ShapeDtypeStruct((n, S5_LANES), F32),
                   jax.ShapeDtypeStruct((n, S5_LANES), F32)),
        scratch_shapes=[pltpu.VMEM((rows, S5_LANES), F32), pltpu.VMEM((rows, S5_LANES), F32),
                        pltpu.VMEM((n, S5_LANES), F32), pltpu.VMEM((n, S5_LANES), F32),
                        pltpu.VMEM((S5_DIM // LANES, rows if interleaved else SUBLANES, LANES), F32)],
        compiler_params=_params("arbitrary"),
        name="s5_scan",
    )(u4, h0r, h0i, ar_b, ai_b, wbr, wbi, wcr, wci, d)


def _hgrn_gate_out(o, g, ng):
    parts = []
    for h in range(HG_HEADS):
        oh = o[:, h * HG_HEAD_DIM:(h + 1) * HG_HEAD_DIM]
        ms = jnp.mean(oh * oh, axis=-1, keepdims=True)
        parts.append(oh * lax.rsqrt(ms + EPS))
    return jnp.concatenate(parts, axis=-1) * ng * (g * _sigmoid(g))


def _hgrn_kernel(q_ref, f_ref, i_ref, g_ref, lb_ref, ng_ref, tri_ref, s0_ref,
                 y_ref, s_out_ref, st_s, o_s, *, th, valid):
    i = pl.program_id(1)
    c = HG_CHUNK

    @pl.when(i == 0)
    def _():
        for h in range(HG_HEADS):
            st_s[h] = s0_ref[0, h].T

    lb = lb_ref[...]
    f = lb + (1.0 - lb) * _sigmoid(f_ref[0])
    lc = jnp.log(f)
    k = 1.0 - f
    q = q_ref[0]
    if valid < th:
        live = lax.broadcasted_iota(jnp.int32, (th, 1), 0) < valid
        lc = jnp.where(live, lc, 0.0)
        k = jnp.where(live, k, 0.0)
        q = jnp.where(live, q, 0.0)
    tri = tri_ref[...]
    tr = tri.shape[0]
    lc_hi = lc.astype(BF16)
    lc_lo = (lc - lc_hi.astype(F32)).astype(BF16)
    b = jnp.concatenate([_dot(tri, lc_hi[r:r + tr]) + _dot(tri, lc_lo[r:r + tr])
                         for r in range(0, th, tr)], axis=0)
    qd = (q * jnp.exp(b)).astype(BF16)
    kd = k * jnp.exp(-b)
    kdb = kd.astype(BF16)
    v = i_ref[0]
    vb = v.astype(BF16)
    causal = (lax.broadcasted_iota(jnp.int32, (c, c), 1) <= lax.broadcasted_iota(jnp.int32, (c, c), 0))
    nt = (((1,), (1,)), ((), ()))
    for cc in range(th // c):
        rows = slice(cc * c, (cc + 1) * c)
        dec = jnp.exp(b[cc * c + c - 1:cc * c + c, :])
        kdec = (kd[rows] * dec).astype(BF16)
        for h in range(HG_HEADS):
            ls = slice(h * HG_HEAD_DIM, (h + 1) * HG_HEAD_DIM)
            st = st_s[h]
            att = lax.dot_general(qd[rows, ls], kdb[rows, ls], nt, preferred_element_type=F32)
            att = jnp.where(causal, att, 0.0).astype(BF16)
            o = (lax.dot_general(qd[rows, ls], st.astype(BF16), nt, preferred_element_type=F32)
                 + _dot(att, vb[rows, ls]))
            o_s[rows, ls] = o
            vt = v[rows, ls].T.astype(BF16)
            st_s[h] = dec[:, ls] * st + _dot(vt, kdec[:, ls])
    y_ref[0] = _hgrn_gate_out(o_s[...], g_ref[0], ng_ref[...])

    @pl.when(i == pl.num_programs(1) - 1)
    def _():
        for h in range(HG_HEADS):
            s_out_ref[0, h] = st_s[h].T


def _hgrn(z, lb, ng, s0, th, n_tiles, valid):
    n = z.shape[0]
    tr = min(th, 256)
    idx = jnp.arange(tr)
    tri = ((idx[:, None] // HG_CHUNK == idx[None, :] // HG_CHUNK)
           & (idx[None, :] <= idx[:, None])).astype(BF16)
    col = lambda j: pl.BlockSpec((1, th, HG_DIM), lambda b, i, j=j: (b, i, j))
    const = lambda shape: pl.BlockSpec(shape, lambda b, i: (0,) * len(shape),
                                       pipeline_mode=pl.Buffered(1))
    st = pl.BlockSpec((1, HG_HEADS, HG_HEAD_DIM, HG_HEAD_DIM), lambda b, i: (b, 0, 0, 0))
    return pl.pallas_call(
        functools.partial(_hgrn_kernel, th=th, valid=valid),
        grid=(n, n_tiles),
        in_specs=[col(0), col(1), col(2), col(3), const((1, HG_DIM)), const((1, HG_DIM)),
                  const((tr, tr)), st],
        out_specs=[pl.BlockSpec((1, th, HG_DIM), lambda b, i: (b, i, 0)), st],
        out_shape=(jax.ShapeDtypeStruct((n, n_tiles * th, HG_DIM), F32),
                   jax.ShapeDtypeStruct((n, HG_HEADS, HG_HEAD_DIM, HG_HEAD_DIM), F32)),
        scratch_shapes=[pltpu.VMEM((HG_HEADS, HG_HEAD_DIM, HG_HEAD_DIM), F32),
                        pltpu.VMEM((th, HG_DIM), F32)],
        compiler_params=_params("arbitrary", "arbitrary"),
        name="hgrn_chunks",
    )(z, z, z, z, lb, ng, tri, s0)


def _hgrn_step_kernel(q_ref, f_ref, i_ref, g_ref, lb_ref, ng_ref, s0_ref,
                      y_ref, s_out_ref, o_s, *, sb):
    lb = lb_ref[...]
    f = lb + (1.0 - lb) * _sigmoid(f_ref[0])
    k = 1.0 - f
    q = q_ref[0]
    v = i_ref[0]
    pad = jnp.zeros((HG_HEAD_DIM - 3 * sb, HG_HEAD_DIM), F32)
    for h in range(HG_HEADS):
        ls = slice(h * HG_HEAD_DIM, (h + 1) * HG_HEAD_DIM)
        cols = jnp.concatenate([q[:, ls], f[:, ls], k[:, ls], pad], axis=0).T
        for s in range(sb):
            qc = cols[:, s:s + 1]
            fc = cols[:, sb + s:sb + s + 1]
            kc = cols[:, 2 * sb + s:2 * sb + s + 1]
            sn = fc * s0_ref[s, h] + kc * v[s:s + 1, ls]
            s_out_ref[s, h] = sn
            o_s[s:s + 1, ls] = jnp.sum(qc * sn, axis=0, keepdims=True)
    y_ref[0] = _hgrn_gate_out(o_s[...], g_ref[0], ng_ref[...])


def _hgrn_step(z, lb, ng, s0, sb):
    r = z.shape[1]
    col = lambda j: pl.BlockSpec((1, sb, HG_DIM), lambda i, j=j: (0, i, j))
    vec = pl.BlockSpec((1, HG_DIM), lambda i: (0, 0))
    st = pl.BlockSpec((sb, HG_HEADS, HG_HEAD_DIM, HG_HEAD_DIM), lambda i: (i, 0, 0, 0))
    return pl.pallas_call(
        functools.partial(_hgrn_step_kernel, sb=sb),
        grid=(r // sb,),
        in_specs=[col(0), col(1), col(2), col(3), vec, vec, st],
        out_specs=[pl.BlockSpec((1, sb, HG_DIM), lambda i: (0, i, 0)), st],
        out_shape=(jax.ShapeDtypeStruct((1, r, HG_DIM), F32),
                   jax.ShapeDtypeStruct(s0.shape, F32)),
        scratch_shapes=[pltpu.VMEM((sb, HG_DIM), F32)],
        compiler_params=_params("arbitrary"),
        name="hgrn_step",
    )(z, z, z, z, lb, ng, s0)


def _ffn_kernel(y5_ref, yh_ref, h_ref, wglu_ref, bglu_ref, wout_ref, g2_ref,
                wup_ref, wd_ref, cw_ref, cin_a_ref, cin_b_ref, gf_ref,
                out_ref, cout_ref, carry_s, hn_s, h1_s, s_s, *, tm, per_row):
    y5p = y5_ref[0]
    y5 = y5p * _sigmoid(_dot(y5p.astype(BF16), wglu_ref[...]) + bglu_ref[...])
    ymix = jnp.concatenate([y5, yh_ref[0]], axis=-1).astype(BF16)
    h1 = h_ref[0] + _dot(ymix, wout_ref[...])
    h1_s[...] = h1
    hn_s[...] = _rms_norm(h1, g2_ref[...]).astype(BF16)

    if not per_row:
        @pl.when(pl.program_id(1) == 0)
        def _():
            carry_s[...] = cin_a_ref[0]
        rid = lax.broadcasted_iota(jnp.int32, (SUBLANES, 1), 0)

    for j in range(N_FF_CHUNKS):
        cs = slice(j * FF_CHUNK, (j + 1) * FF_CHUNK)
        hn = hn_s[...]
        a = _dot(hn, wup_ref[:, cs])
        v = _dot(hn, wup_ref[:, D_FF + j * FF_CHUNK:D_FF + (j + 1) * FF_CHUNK])
        cw = cw_ref[:, cs]
        if per_row:
            cout_ref[:, cs] = a
            cv = cw[3:4] + cin_a_ref[:, cs] * cw[0:1] + cin_b_ref[:, cs] * cw[1:2] + a * cw[2:3]
        else:
            cv = (cw[3:4] + pltpu.roll(a, 2, 0) * cw[0:1] + pltpu.roll(a, 1, 0) * cw[1:2]
                  + a * cw[2:3])
            last = a[tm - SUBLANES:tm]
            d = carry_s[:, cs] - last
            fix = (jnp.where(rid < 1, pltpu.roll(d, 1, 0), 0.0) * cw[1:2]
                   + jnp.where(rid < 2, pltpu.roll(d, 2, 0), 0.0) * cw[0:1])
            cv = jnp.concatenate([cv[:SUBLANES] + fix, cv[SUBLANES:]], axis=0)
            carry_s[:, cs] = last
        s_s[:, cs] = ((cv * _sigmoid(cv)) * v).astype(BF16)

    out_ref[0] = _rms_norm(h1_s[...] + _dot(s_s[...], wd_ref[...]), gf_ref[...])
    if not per_row:
        cout_ref[0] = carry_s[...]


def _ffn(y5, yh, h, wglu, bglu, wout, g2, wup, wd, cw, cin_a, cin_b, gf, tm, n_tiles, per_row):
    n = h.shape[0]
    const = lambda shape: pl.BlockSpec(shape, lambda b, i: (0,) * len(shape),
                                       pipeline_mode=pl.Buffered(1))
    if per_row:
        cin_spec = const(cin_a.shape)
        cout_spec, cout_shape = pl.BlockSpec(cin_a.shape, lambda b, i: (0, 0)), cin_a.shape
    else:
        cin_spec = pl.BlockSpec((1, SUBLANES, D_FF), lambda b, i: (b, 0, 0))
        cout_spec, cout_shape = cin_spec, (n, SUBLANES, D_FF)
    return pl.pallas_call(
        functools.partial(_ffn_kernel, tm=tm, per_row=per_row),
        grid=(n, n_tiles),
        in_specs=[pl.BlockSpec((1, tm, S5_DIM), lambda b, i: (b, i, 0)),
                  pl.BlockSpec((1, tm, HG_DIM), lambda b, i: (b, i, 0)),
                  pl.BlockSpec((1, tm, D_MODEL), lambda b, i: (b, i, 0)),
                  const(wglu.shape), const(bglu.shape), const(wout.shape), const(g2.shape),
                  const(wup.shape), const(wd.shape), const(cw.shape),
                  cin_spec, cin_spec, const(gf.shape)],
        out_specs=[pl.BlockSpec((1, tm, D_MODEL), lambda b, i: (b, i, 0)), cout_spec],
        out_shape=(jax.ShapeDtypeStruct((n, n_tiles * tm, D_MODEL), F32),
                   jax.ShapeDtypeStruct(cout_shape, F32)),
        scratch_shapes=[pltpu.VMEM((SUBLANES, D_FF), F32),
                        pltpu.VMEM((tm, D_MODEL), BF16),
                        pltpu.VMEM((tm, D_MODEL), F32),
                        pltpu.VMEM((tm, D_FF), BF16)],
        compiler_params=_params("arbitrary", "arbitrary"),
        name="mix_out_ffn",
    )(y5, yh, h, wglu, bglu, wout, g2, wup, wd, cw, cin_a, cin_b, gf)


def kernel(x_prompt, x_sample, state_s5_re, state_s5_im, state_hgrn, state_ffn_conv, meta_tokens, norm_mix_g, w_in, s5_lambda_re, s5_lambda_im, s5_log_dt, s5_b_re, s5_b_im, s5_c_re, s5_c_im, s5_d, s5_w_glu, s5_b_glu, hg_lower_bounds, hg_norm_g, w_out, norm_ffn_g, ffn_w_up, ffn_conv_w, ffn_conv_b, ffn_w_down, final_norm_g):
    nb, seq, _ = x_prompt.shape
    ns = x_sample.shape[0]
    li = 0

    ar, ai, bb_re, bb_im, lb = _prep(s5_lambda_re[li], s5_lambda_im[li], s5_log_dt[li],
                                     s5_b_re[li], s5_b_im[li], hg_lower_bounds)
    ar = ar.reshape(1, S5_LANES)
    ai = ai.reshape(1, S5_LANES)
    halves = lambda t: t.reshape((S5_HALVES, S5_GROUPS // S5_HALVES) + t.shape[1:])
    wbr = _block_diag(halves(bb_re)).astype(BF16)
    wbi = _block_diag(halves(bb_im)).astype(BF16)
    wcr = _block_diag(halves(s5_c_re[li].transpose(0, 2, 1))).astype(BF16)
    wci = _block_diag(halves(-s5_c_im[li].transpose(0, 2, 1))).astype(BF16)
    d5 = s5_d[li].reshape(1, S5_DIM)
    g1 = norm_mix_g[li].reshape(1, D_MODEL)
    g2 = norm_ffn_g[li].reshape(1, D_MODEL)
    gf = final_norm_g.reshape(1, D_MODEL)
    ng = hg_norm_g[li].reshape(1, HG_DIM)
    bglu = s5_b_glu[li].reshape(1, S5_DIM)
    w_in_b = w_in[li].astype(BF16)
    wglu = s5_w_glu[li].astype(BF16)
    wout = w_out[li].astype(BF16)
    wup = ffn_w_up[li].astype(BF16)
    wd = ffn_w_down[li].astype(BF16)
    cw = jnp.concatenate([ffn_conv_w[li], ffn_conv_b[li][None],
                          jnp.zeros((SUBLANES - CONV_W - 1, D_FF), F32)], axis=0)
    s5w = (ar, ai, wbr, wbi, wcr, wci, d5)
    ffw = (wglu, bglu, wout, g2, wup, wd, cw)

    def seq_run(x, s5r0, s5i0, hg0, conv0, tm, n_tiles, tt, s5_tiles, th, hg_tiles, valid):
        n = x.shape[0]
        u_tm, z = _inproj(x, g1, w_in_b, tm, n_tiles)
        y5, s5r, s5i = _s5(u_tm, s5r0, s5i0, *s5w, n, tt, s5_tiles)
        yh, hg = _hgrn(z, lb, ng, hg0, th, hg_tiles, valid)
        y, conv = _ffn(y5, yh, x, *ffw, conv0, conv0, gf, tm, n_tiles, False)
        return y, s5r, s5i, hg, conv

    meta = jnp.zeros((1, HG_CHUNK, D_MODEL), F32).at[0, :N_META].set(meta_tokens)
    z5 = jnp.zeros((1, S5_LANES), F32)
    zh = jnp.zeros((1, HG_HEADS, HG_HEAD_DIM, HG_HEAD_DIM), F32)
    zc = jnp.zeros((1, SUBLANES, D_FF), F32)
    u_tm, z = _inproj(meta, g1, w_in_b, HG_CHUNK, 1)
    y5, m5r, m5i = _s5(u_tm, z5, z5, *s5w, 1, N_META, 1)
    yh, mhg = _hgrn(z, lb, ng, zh, HG_CHUNK, 1, N_META)
    _, mconv = _ffn(y5, yh, meta, *ffw, zc, zc, gf, N_META, 1, False)

    bc = lambda t: jnp.broadcast_to(t, (nb,) + t.shape[1:])
    tm, tt, th = 512, 128, 512
    y_prompt, p5r, p5i, phg, pconv = seq_run(
        x_prompt, bc(m5r), bc(m5i), bc(mhg), bc(mconv),
        tm, seq // tm, tt, seq // tt, th, seq // th, th)

    xs = x_sample.reshape(1, ns, D_MODEL)
    u_tm, z = _inproj(xs, g1, w_in_b, ns, 1)
    y5, s5r, s5i = _s5(u_tm, state_s5_re[li].reshape(ns, S5_LANES), state_s5_im[li].reshape(ns, S5_LANES),
                       *s5w, ns, 1, 1)
    yh, shg = _hgrn_step(z, lb, ng, state_hgrn[li], 16)
    buf = state_ffn_conv[li]
    y_sample, a_new = _ffn(y5, yh, xs, *ffw, buf[:, 0], buf[:, 1], gf, ns, 1, True)
    sconv = jnp.stack([buf[:, 1], a_new], axis=1)

    st5 = lambda t: t.reshape(1, -1, S5_GROUPS, S5_STATE)
    pconv = pconv[:, SUBLANES - (CONV_W - 1):, :]
    return (y_prompt, y_sample.reshape(ns, 1, D_MODEL),
            st5(p5r), st5(p5i), phg[None], pconv[None],
            st5(s5r), st5(s5i), shg[None], sconv[None])
```

```python
import functools

import jax
import jax.numpy as jnp
from jax import lax
from jax.experimental import pallas as pl
from jax.experimental.pallas import tpu as pltpu

F32 = jnp.float32
BF16 = jnp.bfloat16

D_MODEL = 1024
N_META = 16
S5_DIM = 512
S5_GROUP = 16
S5_GROUPS = 32
S5_STATE = 64
S5_LANES = S5_GROUPS * S5_STATE
HG_DIM = 512
HG_HEAD_DIM = 128
HG_HEADS = 4
HG_CHUNK = 64
D_FF = 2816
CONV_W = 3
EPS = 1e-6

S5_HALVES = 2
S5_HALF_CH = S5_DIM // S5_HALVES
S5_HALF_ST = S5_LANES // S5_HALVES
SCAN_LANES = 512
FF_CHUNK = 256
N_FF_CHUNKS = D_FF // FF_CHUNK
SUBLANES = 8
LANES = 128
VMEM_LIMIT = 56 * 1024 * 1024


def _sigmoid(x):
    return 1.0 / (1.0 + jnp.exp(-x))


def _rms_norm(x, g):
    ms = jnp.mean(x * x, axis=-1, keepdims=True)
    return x * lax.rsqrt(ms + EPS) * g


def _dot(a, b):
    return jnp.dot(a, b, preferred_element_type=F32)


def _params(*sem):
    return pltpu.CompilerParams(dimension_semantics=sem, vmem_limit_bytes=VMEM_LIMIT)


def _prep_kernel(lam_re_ref, lam_im_ref, dt_ref, bt_re_ref, bt_im_ref, hlb_ref,
                 ar_ref, ai_ref, bb_re_ref, bb_im_ref, lb_ref):
    lam_re = lam_re_ref[...]
    lam_im = lam_im_ref[...]
    dt = jnp.exp(dt_ref[...])
    mag = jnp.exp(lam_re * dt)
    ar = mag * jnp.cos(lam_im * dt)
    ai = mag * jnp.sin(lam_im * dt)
    nr = ar - 1.0
    den = lam_re * lam_re + lam_im * lam_im
    cr = (nr * lam_re + ai * lam_im) / den
    ci = (ai * lam_re - nr * lam_im) / den
    ar_ref[...] = ar
    ai_ref[...] = ai
    bt_re = bt_re_ref[...]
    bt_im = bt_im_ref[...]
    bb_re_ref[...] = cr * bt_re - ci * bt_im
    bb_im_ref[...] = cr * bt_im + ci * bt_re
    hlb = hlb_ref[...]
    e = jnp.exp(hlb - jnp.max(hlb, axis=0, keepdims=True))
    lb_ref[...] = e[0:1] / jnp.sum(e, axis=0, keepdims=True)


def _prep(lam_re, lam_im, log_dt, b_re, b_im, hlb):
    g, p = lam_re.shape
    sds = jax.ShapeDtypeStruct
    return pl.pallas_call(
        _prep_kernel,
        out_shape=(sds((g, 1, p), F32), sds((g, 1, p), F32),
                   sds((g, S5_GROUP, p), F32), sds((g, S5_GROUP, p), F32),
                   sds((1, HG_DIM), F32)),
        name="param_prep",
    )(lam_re.reshape(g, 1, p), lam_im.reshape(g, 1, p),
      jnp.broadcast_to(log_dt.reshape(g, 1, 1), (g, 1, p)),
      b_re.transpose(0, 2, 1), b_im.transpose(0, 2, 1), hlb)


def _block_diag(blocks):
    h, g, r, c = blocks.shape
    eye = jnp.eye(g, dtype=blocks.dtype)
    return jnp.einsum("jgrc,gh->jgrhc", blocks, eye).reshape(h, g * r, g * c)


def _inproj_kernel(x_ref, g_ref, w_ref, u_ref, f_ref, qig_ref, *, n, tm):
    hn = _rms_norm(x_ref[0], g_ref[...])
    z = _dot(hn.astype(BF16), w_ref[...])
    b = pl.program_id(1)
    for l in range(S5_DIM // LANES):
        zl = z[:, l * LANES:(l + 1) * LANES]
        if n == 1:
            u_ref[l] = zl
        else:
            u_ref[l, pl.ds(b, tm, stride=n), :] = zl
    col = lambda j: z[:, S5_DIM + j * HG_DIM:S5_DIM + (j + 1) * HG_DIM]
    f_ref[0] = col(1)
    for j, src in enumerate((0, 2, 3)):
        qig_ref[0, :, j * HG_DIM:(j + 1) * HG_DIM] = col(src).astype(BF16)


def _inproj(x, g, w, tm, n_tiles):
    n, l, d = x.shape
    cols = w.shape[1]
    rows = tm * n_tiles
    const = lambda shape: pl.BlockSpec(shape, lambda i, b: (0,) * len(shape),
                                       pipeline_mode=pl.Buffered(1))
    return pl.pallas_call(
        functools.partial(_inproj_kernel, n=n, tm=tm),
        grid=(n_tiles, n),
        in_specs=[pl.BlockSpec((1, tm, d), lambda i, b: (b, i, 0)),
                  const((1, d)), const((d, cols))],
        out_specs=[pl.BlockSpec((S5_DIM // LANES, tm * n, LANES), lambda i, b: (0, i, 0)),
                   pl.BlockSpec((1, tm, HG_DIM), lambda i, b: (b, i, 0)),
                   pl.BlockSpec((1, tm, 3 * HG_DIM), lambda i, b: (b, i, 0))],
        out_shape=(jax.ShapeDtypeStruct((S5_DIM // LANES, rows * n, LANES), F32),
                   jax.ShapeDtypeStruct((n, rows, HG_DIM), F32),
                   jax.ShapeDtypeStruct((n, rows, 3 * HG_DIM), BF16)),
        compiler_params=_params("arbitrary", "arbitrary"),
        name="inproj",
    )(x, g, w)


def _gelu_tanh(y):
    return 0.5 * y * (1.0 + jnp.tanh(0.7978845608028654 * (y + 0.044715 * (y * y * y))))


def _s5_kernel(u_ref, h0r_ref, h0i_ref, ar_ref, ai_ref, wbr_ref, wbi_ref, wcr_ref, wci_ref, d_ref,
               y_ref, hr_out_ref, hi_out_ref, xr_s, xi_s, hr_s, hi_s, y_s, *, n, tt, interleaved):
    @pl.when(pl.program_id(0) == 0)
    def _():
        hr_s[...] = jnp.broadcast_to(h0r_ref[...], hr_s.shape)
        hi_s[...] = jnp.broadcast_to(h0i_ref[...], hi_s.shape)

    slabs = S5_HALF_CH // LANES
    load_u = lambda j: jnp.concatenate([u_ref[j * slabs + l] for l in range(slabs)], axis=1)
    for j in range(S5_HALVES):
        st = slice(j * S5_HALF_ST, (j + 1) * S5_HALF_ST)
        ub = load_u(j).astype(BF16)
        xr_s[:, st] = _dot(ub, wbr_ref[j])
        xi_s[:, st] = _dot(ub, wbi_ref[j])
    for c in range(S5_LANES // SCAN_LANES):
        loc = slice(c * SCAN_LANES, (c + 1) * SCAN_LANES)
        ar = jnp.broadcast_to(ar_ref[:, loc], (n, SCAN_LANES))
        ai = jnp.broadcast_to(ai_ref[:, loc], (n, SCAN_LANES))

        def step(t, carry, ar=ar, ai=ai, loc=loc):
            hr, hi = carry
            r = 0 if tt == 1 else pl.multiple_of(t * n, n)
            nhr = ar * hr - ai * hi + xr_s[pl.ds(r, n), loc]
            nhi = ar * hi + ai * hr + xi_s[pl.ds(r, n), loc]
            xr_s[pl.ds(r, n), loc] = nhr
            xi_s[pl.ds(r, n), loc] = nhi
            return nhr, nhi

        carry = (hr_s[:, loc], hi_s[:, loc])
        if tt == 1:
            carry = step(0, carry)
        else:
            carry = lax.fori_loop(0, tt, step, carry, unroll=True)
        hr_s[:, loc] = carry[0]
        hi_s[:, loc] = carry[1]
    for j in range(S5_HALVES):
        ch = slice(j * S5_HALF_CH, (j + 1) * S5_HALF_CH)
        st = slice(j * S5_HALF_ST, (j + 1) * S5_HALF_ST)
        y = _gelu_tanh(_dot(xr_s[:, st].astype(BF16), wcr_ref[j]) + _dot(xi_s[:, st].astype(BF16), wci_ref[j])
                       + d_ref[:, ch] * load_u(j))
        if interleaved:
            for l in range(slabs):
                y_s[j * slabs + l] = y[:, l * LANES:(l + 1) * LANES]
        else:
            y_ref[0, :, ch] = y
    if interleaved:
        for b in range(n):
            for l in range(S5_DIM // LANES):
                y_ref[b, :, l * LANES:(l + 1) * LANES] = y_s[l, pl.ds(b, tt, stride=n), :]
    hr_out_ref[...] = hr_s[...]
    hi_out_ref[...] = hi_s[...]


def _s5(u4, h0r, h0i, ar, ai, wbr, wbi, wcr, wci, d, n, tt, n_tiles):
    rows = tt * n
    interleaved = n > 1 and tt > 1
    const = lambda shape: pl.BlockSpec(shape, lambda i: (0,) * len(shape),
                                       pipeline_mode=pl.Buffered(1))
    state = pl.BlockSpec((n, S5_LANES), lambda i: (0, 0))
    y_block = (n, tt, S5_DIM) if interleaved else (1, rows, S5_DIM)
    y_shape = (n, tt * n_tiles, S5_DIM) if interleaved else (1, rows * n_tiles, S5_DIM)
    return pl.pallas_call(
        functools.partial(_s5_kernel, n=n, tt=tt, interleaved=interleaved),
        grid=(n_tiles,),
        in_specs=[pl.BlockSpec((S5_DIM // LANES, rows, LANES), lambda i: (0, i, 0)),
                  const(h0r.shape), const(h0i.shape), const(ar.shape), const(ai.shape),
                  const(wbr.shape), const(wbi.shape), const(wcr.shape), const(wci.shape),
                  const((1, S5_DIM))],
        out_specs=[pl.BlockSpec(y_block, lambda i: (0, i, 0)), state, state],
        out_shape=(jax.ShapeDtypeStruct(y_shape, F32),
                   jax.ShapeDtypeStruct((n, S5_LANES), F32),
                   jax.ShapeDtypeStruct((n, S5_LANES), F32)),
        scratch_shapes=[pltpu.VMEM((rows, S5_LANES), F32), pltpu.VMEM((rows, S5_LANES), F32),
                        pltpu.VMEM((n, S5_LANES), F32), pltpu.VMEM((n, S5_LANES), F32),
                        pltpu.VMEM((S5_DIM // LANES, rows if interleaved else SUBLANES, LANES), F32)],
        compiler_params=_params("arbitrary"),
        name="s5_scan",
    )(u4, h0r, h0i, ar, ai, wbr, wbi, wcr, wci, d)


def _hgrn_gate_out(o, g, ng):
    parts = []
    for h in range(HG_HEADS):
        oh = o[:, h * HG_HEAD_DIM:(h + 1) * HG_HEAD_DIM]
        ms = jnp.mean(oh * oh, axis=-1, keepdims=True)
        parts.append(oh * lax.rsqrt(ms + EPS))
    return jnp.concatenate(parts, axis=-1) * ng * (g * _sigmoid(g))


def _hgrn_kernel(q_ref, f_ref, i_ref, g_ref, lb_ref, ng_ref, tri_ref, s0_ref,
                 y_ref, s_out_ref, st_s, o_s, *, th, valid):
    i = pl.program_id(1)
    c = HG_CHUNK

    @pl.when(i == 0)
    def _():
        for h in range(HG_HEADS):
            st_s[h] = s0_ref[0, h].T

    lb = lb_ref[...]
    f = lb + (1.0 - lb) * _sigmoid(f_ref[0])
    lc = jnp.log(f)
    k = 1.0 - f
    q = q_ref[0].astype(F32)
    if valid < th:
        live = lax.broadcasted_iota(jnp.int32, (th, 1), 0) < valid
        lc = jnp.where(live, lc, 0.0)
        k = jnp.where(live, k, 0.0)
        q = jnp.where(live, q, 0.0)
    tri = tri_ref[...]
    tr = tri.shape[0]
    lc_hi = lc.astype(BF16)
    lc_lo = (lc - lc_hi.astype(F32)).astype(BF16)
    b = jnp.concatenate([_dot(tri, lc_hi[r:r + tr]) + _dot(tri, lc_lo[r:r + tr])
                         for r in range(0, th, tr)], axis=0)
    qd = (q * jnp.exp(b)).astype(BF16)
    kd = k * jnp.exp(-b)
    kdb = kd.astype(BF16)
    vb = i_ref[0]
    v = vb.astype(F32)
    causal = (lax.broadcasted_iota(jnp.int32, (c, c), 1) <= lax.broadcasted_iota(jnp.int32, (c, c), 0))
    nt = (((1,), (1,)), ((), ()))
    for cc in range(th // c):
        rows = slice(cc * c, (cc + 1) * c)
        dec = jnp.exp(b[cc * c + c - 1:cc * c + c, :])
        kdec = (kd[rows] * dec).astype(BF16)
        for h in range(HG_HEADS):
            ls = slice(h * HG_HEAD_DIM, (h + 1) * HG_HEAD_DIM)
            st = st_s[h]
            att = lax.dot_general(qd[rows, ls], kdb[rows, ls], nt, preferred_element_type=F32)
            att = jnp.where(causal, att, 0.0).astype(BF16)
            o = (lax.dot_general(qd[rows, ls], st.astype(BF16), nt, preferred_element_type=F32)
                 + _dot(att, vb[rows, ls]))
            o_s[rows, ls] = o
            vt = v[rows, ls].T.astype(BF16)
            st_s[h] = dec[:, ls] * st + _dot(vt, kdec[:, ls])
    y_ref[0] = _hgrn_gate_out(o_s[...], g_ref[0].astype(F32), ng_ref[...])

    @pl.when(i == pl.num_programs(1) - 1)
    def _():
        for h in range(HG_HEADS):
            s_out_ref[0, h] = st_s[h].T


def _hgrn(zf, qig, lb, ng, s0, th, n_tiles, valid):
    n = zf.shape[0]
    own = s0.shape[0] == n
    tr = min(th, 256)
    idx = jnp.arange(tr)
    tri = ((idx[:, None] // HG_CHUNK == idx[None, :] // HG_CHUNK)
           & (idx[None, :] <= idx[:, None])).astype(BF16)
    col = lambda j: pl.BlockSpec((1, th, HG_DIM), lambda b, i, j=j: (b, i, j))
    const = lambda shape: pl.BlockSpec(shape, lambda b, i: (0,) * len(shape),
                                       pipeline_mode=pl.Buffered(1))
    st = pl.BlockSpec((1, HG_HEADS, HG_HEAD_DIM, HG_HEAD_DIM), lambda b, i: (b, 0, 0, 0))
    st_in = pl.BlockSpec((1, HG_HEADS, HG_HEAD_DIM, HG_HEAD_DIM),
                         lambda b, i: (b if own else 0, 0, 0, 0))
    return pl.pallas_call(
        functools.partial(_hgrn_kernel, th=th, valid=valid),
        grid=(n, n_tiles),
        in_specs=[col(0), col(0), col(1), col(2), const((1, HG_DIM)), const((1, HG_DIM)),
                  const((tr, tr)), st_in],
        out_specs=[pl.BlockSpec((1, th, HG_DIM), lambda b, i: (b, i, 0)), st],
        out_shape=(jax.ShapeDtypeStruct((n, n_tiles * th, HG_DIM), F32),
                   jax.ShapeDtypeStruct((n, HG_HEADS, HG_HEAD_DIM, HG_HEAD_DIM), F32)),
        scratch_shapes=[pltpu.VMEM((HG_HEADS, HG_HEAD_DIM, HG_HEAD_DIM), F32),
                        pltpu.VMEM((th, HG_DIM), F32)],
        compiler_params=_params("arbitrary", "arbitrary"),
        name="hgrn_chunks",
    )(qig, zf, qig, qig, lb, ng, tri, s0)


def _hgrn_step_kernel(q_ref, f_ref, i_ref, g_ref, lb_ref, ng_ref, s0_ref,
                      y_ref, s_out_ref, o_s, *, sb):
    lb = lb_ref[...]
    f = lb + (1.0 - lb) * _sigmoid(f_ref[0])
    q = q_ref[0].astype(F32)
    v = i_ref[0].astype(F32)
    pad = jnp.zeros((HG_HEAD_DIM - sb, HG_HEAD_DIM), F32)
    sq = (HG_HEAD_DIM, HG_HEAD_DIM)
    for h in range(HG_HEADS):
        ls = slice(h * HG_HEAD_DIM, (h + 1) * HG_HEAD_DIM)
        fcols = jnp.concatenate([f[:, ls], pad], axis=0).T
        for s in range(sb):
            fc = jnp.broadcast_to(fcols[:, s:s + 1], sq)
            sn = fc * s0_ref[s, h] + (1.0 - fc) * v[s:s + 1, ls]
            s_out_ref[s, h] = sn
            o_s[s:s + 1, ls] = _dot(q[s:s + 1, ls].astype(BF16), sn.astype(BF16))
    y_ref[0] = _hgrn_gate_out(o_s[...], g_ref[0].astype(F32), ng_ref[...])


def _hgrn_step(zf, qig, lb, ng, s0, sb):
    r = zf.shape[1]
    col = lambda j: pl.BlockSpec((1, sb, HG_DIM), lambda i, j=j: (0, i, j))
    vec = pl.BlockSpec((1, HG_DIM), lambda i: (0, 0))
    st = pl.BlockSpec((sb, HG_HEADS, HG_HEAD_DIM, HG_HEAD_DIM), lambda i: (i, 0, 0, 0))
    return pl.pallas_call(
        functools.partial(_hgrn_step_kernel, sb=sb),
        grid=(r // sb,),
        in_specs=[col(0), col(0), col(1), col(2), vec, vec, st],
        out_specs=[pl.BlockSpec((1, sb, HG_DIM), lambda i: (0, i, 0)), st],
        out_shape=(jax.ShapeDtypeStruct((1, r, HG_DIM), F32),
                   jax.ShapeDtypeStruct(s0.shape, F32)),
        scratch_shapes=[pltpu.VMEM((sb, HG_DIM), F32)],
        compiler_params=_params("arbitrary"),
        name="hgrn_step",
    )(qig, zf, qig, qig, lb, ng, s0)


def _ffn_kernel(y5_ref, yh_ref, h_ref, wglu_ref, bglu_ref, wout_ref, g2_ref,
                wup_ref, wd_ref, cw_ref, cin_a_ref, cin_b_ref, gf_ref,
                out_ref, cout_ref, carry_s, hn_s, h1_s, s_s, *, tm, per_row):
    y5p = y5_ref[0]
    y5 = y5p * _sigmoid(_dot(y5p.astype(BF16), wglu_ref[...]) + bglu_ref[...])
    ymix = jnp.concatenate([y5, yh_ref[0]], axis=-1).astype(BF16)
    h1 = h_ref[0] + _dot(ymix, wout_ref[...])
    h1_s[...] = h1
    hn_s[...] = _rms_norm(h1, g2_ref[...]).astype(BF16)

    if not per_row:
        @pl.when(pl.program_id(1) == 0)
        def _():
            carry_s[...] = cin_a_ref[0]
        rid = lax.broadcasted_iota(jnp.int32, (SUBLANES, 1), 0)

    for j in range(N_FF_CHUNKS):
        cs = slice(j * FF_CHUNK, (j + 1) * FF_CHUNK)
        hn = hn_s[...]
        a = _dot(hn, wup_ref[:, cs])
        v = _dot(hn, wup_ref[:, D_FF + j * FF_CHUNK:D_FF + (j + 1) * FF_CHUNK])
        cw = cw_ref[:, cs]
        if per_row:
            cout_ref[:, cs] = a
            cv = cw[3:4] + cin_a_ref[:, cs] * cw[0:1] + cin_b_ref[:, cs] * cw[1:2] + a * cw[2:3]
        else:
            cv = (cw[3:4] + pltpu.roll(a, 2, 0) * cw[0:1] + pltpu.roll(a, 1, 0) * cw[1:2]
                  + a * cw[2:3])
            last = a[tm - SUBLANES:tm]
            d = carry_s[:, cs] - last
            fix = (jnp.where(rid < 1, pltpu.roll(d, 1, 0), 0.0) * cw[1:2]
                   + jnp.where(rid < 2, pltpu.roll(d, 2, 0), 0.0) * cw[0:1])
            cv = jnp.concatenate([cv[:SUBLANES] + fix, cv[SUBLANES:]], axis=0)
            carry_s[:, cs] = last
        s_s[:, cs] = ((cv * _sigmoid(cv)) * v).astype(BF16)

    out_ref[0] = _rms_norm(h1_s[...] + _dot(s_s[...], wd_ref[...]), gf_ref[...])
    if not per_row:
        cout_ref[0] = carry_s[...]


def _ffn(y5, yh, h, wglu, bglu, wout, g2, wup, wd, cw, cin_a, cin_b, gf, tm, n_tiles, per_row):
    n = h.shape[0]
    const = lambda shape: pl.BlockSpec(shape, lambda b, i: (0,) * len(shape),
                                       pipeline_mode=pl.Buffered(1))
    if per_row:
        cin_spec = const(cin_a.shape)
        cout_spec, cout_shape = pl.BlockSpec(cin_a.shape, lambda b, i: (0, 0)), cin_a.shape
    else:
        own = cin_a.shape[0] == n
        cin_spec = pl.BlockSpec((1, SUBLANES, D_FF), lambda b, i: (b if own else 0, 0, 0))
        cout_spec = pl.BlockSpec((1, SUBLANES, D_FF), lambda b, i: (b, 0, 0))
        cout_shape = (n, SUBLANES, D_FF)
    return pl.pallas_call(
        functools.partial(_ffn_kernel, tm=tm, per_row=per_row),
        grid=(n, n_tiles),
        in_specs=[pl.BlockSpec((1, tm, S5_DIM), lambda b, i: (b, i, 0)),
                  pl.BlockSpec((1, tm, HG_DIM), lambda b, i: (b, i, 0)),
                  pl.BlockSpec((1, tm, D_MODEL), lambda b, i: (b, i, 0)),
                  const(wglu.shape), const(bglu.shape), const(wout.shape), const(g2.shape),
                  const(wup.shape), const(wd.shape), const(cw.shape),
                  cin_spec, cin_spec, const(gf.shape)],
        out_specs=[pl.BlockSpec((1, tm, D_MODEL), lambda b, i: (b, i, 0)), cout_spec],
        out_shape=(jax.ShapeDtypeStruct((n, n_tiles * tm, D_MODEL), F32),
                   jax.ShapeDtypeStruct(cout_shape, F32)),
        scratch_shapes=[pltpu.VMEM((SUBLANES, D_FF), F32),
                        pltpu.VMEM((tm, D_MODEL), BF16),
                        pltpu.VMEM((tm, D_MODEL), F32),
                        pltpu.VMEM((tm, D_FF), BF16)],
        compiler_params=_params("arbitrary", "arbitrary"),
        name="mix_out_ffn",
    )(y5, yh, h, wglu, bglu, wout, g2, wup, wd, cw, cin_a, cin_b, gf)


def kernel(x_prompt, x_sample, state_s5_re, state_s5_im, state_hgrn, state_ffn_conv, meta_tokens, norm_mix_g, w_in, s5_lambda_re, s5_lambda_im, s5_log_dt, s5_b_re, s5_b_im, s5_c_re, s5_c_im, s5_d, s5_w_glu, s5_b_glu, hg_lower_bounds, hg_norm_g, w_out, norm_ffn_g, ffn_w_up, ffn_conv_w, ffn_conv_b, ffn_w_down, final_norm_g):
    nb, seq, _ = x_prompt.shape
    ns = x_sample.shape[0]
    li = 0

    ar, ai, bb_re, bb_im, lb = _prep(s5_lambda_re[li], s5_lambda_im[li], s5_log_dt[li],
                                     s5_b_re[li], s5_b_im[li], hg_lower_bounds)
    ar = ar.reshape(1, S5_LANES)
    ai = ai.reshape(1, S5_LANES)
    halves = lambda t: t.reshape((S5_HALVES, S5_GROUPS // S5_HALVES) + t.shape[1:])
    wbr = _block_diag(halves(bb_re)).astype(BF16)
    wbi = _block_diag(halves(bb_im)).astype(BF16)
    wcr = _block_diag(halves(s5_c_re[li].transpose(0, 2, 1))).astype(BF16)
    wci = _block_diag(halves(-s5_c_im[li].transpose(0, 2, 1))).astype(BF16)
    d5 = s5_d[li].reshape(1, S5_DIM)
    g1 = norm_mix_g[li].reshape(1, D_MODEL)
    g2 = norm_ffn_g[li].reshape(1, D_MODEL)
    gf = final_norm_g.reshape(1, D_MODEL)
    ng = hg_norm_g[li].reshape(1, HG_DIM)
    bglu = s5_b_glu[li].reshape(1, S5_DIM)
    w_in_b = w_in[li].astype(BF16)
    wglu = s5_w_glu[li].astype(BF16)
    wout = w_out[li].astype(BF16)
    wup = ffn_w_up[li].astype(BF16)
    wd = ffn_w_down[li].astype(BF16)
    cw = jnp.concatenate([ffn_conv_w[li], ffn_conv_b[li][None],
                          jnp.zeros((SUBLANES - CONV_W - 1, D_FF), F32)], axis=0)
    s5w = (ar, ai, wbr, wbi, wcr, wci, d5)
    ffw = (wglu, bglu, wout, g2, wup, wd, cw)

    def seq_run(x, s5r0, s5i0, hg0, conv0, tm, n_tiles, tt, s5_tiles, th, hg_tiles, valid):
        n = x.shape[0]
        u_tm, zf, qig = _inproj(x, g1, w_in_b, tm, n_tiles)
        y5, s5r, s5i = _s5(u_tm, s5r0, s5i0, *s5w, n, tt, s5_tiles)
        yh, hg = _hgrn(zf, qig, lb, ng, hg0, th, hg_tiles, valid)
        y, conv = _ffn(y5, yh, x, *ffw, conv0, conv0, gf, tm, n_tiles, False)
        return y, s5r, s5i, hg, conv

    meta = jnp.zeros((1, HG_CHUNK, D_MODEL), F32).at[0, :N_META].set(meta_tokens)
    z5 = jnp.zeros((1, S5_LANES), F32)
    zh = jnp.zeros((1, HG_HEADS, HG_HEAD_DIM, HG_HEAD_DIM), F32)
    zc = jnp.zeros((1, SUBLANES, D_FF), F32)
    u_tm, zf, qig = _inproj(meta, g1, w_in_b, HG_CHUNK, 1)
    y5, m5r, m5i = _s5(u_tm, z5, z5, *s5w, 1, N_META, 1)
    yh, mhg = _hgrn(zf, qig, lb, ng, zh, HG_CHUNK, 1, N_META)
    _, mconv = _ffn(y5, yh, meta, *ffw, zc, zc, gf, N_META, 1, False)

    tm, tt, th = 512, 128, 512
    y_prompt, p5r, p5i, phg, pconv = seq_run(
        x_prompt, m5r, m5i, mhg, mconv,
        tm, seq // tm, tt, seq // tt, th, seq // th, th)

    xs = x_sample.reshape(1, ns, D_MODEL)
    u_tm, zf, qig = _inproj(xs, g1, w_in_b, ns, 1)
    y5, s5r, s5i = _s5(u_tm, state_s5_re[li].reshape(ns, S5_LANES), state_s5_im[li].reshape(ns, S5_LANES),
                       *s5w, ns, 1, 1)
    yh, shg = _hgrn_step(zf, qig, lb, ng, state_hgrn[li], 16)
    buf = state_ffn_conv[li]
    y_sample, a_new = _ffn(y5, yh, xs, *ffw, buf[:, 0], buf[:, 1], gf, ns, 1, True)
    sconv = jnp.stack([buf[:, 1], a_new], axis=1)

    st5 = lambda t: t.reshape(1, -1, S5_GROUPS, S5_STATE)
    pconv = pconv[:, SUBLANES - (CONV_W - 1):, :]
    return (y_prompt, y_sample.reshape(ns, 1, D_MODEL),
            st5(p5r), st5(p5i), phg[None], pconv[None],
            st5(s5r), st5(s5i), shg[None], sconv[None])
```

```python
import functools

import jax
import jax.numpy as jnp
from jax import lax
from jax.experimental import pallas as pl
from jax.experimental.pallas import tpu as pltpu

F32 = jnp.float32
BF16 = jnp.bfloat16

D_MODEL = 1024
N_META = 16
S5_DIM = 512
S5_GROUP = 16
S5_GROUPS = 32
S5_STATE = 64
S5_LANES = S5_GROUPS * S5_STATE
HG_DIM = 512
HG_HEAD_DIM = 128
HG_HEADS = 4
HG_CHUNK = 64
D_FF = 2816
CONV_W = 3
EPS = 1e-6

S5_HALVES = 2
S5_HALF_CH = S5_DIM // S5_HALVES
S5_HALF_ST = S5_LANES // S5_HALVES
SCAN_LANES = 512
FF_CHUNK = 256
N_FF_CHUNKS = D_FF // FF_CHUNK
SUBLANES = 8
LANES = 128
VMEM_LIMIT = 56 * 1024 * 1024


def _sigmoid(x):
    return 1.0 / (1.0 + jnp.exp(-x))


def _rms_norm(x, g):
    ms = jnp.mean(x * x, axis=-1, keepdims=True)
    return x * lax.rsqrt(ms + EPS) * g


def _dot(a, b):
    return jnp.dot(a, b, preferred_element_type=F32)


def _params(*sem):
    return pltpu.CompilerParams(dimension_semantics=sem, vmem_limit_bytes=VMEM_LIMIT)


def _prep_kernel(lam_re_ref, lam_im_ref, dt_ref, bt_re_ref, bt_im_ref, hlb_ref,
                 ar_ref, ai_ref, bb_re_ref, bb_im_ref, lb_ref):
    lam_re = lam_re_ref[...]
    lam_im = lam_im_ref[...]
    dt = jnp.exp(dt_ref[...])
    mag = jnp.exp(lam_re * dt)
    ar = mag * jnp.cos(lam_im * dt)
    ai = mag * jnp.sin(lam_im * dt)
    nr = ar - 1.0
    den = lam_re * lam_re + lam_im * lam_im
    cr = (nr * lam_re + ai * lam_im) / den
    ci = (ai * lam_re - nr * lam_im) / den
    ar_ref[...] = ar
    ai_ref[...] = ai
    bt_re = bt_re_ref[...]
    bt_im = bt_im_ref[...]
    bb_re_ref[...] = cr * bt_re - ci * bt_im
    bb_im_ref[...] = cr * bt_im + ci * bt_re
    hlb = hlb_ref[...]
    e = jnp.exp(hlb - jnp.max(hlb, axis=0, keepdims=True))
    lb_ref[...] = e[0:1] / jnp.sum(e, axis=0, keepdims=True)


def _prep(lam_re, lam_im, log_dt, b_re, b_im, hlb):
    g, p = lam_re.shape
    sds = jax.ShapeDtypeStruct
    return pl.pallas_call(
        _prep_kernel,
        out_shape=(sds((g, 1, p), F32), sds((g, 1, p), F32),
                   sds((g, S5_GROUP, p), F32), sds((g, S5_GROUP, p), F32),
                   sds((1, HG_DIM), F32)),
        name="param_prep",
    )(lam_re.reshape(g, 1, p), lam_im.reshape(g, 1, p),
      jnp.broadcast_to(log_dt.reshape(g, 1, 1), (g, 1, p)),
      b_re.transpose(0, 2, 1), b_im.transpose(0, 2, 1), hlb)


def _block_diag(blocks):
    h, g, r, c = blocks.shape
    eye = jnp.eye(g, dtype=blocks.dtype)
    return jnp.einsum("jgrc,gh->jgrhc", blocks, eye).reshape(h, g * r, g * c)


def _inproj_kernel(x_ref, g_ref, w_ref, u_ref, f_ref, qig_ref, *, n, tm):
    hn = _rms_norm(x_ref[0], g_ref[...])
    z = _dot(hn.astype(BF16), w_ref[...])
    b = pl.program_id(1)
    for l in range(S5_DIM // LANES):
        zl = z[:, l * LANES:(l + 1) * LANES]
        if n == 1:
            u_ref[l] = zl
        else:
            u_ref[l, pl.ds(b, tm, stride=n), :] = zl
    col = lambda j: z[:, S5_DIM + j * HG_DIM:S5_DIM + (j + 1) * HG_DIM]
    f_ref[0] = col(1)
    for j, src in enumerate((0, 2, 3)):
        qig_ref[0, :, j * HG_DIM:(j + 1) * HG_DIM] = col(src).astype(BF16)


def _inproj(x, g, w, tm, n_tiles):
    n, l, d = x.shape
    cols = w.shape[1]
    rows = tm * n_tiles
    const = lambda shape: pl.BlockSpec(shape, lambda i, b: (0,) * len(shape),
                                       pipeline_mode=pl.Buffered(1))
    return pl.pallas_call(
        functools.partial(_inproj_kernel, n=n, tm=tm),
        grid=(n_tiles, n),
        in_specs=[pl.BlockSpec((1, tm, d), lambda i, b: (b, i, 0)),
                  const((1, d)), const((d, cols))],
        out_specs=[pl.BlockSpec((S5_DIM // LANES, tm * n, LANES), lambda i, b: (0, i, 0)),
                   pl.BlockSpec((1, tm, HG_DIM), lambda i, b: (b, i, 0)),
                   pl.BlockSpec((1, tm, 3 * HG_DIM), lambda i, b: (b, i, 0))],
        out_shape=(jax.ShapeDtypeStruct((S5_DIM // LANES, rows * n, LANES), F32),
                   jax.ShapeDtypeStruct((n, rows, HG_DIM), F32),
                   jax.ShapeDtypeStruct((n, rows, 3 * HG_DIM), BF16)),
        compiler_params=_params("arbitrary", "arbitrary"),
        name="inproj",
    )(x, g, w)


def _gelu_tanh(y):
    return 0.5 * y * (1.0 + jnp.tanh(0.7978845608028654 * (y + 0.044715 * (y * y * y))))


def _s5_tile(step, u_ref, h0r_ref, h0i_ref, ar_ref, ai_ref, wbr_ref, wbi_ref, wcr_ref, wci_ref, d_ref,
             y_ref, hr_out_ref, hi_out_ref, xr_s, xi_s, hr_s, hi_s, y_s, *, n, tt, interleaved):
    @pl.when(step == 0)
    def _():
        hr_s[...] = jnp.broadcast_to(h0r_ref[...], hr_s.shape)
        hi_s[...] = jnp.broadcast_to(h0i_ref[...], hi_s.shape)

    slabs = S5_HALF_CH // LANES
    load_u = lambda j: jnp.concatenate([u_ref[j * slabs + l] for l in range(slabs)], axis=1)
    for j in range(S5_HALVES):
        st = slice(j * S5_HALF_ST, (j + 1) * S5_HALF_ST)
        ub = load_u(j).astype(BF16)
        xr_s[:, st] = _dot(ub, wbr_ref[j])
        xi_s[:, st] = _dot(ub, wbi_ref[j])
    for c in range(S5_LANES // SCAN_LANES):
        loc = slice(c * SCAN_LANES, (c + 1) * SCAN_LANES)
        ar = jnp.broadcast_to(ar_ref[:, loc], (n, SCAN_LANES))
        ai = jnp.broadcast_to(ai_ref[:, loc], (n, SCAN_LANES))

        def step(t, carry, ar=ar, ai=ai, loc=loc):
            hr, hi = carry
            r = 0 if tt == 1 else pl.multiple_of(t * n, n)
            nhr = ar * hr - ai * hi + xr_s[pl.ds(r, n), loc]
            nhi = ar * hi + ai * hr + xi_s[pl.ds(r, n), loc]
            xr_s[pl.ds(r, n), loc] = nhr
            xi_s[pl.ds(r, n), loc] = nhi
            return nhr, nhi

        carry = (hr_s[:, loc], hi_s[:, loc])
        if tt == 1:
            carry = step(0, carry)
        else:
            carry = lax.fori_loop(0, tt, step, carry, unroll=True)
        hr_s[:, loc] = carry[0]
        hi_s[:, loc] = carry[1]
    for j in range(S5_HALVES):
        ch = slice(j * S5_HALF_CH, (j + 1) * S5_HALF_CH)
        st = slice(j * S5_HALF_ST, (j + 1) * S5_HALF_ST)
        y = _gelu_tanh(_dot(xr_s[:, st].astype(BF16), wcr_ref[j]) + _dot(xi_s[:, st].astype(BF16), wci_ref[j])
                       + d_ref[:, ch] * load_u(j))
        if interleaved:
            for l in range(slabs):
                y_s[j * slabs + l] = y[:, l * LANES:(l + 1) * LANES]
        else:
            y_ref[0, :, ch] = y
    if interleaved:
        for b in range(n):
            for l in range(S5_DIM // LANES):
                y_ref[b, :, l * LANES:(l + 1) * LANES] = y_s[l, pl.ds(b, tt, stride=n), :]
    hr_out_ref[...] = hr_s[...]
    hi_out_ref[...] = hi_s[...]


def _s5_kernel(*refs, **static):
    _s5_tile(pl.program_id(0), *refs, **static)


def _s5_scratch(n, rows, interleaved):
    return [pltpu.VMEM((rows, S5_LANES), F32), pltpu.VMEM((rows, S5_LANES), F32),
            pltpu.VMEM((n, S5_LANES), F32), pltpu.VMEM((n, S5_LANES), F32),
            pltpu.VMEM((S5_DIM // LANES, rows if interleaved else SUBLANES, LANES), F32)]


def _s5(u4, h0r, h0i, ar, ai, wbr, wbi, wcr, wci, d, n, tt, n_tiles):
    rows = tt * n
    interleaved = n > 1 and tt > 1
    const = lambda shape: pl.BlockSpec(shape, lambda i: (0,) * len(shape),
                                       pipeline_mode=pl.Buffered(1))
    state = pl.BlockSpec((n, S5_LANES), lambda i: (0, 0))
    y_block = (n, tt, S5_DIM) if interleaved else (1, rows, S5_DIM)
    y_shape = (n, tt * n_tiles, S5_DIM) if interleaved else (1, rows * n_tiles, S5_DIM)
    return pl.pallas_call(
        functools.partial(_s5_kernel, n=n, tt=tt, interleaved=interleaved),
        grid=(n_tiles,),
        in_specs=[pl.BlockSpec((S5_DIM // LANES, rows, LANES), lambda i: (0, i, 0)),
                  const(h0r.shape), const(h0i.shape), const(ar.shape), const(ai.shape),
                  const(wbr.shape), const(wbi.shape), const(wcr.shape), const(wci.shape),
                  const((1, S5_DIM))],
        out_specs=[pl.BlockSpec(y_block, lambda i: (0, i, 0)), state, state],
        out_shape=(jax.ShapeDtypeStruct(y_shape, F32),
                   jax.ShapeDtypeStruct((n, S5_LANES), F32),
                   jax.ShapeDtypeStruct((n, S5_LANES), F32)),
        scratch_shapes=_s5_scratch(n, rows, interleaved),
        compiler_params=_params("arbitrary"),
        name="s5_scan",
    )(u4, h0r, h0i, ar, ai, wbr, wbi, wcr, wci, d)


def _hgrn_gate_out(o, g, ng):
    parts = []
    for h in range(HG_HEADS):
        oh = o[:, h * HG_HEAD_DIM:(h + 1) * HG_HEAD_DIM]
        ms = jnp.mean(oh * oh, axis=-1, keepdims=True)
        parts.append(oh * lax.rsqrt(ms + EPS))
    return jnp.concatenate(parts, axis=-1) * ng * (g * _sigmoid(g))


def _hgrn_tile(row, q_ref, f_ref, i_ref, g_ref, lb_ref, ng_ref, tri_ref, y_ref, st_s, o_s, *, th, valid):
    c = HG_CHUNK
    lb = lb_ref[...]
    f = lb + (1.0 - lb) * _sigmoid(f_ref[row])
    lc = jnp.log(f)
    k = 1.0 - f
    q = q_ref[row].astype(F32)
    if valid < th:
        live = lax.broadcasted_iota(jnp.int32, (th, 1), 0) < valid
        lc = jnp.where(live, lc, 0.0)
        k = jnp.where(live, k, 0.0)
        q = jnp.where(live, q, 0.0)
    tri = tri_ref[...]
    tr = tri.shape[0]
    lc_hi = lc.astype(BF16)
    lc_lo = (lc - lc_hi.astype(F32)).astype(BF16)
    b = jnp.concatenate([_dot(tri, lc_hi[r:r + tr]) + _dot(tri, lc_lo[r:r + tr])
                         for r in range(0, th, tr)], axis=0)
    qd = (q * jnp.exp(b)).astype(BF16)
    kd = k * jnp.exp(-b)
    kdb = kd.astype(BF16)
    vb = i_ref[row]
    v = vb.astype(F32)
    causal = (lax.broadcasted_iota(jnp.int32, (c, c), 1) <= lax.broadcasted_iota(jnp.int32, (c, c), 0))
    nt = (((1,), (1,)), ((), ()))
    n_chunks = th // c
    rows = [slice(cc * c, (cc + 1) * c) for cc in range(n_chunks)]
    lanes = [slice(h * HG_HEAD_DIM, (h + 1) * HG_HEAD_DIM) for h in range(HG_HEADS)]
    dec = [jnp.exp(b[cc * c + c - 1:cc * c + c, :]) for cc in range(n_chunks)]
    kdec = [(kd[rows[cc]] * dec[cc]).astype(BF16) for cc in range(n_chunks)]
    att = [[jnp.where(causal, lax.dot_general(qd[rows[cc], ls], kdb[rows[cc], ls], nt,
                                              preferred_element_type=F32), 0.0).astype(BF16)
            for ls in lanes] for cc in range(n_chunks)]
    upd = [[_dot(v[rows[cc], ls].T.astype(BF16), kdec[cc][:, ls])
            for ls in lanes] for cc in range(n_chunks)]
    st_in = [[None] * HG_HEADS for _ in range(n_chunks)]
    for h, ls in enumerate(lanes):
        st = st_s[row, h]
        for cc in range(n_chunks):
            st_in[cc][h] = st.astype(BF16)
            st = dec[cc][:, ls] * st + upd[cc][h]
        st_s[row, h] = st
    for cc in range(n_chunks):
        for h, ls in enumerate(lanes):
            o_s[row, rows[cc], ls] = (
                lax.dot_general(qd[rows[cc], ls], st_in[cc][h], nt, preferred_element_type=F32)
                + _dot(att[cc][h], vb[rows[cc], ls]))
    y_ref[row] = _hgrn_gate_out(o_s[row], g_ref[row].astype(F32), ng_ref[...])


def _hgrn_load_state(s0_ref, st_s):
    for r in range(st_s.shape[0]):
        for h in range(HG_HEADS):
            st_s[r, h] = s0_ref[r if s0_ref.shape[0] > 1 else 0, h].T


def _hgrn_store_state(s_out_ref, st_s):
    for r in range(st_s.shape[0]):
        for h in range(HG_HEADS):
            s_out_ref[r, h] = st_s[r, h].T


def _hgrn_kernel(q_ref, f_ref, i_ref, g_ref, lb_ref, ng_ref, tri_ref, s0_ref,
                 y_ref, s_out_ref, st_s, o_s, *, th, valid):
    i = pl.program_id(1)
    pl.when(i == 0)(lambda: _hgrn_load_state(s0_ref, st_s))
    _hgrn_tile(0, q_ref, f_ref, i_ref, g_ref, lb_ref, ng_ref, tri_ref, y_ref, st_s, o_s,
               th=th, valid=valid)
    pl.when(i == pl.num_programs(1) - 1)(lambda: _hgrn_store_state(s_out_ref, st_s))


def _hgrn_tri(tr):
    idx = jnp.arange(tr)
    return ((idx[:, None] // HG_CHUNK == idx[None, :] // HG_CHUNK)
            & (idx[None, :] <= idx[:, None])).astype(BF16)


def _hgrn(zf, qig, lb, ng, s0, th, n_tiles, valid):
    n = zf.shape[0]
    own = s0.shape[0] == n
    tr = min(th, 256)
    col = lambda j: pl.BlockSpec((1, th, HG_DIM), lambda b, i, j=j: (b, i, j))
    const = lambda shape: pl.BlockSpec(shape, lambda b, i: (0,) * len(shape),
                                       pipeline_mode=pl.Buffered(1))
    st = pl.BlockSpec((1, HG_HEADS, HG_HEAD_DIM, HG_HEAD_DIM), lambda b, i: (b, 0, 0, 0))
    st_in = pl.BlockSpec((1, HG_HEADS, HG_HEAD_DIM, HG_HEAD_DIM),
                         lambda b, i: (b if own else 0, 0, 0, 0))
    return pl.pallas_call(
        functools.partial(_hgrn_kernel, th=th, valid=valid),
        grid=(n, n_tiles),
        in_specs=[col(0), col(0), col(1), col(2), const((1, HG_DIM)), const((1, HG_DIM)),
                  const((tr, tr)), st_in],
        out_specs=[pl.BlockSpec((1, th, HG_DIM), lambda b, i: (b, i, 0)), st],
        out_shape=(jax.ShapeDtypeStruct((n, n_tiles * th, HG_DIM), F32),
                   jax.ShapeDtypeStruct((n, HG_HEADS, HG_HEAD_DIM, HG_HEAD_DIM), F32)),
        scratch_shapes=[pltpu.VMEM((1, HG_HEADS, HG_HEAD_DIM, HG_HEAD_DIM), F32),
                        pltpu.VMEM((1, th, HG_DIM), F32)],
        compiler_params=_params("arbitrary", "arbitrary"),
        name="hgrn_chunks",
    )(qig, zf, qig, qig, lb, ng, _hgrn_tri(tr), s0)


def _hgrn_step_kernel(q_ref, f_ref, i_ref, g_ref, lb_ref, ng_ref, s0_ref,
                      y_ref, s_out_ref, o_s, *, sb):
    lb = lb_ref[...]
    f = lb + (1.0 - lb) * _sigmoid(f_ref[0])
    q = q_ref[0].astype(F32)
    v = i_ref[0].astype(F32)
    pad = jnp.zeros((HG_HEAD_DIM - sb, HG_HEAD_DIM), F32)
    sq = (HG_HEAD_DIM, HG_HEAD_DIM)
    for h in range(HG_HEADS):
        ls = slice(h * HG_HEAD_DIM, (h + 1) * HG_HEAD_DIM)
        fcols = jnp.concatenate([f[:, ls], pad], axis=0).T
        for s in range(sb):
            fc = jnp.broadcast_to(fcols[:, s:s + 1], sq)
            sn = fc * s0_ref[s, h] + (1.0 - fc) * v[s:s + 1, ls]
            s_out_ref[s, h] = sn
            o_s[s:s + 1, ls] = _dot(q[s:s + 1, ls].astype(BF16), sn.astype(BF16))
    y_ref[0] = _hgrn_gate_out(o_s[...], g_ref[0].astype(F32), ng_ref[...])


def _hgrn_step(zf, qig, lb, ng, s0, sb):
    r = zf.shape[1]
    col = lambda j: pl.BlockSpec((1, sb, HG_DIM), lambda i, j=j: (0, i, j))
    vec = pl.BlockSpec((1, HG_DIM), lambda i: (0, 0))
    st = pl.BlockSpec((sb, HG_HEADS, HG_HEAD_DIM, HG_HEAD_DIM), lambda i: (i, 0, 0, 0))
    return pl.pallas_call(
        functools.partial(_hgrn_step_kernel, sb=sb),
        grid=(r // sb,),
        in_specs=[col(0), col(0), col(1), col(2), vec, vec, st],
        out_specs=[pl.BlockSpec((1, sb, HG_DIM), lambda i: (0, i, 0)), st],
        out_shape=(jax.ShapeDtypeStruct((1, r, HG_DIM), F32),
                   jax.ShapeDtypeStruct(s0.shape, F32)),
        scratch_shapes=[pltpu.VMEM((sb, HG_DIM), F32)],
        compiler_params=_params("arbitrary"),
        name="hgrn_step",
    )(qig, zf, qig, qig, lb, ng, s0)


def _ffn_kernel(y5_ref, yh_ref, h_ref, wglu_ref, bglu_ref, wout_ref, g2_ref,
                wup_ref, wd_ref, cw_ref, cin_a_ref, cin_b_ref, gf_ref,
                out_ref, cout_ref, carry_s, hn_s, h1_s, s_s, *, tm, per_row):
    y5p = y5_ref[0]
    y5 = y5p * _sigmoid(_dot(y5p.astype(BF16), wglu_ref[...]) + bglu_ref[...])
    ymix = jnp.concatenate([y5, yh_ref[0]], axis=-1).astype(BF16)
    h1 = h_ref[0] + _dot(ymix, wout_ref[...])
    h1_s[...] = h1
    hn_s[...] = _rms_norm(h1, g2_ref[...]).astype(BF16)

    if not per_row:
        @pl.when(pl.program_id(1) == 0)
        def _():
            carry_s[...] = cin_a_ref[0]
        rid = lax.broadcasted_iota(jnp.int32, (SUBLANES, 1), 0)

    for j in range(N_FF_CHUNKS):
        cs = slice(j * FF_CHUNK, (j + 1) * FF_CHUNK)
        hn = hn_s[...]
        a = _dot(hn, wup_ref[:, cs])
        v = _dot(hn, wup_ref[:, D_FF + j * FF_CHUNK:D_FF + (j + 1) * FF_CHUNK])
        cw = cw_ref[:, cs]
        if per_row:
            cout_ref[:, cs] = a
            cv = cw[3:4] + cin_a_ref[:, cs] * cw[0:1] + cin_b_ref[:, cs] * cw[1:2] + a * cw[2:3]
        else:
            cv = (cw[3:4] + pltpu.roll(a, 2, 0) * cw[0:1] + pltpu.roll(a, 1, 0) * cw[1:2]
                  + a * cw[2:3])
            last = a[tm - SUBLANES:tm]
            d = carry_s[:, cs] - last
            fix = (jnp.where(rid < 1, pltpu.roll(d, 1, 0), 0.0) * cw[1:2]
                   + jnp.where(rid < 2, pltpu.roll(d, 2, 0), 0.0) * cw[0:1])
            cv = jnp.concatenate([cv[:SUBLANES] + fix, cv[SUBLANES:]], axis=0)
            carry_s[:, cs] = last
        s_s[:, cs] = ((cv * _sigmoid(cv)) * v).astype(BF16)

    out_ref[0] = _rms_norm(h1_s[...] + _dot(s_s[...], wd_ref[...]), gf_ref[...])
    if not per_row:
        cout_ref[0] = carry_s[...]


def _ffn(y5, yh, h, wglu, bglu, wout, g2, wup, wd, cw, cin_a, cin_b, gf, tm, n_tiles, per_row):
    n = h.shape[0]
    const = lambda shape: pl.BlockSpec(shape, lambda b, i: (0,) * len(shape),
                                       pipeline_mode=pl.Buffered(1))
    if per_row:
        cin_spec = const(cin_a.shape)
        cout_spec, cout_shape = pl.BlockSpec(cin_a.shape, lambda b, i: (0, 0)), cin_a.shape
    else:
        own = cin_a.shape[0] == n
        cin_spec = pl.BlockSpec((1, SUBLANES, D_FF), lambda b, i: (b if own else 0, 0, 0))
        cout_spec = pl.BlockSpec((1, SUBLANES, D_FF), lambda b, i: (b, 0, 0))
        cout_shape = (n, SUBLANES, D_FF)
    return pl.pallas_call(
        functools.partial(_ffn_kernel, tm=tm, per_row=per_row),
        grid=(n, n_tiles),
        in_specs=[pl.BlockSpec((1, tm, S5_DIM), lambda b, i: (b, i, 0)),
                  pl.BlockSpec((1, tm, HG_DIM), lambda b, i: (b, i, 0)),
                  pl.BlockSpec((1, tm, D_MODEL), lambda b, i: (b, i, 0)),
                  const(wglu.shape), const(bglu.shape), const(wout.shape), const(g2.shape),
                  const(wup.shape), const(wd.shape), const(cw.shape),
                  cin_spec, cin_spec, const(gf.shape)],
        out_specs=[pl.BlockSpec((1, tm, D_MODEL), lambda b, i: (b, i, 0)), cout_spec],
        out_shape=(jax.ShapeDtypeStruct((n, n_tiles * tm, D_MODEL), F32),
                   jax.ShapeDtypeStruct(cout_shape, F32)),
        scratch_shapes=[pltpu.VMEM((SUBLANES, D_FF), F32),
                        pltpu.VMEM((tm, D_MODEL), BF16),
                        pltpu.VMEM((tm, D_MODEL), F32),
                        pltpu.VMEM((tm, D_FF), BF16)],
        compiler_params=_params("arbitrary", "arbitrary"),
        name="mix_out_ffn",
    )(y5, yh, h, wglu, bglu, wout, g2, wup, wd, cw, cin_a, cin_b, gf)


def kernel(x_prompt, x_sample, state_s5_re, state_s5_im, state_hgrn, state_ffn_conv, meta_tokens, norm_mix_g, w_in, s5_lambda_re, s5_lambda_im, s5_log_dt, s5_b_re, s5_b_im, s5_c_re, s5_c_im, s5_d, s5_w_glu, s5_b_glu, hg_lower_bounds, hg_norm_g, w_out, norm_ffn_g, ffn_w_up, ffn_conv_w, ffn_conv_b, ffn_w_down, final_norm_g):
    nb, seq, _ = x_prompt.shape
    ns = x_sample.shape[0]
    li = 0

    ar, ai, bb_re, bb_im, lb = _prep(s5_lambda_re[li], s5_lambda_im[li], s5_log_dt[li],
                                     s5_b_re[li], s5_b_im[li], hg_lower_bounds)
    ar = ar.reshape(1, S5_LANES)
    ai = ai.reshape(1, S5_LANES)
    halves = lambda t: t.reshape((S5_HALVES, S5_GROUPS // S5_HALVES) + t.shape[1:])
    wbr = _block_diag(halves(bb_re)).astype(BF16)
    wbi = _block_diag(halves(bb_im)).astype(BF16)
    wcr = _block_diag(halves(s5_c_re[li].transpose(0, 2, 1))).astype(BF16)
    wci = _block_diag(halves(-s5_c_im[li].transpose(0, 2, 1))).astype(BF16)
    d5 = s5_d[li].reshape(1, S5_DIM)
    g1 = norm_mix_g[li].reshape(1, D_MODEL)
    g2 = norm_ffn_g[li].reshape(1, D_MODEL)
    gf = final_norm_g.reshape(1, D_MODEL)
    ng = hg_norm_g[li].reshape(1, HG_DIM)
    bglu = s5_b_glu[li].reshape(1, S5_DIM)
    w_in_b = w_in[li].astype(BF16)
    wglu = s5_w_glu[li].astype(BF16)
    wout = w_out[li].astype(BF16)
    wup = ffn_w_up[li].astype(BF16)
    wd = ffn_w_down[li].astype(BF16)
    cw = jnp.concatenate([ffn_conv_w[li], ffn_conv_b[li][None],
                          jnp.zeros((SUBLANES - CONV_W - 1, D_FF), F32)], axis=0)
    s5w = (ar, ai, wbr, wbi, wcr, wci, d5)
    ffw = (wglu, bglu, wout, g2, wup, wd, cw)

    def seq_run(x, s5r0, s5i0, hg0, conv0, tm, n_tiles, tt, s5_tiles, th, hg_tiles, valid):
        n = x.shape[0]
        u_tm, zf, qig = _inproj(x, g1, w_in_b, tm, n_tiles)
        y5, s5r, s5i = _s5(u_tm, s5r0, s5i0, *s5w, n, tt, s5_tiles)
        yh, hg = _hgrn(zf, qig, lb, ng, hg0, th, hg_tiles, valid)
        y, conv = _ffn(y5, yh, x, *ffw, conv0, conv0, gf, tm, n_tiles, False)
        return y, s5r, s5i, hg, conv

    meta = jnp.zeros((1, HG_CHUNK, D_MODEL), F32).at[0, :N_META].set(meta_tokens)
    z5 = jnp.zeros((1, S5_LANES), F32)
    zh = jnp.zeros((1, HG_HEADS, HG_HEAD_DIM, HG_HEAD_DIM), F32)
    zc = jnp.zeros((1, SUBLANES, D_FF), F32)
    u_tm, zf, qig = _inproj(meta, g1, w_in_b, HG_CHUNK, 1)
    y5, m5r, m5i = _s5(u_tm, z5, z5, *s5w, 1, N_META, 1)
    yh, mhg = _hgrn(zf, qig, lb, ng, zh, HG_CHUNK, 1, N_META)
    _, mconv = _ffn(y5, yh, meta, *ffw, zc, zc, gf, N_META, 1, False)

    tm, tt, th = 512, 128, 512
    y_prompt, p5r, p5i, phg, pconv = seq_run(
        x_prompt, m5r, m5i, mhg, mconv,
        tm, seq // tm, tt, seq // tt, th, seq // th, th)

    xs = x_sample.reshape(1, ns, D_MODEL)
    u_tm, zf, qig = _inproj(xs, g1, w_in_b, ns, 1)
    y5, s5r, s5i = _s5(u_tm, state_s5_re[li].reshape(ns, S5_LANES), state_s5_im[li].reshape(ns, S5_LANES),
                       *s5w, ns, 1, 1)
    yh, shg = _hgrn_step(zf, qig, lb, ng, state_hgrn[li], 16)
    buf = state_ffn_conv[li]
    y_sample, a_new = _ffn(y5, yh, xs, *ffw, buf[:, 0], buf[:, 1], gf, ns, 1, True)
    sconv = jnp.stack([buf[:, 1], a_new], axis=1)

    st5 = lambda t: t.reshape(1, -1, S5_GROUPS, S5_STATE)
    pconv = pconv[:, SUBLANES - (CONV_W - 1):, :]
    return (y_prompt, y_sample.reshape(ns, 1, D_MODEL),
            st5(p5r), st5(p5i), phg[None], pconv[None],
            st5(s5r), st5(s5i), shg[None], sconv[None])
```

```python
import functools

import jax
import jax.numpy as jnp
from jax import lax
from jax.experimental import pallas as pl
from jax.experimental.pallas import tpu as pltpu

F32 = jnp.float32
BF16 = jnp.bfloat16

D_MODEL = 1024
N_META = 16
S5_DIM = 512
S5_GROUP = 16
S5_GROUPS = 32
S5_STATE = 64
S5_LANES = S5_GROUPS * S5_STATE
HG_DIM = 512
HG_HEAD_DIM = 128
HG_HEADS = 4
HG_CHUNK = 64
D_FF = 2816
CONV_W = 3
EPS = 1e-6

S5_HALVES = 2
S5_HALF_CH = S5_DIM // S5_HALVES
S5_HALF_ST = S5_LANES // S5_HALVES
SCAN_LANES = 512
FF_CHUNK = 256
N_FF_CHUNKS = D_FF // FF_CHUNK
SUBLANES = 8
LANES = 128
VMEM_LIMIT = 56 * 1024 * 1024


def _sigmoid(x):
    return 1.0 / (1.0 + jnp.exp(-x))


def _rms_norm(x, g):
    ms = jnp.mean(x * x, axis=-1, keepdims=True)
    return x * lax.rsqrt(ms + EPS) * g


def _dot(a, b):
    return jnp.dot(a, b, preferred_element_type=F32)


def _params(*sem):
    return pltpu.CompilerParams(dimension_semantics=sem, vmem_limit_bytes=VMEM_LIMIT)


def _prep_kernel(lam_re_ref, lam_im_ref, dt_ref, bt_re_ref, bt_im_ref, hlb_ref,
                 ar_ref, ai_ref, bb_re_ref, bb_im_ref, lb_ref):
    lam_re = lam_re_ref[...]
    lam_im = lam_im_ref[...]
    dt = jnp.exp(dt_ref[...])
    mag = jnp.exp(lam_re * dt)
    ar = mag * jnp.cos(lam_im * dt)
    ai = mag * jnp.sin(lam_im * dt)
    nr = ar - 1.0
    den = lam_re * lam_re + lam_im * lam_im
    cr = (nr * lam_re + ai * lam_im) / den
    ci = (ai * lam_re - nr * lam_im) / den
    ar_ref[...] = ar
    ai_ref[...] = ai
    bt_re = bt_re_ref[...]
    bt_im = bt_im_ref[...]
    bb_re_ref[...] = cr * bt_re - ci * bt_im
    bb_im_ref[...] = cr * bt_im + ci * bt_re
    hlb = hlb_ref[...]
    e = jnp.exp(hlb - jnp.max(hlb, axis=0, keepdims=True))
    lb_ref[...] = e[0:1] / jnp.sum(e, axis=0, keepdims=True)


def _prep(lam_re, lam_im, log_dt, b_re, b_im, hlb):
    g, p = lam_re.shape
    sds = jax.ShapeDtypeStruct
    return pl.pallas_call(
        _prep_kernel,
        out_shape=(sds((g, 1, p), F32), sds((g, 1, p), F32),
                   sds((g, S5_GROUP, p), F32), sds((g, S5_GROUP, p), F32),
                   sds((1, HG_DIM), F32)),
        name="param_prep",
    )(lam_re.reshape(g, 1, p), lam_im.reshape(g, 1, p),
      jnp.broadcast_to(log_dt.reshape(g, 1, 1), (g, 1, p)),
      b_re.transpose(0, 2, 1), b_im.transpose(0, 2, 1), hlb)


def _block_diag(blocks):
    h, g, r, c = blocks.shape
    eye = jnp.eye(g, dtype=blocks.dtype)
    return jnp.einsum("jgrc,gh->jgrhc", blocks, eye).reshape(h, g * r, g * c)


def _inproj_kernel(x_ref, g_ref, w_ref, u_ref, f_ref, qig_ref, *, n, tm):
    hn = _rms_norm(x_ref[0], g_ref[...])
    z = _dot(hn.astype(BF16), w_ref[...])
    b = pl.program_id(1)
    for l in range(S5_DIM // LANES):
        zl = z[:, l * LANES:(l + 1) * LANES]
        if n == 1:
            u_ref[l] = zl
        else:
            u_ref[l, pl.ds(b, tm, stride=n), :] = zl
    col = lambda j: z[:, S5_DIM + j * HG_DIM:S5_DIM + (j + 1) * HG_DIM]
    f_ref[0] = col(1)
    for j, src in enumerate((0, 2, 3)):
        qig_ref[0, :, j * HG_DIM:(j + 1) * HG_DIM] = col(src).astype(BF16)


def _inproj(x, g, w, tm, n_tiles):
    n, l, d = x.shape
    cols = w.shape[1]
    rows = tm * n_tiles
    const = lambda shape: pl.BlockSpec(shape, lambda i, b: (0,) * len(shape),
                                       pipeline_mode=pl.Buffered(1))
    return pl.pallas_call(
        functools.partial(_inproj_kernel, n=n, tm=tm),
        grid=(n_tiles, n),
        in_specs=[pl.BlockSpec((1, tm, d), lambda i, b: (b, i, 0)),
                  const((1, d)), const((d, cols))],
        out_specs=[pl.BlockSpec((S5_DIM // LANES, tm * n, LANES), lambda i, b: (0, i, 0)),
                   pl.BlockSpec((1, tm, HG_DIM), lambda i, b: (b, i, 0)),
                   pl.BlockSpec((1, tm, 3 * HG_DIM), lambda i, b: (b, i, 0))],
        out_shape=(jax.ShapeDtypeStruct((S5_DIM // LANES, rows * n, LANES), F32),
                   jax.ShapeDtypeStruct((n, rows, HG_DIM), F32),
                   jax.ShapeDtypeStruct((n, rows, 3 * HG_DIM), BF16)),
        compiler_params=_params("arbitrary", "arbitrary"),
        name="inproj",
    )(x, g, w)


def _gelu_tanh(y):
    return 0.5 * y * (1.0 + jnp.tanh(0.7978845608028654 * (y + 0.044715 * (y * y * y))))


def _s5_tile(step, u_ref, h0r_ref, h0i_ref, ar_ref, ai_ref, wbr_ref, wbi_ref, wcr_ref, wci_ref, d_ref,
             y_ref, hr_out_ref, hi_out_ref, xr_s, xi_s, hr_s, hi_s, y_s, *, n, tt, interleaved):
    @pl.when(step == 0)
    def _():
        hr_s[...] = jnp.broadcast_to(h0r_ref[...], hr_s.shape)
        hi_s[...] = jnp.broadcast_to(h0i_ref[...], hi_s.shape)

    slabs = S5_HALF_CH // LANES
    load_u = lambda j: jnp.concatenate([u_ref[j * slabs + l] for l in range(slabs)], axis=1)
    for j in range(S5_HALVES):
        st = slice(j * S5_HALF_ST, (j + 1) * S5_HALF_ST)
        ub = load_u(j).astype(BF16)
        xr_s[:, st] = _dot(ub, wbr_ref[j])
        xi_s[:, st] = _dot(ub, wbi_ref[j])
    for c in range(S5_LANES // SCAN_LANES):
        loc = slice(c * SCAN_LANES, (c + 1) * SCAN_LANES)
        ar = jnp.broadcast_to(ar_ref[:, loc], (n, SCAN_LANES))
        ai = jnp.broadcast_to(ai_ref[:, loc], (n, SCAN_LANES))

        def step(t, carry, ar=ar, ai=ai, loc=loc):
            hr, hi = carry
            r = 0 if tt == 1 else pl.multiple_of(t * n, n)
            nhr = ar * hr - ai * hi + xr_s[pl.ds(r, n), loc]
            nhi = ar * hi + ai * hr + xi_s[pl.ds(r, n), loc]
            xr_s[pl.ds(r, n), loc] = nhr
            xi_s[pl.ds(r, n), loc] = nhi
            return nhr, nhi

        carry = (hr_s[:, loc], hi_s[:, loc])
        if tt == 1:
            carry = step(0, carry)
        else:
            carry = lax.fori_loop(0, tt, step, carry, unroll=True)
        hr_s[:, loc] = carry[0]
        hi_s[:, loc] = carry[1]
    for j in range(S5_HALVES):
        ch = slice(j * S5_HALF_CH, (j + 1) * S5_HALF_CH)
        st = slice(j * S5_HALF_ST, (j + 1) * S5_HALF_ST)
        y = (_dot(xr_s[:, st].astype(BF16), wcr_ref[j]) + _dot(xi_s[:, st].astype(BF16), wci_ref[j])
             + d_ref[:, ch] * load_u(j))
        if interleaved:
            for l in range(slabs):
                y_s[j * slabs + l] = y[:, l * LANES:(l + 1) * LANES]
        else:
            y_ref[0, :, ch] = y
    if interleaved:
        for b in range(n):
            for l in range(S5_DIM // LANES):
                y_ref[b, :, l * LANES:(l + 1) * LANES] = y_s[l, pl.ds(b, tt, stride=n), :]
    hr_out_ref[...] = hr_s[...]
    hi_out_ref[...] = hi_s[...]


def _s5_kernel(*refs, **static):
    _s5_tile(pl.program_id(0), *refs, **static)


def _s5_scratch(n, rows, interleaved):
    return [pltpu.VMEM((rows, S5_LANES), F32), pltpu.VMEM((rows, S5_LANES), F32),
            pltpu.VMEM((n, S5_LANES), F32), pltpu.VMEM((n, S5_LANES), F32),
            pltpu.VMEM((S5_DIM // LANES, rows if interleaved else SUBLANES, LANES), F32)]


def _s5(u4, h0r, h0i, ar, ai, wbr, wbi, wcr, wci, d, n, tt, n_tiles):
    rows = tt * n
    interleaved = n > 1 and tt > 1
    const = lambda shape: pl.BlockSpec(shape, lambda i: (0,) * len(shape),
                                       pipeline_mode=pl.Buffered(1))
    state = pl.BlockSpec((n, S5_LANES), lambda i: (0, 0))
    y_block = (n, tt, S5_DIM) if interleaved else (1, rows, S5_DIM)
    y_shape = (n, tt * n_tiles, S5_DIM) if interleaved else (1, rows * n_tiles, S5_DIM)
    return pl.pallas_call(
        functools.partial(_s5_kernel, n=n, tt=tt, interleaved=interleaved),
        grid=(n_tiles,),
        in_specs=[pl.BlockSpec((S5_DIM // LANES, rows, LANES), lambda i: (0, i, 0)),
                  const(h0r.shape), const(h0i.shape), const(ar.shape), const(ai.shape),
                  const(wbr.shape), const(wbi.shape), const(wcr.shape), const(wci.shape),
                  const((1, S5_DIM))],
        out_specs=[pl.BlockSpec(y_block, lambda i: (0, i, 0)), state, state],
        out_shape=(jax.ShapeDtypeStruct(y_shape, F32),
                   jax.ShapeDtypeStruct((n, S5_LANES), F32),
                   jax.ShapeDtypeStruct((n, S5_LANES), F32)),
        scratch_shapes=_s5_scratch(n, rows, interleaved),
        compiler_params=_params("arbitrary"),
        name="s5_scan",
    )(u4, h0r, h0i, ar, ai, wbr, wbi, wcr, wci, d)


def _hgrn_gate_out(o, g, ng):
    parts = []
    for h in range(HG_HEADS):
        oh = o[:, h * HG_HEAD_DIM:(h + 1) * HG_HEAD_DIM]
        ms = jnp.mean(oh * oh, axis=-1, keepdims=True)
        parts.append(oh * lax.rsqrt(ms + EPS))
    return jnp.concatenate(parts, axis=-1) * ng * (g * _sigmoid(g))


def _hgrn_tile(row, q_ref, f_ref, i_ref, g_ref, lb_ref, ng_ref, tri_ref, y_ref, st_s, o_s, *, th, valid):
    c = HG_CHUNK
    lb = lb_ref[...]
    f = lb + (1.0 - lb) * _sigmoid(f_ref[row])
    lc = jnp.log(f)
    k = 1.0 - f
    q = q_ref[row].astype(F32)
    if valid < th:
        live = lax.broadcasted_iota(jnp.int32, (th, 1), 0) < valid
        lc = jnp.where(live, lc, 0.0)
        k = jnp.where(live, k, 0.0)
        q = jnp.where(live, q, 0.0)
    tri = tri_ref[...]
    tr = tri.shape[0]
    lc_hi = lc.astype(BF16)
    lc_lo = (lc - lc_hi.astype(F32)).astype(BF16)
    b = jnp.concatenate([_dot(tri, lc_hi[r:r + tr]) + _dot(tri, lc_lo[r:r + tr])
                         for r in range(0, th, tr)], axis=0)
    qd = (q * jnp.exp(b)).astype(BF16)
    kd = k * jnp.exp(-b)
    kdb = kd.astype(BF16)
    vb = i_ref[row]
    v = vb.astype(F32)
    causal = (lax.broadcasted_iota(jnp.int32, (c, c), 1) <= lax.broadcasted_iota(jnp.int32, (c, c), 0))
    nt = (((1,), (1,)), ((), ()))
    n_chunks = th // c
    rows = [slice(cc * c, (cc + 1) * c) for cc in range(n_chunks)]
    lanes = [slice(h * HG_HEAD_DIM, (h + 1) * HG_HEAD_DIM) for h in range(HG_HEADS)]
    dec = [jnp.exp(b[cc * c + c - 1:cc * c + c, :]) for cc in range(n_chunks)]
    kdec = [(kd[rows[cc]] * dec[cc]).astype(BF16) for cc in range(n_chunks)]
    att = [[jnp.where(causal, lax.dot_general(qd[rows[cc], ls], kdb[rows[cc], ls], nt,
                                              preferred_element_type=F32), 0.0).astype(BF16)
            for ls in lanes] for cc in range(n_chunks)]
    upd = [[_dot(v[rows[cc], ls].T.astype(BF16), kdec[cc][:, ls])
            for ls in lanes] for cc in range(n_chunks)]
    st_in = [[None] * HG_HEADS for _ in range(n_chunks)]
    for h, ls in enumerate(lanes):
        st = st_s[row, h]
        for cc in range(n_chunks):
            st_in[cc][h] = st.astype(BF16)
            st = dec[cc][:, ls] * st + upd[cc][h]
        st_s[row, h] = st
    for cc in range(n_chunks):
        for h, ls in enumerate(lanes):
            o_s[row, rows[cc], ls] = (
                lax.dot_general(qd[rows[cc], ls], st_in[cc][h], nt, preferred_element_type=F32)
                + _dot(att[cc][h], vb[rows[cc], ls]))
    y_ref[row] = _hgrn_gate_out(o_s[row], g_ref[row].astype(F32), ng_ref[...])


def _hgrn_load_state(s0_ref, st_s):
    for r in range(st_s.shape[0]):
        for h in range(HG_HEADS):
            st_s[r, h] = s0_ref[r if s0_ref.shape[0] > 1 else 0, h].T


def _hgrn_store_state(s_out_ref, st_s):
    for r in range(st_s.shape[0]):
        for h in range(HG_HEADS):
            s_out_ref[r, h] = st_s[r, h].T


def _hgrn_kernel(q_ref, f_ref, i_ref, g_ref, lb_ref, ng_ref, tri_ref, s0_ref,
                 y_ref, s_out_ref, st_s, o_s, *, th, valid):
    i = pl.program_id(1)
    pl.when(i == 0)(lambda: _hgrn_load_state(s0_ref, st_s))
    _hgrn_tile(0, q_ref, f_ref, i_ref, g_ref, lb_ref, ng_ref, tri_ref, y_ref, st_s, o_s,
               th=th, valid=valid)
    pl.when(i == pl.num_programs(1) - 1)(lambda: _hgrn_store_state(s_out_ref, st_s))


def _hgrn_tri(tr):
    idx = jnp.arange(tr)
    return ((idx[:, None] // HG_CHUNK == idx[None, :] // HG_CHUNK)
            & (idx[None, :] <= idx[:, None])).astype(BF16)


def _hgrn(zf, qig, lb, ng, s0, th, n_tiles, valid):
    n = zf.shape[0]
    own = s0.shape[0] == n
    tr = min(th, 256)
    col = lambda j: pl.BlockSpec((1, th, HG_DIM), lambda b, i, j=j: (b, i, j))
    const = lambda shape: pl.BlockSpec(shape, lambda b, i: (0,) * len(shape),
                                       pipeline_mode=pl.Buffered(1))
    st = pl.BlockSpec((1, HG_HEADS, HG_HEAD_DIM, HG_HEAD_DIM), lambda b, i: (b, 0, 0, 0))
    st_in = pl.BlockSpec((1, HG_HEADS, HG_HEAD_DIM, HG_HEAD_DIM),
                         lambda b, i: (b if own else 0, 0, 0, 0))
    return pl.pallas_call(
        functools.partial(_hgrn_kernel, th=th, valid=valid),
        grid=(n, n_tiles),
        in_specs=[col(0), col(0), col(1), col(2), const((1, HG_DIM)), const((1, HG_DIM)),
                  const((tr, tr)), st_in],
        out_specs=[pl.BlockSpec((1, th, HG_DIM), lambda b, i: (b, i, 0)), st],
        out_shape=(jax.ShapeDtypeStruct((n, n_tiles * th, HG_DIM), F32),
                   jax.ShapeDtypeStruct((n, HG_HEADS, HG_HEAD_DIM, HG_HEAD_DIM), F32)),
        scratch_shapes=[pltpu.VMEM((1, HG_HEADS, HG_HEAD_DIM, HG_HEAD_DIM), F32),
                        pltpu.VMEM((1, th, HG_DIM), F32)],
        compiler_params=_params("arbitrary", "arbitrary"),
        name="hgrn_chunks",
    )(qig, zf, qig, qig, lb, ng, _hgrn_tri(tr), s0)


def _hgrn_step_kernel(q_ref, f_ref, i_ref, g_ref, lb_ref, ng_ref, s0_ref,
                      y_ref, s_out_ref, o_s, *, sb):
    lb = lb_ref[...]
    f = lb + (1.0 - lb) * _sigmoid(f_ref[0])
    q = q_ref[0].astype(F32)
    v = i_ref[0].astype(F32)
    pad = jnp.zeros((HG_HEAD_DIM - sb, HG_HEAD_DIM), F32)
    sq = (HG_HEAD_DIM, HG_HEAD_DIM)
    for h in range(HG_HEADS):
        ls = slice(h * HG_HEAD_DIM, (h + 1) * HG_HEAD_DIM)
        fcols = jnp.concatenate([f[:, ls], pad], axis=0).T
        for s in range(sb):
            fc = jnp.broadcast_to(fcols[:, s:s + 1], sq)
            sn = fc * s0_ref[s, h] + (1.0 - fc) * v[s:s + 1, ls]
            s_out_ref[s, h] = sn
            o_s[s:s + 1, ls] = _dot(q[s:s + 1, ls].astype(BF16), sn.astype(BF16))
    y_ref[0] = _hgrn_gate_out(o_s[...], g_ref[0].astype(F32), ng_ref[...])


def _hgrn_step(zf, qig, lb, ng, s0, sb):
    r = zf.shape[1]
    col = lambda j: pl.BlockSpec((1, sb, HG_DIM), lambda i, j=j: (0, i, j))
    vec = pl.BlockSpec((1, HG_DIM), lambda i: (0, 0))
    st = pl.BlockSpec((sb, HG_HEADS, HG_HEAD_DIM, HG_HEAD_DIM), lambda i: (i, 0, 0, 0))
    return pl.pallas_call(
        functools.partial(_hgrn_step_kernel, sb=sb),
        grid=(r // sb,),
        in_specs=[col(0), col(0), col(1), col(2), vec, vec, st],
        out_specs=[pl.BlockSpec((1, sb, HG_DIM), lambda i: (0, i, 0)), st],
        out_shape=(jax.ShapeDtypeStruct((1, r, HG_DIM), F32),
                   jax.ShapeDtypeStruct(s0.shape, F32)),
        scratch_shapes=[pltpu.VMEM((sb, HG_DIM), F32)],
        compiler_params=_params("arbitrary"),
        name="hgrn_step",
    )(qig, zf, qig, qig, lb, ng, s0)


def _ffn_kernel(y5_ref, yh_ref, h_ref, wglu_ref, bglu_ref, wout_ref, g2_ref,
                wup_ref, wd_ref, cw_ref, cin_a_ref, cin_b_ref, gf_ref,
                out_ref, cout_ref, carry_s, hn_s, h1_s, s_s, *, tm, per_row):
    y5p = _gelu_tanh(y5_ref[0])
    y5 = y5p * _sigmoid(_dot(y5p.astype(BF16), wglu_ref[...]) + bglu_ref[...])
    ymix = jnp.concatenate([y5, yh_ref[0]], axis=-1).astype(BF16)
    h1 = h_ref[0] + _dot(ymix, wout_ref[...])
    h1_s[...] = h1
    hn_s[...] = _rms_norm(h1, g2_ref[...]).astype(BF16)

    if not per_row:
        @pl.when(pl.program_id(1) == 0)
        def _():
            carry_s[...] = cin_a_ref[0]
        rid = lax.broadcasted_iota(jnp.int32, (SUBLANES, 1), 0)

    for j in range(N_FF_CHUNKS):
        cs = slice(j * FF_CHUNK, (j + 1) * FF_CHUNK)
        hn = hn_s[...]
        a = _dot(hn, wup_ref[:, cs])
        v = _dot(hn, wup_ref[:, D_FF + j * FF_CHUNK:D_FF + (j + 1) * FF_CHUNK])
        cw = cw_ref[:, cs]
        if per_row:
            cout_ref[:, cs] = a
            cv = cw[3:4] + cin_a_ref[:, cs] * cw[0:1] + cin_b_ref[:, cs] * cw[1:2] + a * cw[2:3]
        else:
            cv = (cw[3:4] + pltpu.roll(a, 2, 0) * cw[0:1] + pltpu.roll(a, 1, 0) * cw[1:2]
                  + a * cw[2:3])
            last = a[tm - SUBLANES:tm]
            d = carry_s[:, cs] - last
            fix = (jnp.where(rid < 1, pltpu.roll(d, 1, 0), 0.0) * cw[1:2]
                   + jnp.where(rid < 2, pltpu.roll(d, 2, 0), 0.0) * cw[0:1])
            cv = jnp.concatenate([cv[:SUBLANES] + fix, cv[SUBLANES:]], axis=0)
            carry_s[:, cs] = last
        s_s[:, cs] = ((cv * _sigmoid(cv)) * v).astype(BF16)

    out_ref[0] = _rms_norm(h1_s[...] + _dot(s_s[...], wd_ref[...]), gf_ref[...])
    if not per_row:
        cout_ref[0] = carry_s[...]


def _ffn(y5, yh, h, wglu, bglu, wout, g2, wup, wd, cw, cin_a, cin_b, gf, tm, n_tiles, per_row):
    n = h.shape[0]
    const = lambda shape: pl.BlockSpec(shape, lambda b, i: (0,) * len(shape),
                                       pipeline_mode=pl.Buffered(1))
    if per_row:
        cin_spec = const(cin_a.shape)
        cout_spec, cout_shape = pl.BlockSpec(cin_a.shape, lambda b, i: (0, 0)), cin_a.shape
    else:
        own = cin_a.shape[0] == n
        cin_spec = pl.BlockSpec((1, SUBLANES, D_FF), lambda b, i: (b if own else 0, 0, 0))
        cout_spec = pl.BlockSpec((1, SUBLANES, D_FF), lambda b, i: (b, 0, 0))
        cout_shape = (n, SUBLANES, D_FF)
    return pl.pallas_call(
        functools.partial(_ffn_kernel, tm=tm, per_row=per_row),
        grid=(n, n_tiles),
        in_specs=[pl.BlockSpec((1, tm, S5_DIM), lambda b, i: (b, i, 0)),
                  pl.BlockSpec((1, tm, HG_DIM), lambda b, i: (b, i, 0)),
                  pl.BlockSpec((1, tm, D_MODEL), lambda b, i: (b, i, 0)),
                  const(wglu.shape), const(bglu.shape), const(wout.shape), const(g2.shape),
                  const(wup.shape), const(wd.shape), const(cw.shape),
                  cin_spec, cin_spec, const(gf.shape)],
        out_specs=[pl.BlockSpec((1, tm, D_MODEL), lambda b, i: (b, i, 0)), cout_spec],
        out_shape=(jax.ShapeDtypeStruct((n, n_tiles * tm, D_MODEL), F32),
                   jax.ShapeDtypeStruct(cout_shape, F32)),
        scratch_shapes=[pltpu.VMEM((SUBLANES, D_FF), F32),
                        pltpu.VMEM((tm, D_MODEL), BF16),
                        pltpu.VMEM((tm, D_MODEL), F32),
                        pltpu.VMEM((tm, D_FF), BF16)],
        compiler_params=_params("arbitrary", "arbitrary"),
        name="mix_out_ffn",
    )(y5, yh, h, wglu, bglu, wout, g2, wup, wd, cw, cin_a, cin_b, gf)


def kernel(x_prompt, x_sample, state_s5_re, state_s5_im, state_hgrn, state_ffn_conv, meta_tokens, norm_mix_g, w_in, s5_lambda_re, s5_lambda_im, s5_log_dt, s5_b_re, s5_b_im, s5_c_re, s5_c_im, s5_d, s5_w_glu, s5_b_glu, hg_lower_bounds, hg_norm_g, w_out, norm_ffn_g, ffn_w_up, ffn_conv_w, ffn_conv_b, ffn_w_down, final_norm_g):
    nb, seq, _ = x_prompt.shape
    ns = x_sample.shape[0]
    li = 0

    ar, ai, bb_re, bb_im, lb = _prep(s5_lambda_re[li], s5_lambda_im[li], s5_log_dt[li],
                                     s5_b_re[li], s5_b_im[li], hg_lower_bounds)
    ar = ar.reshape(1, S5_LANES)
    ai = ai.reshape(1, S5_LANES)
    halves = lambda t: t.reshape((S5_HALVES, S5_GROUPS // S5_HALVES) + t.shape[1:])
    wbr = _block_diag(halves(bb_re)).astype(BF16)
    wbi = _block_diag(halves(bb_im)).astype(BF16)
    wcr = _block_diag(halves(s5_c_re[li].transpose(0, 2, 1))).astype(BF16)
    wci = _block_diag(halves(-s5_c_im[li].transpose(0, 2, 1))).astype(BF16)
    d5 = s5_d[li].reshape(1, S5_DIM)
    g1 = norm_mix_g[li].reshape(1, D_MODEL)
    g2 = norm_ffn_g[li].reshape(1, D_MODEL)
    gf = final_norm_g.reshape(1, D_MODEL)
    ng = hg_norm_g[li].reshape(1, HG_DIM)
    bglu = s5_b_glu[li].reshape(1, S5_DIM)
    w_in_b = w_in[li].astype(BF16)
    wglu = s5_w_glu[li].astype(BF16)
    wout = w_out[li].astype(BF16)
    wup = ffn_w_up[li].astype(BF16)
    wd = ffn_w_down[li].astype(BF16)
    cw = jnp.concatenate([ffn_conv_w[li], ffn_conv_b[li][None],
                          jnp.zeros((SUBLANES - CONV_W - 1, D_FF), F32)], axis=0)
    s5w = (ar, ai, wbr, wbi, wcr, wci, d5)
    ffw = (wglu, bglu, wout, g2, wup, wd, cw)

    def seq_run(x, s5r0, s5i0, hg0, conv0, ti, in_tiles, tm, n_tiles, tt, s5_tiles, th, hg_tiles, valid):
        n = x.shape[0]
        u_tm, zf, qig = _inproj(x, g1, w_in_b, ti, in_tiles)
        y5, s5r, s5i = _s5(u_tm, s5r0, s5i0, *s5w, n, tt, s5_tiles)
        yh, hg = _hgrn(zf, qig, lb, ng, hg0, th, hg_tiles, valid)
        y, conv = _ffn(y5, yh, x, *ffw, conv0, conv0, gf, tm, n_tiles, False)
        return y, s5r, s5i, hg, conv

    meta = jnp.zeros((1, HG_CHUNK, D_MODEL), F32).at[0, :N_META].set(meta_tokens)
    z5 = jnp.zeros((1, S5_LANES), F32)
    zh = jnp.zeros((1, HG_HEADS, HG_HEAD_DIM, HG_HEAD_DIM), F32)
    zc = jnp.zeros((1, SUBLANES, D_FF), F32)
    u_tm, zf, qig = _inproj(meta, g1, w_in_b, HG_CHUNK, 1)
    y5, m5r, m5i = _s5(u_tm, z5, z5, *s5w, 1, N_META, 1)
    yh, mhg = _hgrn(zf, qig, lb, ng, zh, HG_CHUNK, 1, N_META)
    _, mconv = _ffn(y5, yh, meta, *ffw, zc, zc, gf, N_META, 1, False)

    ti, tm, tt, th = 512, 512, 128, 1024
    y_prompt, p5r, p5i, phg, pconv = seq_run(
        x_prompt, m5r, m5i, mhg, mconv,
        ti, seq // ti, tm, seq // tm, tt, seq // tt, th, seq // th, th)

    xs = x_sample.reshape(1, ns, D_MODEL)
    u_tm, zf, qig = _inproj(xs, g1, w_in_b, ns, 1)
    y5, s5r, s5i = _s5(u_tm, state_s5_re[li].reshape(ns, S5_LANES), state_s5_im[li].reshape(ns, S5_LANES),
                       *s5w, ns, 1, 1)
    yh, shg = _hgrn_step(zf, qig, lb, ng, state_hgrn[li], 16)
    buf = state_ffn_conv[li]
    y_sample, a_new = _ffn(y5, yh, xs, *ffw, buf[:, 0], buf[:, 1], gf, ns, 1, True)
    sconv = jnp.stack([buf[:, 1], a_new], axis=1)

    st5 = lambda t: t.reshape(1, -1, S5_GROUPS, S5_STATE)
    pconv = pconv[:, SUBLANES - (CONV_W - 1):, :]
    return (y_prompt, y_sample.reshape(ns, 1, D_MODEL),
            st5(p5r), st5(p5i), phg[None], pconv[None],
            st5(s5r), st5(s5i), shg[None], sconv[None])
```

```python
import functools

import jax
import jax.numpy as jnp
from jax import lax
from jax.experimental import pallas as pl
from jax.experimental.pallas import tpu as pltpu

F32 = jnp.float32
BF16 = jnp.bfloat16

D_MODEL = 1024
N_META = 16
S5_DIM = 512
S5_GROUP = 16
S5_GROUPS = 32
S5_STATE = 64
S5_LANES = S5_GROUPS * S5_STATE
HG_DIM = 512
HG_HEAD_DIM = 128
HG_HEADS = 4
HG_CHUNK = 64
D_FF = 2816
CONV_W = 3
EPS = 1e-6

S5_HALVES = 2
S5_HALF_CH = S5_DIM // S5_HALVES
S5_HALF_ST = S5_LANES // S5_HALVES
SCAN_LANES = 512
FF_CHUNK = 256
N_FF_CHUNKS = D_FF // FF_CHUNK
SUBLANES = 8
LANES = 128
VMEM_LIMIT = 56 * 1024 * 1024


def _sigmoid(x):
    return 1.0 / (1.0 + jnp.exp(-x))


def _rms_norm(x, g):
    ms = jnp.mean(x * x, axis=-1, keepdims=True)
    return x * lax.rsqrt(ms + EPS) * g


def _dot(a, b):
    return jnp.dot(a, b, preferred_element_type=F32)


def _params(*sem):
    return pltpu.CompilerParams(dimension_semantics=sem, vmem_limit_bytes=VMEM_LIMIT)


def _prep_kernel(lam_re_ref, lam_im_ref, dt_ref, bt_re_ref, bt_im_ref, hlb_ref,
                 ar_ref, ai_ref, bb_re_ref, bb_im_ref, lb_ref):
    lam_re = lam_re_ref[...]
    lam_im = lam_im_ref[...]
    dt = jnp.exp(dt_ref[...])
    mag = jnp.exp(lam_re * dt)
    ar = mag * jnp.cos(lam_im * dt)
    ai = mag * jnp.sin(lam_im * dt)
    nr = ar - 1.0
    den = lam_re * lam_re + lam_im * lam_im
    cr = (nr * lam_re + ai * lam_im) / den
    ci = (ai * lam_re - nr * lam_im) / den
    ar_ref[...] = ar
    ai_ref[...] = ai
    bt_re = bt_re_ref[...]
    bt_im = bt_im_ref[...]
    bb_re_ref[...] = cr * bt_re - ci * bt_im
    bb_im_ref[...] = cr * bt_im + ci * bt_re
    hlb = hlb_ref[...]
    e = jnp.exp(hlb - jnp.max(hlb, axis=0, keepdims=True))
    lb_ref[...] = e[0:1] / jnp.sum(e, axis=0, keepdims=True)


def _prep(lam_re, lam_im, log_dt, b_re, b_im, hlb):
    g, p = lam_re.shape
    sds = jax.ShapeDtypeStruct
    return pl.pallas_call(
        _prep_kernel,
        out_shape=(sds((g, 1, p), F32), sds((g, 1, p), F32),
                   sds((g, S5_GROUP, p), F32), sds((g, S5_GROUP, p), F32),
                   sds((1, HG_DIM), F32)),
        name="param_prep",
    )(lam_re.reshape(g, 1, p), lam_im.reshape(g, 1, p),
      jnp.broadcast_to(log_dt.reshape(g, 1, 1), (g, 1, p)),
      b_re.transpose(0, 2, 1), b_im.transpose(0, 2, 1), hlb)


def _block_diag(blocks):
    h, g, r, c = blocks.shape
    eye = jnp.eye(g, dtype=blocks.dtype)
    return jnp.einsum("jgrc,gh->jgrhc", blocks, eye).reshape(h, g * r, g * c)


def _inproj_kernel(x_ref, g_ref, w_ref, u_ref, f_ref, qig_ref, *, n, tm):
    hn = _rms_norm(x_ref[0], g_ref[...])
    z = _dot(hn.astype(BF16), w_ref[...])
    b = pl.program_id(1)
    for l in range(S5_DIM // LANES):
        zl = z[:, l * LANES:(l + 1) * LANES]
        if n == 1:
            u_ref[l] = zl
        else:
            u_ref[l, pl.ds(b, tm, stride=n), :] = zl
    col = lambda j: z[:, S5_DIM + j * HG_DIM:S5_DIM + (j + 1) * HG_DIM]
    f_ref[0] = col(1)
    for j, src in enumerate((0, 2, 3)):
        qig_ref[0, :, j * HG_DIM:(j + 1) * HG_DIM] = col(src).astype(BF16)


def _inproj(x, g, w, tm, n_tiles):
    n, l, d = x.shape
    cols = w.shape[1]
    rows = tm * n_tiles
    const = lambda shape: pl.BlockSpec(shape, lambda i, b: (0,) * len(shape),
                                       pipeline_mode=pl.Buffered(1))
    return pl.pallas_call(
        functools.partial(_inproj_kernel, n=n, tm=tm),
        grid=(n_tiles, n),
        in_specs=[pl.BlockSpec((1, tm, d), lambda i, b: (b, i, 0)),
                  const((1, d)), const((d, cols))],
        out_specs=[pl.BlockSpec((S5_DIM // LANES, tm * n, LANES), lambda i, b: (0, i, 0)),
                   pl.BlockSpec((1, tm, HG_DIM), lambda i, b: (b, i, 0)),
                   pl.BlockSpec((1, tm, 3 * HG_DIM), lambda i, b: (b, i, 0))],
        out_shape=(jax.ShapeDtypeStruct((S5_DIM // LANES, rows * n, LANES), F32),
                   jax.ShapeDtypeStruct((n, rows, HG_DIM), F32),
                   jax.ShapeDtypeStruct((n, rows, 3 * HG_DIM), BF16)),
        compiler_params=_params("arbitrary", "arbitrary"),
        name="inproj",
    )(x, g, w)


def _gelu_tanh(y):
    return 0.5 * y * (1.0 + jnp.tanh(0.7978845608028654 * (y + 0.044715 * (y * y * y))))


def _s5_tile(step, u_ref, h0r_ref, h0i_ref, ar_ref, ai_ref, wbr_ref, wbi_ref, wcr_ref, wci_ref, d_ref,
             y_ref, hr_out_ref, hi_out_ref, xr_s, xi_s, hr_s, hi_s, y_s, *, n, tt, interleaved):
    @pl.when(step == 0)
    def _():
        hr_s[...] = jnp.broadcast_to(h0r_ref[...], hr_s.shape)
        hi_s[...] = jnp.broadcast_to(h0i_ref[...], hi_s.shape)

    slabs = S5_HALF_CH // LANES
    load_u = lambda j: jnp.concatenate([u_ref[j * slabs + l] for l in range(slabs)], axis=1)
    for j in range(S5_HALVES):
        st = slice(j * S5_HALF_ST, (j + 1) * S5_HALF_ST)
        ub = load_u(j).astype(BF16)
        xr_s[:, st] = _dot(ub, wbr_ref[j])
        xi_s[:, st] = _dot(ub, wbi_ref[j])
    for c in range(S5_LANES // SCAN_LANES):
        loc = slice(c * SCAN_LANES, (c + 1) * SCAN_LANES)
        ar = jnp.broadcast_to(ar_ref[:, loc], (n, SCAN_LANES))
        ai = jnp.broadcast_to(ai_ref[:, loc], (n, SCAN_LANES))

        def step(t, carry, ar=ar, ai=ai, loc=loc):
            hr, hi = carry
            r = 0 if tt == 1 else pl.multiple_of(t * n, n)
            nhr = ar * hr - ai * hi + xr_s[pl.ds(r, n), loc]
            nhi = ar * hi + ai * hr + xi_s[pl.ds(r, n), loc]
            xr_s[pl.ds(r, n), loc] = nhr
            xi_s[pl.ds(r, n), loc] = nhi
            return nhr, nhi

        carry = (hr_s[:, loc], hi_s[:, loc])
        if tt == 1:
            carry = step(0, carry)
        else:
            carry = lax.fori_loop(0, tt, step, carry, unroll=True)
        hr_s[:, loc] = carry[0]
        hi_s[:, loc] = carry[1]
    for j in range(S5_HALVES):
        ch = slice(j * S5_HALF_CH, (j + 1) * S5_HALF_CH)
        st = slice(j * S5_HALF_ST, (j + 1) * S5_HALF_ST)
        y = (_dot(xr_s[:, st].astype(BF16), wcr_ref[j]) + _dot(xi_s[:, st].astype(BF16), wci_ref[j])
             + d_ref[:, ch] * load_u(j))
        if interleaved:
            for l in range(slabs):
                y_s[j * slabs + l] = y[:, l * LANES:(l + 1) * LANES]
        else:
            y_ref[0, :, ch] = y
    if interleaved:
        for b in range(n):
            for l in range(S5_DIM // LANES):
                y_ref[b, :, l * LANES:(l + 1) * LANES] = y_s[l, pl.ds(b, tt, stride=n), :]
    hr_out_ref[...] = hr_s[...]
    hi_out_ref[...] = hi_s[...]


def _s5_kernel(*refs, **static):
    _s5_tile(pl.program_id(0), *refs, **static)


def _s5_scratch(n, rows, interleaved):
    return [pltpu.VMEM((rows, S5_LANES), F32), pltpu.VMEM((rows, S5_LANES), F32),
            pltpu.VMEM((n, S5_LANES), F32), pltpu.VMEM((n, S5_LANES), F32),
            pltpu.VMEM((S5_DIM // LANES, rows if interleaved else SUBLANES, LANES), F32)]


def _s5(u4, h0r, h0i, ar, ai, wbr, wbi, wcr, wci, d, n, tt, n_tiles):
    rows = tt * n
    interleaved = n > 1 and tt > 1
    const = lambda shape: pl.BlockSpec(shape, lambda i: (0,) * len(shape),
                                       pipeline_mode=pl.Buffered(1))
    state = pl.BlockSpec((n, S5_LANES), lambda i: (0, 0))
    y_block = (n, tt, S5_DIM) if interleaved else (1, rows, S5_DIM)
    y_shape = (n, tt * n_tiles, S5_DIM) if interleaved else (1, rows * n_tiles, S5_DIM)
    return pl.pallas_call(
        functools.partial(_s5_kernel, n=n, tt=tt, interleaved=interleaved),
        grid=(n_tiles,),
        in_specs=[pl.BlockSpec((S5_DIM // LANES, rows, LANES), lambda i: (0, i, 0)),
                  const(h0r.shape), const(h0i.shape), const(ar.shape), const(ai.shape),
                  const(wbr.shape), const(wbi.shape), const(wcr.shape), const(wci.shape),
                  const((1, S5_DIM))],
        out_specs=[pl.BlockSpec(y_block, lambda i: (0, i, 0)), state, state],
        out_shape=(jax.ShapeDtypeStruct(y_shape, F32),
                   jax.ShapeDtypeStruct((n, S5_LANES), F32),
                   jax.ShapeDtypeStruct((n, S5_LANES), F32)),
        scratch_shapes=_s5_scratch(n, rows, interleaved),
        compiler_params=_params("arbitrary"),
        name="s5_scan",
    )(u4, h0r, h0i, ar, ai, wbr, wbi, wcr, wci, d)


def _hgrn_gate_out(o, g, ng):
    parts = []
    for h in range(HG_HEADS):
        oh = o[:, h * HG_HEAD_DIM:(h + 1) * HG_HEAD_DIM]
        ms = jnp.mean(oh * oh, axis=-1, keepdims=True)
        parts.append(oh * lax.rsqrt(ms + EPS))
    return jnp.concatenate(parts, axis=-1) * ng * (g * _sigmoid(g))


def _hgrn_tile(row, q_ref, f_ref, i_ref, g_ref, lb_ref, ng_ref, tri_ref, y_ref, st_s, o_s, *, th, valid):
    c = HG_CHUNK
    lb = lb_ref[...]
    f = lb + (1.0 - lb) * _sigmoid(f_ref[row])
    lc = jnp.log(f)
    k = 1.0 - f
    q = q_ref[row].astype(F32)
    if valid < th:
        live = lax.broadcasted_iota(jnp.int32, (th, 1), 0) < valid
        lc = jnp.where(live, lc, 0.0)
        k = jnp.where(live, k, 0.0)
        q = jnp.where(live, q, 0.0)
    tri = tri_ref[...]
    tr = tri.shape[0]
    lc_hi = lc.astype(BF16)
    lc_lo = (lc - lc_hi.astype(F32)).astype(BF16)
    b = jnp.concatenate([_dot(tri, lc_hi[r:r + tr]) + _dot(tri, lc_lo[r:r + tr])
                         for r in range(0, th, tr)], axis=0)
    qd = (q * jnp.exp(b)).astype(BF16)
    kd = k * jnp.exp(-b)
    kdb = kd.astype(BF16)
    vb = i_ref[row]
    v = vb.astype(F32)
    causal = (lax.broadcasted_iota(jnp.int32, (c, c), 1) <= lax.broadcasted_iota(jnp.int32, (c, c), 0))
    nt = (((1,), (1,)), ((), ()))
    n_chunks = th // c
    rows = [slice(cc * c, (cc + 1) * c) for cc in range(n_chunks)]
    lanes = [slice(h * HG_HEAD_DIM, (h + 1) * HG_HEAD_DIM) for h in range(HG_HEADS)]
    dec = [jnp.exp(b[cc * c + c - 1:cc * c + c, :]) for cc in range(n_chunks)]
    kdec = [(kd[rows[cc]] * dec[cc]).astype(BF16) for cc in range(n_chunks)]
    att = [[jnp.where(causal, lax.dot_general(qd[rows[cc], ls], kdb[rows[cc], ls], nt,
                                              preferred_element_type=F32), 0.0).astype(BF16)
            for ls in lanes] for cc in range(n_chunks)]
    upd = [[_dot(v[rows[cc], ls].T.astype(BF16), kdec[cc][:, ls])
            for ls in lanes] for cc in range(n_chunks)]
    st_in = [[None] * HG_HEADS for _ in range(n_chunks)]
    for h, ls in enumerate(lanes):
        st = st_s[row, h]
        for cc in range(n_chunks):
            st_in[cc][h] = st.astype(BF16)
            st = dec[cc][:, ls] * st + upd[cc][h]
        st_s[row, h] = st
    for cc in range(n_chunks):
        for h, ls in enumerate(lanes):
            o_s[row, rows[cc], ls] = (
                lax.dot_general(qd[rows[cc], ls], st_in[cc][h], nt, preferred_element_type=F32)
                + _dot(att[cc][h], vb[rows[cc], ls]))
    y_ref[row] = _hgrn_gate_out(o_s[row], g_ref[row].astype(F32), ng_ref[...])


def _hgrn_load_state(s0_ref, st_s):
    for r in range(st_s.shape[0]):
        for h in range(HG_HEADS):
            st_s[r, h] = s0_ref[r if s0_ref.shape[0] > 1 else 0, h].T


def _hgrn_store_state(s_out_ref, st_s):
    for r in range(st_s.shape[0]):
        for h in range(HG_HEADS):
            s_out_ref[r, h] = st_s[r, h].T


def _hgrn_kernel(q_ref, f_ref, i_ref, g_ref, lb_ref, ng_ref, tri_ref, s0_ref,
                 y_ref, s_out_ref, st_s, o_s, *, th, valid):
    i = pl.program_id(1)
    pl.when(i == 0)(lambda: _hgrn_load_state(s0_ref, st_s))
    _hgrn_tile(0, q_ref, f_ref, i_ref, g_ref, lb_ref, ng_ref, tri_ref, y_ref, st_s, o_s,
               th=th, valid=valid)
    pl.when(i == pl.num_programs(1) - 1)(lambda: _hgrn_store_state(s_out_ref, st_s))


def _hgrn_tri(tr):
    idx = jnp.arange(tr)
    return ((idx[:, None] // HG_CHUNK == idx[None, :] // HG_CHUNK)
            & (idx[None, :] <= idx[:, None])).astype(BF16)


def _hgrn(zf, qig, lb, ng, s0, th, n_tiles, valid):
    n = zf.shape[0]
    own = s0.shape[0] == n
    tr = min(th, 256)
    col = lambda j: pl.BlockSpec((1, th, HG_DIM), lambda b, i, j=j: (b, i, j))
    const = lambda shape: pl.BlockSpec(shape, lambda b, i: (0,) * len(shape),
                                       pipeline_mode=pl.Buffered(1))
    st = pl.BlockSpec((1, HG_HEADS, HG_HEAD_DIM, HG_HEAD_DIM), lambda b, i: (b, 0, 0, 0))
    st_in = pl.BlockSpec((1, HG_HEADS, HG_HEAD_DIM, HG_HEAD_DIM),
                         lambda b, i: (b if own else 0, 0, 0, 0))
    return pl.pallas_call(
        functools.partial(_hgrn_kernel, th=th, valid=valid),
        grid=(n, n_tiles),
        in_specs=[col(0), col(0), col(1), col(2), const((1, HG_DIM)), const((1, HG_DIM)),
                  const((tr, tr)), st_in],
        out_specs=[pl.BlockSpec((1, th, HG_DIM), lambda b, i: (b, i, 0)), st],
        out_shape=(jax.ShapeDtypeStruct((n, n_tiles * th, HG_DIM), F32),
                   jax.ShapeDtypeStruct((n, HG_HEADS, HG_HEAD_DIM, HG_HEAD_DIM), F32)),
        scratch_shapes=[pltpu.VMEM((1, HG_HEADS, HG_HEAD_DIM, HG_HEAD_DIM), F32),
                        pltpu.VMEM((1, th, HG_DIM), F32)],
        compiler_params=_params("arbitrary", "arbitrary"),
        name="hgrn_chunks",
    )(qig, zf, qig, qig, lb, ng, _hgrn_tri(tr), s0)


def _hgrn_step_kernel(q_ref, f_ref, i_ref, g_ref, lb_ref, ng_ref, s0_ref,
                      y_ref, s_out_ref, o_s, *, sb):
    lb = lb_ref[...]
    f = lb + (1.0 - lb) * _sigmoid(f_ref[0])
    q = q_ref[0].astype(F32)
    v = i_ref[0].astype(F32)
    pad = jnp.zeros((HG_HEAD_DIM - sb, HG_HEAD_DIM), F32)
    sq = (HG_HEAD_DIM, HG_HEAD_DIM)
    for h in range(HG_HEADS):
        ls = slice(h * HG_HEAD_DIM, (h + 1) * HG_HEAD_DIM)
        fcols = jnp.concatenate([f[:, ls], pad], axis=0).T
        for s in range(sb):
            fc = jnp.broadcast_to(fcols[:, s:s + 1], sq)
            sn = fc * s0_ref[s, h] + (1.0 - fc) * v[s:s + 1, ls]
            s_out_ref[s, h] = sn
            o_s[s:s + 1, ls] = _dot(q[s:s + 1, ls].astype(BF16), sn.astype(BF16))
    y_ref[0] = _hgrn_gate_out(o_s[...], g_ref[0].astype(F32), ng_ref[...])


def _hgrn_step(zf, qig, lb, ng, s0, sb):
    r = zf.shape[1]
    col = lambda j: pl.BlockSpec((1, sb, HG_DIM), lambda i, j=j: (0, i, j))
    vec = pl.BlockSpec((1, HG_DIM), lambda i: (0, 0))
    st = pl.BlockSpec((sb, HG_HEADS, HG_HEAD_DIM, HG_HEAD_DIM), lambda i: (i, 0, 0, 0))
    return pl.pallas_call(
        functools.partial(_hgrn_step_kernel, sb=sb),
        grid=(r // sb,),
        in_specs=[col(0), col(0), col(1), col(2), vec, vec, st],
        out_specs=[pl.BlockSpec((1, sb, HG_DIM), lambda i: (0, i, 0)), st],
        out_shape=(jax.ShapeDtypeStruct((1, r, HG_DIM), F32),
                   jax.ShapeDtypeStruct(s0.shape, F32)),
        scratch_shapes=[pltpu.VMEM((sb, HG_DIM), F32)],
        compiler_params=_params("arbitrary"),
        name="hgrn_step",
    )(qig, zf, qig, qig, lb, ng, s0)


def _mix_out(y5_raw, yh, h, wglu, bglu, wout, g2):
    y5p = _gelu_tanh(y5_raw)
    y5 = y5p * _sigmoid(_dot(y5p.astype(BF16), wglu) + bglu)
    ymix = jnp.concatenate([y5, yh], axis=-1).astype(BF16)
    h1 = h + _dot(ymix, wout)
    return h1, _rms_norm(h1, g2).astype(BF16)


def _conv_taps(a, prev8, cw):
    rows = a.shape[0]
    cv = cw[3:4] + pltpu.roll(a, 2, 0) * cw[0:1] + pltpu.roll(a, 1, 0) * cw[1:2] + a * cw[2:3]
    d = prev8 - a[rows - SUBLANES:rows]
    rid = lax.broadcasted_iota(jnp.int32, (SUBLANES, 1), 0)
    fix = (jnp.where(rid < 1, pltpu.roll(d, 1, 0), 0.0) * cw[1:2]
           + jnp.where(rid < 2, pltpu.roll(d, 2, 0), 0.0) * cw[0:1])
    return jnp.concatenate([cv[:SUBLANES] + fix, cv[SUBLANES:]], axis=0)


def _ffn_kernel(y5_ref, yh_ref, h_ref, wglu_ref, bglu_ref, wout_ref, g2_ref,
                wa_ref, wv_ref, wd_ref, cw_ref, cin_ref, gf_ref,
                out_ref, cout_ref, carry_s, hn_s, h1_s, s_s, *, tm):
    h1, hn = _mix_out(y5_ref[0], yh_ref[0], h_ref[0], wglu_ref[...], bglu_ref[...],
                      wout_ref[...], g2_ref[...])
    h1_s[...] = h1
    hn_s[...] = hn

    @pl.when(pl.program_id(1) == 0)
    def _():
        carry_s[...] = cin_ref[0]

    for j in range(N_FF_CHUNKS):
        cs = slice(j * FF_CHUNK, (j + 1) * FF_CHUNK)
        hn = hn_s[...]
        a = _dot(hn, wa_ref[:, cs])
        v = _dot(hn, wv_ref[:, cs])
        cv = _conv_taps(a, carry_s[:, cs], cw_ref[:, cs])
        carry_s[:, cs] = a[tm - SUBLANES:tm]
        s_s[:, cs] = ((cv * _sigmoid(cv)) * v).astype(BF16)

    out_ref[0] = _rms_norm(h1_s[...] + _dot(s_s[...], wd_ref[...]), gf_ref[...])
    cout_ref[0] = carry_s[...]


def _ffn_small_kernel(y5m_ref, y5s_ref, yhm_ref, yhs_ref, hm_ref, hs_ref,
                      wglu_ref, bglu_ref, wout_ref, g2_ref, wa_ref, wv_ref, wd_ref, cw_ref,
                      cin_a_ref, cin_b_ref, gf_ref,
                      out_ref, a_out_ref, mcarry_ref,
                      wglu_bf_ref, wout_bf_ref, wa_bf_ref, wv_bf_ref, wd_bf_ref,
                      hn_s, h1_s, acc_s, *, n_meta):
    j = pl.program_id(0)

    @pl.when(j == 0)
    def _():
        wglu = wglu_ref[...].astype(BF16)
        wout = wout_ref[...].astype(BF16)
        wglu_bf_ref[...] = wglu
        wout_bf_ref[...] = wout
        rows = lambda m_ref, s_ref: jnp.concatenate([m_ref[0], s_ref[0]], axis=0)
        h1, hn = _mix_out(rows(y5m_ref, y5s_ref), rows(yhm_ref, yhs_ref), rows(hm_ref, hs_ref),
                          wglu, bglu_ref[...], wout, g2_ref[...])
        h1_s[...] = h1
        hn_s[...] = hn
        acc_s[...] = jnp.zeros_like(acc_s)

    wa = wa_ref[...].astype(BF16)
    wv = wv_ref[...].astype(BF16)
    wd = wd_ref[...].astype(BF16)
    wa_bf_ref[...] = wa
    wv_bf_ref[...] = wv
    wd_bf_ref[...] = wd
    hn = hn_s[...]
    a = _dot(hn, wa)
    v = _dot(hn, wv)
    cw = cw_ref[...]
    a_m, a_s = a[:n_meta], a[n_meta:]
    cv_m = _conv_taps(a_m, jnp.zeros((SUBLANES, FF_CHUNK), F32), cw)
    cv_s = cw[3:4] + cin_a_ref[...] * cw[0:1] + cin_b_ref[...] * cw[1:2] + a_s * cw[2:3]
    mcarry_ref[...] = a_m[n_meta - SUBLANES:]
    a_out_ref[...] = a_s
    cv = jnp.concatenate([cv_m, cv_s], axis=0)
    acc_s[...] += _dot(((cv * _sigmoid(cv)) * v).astype(BF16), wd)

    @pl.when(j == pl.num_programs(0) - 1)
    def _():
        out_ref[...] = _rms_norm((h1_s[...] + acc_s[...])[n_meta:], gf_ref[...])


def _ffn_small(y5m, y5s, yhm, yhs, hm, hs, wglu, bglu, wout, g2, wup, wd, cw, cin_a, cin_b, gf):
    n_meta, r = y5m.shape[1], y5s.shape[1]
    const = lambda shape: pl.BlockSpec(shape, lambda j: (0,) * len(shape))
    head = lambda rows, width: pl.BlockSpec((1, rows, width), lambda j: (0, 0, 0))
    ff_cols = lambda rows: pl.BlockSpec((rows, FF_CHUNK), lambda j: (0, j))
    sds = jax.ShapeDtypeStruct
    return pl.pallas_call(
        functools.partial(_ffn_small_kernel, n_meta=n_meta),
        grid=(N_FF_CHUNKS,),
        in_specs=[head(n_meta, S5_DIM), head(r, S5_DIM), head(n_meta, HG_DIM), head(r, HG_DIM),
                  head(n_meta, D_MODEL), head(r, D_MODEL),
                  const(wglu.shape), const(bglu.shape), const(wout.shape), const(g2.shape),
                  ff_cols(D_MODEL),
                  pl.BlockSpec((D_MODEL, FF_CHUNK), lambda j: (0, N_FF_CHUNKS + j)),
                  pl.BlockSpec((FF_CHUNK, D_MODEL), lambda j: (j, 0)),
                  ff_cols(SUBLANES), ff_cols(r), ff_cols(r), const(gf.shape)],
        out_specs=[const((r, D_MODEL)), ff_cols(r), ff_cols(SUBLANES),
                   const(wglu.shape), const(wout.shape), ff_cols(D_MODEL), ff_cols(D_MODEL),
                   pl.BlockSpec((FF_CHUNK, D_MODEL), lambda j: (j, 0))],
        out_shape=(sds((r, D_MODEL), F32), sds((r, D_FF), F32), sds((SUBLANES, D_FF), F32),
                   sds(wglu.shape, BF16), sds(wout.shape, BF16),
                   sds((D_MODEL, D_FF), BF16), sds((D_MODEL, D_FF), BF16), sds(wd.shape, BF16)),
        scratch_shapes=[pltpu.VMEM((n_meta + r, D_MODEL), BF16),
                        pltpu.VMEM((n_meta + r, D_MODEL), F32),
                        pltpu.VMEM((n_meta + r, D_MODEL), F32)],
        compiler_params=_params("arbitrary"),
        name="ffn_small",
    )(y5m, y5s, yhm, yhs, hm, hs, wglu, bglu, wout, g2, wup, wup, wd, cw, cin_a, cin_b, gf)


def _ffn(y5, yh, h, wglu, bglu, wout, g2, wa, wv, wd, cw, cin, gf, tm, n_tiles):
    n = h.shape[0]
    const = lambda shape: pl.BlockSpec(shape, lambda b, i: (0,) * len(shape),
                                       pipeline_mode=pl.Buffered(1))
    own = cin.shape[0] == n
    conv = lambda index: pl.BlockSpec((1, SUBLANES, D_FF), index)
    return pl.pallas_call(
        functools.partial(_ffn_kernel, tm=tm),
        grid=(n, n_tiles),
        in_specs=[pl.BlockSpec((1, tm, S5_DIM), lambda b, i: (b, i, 0)),
                  pl.BlockSpec((1, tm, HG_DIM), lambda b, i: (b, i, 0)),
                  pl.BlockSpec((1, tm, D_MODEL), lambda b, i: (b, i, 0)),
                  const(wglu.shape), const(bglu.shape), const(wout.shape), const(g2.shape),
                  const(wa.shape), const(wv.shape), const(wd.shape), const(cw.shape),
                  conv(lambda b, i: (b if own else 0, 0, 0)), const(gf.shape)],
        out_specs=[pl.BlockSpec((1, tm, D_MODEL), lambda b, i: (b, i, 0)),
                   conv(lambda b, i: (b, 0, 0))],
        out_shape=(jax.ShapeDtypeStruct((n, n_tiles * tm, D_MODEL), F32),
                   jax.ShapeDtypeStruct((n, SUBLANES, D_FF), F32)),
        scratch_shapes=[pltpu.VMEM((SUBLANES, D_FF), F32),
                        pltpu.VMEM((tm, D_MODEL), BF16),
                        pltpu.VMEM((tm, D_MODEL), F32),
                        pltpu.VMEM((tm, D_FF), BF16)],
        compiler_params=_params("arbitrary", "arbitrary"),
        name="mix_out_ffn",
    )(y5, yh, h, wglu, bglu, wout, g2, wa, wv, wd, cw, cin, gf)


def kernel(x_prompt, x_sample, state_s5_re, state_s5_im, state_hgrn, state_ffn_conv, meta_tokens, norm_mix_g, w_in, s5_lambda_re, s5_lambda_im, s5_log_dt, s5_b_re, s5_b_im, s5_c_re, s5_c_im, s5_d, s5_w_glu, s5_b_glu, hg_lower_bounds, hg_norm_g, w_out, norm_ffn_g, ffn_w_up, ffn_conv_w, ffn_conv_b, ffn_w_down, final_norm_g):
    nb, seq, _ = x_prompt.shape
    ns = x_sample.shape[0]
    li = 0

    ar, ai, bb_re, bb_im, lb = _prep(s5_lambda_re[li], s5_lambda_im[li], s5_log_dt[li],
                                     s5_b_re[li], s5_b_im[li], hg_lower_bounds)
    ar = ar.reshape(1, S5_LANES)
    ai = ai.reshape(1, S5_LANES)
    halves = lambda t: t.reshape((S5_HALVES, S5_GROUPS // S5_HALVES) + t.shape[1:])
    wbr = _block_diag(halves(bb_re)).astype(BF16)
    wbi = _block_diag(halves(bb_im)).astype(BF16)
    wcr = _block_diag(halves(s5_c_re[li].transpose(0, 2, 1))).astype(BF16)
    wci = _block_diag(halves(-s5_c_im[li].transpose(0, 2, 1))).astype(BF16)
    d5 = s5_d[li].reshape(1, S5_DIM)
    g1 = norm_mix_g[li].reshape(1, D_MODEL)
    g2 = norm_ffn_g[li].reshape(1, D_MODEL)
    gf = final_norm_g.reshape(1, D_MODEL)
    ng = hg_norm_g[li].reshape(1, HG_DIM)
    bglu = s5_b_glu[li].reshape(1, S5_DIM)
    w_in_b = w_in[li].astype(BF16)
    cw = jnp.concatenate([ffn_conv_w[li], ffn_conv_b[li][None],
                          jnp.zeros((SUBLANES - CONV_W - 1, D_FF), F32)], axis=0)
    s5w = (ar, ai, wbr, wbi, wcr, wci, d5)

    meta = jnp.zeros((1, HG_CHUNK, D_MODEL), F32).at[0, :N_META].set(meta_tokens)
    z5 = jnp.zeros((1, S5_LANES), F32)
    zh = jnp.zeros((1, HG_HEADS, HG_HEAD_DIM, HG_HEAD_DIM), F32)
    u_tm, zf, qig = _inproj(meta, g1, w_in_b, HG_CHUNK, 1)
    y5m, m5r, m5i = _s5(u_tm, z5, z5, *s5w, 1, N_META, 1)
    yhm, mhg = _hgrn(zf, qig, lb, ng, zh, HG_CHUNK, 1, N_META)

    xs = x_sample.reshape(1, ns, D_MODEL)
    u_tm, zf, qig = _inproj(xs, g1, w_in_b, ns, 1)
    y5s, s5r, s5i = _s5(u_tm, state_s5_re[li].reshape(ns, S5_LANES), state_s5_im[li].reshape(ns, S5_LANES),
                        *s5w, ns, 1, 1)
    yhs, shg = _hgrn_step(zf, qig, lb, ng, state_hgrn[li], 16)

    buf = state_ffn_conv[li]
    y_sample, a_new, mconv, wglu, wout, wa, wv, wd = _ffn_small(
        y5m, y5s, yhm, yhs, meta, xs, s5_w_glu[li], bglu, w_out[li], g2,
        ffn_w_up[li], ffn_w_down[li], cw, buf[:, 0], buf[:, 1], gf)
    sconv = jnp.stack([buf[:, 1], a_new], axis=1)

    ti, tm, tt, th = 512, 512, 128, 1024
    u_tm, zf, qig = _inproj(x_prompt, g1, w_in_b, ti, seq // ti)
    y5, p5r, p5i = _s5(u_tm, m5r, m5i, *s5w, nb, tt, seq // tt)
    yh, phg = _hgrn(zf, qig, lb, ng, mhg, th, seq // th, th)
    y_prompt, pconv = _ffn(y5, yh, x_prompt, wglu, bglu, wout, g2, wa, wv, wd, cw, mconv[None], gf,
                           tm, seq // tm)

    st5 = lambda t: t.reshape(1, -1, S5_GROUPS, S5_STATE)
    pconv = pconv[:, SUBLANES - (CONV_W - 1):, :]
    return (y_prompt, y_sample.reshape(ns, 1, D_MODEL),
            st5(p5r), st5(p5i), phg[None], pconv[None],
            st5(s5r), st5(s5i), shg[None], sconv[None])
```

```python
import functools

import jax
import jax.numpy as jnp
import numpy as np
from jax import lax
from jax.experimental import pallas as pl
from jax.experimental.pallas import tpu as pltpu

F32 = jnp.float32
BF16 = jnp.bfloat16

D_MODEL = 1024
N_META = 16
S5_DIM = 512
S5_GROUP = 16
S5_GROUPS = 32
S5_STATE = 64
S5_LANES = S5_GROUPS * S5_STATE
HG_DIM = 512
HG_HEAD_DIM = 128
HG_HEADS = 4
HG_CHUNK = 64
D_FF = 2816
CONV_W = 3
EPS = 1e-6

S5_HALVES = 2
S5_HALF_CH = S5_DIM // S5_HALVES
S5_HALF_ST = S5_LANES // S5_HALVES
SCAN_LANES = 512
FF_CHUNK = 256
N_FF_CHUNKS = D_FF // FF_CHUNK
SUBLANES = 8
LANES = 128
VMEM_LIMIT = 56 * 1024 * 1024


def _sigmoid(x):
    return 1.0 / (1.0 + jnp.exp(-x))


def _rms_norm(x, g):
    ms = jnp.mean(x * x, axis=-1, keepdims=True)
    return x * lax.rsqrt(ms + EPS) * g


def _dot(a, b):
    return jnp.dot(a, b, preferred_element_type=F32)


def _params(*sem):
    return pltpu.CompilerParams(dimension_semantics=sem, vmem_limit_bytes=VMEM_LIMIT)


def _prep_kernel(lam_ref, dt_ref, bt_ref, hlb_ref, a_ref, bb_ref, lb_ref):
    lam_re = lam_ref[0]
    lam_im = lam_ref[1]
    dt = jnp.exp(dt_ref[...])
    mag = jnp.exp(lam_re * dt)
    ar = mag * jnp.cos(lam_im * dt)
    ai = mag * jnp.sin(lam_im * dt)
    nr = ar - 1.0
    den = lam_re * lam_re + lam_im * lam_im
    cr = (nr * lam_re + ai * lam_im) / den
    ci = (ai * lam_re - nr * lam_im) / den
    a_ref[0] = ar
    a_ref[1] = ai
    bt_re = bt_ref[0]
    bt_im = bt_ref[1]
    bb_ref[0] = cr * bt_re - ci * bt_im
    bb_ref[1] = cr * bt_im + ci * bt_re
    hlb = hlb_ref[...]
    e = jnp.exp(hlb - jnp.max(hlb, axis=0, keepdims=True))
    lb_ref[...] = e[0:1] / jnp.sum(e, axis=0, keepdims=True)


def _prep(lam_re, lam_im, log_dt, b_re, b_im, hlb):
    g, p = lam_re.shape
    sds = jax.ShapeDtypeStruct
    return pl.pallas_call(
        _prep_kernel,
        out_shape=(sds((2, g, 1, p), F32), sds((2, g, S5_GROUP, p), F32), sds((1, HG_DIM), F32)),
        name="param_prep",
    )(jnp.stack([lam_re, lam_im]).reshape(2, g, 1, p),
      jnp.broadcast_to(log_dt.reshape(g, 1, 1), (g, 1, p)),
      jnp.stack([b_re, b_im]).transpose(0, 1, 3, 2), hlb)


def _block_diag(blocks):
    two, g, r, c = blocks.shape
    per = g // S5_HALVES
    eye = np.eye(per, dtype=np.float32)
    out = jnp.einsum("ajgrc,gh->ajgrhc", blocks.reshape(two, S5_HALVES, per, r, c), eye)
    return out.reshape(two, S5_HALVES, per * r, per * c).astype(BF16)


def _inproj_kernel(x_ref, g_ref, w_ref, u_ref, f_ref, qig_ref, *, n, tm):
    hn = _rms_norm(x_ref[0], g_ref[...]).astype(BF16)
    col = lambda j: _dot(hn, w_ref[:, j * HG_DIM:(j + 1) * HG_DIM])
    b = pl.program_id(1)
    u = col(0)
    for l in range(S5_DIM // LANES):
        ul = u[:, l * LANES:(l + 1) * LANES]
        if n == 1:
            u_ref[l] = ul
        else:
            u_ref[l, pl.ds(b, tm, stride=n), :] = ul
    f_ref[0] = col(2)
    for j, src in enumerate((1, 3, 4)):
        qig_ref[0, :, j * HG_DIM:(j + 1) * HG_DIM] = col(src).astype(BF16)


def _inproj(x, g, w, tm, n_tiles):
    n, l, d = x.shape
    cols = w.shape[1]
    rows = tm * n_tiles
    const = lambda shape: pl.BlockSpec(shape, lambda i, b: (0,) * len(shape),
                                       pipeline_mode=pl.Buffered(1))
    return pl.pallas_call(
        functools.partial(_inproj_kernel, n=n, tm=tm),
        grid=(n_tiles, n),
        in_specs=[pl.BlockSpec((1, tm, d), lambda i, b: (b, i, 0)),
                  const((1, d)), const((d, cols))],
        out_specs=[pl.BlockSpec((S5_DIM // LANES, tm * n, LANES), lambda i, b: (0, i, 0)),
                   pl.BlockSpec((1, tm, HG_DIM), lambda i, b: (b, i, 0)),
                   pl.BlockSpec((1, tm, 3 * HG_DIM), lambda i, b: (b, i, 0))],
        out_shape=(jax.ShapeDtypeStruct((S5_DIM // LANES, rows * n, LANES), F32),
                   jax.ShapeDtypeStruct((n, rows, HG_DIM), F32),
                   jax.ShapeDtypeStruct((n, rows, 3 * HG_DIM), BF16)),
        compiler_params=_params("arbitrary", "arbitrary"),
        name="inproj",
    )(x, g, w)


def _gelu_tanh(y):
    return 0.5 * y * (1.0 + jnp.tanh(0.7978845608028654 * (y + 0.044715 * (y * y * y))))


def _s5_tile(step, u_ref, h0r_ref, h0i_ref, a_ref, wb_ref, wc_ref, d_ref,
             y_ref, hr_out_ref, hi_out_ref, xr_s, xi_s, hr_s, hi_s, y_s, *, n, tt, interleaved):
    @pl.when(step == 0)
    def _():
        hr_s[...] = jnp.broadcast_to(h0r_ref[...], hr_s.shape)
        hi_s[...] = jnp.broadcast_to(h0i_ref[...], hi_s.shape)

    slabs = S5_HALF_CH // LANES
    load_u = lambda j: jnp.concatenate([u_ref[j * slabs + l] for l in range(slabs)], axis=1)
    for j in range(S5_HALVES):
        st = slice(j * S5_HALF_ST, (j + 1) * S5_HALF_ST)
        ub = load_u(j).astype(BF16)
        xr_s[:, st] = _dot(ub, wb_ref[0, j])
        xi_s[:, st] = _dot(ub, wb_ref[1, j])
    for c in range(S5_LANES // SCAN_LANES):
        loc = slice(c * SCAN_LANES, (c + 1) * SCAN_LANES)
        ar = jnp.broadcast_to(a_ref[0, :, loc], (n, SCAN_LANES))
        ai = jnp.broadcast_to(a_ref[1, :, loc], (n, SCAN_LANES))

        def step(t, carry, ar=ar, ai=ai, loc=loc):
            hr, hi = carry
            r = 0 if tt == 1 else pl.multiple_of(t * n, n)
            nhr = ar * hr - ai * hi + xr_s[pl.ds(r, n), loc]
            nhi = ar * hi + ai * hr + xi_s[pl.ds(r, n), loc]
            xr_s[pl.ds(r, n), loc] = nhr
            xi_s[pl.ds(r, n), loc] = nhi
            return nhr, nhi

        carry = (hr_s[:, loc], hi_s[:, loc])
        if tt == 1:
            carry = step(0, carry)
        else:
            carry = lax.fori_loop(0, tt, step, carry, unroll=True)
        hr_s[:, loc] = carry[0]
        hi_s[:, loc] = carry[1]
    for j in range(S5_HALVES):
        ch = slice(j * S5_HALF_CH, (j + 1) * S5_HALF_CH)
        st = slice(j * S5_HALF_ST, (j + 1) * S5_HALF_ST)
        y = (_dot(xr_s[:, st].astype(BF16), wc_ref[0, j]) + _dot(xi_s[:, st].astype(BF16), wc_ref[1, j])
             + d_ref[:, ch] * load_u(j))
        if interleaved:
            for l in range(slabs):
                y_s[j * slabs + l] = y[:, l * LANES:(l + 1) * LANES]
        else:
            y_ref[0, :, ch] = y
    if interleaved:
        for b in range(n):
            for l in range(S5_DIM // LANES):
                y_ref[b, :, l * LANES:(l + 1) * LANES] = y_s[l, pl.ds(b, tt, stride=n), :]
    hr_out_ref[...] = hr_s[...]
    hi_out_ref[...] = hi_s[...]


def _s5_kernel(*refs, **static):
    _s5_tile(pl.program_id(0), *refs, **static)


def _s5_scratch(n, rows, interleaved):
    return [pltpu.VMEM((rows, S5_LANES), F32), pltpu.VMEM((rows, S5_LANES), F32),
            pltpu.VMEM((n, S5_LANES), F32), pltpu.VMEM((n, S5_LANES), F32),
            pltpu.VMEM((S5_DIM // LANES, rows if interleaved else SUBLANES, LANES), F32)]


def _s5(u4, h0r, h0i, a, wb, wc, d, n, tt, n_tiles):
    rows = tt * n
    interleaved = n > 1 and tt > 1
    const = lambda shape: pl.BlockSpec(shape, lambda i: (0,) * len(shape),
                                       pipeline_mode=pl.Buffered(1))
    state = pl.BlockSpec((n, S5_LANES), lambda i: (0, 0))
    y_block = (n, tt, S5_DIM) if interleaved else (1, rows, S5_DIM)
    y_shape = (n, tt * n_tiles, S5_DIM) if interleaved else (1, rows * n_tiles, S5_DIM)
    return pl.pallas_call(
        functools.partial(_s5_kernel, n=n, tt=tt, interleaved=interleaved),
        grid=(n_tiles,),
        in_specs=[pl.BlockSpec((S5_DIM // LANES, rows, LANES), lambda i: (0, i, 0)),
                  const(h0r.shape), const(h0i.shape), const(a.shape), const(wb.shape), const(wc.shape),
                  const((1, S5_DIM))],
        out_specs=[pl.BlockSpec(y_block, lambda i: (0, i, 0)), state, state],
        out_shape=(jax.ShapeDtypeStruct(y_shape, F32),
                   jax.ShapeDtypeStruct((n, S5_LANES), F32),
                   jax.ShapeDtypeStruct((n, S5_LANES), F32)),
        scratch_shapes=_s5_scratch(n, rows, interleaved),
        compiler_params=_params("arbitrary"),
        name="s5_scan",
    )(u4, h0r, h0i, a, wb, wc, d)


def _hgrn_gate_out(o, g, ng):
    parts = []
    for h in range(HG_HEADS):
        oh = o[:, h * HG_HEAD_DIM:(h + 1) * HG_HEAD_DIM]
        ms = jnp.mean(oh * oh, axis=-1, keepdims=True)
        parts.append(oh * lax.rsqrt(ms + EPS))
    return jnp.concatenate(parts, axis=-1) * ng * (g * _sigmoid(g))


def _hgrn_tile(row, q_ref, f_ref, i_ref, g_ref, lb_ref, ng_ref, tri_ref, y_ref, st_s, o_s, *, th, valid):
    c = HG_CHUNK
    lb = lb_ref[...]
    f = lb + (1.0 - lb) * _sigmoid(f_ref[row])
    lc = jnp.log(f)
    k = 1.0 - f
    q = q_ref[row].astype(F32)
    if valid < th:
        live = lax.broadcasted_iota(jnp.int32, (th, 1), 0) < valid
        lc = jnp.where(live, lc, 0.0)
        k = jnp.where(live, k, 0.0)
        q = jnp.where(live, q, 0.0)
    tri = tri_ref[...]
    tr = tri.shape[0]
    lc_hi = lc.astype(BF16)
    lc_lo = (lc - lc_hi.astype(F32)).astype(BF16)
    b = jnp.concatenate([_dot(tri, lc_hi[r:r + tr]) + _dot(tri, lc_lo[r:r + tr])
                         for r in range(0, th, tr)], axis=0)
    qd = (q * jnp.exp(b)).astype(BF16)
    kd = k * jnp.exp(-b)
    kdb = kd.astype(BF16)
    vb = i_ref[row]
    v = vb.astype(F32)
    causal = (lax.broadcasted_iota(jnp.int32, (c, c), 1) <= lax.broadcasted_iota(jnp.int32, (c, c), 0))
    nt = (((1,), (1,)), ((), ()))
    n_chunks = th // c
    rows = [slice(cc * c, (cc + 1) * c) for cc in range(n_chunks)]
    lanes = [slice(h * HG_HEAD_DIM, (h + 1) * HG_HEAD_DIM) for h in range(HG_HEADS)]
    dec = [jnp.exp(b[cc * c + c - 1:cc * c + c, :]) for cc in range(n_chunks)]
    kdec = [(kd[rows[cc]] * dec[cc]).astype(BF16) for cc in range(n_chunks)]
    att = [[jnp.where(causal, lax.dot_general(qd[rows[cc], ls], kdb[rows[cc], ls], nt,
                                              preferred_element_type=F32), 0.0).astype(BF16)
            for ls in lanes] for cc in range(n_chunks)]
    upd = [[_dot(v[rows[cc], ls].T.astype(BF16), kdec[cc][:, ls])
            for ls in lanes] for cc in range(n_chunks)]
    st_in = [[None] * HG_HEADS for _ in range(n_chunks)]
    for h, ls in enumerate(lanes):
        st = st_s[row, h]
        for cc in range(n_chunks):
            st_in[cc][h] = st.astype(BF16)
            st = dec[cc][:, ls] * st + upd[cc][h]
        st_s[row, h] = st
    for cc in range(n_chunks):
        for h, ls in enumerate(lanes):
            o_s[row, rows[cc], ls] = (
                lax.dot_general(qd[rows[cc], ls], st_in[cc][h], nt, preferred_element_type=F32)
                + _dot(att[cc][h], vb[rows[cc], ls]))
    y_ref[row] = _hgrn_gate_out(o_s[row], g_ref[row].astype(F32), ng_ref[...])


def _hgrn_load_state(s0_ref, st_s):
    for r in range(st_s.shape[0]):
        for h in range(HG_HEADS):
            st_s[r, h] = s0_ref[r if s0_ref.shape[0] > 1 else 0, h].T


def _hgrn_store_state(s_out_ref, st_s):
    for r in range(st_s.shape[0]):
        for h in range(HG_HEADS):
            s_out_ref[r, h] = st_s[r, h].T


def _hgrn_kernel(q_ref, f_ref, i_ref, g_ref, lb_ref, ng_ref, tri_ref, s0_ref,
                 y_ref, s_out_ref, st_s, o_s, *, th, valid):
    i = pl.program_id(1)
    pl.when(i == 0)(lambda: _hgrn_load_state(s0_ref, st_s))
    _hgrn_tile(0, q_ref, f_ref, i_ref, g_ref, lb_ref, ng_ref, tri_ref, y_ref, st_s, o_s,
               th=th, valid=valid)
    pl.when(i == pl.num_programs(1) - 1)(lambda: _hgrn_store_state(s_out_ref, st_s))


def _hgrn_tri(tr):
    idx = np.arange(tr)
    tri = (idx[:, None] // HG_CHUNK == idx[None, :] // HG_CHUNK) & (idx[None, :] <= idx[:, None])
    return jnp.asarray(tri, dtype=BF16)


def _hgrn(zf, qig, lb, ng, s0, th, n_tiles, valid):
    n = zf.shape[0]
    own = s0.shape[0] == n
    tr = min(th, 256)
    col = lambda j: pl.BlockSpec((1, th, HG_DIM), lambda b, i, j=j: (b, i, j))
    const = lambda shape: pl.BlockSpec(shape, lambda b, i: (0,) * len(shape),
                                       pipeline_mode=pl.Buffered(1))
    st = pl.BlockSpec((1, HG_HEADS, HG_HEAD_DIM, HG_HEAD_DIM), lambda b, i: (b, 0, 0, 0))
    st_in = pl.BlockSpec((1, HG_HEADS, HG_HEAD_DIM, HG_HEAD_DIM),
                         lambda b, i: (b if own else 0, 0, 0, 0))
    return pl.pallas_call(
        functools.partial(_hgrn_kernel, th=th, valid=valid),
        grid=(n, n_tiles),
        in_specs=[col(0), col(0), col(1), col(2), const((1, HG_DIM)), const((1, HG_DIM)),
                  const((tr, tr)), st_in],
        out_specs=[pl.BlockSpec((1, th, HG_DIM), lambda b, i: (b, i, 0)), st],
        out_shape=(jax.ShapeDtypeStruct((n, n_tiles * th, HG_DIM), F32),
                   jax.ShapeDtypeStruct((n, HG_HEADS, HG_HEAD_DIM, HG_HEAD_DIM), F32)),
        scratch_shapes=[pltpu.VMEM((1, HG_HEADS, HG_HEAD_DIM, HG_HEAD_DIM), F32),
                        pltpu.VMEM((1, th, HG_DIM), F32)],
        compiler_params=_params("arbitrary", "arbitrary"),
        name="hgrn_chunks",
    )(qig, zf, qig, qig, lb, ng, _hgrn_tri(tr), s0)


def _hgrn_step_kernel(q_ref, f_ref, i_ref, g_ref, lb_ref, ng_ref, s0_ref,
                      y_ref, s_out_ref, o_s, *, sb):
    lb = lb_ref[...]
    f = lb + (1.0 - lb) * _sigmoid(f_ref[0])
    q = q_ref[0].astype(F32)
    v = i_ref[0].astype(F32)
    pad = jnp.zeros((HG_HEAD_DIM - sb, HG_HEAD_DIM), F32)
    sq = (HG_HEAD_DIM, HG_HEAD_DIM)
    for h in range(HG_HEADS):
        ls = slice(h * HG_HEAD_DIM, (h + 1) * HG_HEAD_DIM)
        fcols = jnp.concatenate([f[:, ls], pad], axis=0).T
        for s in range(sb):
            fc = jnp.broadcast_to(fcols[:, s:s + 1], sq)
            sn = fc * s0_ref[s, h] + (1.0 - fc) * v[s:s + 1, ls]
            s_out_ref[s, h] = sn
            o_s[s:s + 1, ls] = _dot(q[s:s + 1, ls].astype(BF16), sn.astype(BF16))
    y_ref[0] = _hgrn_gate_out(o_s[...], g_ref[0].astype(F32), ng_ref[...])


def _hgrn_step(zf, qig, lb, ng, s0, sb):
    r = zf.shape[1]
    col = lambda j: pl.BlockSpec((1, sb, HG_DIM), lambda i, j=j: (0, i, j))
    vec = pl.BlockSpec((1, HG_DIM), lambda i: (0, 0))
    st = pl.BlockSpec((sb, HG_HEADS, HG_HEAD_DIM, HG_HEAD_DIM), lambda i: (i, 0, 0, 0))
    return pl.pallas_call(
        functools.partial(_hgrn_step_kernel, sb=sb),
        grid=(r // sb,),
        in_specs=[col(0), col(0), col(1), col(2), vec, vec, st],
        out_specs=[pl.BlockSpec((1, sb, HG_DIM), lambda i: (0, i, 0)), st],
        out_shape=(jax.ShapeDtypeStruct((1, r, HG_DIM), F32),
                   jax.ShapeDtypeStruct(s0.shape, F32)),
        scratch_shapes=[pltpu.VMEM((sb, HG_DIM), F32)],
        compiler_params=_params("arbitrary"),
        name="hgrn_step",
    )(qig, zf, qig, qig, lb, ng, s0)


def _mix_out(y5_raw, yh, h, wglu, bglu, wout, g2):
    y5p = _gelu_tanh(y5_raw)
    y5 = y5p * _sigmoid(_dot(y5p.astype(BF16), wglu) + bglu)
    ymix = jnp.concatenate([y5, yh], axis=-1).astype(BF16)
    h1 = h + _dot(ymix, wout)
    return h1, _rms_norm(h1, g2).astype(BF16)


def _conv_taps(a, prev8, cw):
    rows = a.shape[0]
    cv = cw[3:4] + pltpu.roll(a, 2, 0) * cw[0:1] + pltpu.roll(a, 1, 0) * cw[1:2] + a * cw[2:3]
    d = prev8 - a[rows - SUBLANES:rows]
    rid = lax.broadcasted_iota(jnp.int32, (SUBLANES, 1), 0)
    fix = (jnp.where(rid < 1, pltpu.roll(d, 1, 0), 0.0) * cw[1:2]
           + jnp.where(rid < 2, pltpu.roll(d, 2, 0), 0.0) * cw[0:1])
    return jnp.concatenate([cv[:SUBLANES] + fix, cv[SUBLANES:]], axis=0)


def _ffn_kernel(y5_ref, yh_ref, h_ref, wglu_ref, bglu_ref, wout_ref, g2_ref,
                wa_ref, wv_ref, wd_ref, cw_ref, cin_ref, gf_ref,
                out_ref, cout_ref, carry_s, hn_s, h1_s, s_s, *, tm):
    h1, hn = _mix_out(y5_ref[0], yh_ref[0], h_ref[0], wglu_ref[...], bglu_ref[...],
                      wout_ref[...], g2_ref[...])
    h1_s[...] = h1
    hn_s[...] = hn

    @pl.when(pl.program_id(1) == 0)
    def _():
        carry_s[...] = cin_ref[0]

    for j in range(N_FF_CHUNKS):
        cs = slice(j * FF_CHUNK, (j + 1) * FF_CHUNK)
        hn = hn_s[...]
        a = _dot(hn, wa_ref[:, cs])
        v = _dot(hn, wv_ref[:, cs])
        cv = _conv_taps(a, carry_s[:, cs], cw_ref[:, cs])
        carry_s[:, cs] = a[tm - SUBLANES:tm]
        s_s[:, cs] = ((cv * _sigmoid(cv)) * v).astype(BF16)

    out_ref[0] = _rms_norm(h1_s[...] + _dot(s_s[...], wd_ref[...]), gf_ref[...])
    cout_ref[0] = carry_s[...]


def _ffn_small_kernel(y5m_ref, y5s_ref, yhm_ref, yhs_ref, hm_ref, hs_ref,
                      wglu_ref, bglu_ref, wout_ref, g2_ref, wa_ref, wv_ref, wd_ref, cw_ref,
                      cin_a_ref, cin_b_ref, gf_ref,
                      out_ref, a_out_ref, mcarry_ref,
                      wglu_bf_ref, wout_bf_ref, wa_bf_ref, wv_bf_ref, wd_bf_ref,
                      hn_s, h1_s, acc_s, *, n_meta):
    j = pl.program_id(0)

    @pl.when(j == 0)
    def _():
        wglu = wglu_ref[...].astype(BF16)
        wout = wout_ref[...].astype(BF16)
        wglu_bf_ref[...] = wglu
        wout_bf_ref[...] = wout
        rows = lambda m_ref, s_ref: jnp.concatenate([m_ref[0], s_ref[0]], axis=0)
        h1, hn = _mix_out(rows(y5m_ref, y5s_ref), rows(yhm_ref, yhs_ref), rows(hm_ref, hs_ref),
                          wglu, bglu_ref[...], wout, g2_ref[...])
        h1_s[...] = h1
        hn_s[...] = hn
        acc_s[...] = jnp.zeros_like(acc_s)

    wa = wa_ref[...].astype(BF16)
    wv = wv_ref[...].astype(BF16)
    wd = wd_ref[...].astype(BF16)
    wa_bf_ref[...] = wa
    wv_bf_ref[...] = wv
    wd_bf_ref[...] = wd
    hn = hn_s[...]
    a = _dot(hn, wa)
    v = _dot(hn, wv)
    cw = cw_ref[...]
    a_m, a_s = a[:n_meta], a[n_meta:]
    cv_m = _conv_taps(a_m, jnp.zeros((SUBLANES, FF_CHUNK), F32), cw)
    cv_s = cw[3:4] + cin_a_ref[...] * cw[0:1] + cin_b_ref[...] * cw[1:2] + a_s * cw[2:3]
    mcarry_ref[...] = a_m[n_meta - SUBLANES:]
    a_out_ref[...] = a_s
    cv = jnp.concatenate([cv_m, cv_s], axis=0)
    acc_s[...] += _dot(((cv * _sigmoid(cv)) * v).astype(BF16), wd)

    @pl.when(j == pl.num_programs(0) - 1)
    def _():
        out_ref[...] = _rms_norm((h1_s[...] + acc_s[...])[n_meta:], gf_ref[...])


def _ffn_small(y5m, y5s, yhm, yhs, hm, hs, wglu, bglu, wout, g2, wup, wd, cw, cin_a, cin_b, gf):
    n_meta, r = y5m.shape[1], y5s.shape[1]
    const = lambda shape: pl.BlockSpec(shape, lambda j: (0,) * len(shape))
    head = lambda rows, width: pl.BlockSpec((1, rows, width), lambda j: (0, 0, 0))
    ff_cols = lambda rows: pl.BlockSpec((rows, FF_CHUNK), lambda j: (0, j))
    sds = jax.ShapeDtypeStruct
    return pl.pallas_call(
        functools.partial(_ffn_small_kernel, n_meta=n_meta),
        grid=(N_FF_CHUNKS,),
        in_specs=[head(n_meta, S5_DIM), head(r, S5_DIM), head(n_meta, HG_DIM), head(r, HG_DIM),
                  head(n_meta, D_MODEL), head(r, D_MODEL),
                  const(wglu.shape), const(bglu.shape), const(wout.shape), const(g2.shape),
                  ff_cols(D_MODEL),
                  pl.BlockSpec((D_MODEL, FF_CHUNK), lambda j: (0, N_FF_CHUNKS + j)),
                  pl.BlockSpec((FF_CHUNK, D_MODEL), lambda j: (j, 0)),
                  ff_cols(SUBLANES), ff_cols(r), ff_cols(r), const(gf.shape)],
        out_specs=[const((r, D_MODEL)), ff_cols(r), ff_cols(SUBLANES),
                   const(wglu.shape), const(wout.shape), ff_cols(D_MODEL), ff_cols(D_MODEL),
                   pl.BlockSpec((FF_CHUNK, D_MODEL), lambda j: (j, 0))],
        out_shape=(sds((r, D_MODEL), F32), sds((r, D_FF), F32), sds((SUBLANES, D_FF), F32),
                   sds(wglu.shape, BF16), sds(wout.shape, BF16),
                   sds((D_MODEL, D_FF), BF16), sds((D_MODEL, D_FF), BF16), sds(wd.shape, BF16)),
        scratch_shapes=[pltpu.VMEM((n_meta + r, D_MODEL), BF16),
                        pltpu.VMEM((n_meta + r, D_MODEL), F32),
                        pltpu.VMEM((n_meta + r, D_MODEL), F32)],
        compiler_params=_params("arbitrary"),
        name="ffn_small",
    )(y5m, y5s, yhm, yhs, hm, hs, wglu, bglu, wout, g2, wup, wup, wd, cw, cin_a, cin_b, gf)


def _ffn(y5, yh, h, wglu, bglu, wout, g2, wa, wv, wd, cw, cin, gf, tm, n_tiles):
    n = h.shape[0]
    const = lambda shape: pl.BlockSpec(shape, lambda b, i: (0,) * len(shape),
                                       pipeline_mode=pl.Buffered(1))
    own = cin.shape[0] == n
    conv = lambda index: pl.BlockSpec((1, SUBLANES, D_FF), index)
    return pl.pallas_call(
        functools.partial(_ffn_kernel, tm=tm),
        grid=(n, n_tiles),
        in_specs=[pl.BlockSpec((1, tm, S5_DIM), lambda b, i: (b, i, 0)),
                  pl.BlockSpec((1, tm, HG_DIM), lambda b, i: (b, i, 0)),
                  pl.BlockSpec((1, tm, D_MODEL), lambda b, i: (b, i, 0)),
                  const(wglu.shape), const(bglu.shape), const(wout.shape), const(g2.shape),
                  const(wa.shape), const(wv.shape), const(wd.shape), const(cw.shape),
                  conv(lambda b, i: (b if own else 0, 0, 0)), const(gf.shape)],
        out_specs=[pl.BlockSpec((1, tm, D_MODEL), lambda b, i: (b, i, 0)),
                   conv(lambda b, i: (b, 0, 0))],
        out_shape=(jax.ShapeDtypeStruct((n, n_tiles * tm, D_MODEL), F32),
                   jax.ShapeDtypeStruct((n, SUBLANES, D_FF), F32)),
        scratch_shapes=[pltpu.VMEM((SUBLANES, D_FF), F32),
                        pltpu.VMEM((tm, D_MODEL), BF16),
                        pltpu.VMEM((tm, D_MODEL), F32),
                        pltpu.VMEM((tm, D_FF), BF16)],
        compiler_params=_params("arbitrary", "arbitrary"),
        name="mix_out_ffn",
    )(y5, yh, h, wglu, bglu, wout, g2, wa, wv, wd, cw, cin, gf)


def kernel(x_prompt, x_sample, state_s5_re, state_s5_im, state_hgrn, state_ffn_conv, meta_tokens, norm_mix_g, w_in, s5_lambda_re, s5_lambda_im, s5_log_dt, s5_b_re, s5_b_im, s5_c_re, s5_c_im, s5_d, s5_w_glu, s5_b_glu, hg_lower_bounds, hg_norm_g, w_out, norm_ffn_g, ffn_w_up, ffn_conv_w, ffn_conv_b, ffn_w_down, final_norm_g):
    nb, seq, _ = x_prompt.shape
    ns = x_sample.shape[0]
    li = 0

    a5, bb, lb = _prep(s5_lambda_re[li], s5_lambda_im[li], s5_log_dt[li],
                       s5_b_re[li], s5_b_im[li], hg_lower_bounds)
    a5 = a5.reshape(2, 1, S5_LANES)
    wb = _block_diag(bb)
    wc = _block_diag(jnp.stack([s5_c_re[li], -s5_c_im[li]]).transpose(0, 1, 3, 2))
    d5 = s5_d[li].reshape(1, S5_DIM)
    g1 = norm_mix_g[li].reshape(1, D_MODEL)
    g2 = norm_ffn_g[li].reshape(1, D_MODEL)
    gf = final_norm_g.reshape(1, D_MODEL)
    ng = hg_norm_g[li].reshape(1, HG_DIM)
    bglu = s5_b_glu[li].reshape(1, S5_DIM)
    w_in_b = w_in[li].astype(BF16)
    cw = jnp.concatenate([ffn_conv_w[li], ffn_conv_b[li][None],
                          jnp.zeros((SUBLANES - CONV_W - 1, D_FF), F32)], axis=0)
    s5w = (a5, wb, wc, d5)

    meta = jnp.zeros((1, HG_CHUNK, D_MODEL), F32).at[0, :N_META].set(meta_tokens)
    z5 = jnp.zeros((1, S5_LANES), F32)
    zh = jnp.zeros((1, HG_HEADS, HG_HEAD_DIM, HG_HEAD_DIM), F32)
    u_tm, zf, qig = _inproj(meta, g1, w_in_b, HG_CHUNK, 1)
    y5m, m5r, m5i = _s5(u_tm, z5, z5, *s5w, 1, N_META, 1)
    yhm, mhg = _hgrn(zf, qig, lb, ng, zh, HG_CHUNK, 1, N_META)

    xs = x_sample.reshape(1, ns, D_MODEL)
    u_tm, zf, qig = _inproj(xs, g1, w_in_b, ns, 1)
    y5s, s5r, s5i = _s5(u_tm, state_s5_re[li].reshape(ns, S5_LANES), state_s5_im[li].reshape(ns, S5_LANES),
                        *s5w, ns, 1, 1)
    yhs, shg = _hgrn_step(zf, qig, lb, ng, state_hgrn[li], 16)

    buf = state_ffn_conv[li]
    y_sample, a_new, mconv, wglu, wout, wa, wv, wd = _ffn_small(
        y5m, y5s, yhm, yhs, meta, xs, s5_w_glu[li], bglu, w_out[li], g2,
        ffn_w_up[li], ffn_w_down[li], cw, buf[:, 0], buf[:, 1], gf)
    sconv = jnp.stack([buf[:, 1], a_new], axis=1)

    ti, tm, tt, th = 512, 512, 128, 1024
    u_tm, zf, qig = _inproj(x_prompt, g1, w_in_b, ti, seq // ti)
    y5, p5r, p5i = _s5(u_tm, m5r, m5i, *s5w, nb, tt, seq // tt)
    yh, phg = _hgrn(zf, qig, lb, ng, mhg, th, seq // th, th)
    y_prompt, pconv = _ffn(y5, yh, x_prompt, wglu, bglu, wout, g2, wa, wv, wd, cw, mconv[None], gf,
                           tm, seq // tm)

    st5 = lambda t: t.reshape(1, -1, S5_GROUPS, S5_STATE)
    pconv = pconv[:, SUBLANES - (CONV_W - 1):, :]
    return (y_prompt, y_sample.reshape(ns, 1, D_MODEL),
            st5(p5r), st5(p5i), phg[None], pconv[None],
            st5(s5r), st5(s5i), shg[None], sconv[None])
```

```python
import functools

import jax
import jax.numpy as jnp
import numpy as np
from jax import lax
from jax.experimental import pallas as pl
from jax.experimental.pallas import tpu as pltpu

F32 = jnp.float32
BF16 = jnp.bfloat16

D_MODEL = 1024
N_META = 16
S5_DIM = 512
S5_GROUP = 16
S5_GROUPS = 32
S5_STATE = 64
S5_LANES = S5_GROUPS * S5_STATE
HG_DIM = 512
HG_HEAD_DIM = 128
HG_HEADS = 4
HG_CHUNK = 64
D_FF = 2816
CONV_W = 3
EPS = 1e-6

S5_HALVES = 2
S5_HALF_CH = S5_DIM // S5_HALVES
S5_HALF_ST = S5_LANES // S5_HALVES
SCAN_LANES = 512
FF_CHUNK = 256
N_FF_CHUNKS = D_FF // FF_CHUNK
SUBLANES = 8
LANES = 128
VMEM_LIMIT = 56 * 1024 * 1024


def _sigmoid(x):
    return 1.0 / (1.0 + jnp.exp(-x))


def _rms_norm(x, g):
    ms = jnp.mean(x * x, axis=-1, keepdims=True)
    return x * lax.rsqrt(ms + EPS) * g


def _dot(a, b):
    return jnp.dot(a, b, preferred_element_type=F32)


def _params(*sem):
    return pltpu.CompilerParams(dimension_semantics=sem, vmem_limit_bytes=VMEM_LIMIT)


def _prep_kernel(lam_ref, dt_ref, bt_ref, hlb_ref, a_ref, bb_ref, lb_ref):
    lam_re = lam_ref[0]
    lam_im = lam_ref[1]
    dt = jnp.exp(dt_ref[...])
    mag = jnp.exp(lam_re * dt)
    ar = mag * jnp.cos(lam_im * dt)
    ai = mag * jnp.sin(lam_im * dt)
    nr = ar - 1.0
    den = lam_re * lam_re + lam_im * lam_im
    cr = (nr * lam_re + ai * lam_im) / den
    ci = (ai * lam_re - nr * lam_im) / den
    a_ref[0] = ar
    a_ref[1] = ai
    bt_re = bt_ref[0]
    bt_im = bt_ref[1]
    bb_ref[0] = cr * bt_re - ci * bt_im
    bb_ref[1] = cr * bt_im + ci * bt_re
    hlb = hlb_ref[...]
    e = jnp.exp(hlb - jnp.max(hlb, axis=0, keepdims=True))
    lb_ref[...] = e[0:1] / jnp.sum(e, axis=0, keepdims=True)


def _prep(lam_re, lam_im, log_dt, b_re, b_im, hlb):
    g, p = lam_re.shape
    sds = jax.ShapeDtypeStruct
    return pl.pallas_call(
        _prep_kernel,
        out_shape=(sds((2, g, 1, p), F32), sds((2, g, S5_GROUP, p), F32), sds((1, HG_DIM), F32)),
        name="param_prep",
    )(jnp.stack([lam_re, lam_im]).reshape(2, g, 1, p),
      jnp.broadcast_to(log_dt.reshape(g, 1, 1), (g, 1, p)),
      jnp.stack([b_re, b_im]).transpose(0, 1, 3, 2), hlb)


def _block_diag(blocks):
    two, g, r, c = blocks.shape
    per = g // S5_HALVES
    eye = np.eye(per, dtype=np.float32)
    out = jnp.einsum("ajgrc,gh->ajgrhc", blocks.reshape(two, S5_HALVES, per, r, c), eye)
    return out.reshape(two, S5_HALVES, per * r, per * c).astype(BF16)


def _inproj_kernel(x_ref, g_ref, w_ref, u_ref, f_ref, qig_ref, *, n, tm):
    x = x_ref[0]
    hn = (x * g_ref[...]).astype(BF16)
    rn = lax.rsqrt(jnp.mean(x * x, axis=-1, keepdims=True) + EPS)
    rn = jnp.broadcast_to(rn, (tm, HG_DIM))
    col = lambda j: _dot(hn, w_ref[:, j * HG_DIM:(j + 1) * HG_DIM]) * rn
    b = pl.program_id(1)
    u = col(0)
    for l in range(S5_DIM // LANES):
        ul = u[:, l * LANES:(l + 1) * LANES]
        if n == 1:
            u_ref[l] = ul
        else:
            u_ref[l, pl.ds(b, tm, stride=n), :] = ul
    f_ref[0] = col(2)
    for j, src in enumerate((1, 3, 4)):
        qig_ref[0, :, j * HG_DIM:(j + 1) * HG_DIM] = col(src).astype(BF16)


def _inproj(x, g, w, tm, n_tiles):
    n, l, d = x.shape
    cols = w.shape[1]
    rows = tm * n_tiles
    const = lambda shape: pl.BlockSpec(shape, lambda i, b: (0,) * len(shape),
                                       pipeline_mode=pl.Buffered(1))
    return pl.pallas_call(
        functools.partial(_inproj_kernel, n=n, tm=tm),
        grid=(n_tiles, n),
        in_specs=[pl.BlockSpec((1, tm, d), lambda i, b: (b, i, 0)),
                  const((1, d)), const((d, cols))],
        out_specs=[pl.BlockSpec((S5_DIM // LANES, tm * n, LANES), lambda i, b: (0, i, 0)),
                   pl.BlockSpec((1, tm, HG_DIM), lambda i, b: (b, i, 0)),
                   pl.BlockSpec((1, tm, 3 * HG_DIM), lambda i, b: (b, i, 0))],
        out_shape=(jax.ShapeDtypeStruct((S5_DIM // LANES, rows * n, LANES), F32),
                   jax.ShapeDtypeStruct((n, rows, HG_DIM), F32),
                   jax.ShapeDtypeStruct((n, rows, 3 * HG_DIM), BF16)),
        compiler_params=_params("arbitrary", "arbitrary"),
        name="inproj",
    )(x, g, w)


def _gelu_tanh(y):
    return 0.5 * y * (1.0 + jnp.tanh(0.7978845608028654 * (y + 0.044715 * (y * y * y))))


def _s5_tile(step, u_ref, h0r_ref, h0i_ref, a_ref, wb_ref, wc_ref, d_ref,
             y_ref, hr_out_ref, hi_out_ref, xr_s, xi_s, hr_s, hi_s, y_s, *, n, tt, interleaved):
    @pl.when(step == 0)
    def _():
        hr_s[...] = jnp.broadcast_to(h0r_ref[...], hr_s.shape)
        hi_s[...] = jnp.broadcast_to(h0i_ref[...], hi_s.shape)

    slabs = S5_HALF_CH // LANES
    load_u = lambda j: jnp.concatenate([u_ref[j * slabs + l] for l in range(slabs)], axis=1)
    for j in range(S5_HALVES):
        st = slice(j * S5_HALF_ST, (j + 1) * S5_HALF_ST)
        ub = load_u(j).astype(BF16)
        xr_s[:, st] = _dot(ub, wb_ref[0, j])
        xi_s[:, st] = _dot(ub, wb_ref[1, j])
    for c in range(S5_LANES // SCAN_LANES):
        loc = slice(c * SCAN_LANES, (c + 1) * SCAN_LANES)
        ar = jnp.broadcast_to(a_ref[0, :, loc], (n, SCAN_LANES))
        ai = jnp.broadcast_to(a_ref[1, :, loc], (n, SCAN_LANES))

        def step(t, carry, ar=ar, ai=ai, loc=loc):
            hr, hi = carry
            r = 0 if tt == 1 else pl.multiple_of(t * n, n)
            nhr = ar * hr - ai * hi + xr_s[pl.ds(r, n), loc]
            nhi = ar * hi + ai * hr + xi_s[pl.ds(r, n), loc]
            xr_s[pl.ds(r, n), loc] = nhr
            xi_s[pl.ds(r, n), loc] = nhi
            return nhr, nhi

        carry = (hr_s[:, loc], hi_s[:, loc])
        if tt == 1:
            carry = step(0, carry)
        else:
            carry = lax.fori_loop(0, tt, step, carry, unroll=True)
        hr_s[:, loc] = carry[0]
        hi_s[:, loc] = carry[1]
    for j in range(S5_HALVES):
        ch = slice(j * S5_HALF_CH, (j + 1) * S5_HALF_CH)
        st = slice(j * S5_HALF_ST, (j + 1) * S5_HALF_ST)
        y = (_dot(xr_s[:, st].astype(BF16), wc_ref[0, j]) + _dot(xi_s[:, st].astype(BF16), wc_ref[1, j])
             + d_ref[:, ch] * load_u(j))
        if interleaved:
            for l in range(slabs):
                y_s[j * slabs + l] = y[:, l * LANES:(l + 1) * LANES]
        else:
            y_ref[0, :, ch] = y
    if interleaved:
        for b in range(n):
            for l in range(S5_DIM // LANES):
                y_ref[b, :, l * LANES:(l + 1) * LANES] = y_s[l, pl.ds(b, tt, stride=n), :]
    hr_out_ref[...] = hr_s[...]
    hi_out_ref[...] = hi_s[...]


def _s5_kernel(*refs, **static):
    _s5_tile(pl.program_id(0), *refs, **static)


def _s5_scratch(n, rows, interleaved):
    return [pltpu.VMEM((rows, S5_LANES), F32), pltpu.VMEM((rows, S5_LANES), F32),
            pltpu.VMEM((n, S5_LANES), F32), pltpu.VMEM((n, S5_LANES), F32),
            pltpu.VMEM((S5_DIM // LANES, rows if interleaved else SUBLANES, LANES), F32)]


def _s5(u4, h0r, h0i, a, wb, wc, d, n, tt, n_tiles):
    rows = tt * n
    interleaved = n > 1 and tt > 1
    const = lambda shape: pl.BlockSpec(shape, lambda i: (0,) * len(shape),
                                       pipeline_mode=pl.Buffered(1))
    state = pl.BlockSpec((n, S5_LANES), lambda i: (0, 0))
    y_block = (n, tt, S5_DIM) if interleaved else (1, rows, S5_DIM)
    y_shape = (n, tt * n_tiles, S5_DIM) if interleaved else (1, rows * n_tiles, S5_DIM)
    return pl.pallas_call(
        functools.partial(_s5_kernel, n=n, tt=tt, interleaved=interleaved),
        grid=(n_tiles,),
        in_specs=[pl.BlockSpec((S5_DIM // LANES, rows, LANES), lambda i: (0, i, 0)),
                  const(h0r.shape), const(h0i.shape), const(a.shape), const(wb.shape), const(wc.shape),
                  const((1, S5_DIM))],
        out_specs=[pl.BlockSpec(y_block, lambda i: (0, i, 0)), state, state],
        out_shape=(jax.ShapeDtypeStruct(y_shape, F32),
                   jax.ShapeDtypeStruct((n, S5_LANES), F32),
                   jax.ShapeDtypeStruct((n, S5_LANES), F32)),
        scratch_shapes=_s5_scratch(n, rows, interleaved),
        compiler_params=_params("arbitrary"),
        name="s5_scan",
    )(u4, h0r, h0i, a, wb, wc, d)


def _hgrn_gate_out(o, g, ng):
    parts = []
    for h in range(HG_HEADS):
        oh = o[:, h * HG_HEAD_DIM:(h + 1) * HG_HEAD_DIM]
        ms = jnp.mean(oh * oh, axis=-1, keepdims=True)
        parts.append(oh * lax.rsqrt(ms + EPS))
    return jnp.concatenate(parts, axis=-1) * ng * (g * _sigmoid(g))


def _hgrn_tile(row, q_ref, f_ref, i_ref, g_ref, lb_ref, ng_ref, tri_ref, y_ref, st_s, o_s, *, th, valid):
    c = HG_CHUNK
    lb = lb_ref[...]
    f = lb + (1.0 - lb) * _sigmoid(f_ref[row])
    lc = jnp.log(f)
    k = 1.0 - f
    q = q_ref[row].astype(F32)
    if valid < th:
        live = lax.broadcasted_iota(jnp.int32, (th, 1), 0) < valid
        lc = jnp.where(live, lc, 0.0)
        k = jnp.where(live, k, 0.0)
        q = jnp.where(live, q, 0.0)
    tri = tri_ref[...]
    tr = tri.shape[0]
    lc_hi = lc.astype(BF16)
    lc_lo = (lc - lc_hi.astype(F32)).astype(BF16)
    b = jnp.concatenate([_dot(tri, lc_hi[r:r + tr]) + _dot(tri, lc_lo[r:r + tr])
                         for r in range(0, th, tr)], axis=0)
    qd = (q * jnp.exp(b)).astype(BF16)
    kd = k * jnp.exp(-b)
    kdb = kd.astype(BF16)
    vb = i_ref[row]
    v = vb.astype(F32)
    causal = (lax.broadcasted_iota(jnp.int32, (c, c), 1) <= lax.broadcasted_iota(jnp.int32, (c, c), 0))
    nt = (((1,), (1,)), ((), ()))
    n_chunks = th // c
    rows = [slice(cc * c, (cc + 1) * c) for cc in range(n_chunks)]
    lanes = [slice(h * HG_HEAD_DIM, (h + 1) * HG_HEAD_DIM) for h in range(HG_HEADS)]
    dec = [jnp.exp(b[cc * c + c - 1:cc * c + c, :]) for cc in range(n_chunks)]
    kdec = [(kd[rows[cc]] * dec[cc]).astype(BF16) for cc in range(n_chunks)]
    att = [[jnp.where(causal, lax.dot_general(qd[rows[cc], ls], kdb[rows[cc], ls], nt,
                                              preferred_element_type=F32), 0.0).astype(BF16)
            for ls in lanes] for cc in range(n_chunks)]
    upd = [[_dot(v[rows[cc], ls].T.astype(BF16), kdec[cc][:, ls])
            for ls in lanes] for cc in range(n_chunks)]
    st_in = [[None] * HG_HEADS for _ in range(n_chunks)]
    for h, ls in enumerate(lanes):
        st = st_s[row, h]
        for cc in range(n_chunks):
            st_in[cc][h] = st.astype(BF16)
            st = dec[cc][:, ls] * st + upd[cc][h]
        st_s[row, h] = st
    for cc in range(n_chunks):
        for h, ls in enumerate(lanes):
            o_s[row, rows[cc], ls] = (
                lax.dot_general(qd[rows[cc], ls], st_in[cc][h], nt, preferred_element_type=F32)
                + _dot(att[cc][h], vb[rows[cc], ls]))
    y_ref[row] = _hgrn_gate_out(o_s[row], g_ref[row].astype(F32), ng_ref[...])


def _hgrn_load_state(s0_ref, st_s):
    for r in range(st_s.shape[0]):
        for h in range(HG_HEADS):
            st_s[r, h] = s0_ref[r if s0_ref.shape[0] > 1 else 0, h].T


def _hgrn_store_state(s_out_ref, st_s):
    for r in range(st_s.shape[0]):
        for h in range(HG_HEADS):
            s_out_ref[r, h] = st_s[r, h].T


def _hgrn_kernel(q_ref, f_ref, i_ref, g_ref, lb_ref, ng_ref, tri_ref, s0_ref,
                 y_ref, s_out_ref, st_s, o_s, *, th, valid):
    i = pl.program_id(1)
    pl.when(i == 0)(lambda: _hgrn_load_state(s0_ref, st_s))
    _hgrn_tile(0, q_ref, f_ref, i_ref, g_ref, lb_ref, ng_ref, tri_ref, y_ref, st_s, o_s,
               th=th, valid=valid)
    pl.when(i == pl.num_programs(1) - 1)(lambda: _hgrn_store_state(s_out_ref, st_s))


def _hgrn_tri(tr):
    idx = np.arange(tr)
    tri = (idx[:, None] // HG_CHUNK == idx[None, :] // HG_CHUNK) & (idx[None, :] <= idx[:, None])
    return jnp.asarray(tri, dtype=BF16)


def _hgrn(zf, qig, lb, ng, s0, th, n_tiles, valid):
    n = zf.shape[0]
    own = s0.shape[0] == n
    tr = min(th, 256)
    col = lambda j: pl.BlockSpec((1, th, HG_DIM), lambda b, i, j=j: (b, i, j))
    const = lambda shape: pl.BlockSpec(shape, lambda b, i: (0,) * len(shape),
                                       pipeline_mode=pl.Buffered(1))
    st = pl.BlockSpec((1, HG_HEADS, HG_HEAD_DIM, HG_HEAD_DIM), lambda b, i: (b, 0, 0, 0))
    st_in = pl.BlockSpec((1, HG_HEADS, HG_HEAD_DIM, HG_HEAD_DIM),
                         lambda b, i: (b if own else 0, 0, 0, 0))
    return pl.pallas_call(
        functools.partial(_hgrn_kernel, th=th, valid=valid),
        grid=(n, n_tiles),
        in_specs=[col(0), col(0), col(1), col(2), const((1, HG_DIM)), const((1, HG_DIM)),
                  const((tr, tr)), st_in],
        out_specs=[pl.BlockSpec((1, th, HG_DIM), lambda b, i: (b, i, 0)), st],
        out_shape=(jax.ShapeDtypeStruct((n, n_tiles * th, HG_DIM), F32),
                   jax.ShapeDtypeStruct((n, HG_HEADS, HG_HEAD_DIM, HG_HEAD_DIM), F32)),
        scratch_shapes=[pltpu.VMEM((1, HG_HEADS, HG_HEAD_DIM, HG_HEAD_DIM), F32),
                        pltpu.VMEM((1, th, HG_DIM), F32)],
        compiler_params=_params("arbitrary", "arbitrary"),
        name="hgrn_chunks",
    )(qig, zf, qig, qig, lb, ng, _hgrn_tri(tr), s0)


def _hgrn_step_kernel(q_ref, f_ref, i_ref, g_ref, lb_ref, ng_ref, s0_ref,
                      y_ref, s_out_ref, o_s, *, sb):
    lb = lb_ref[...]
    f = lb + (1.0 - lb) * _sigmoid(f_ref[0])
    q = q_ref[0].astype(F32)
    v = i_ref[0].astype(F32)
    pad = jnp.zeros((HG_HEAD_DIM - sb, HG_HEAD_DIM), F32)
    sq = (HG_HEAD_DIM, HG_HEAD_DIM)
    for h in range(HG_HEADS):
        ls = slice(h * HG_HEAD_DIM, (h + 1) * HG_HEAD_DIM)
        fcols = jnp.concatenate([f[:, ls], pad], axis=0).T
        for s in range(sb):
            fc = jnp.broadcast_to(fcols[:, s:s + 1], sq)
            sn = fc * s0_ref[s, h] + (1.0 - fc) * v[s:s + 1, ls]
            s_out_ref[s, h] = sn
            o_s[s:s + 1, ls] = _dot(q[s:s + 1, ls].astype(BF16), sn.astype(BF16))
    y_ref[0] = _hgrn_gate_out(o_s[...], g_ref[0].astype(F32), ng_ref[...])


def _hgrn_step(zf, qig, lb, ng, s0, sb):
    r = zf.shape[1]
    col = lambda j: pl.BlockSpec((1, sb, HG_DIM), lambda i, j=j: (0, i, j))
    vec = pl.BlockSpec((1, HG_DIM), lambda i: (0, 0))
    st = pl.BlockSpec((sb, HG_HEADS, HG_HEAD_DIM, HG_HEAD_DIM), lambda i: (i, 0, 0, 0))
    return pl.pallas_call(
        functools.partial(_hgrn_step_kernel, sb=sb),
        grid=(r // sb,),
        in_specs=[col(0), col(0), col(1), col(2), vec, vec, st],
        out_specs=[pl.BlockSpec((1, sb, HG_DIM), lambda i: (0, i, 0)), st],
        out_shape=(jax.ShapeDtypeStruct((1, r, HG_DIM), F32),
                   jax.ShapeDtypeStruct(s0.shape, F32)),
        scratch_shapes=[pltpu.VMEM((sb, HG_DIM), F32)],
        compiler_params=_params("arbitrary"),
        name="hgrn_step",
    )(qig, zf, qig, qig, lb, ng, s0)


def _mix_out(y5_raw, yh, h, wglu, bglu, wout):
    y5p = _gelu_tanh(y5_raw)
    y5 = y5p * _sigmoid(_dot(y5p.astype(BF16), wglu) + bglu)
    ymix = jnp.concatenate([y5, yh], axis=-1).astype(BF16)
    return h + _dot(ymix, wout)


def _conv_taps(a, prev8, cw):
    rows = a.shape[0]
    cv = cw[3:4] + pltpu.roll(a, 2, 0) * cw[0:1] + pltpu.roll(a, 1, 0) * cw[1:2] + a * cw[2:3]
    d = prev8 - a[rows - SUBLANES:rows]
    rid = lax.broadcasted_iota(jnp.int32, (SUBLANES, 1), 0)
    fix = (jnp.where(rid < 1, pltpu.roll(d, 1, 0), 0.0) * cw[1:2]
           + jnp.where(rid < 2, pltpu.roll(d, 2, 0), 0.0) * cw[0:1])
    return jnp.concatenate([cv[:SUBLANES] + fix, cv[SUBLANES:]], axis=0)


def _ffn_kernel(y5_ref, yh_ref, h_ref, wglu_ref, bglu_ref, wout_ref, g2_ref,
                wa_ref, wv_ref, wd_ref, cw_ref, cin_ref, gf_ref,
                out_ref, cout_ref, carry_s, hn_s, h1_s, rn_s, s_s, *, tm):
    @pl.when(pl.program_id(1) == 0)
    def _():
        carry_s[...] = cin_ref[0]

    h1 = _mix_out(y5_ref[0], yh_ref[0], h_ref[0], wglu_ref[...], bglu_ref[...], wout_ref[...])
    h1_s[...] = h1
    hn_s[...] = (h1 * g2_ref[...]).astype(BF16)
    rn = lax.rsqrt(jnp.mean(h1 * h1, axis=-1, keepdims=True) + EPS)
    rn_s[...] = jnp.broadcast_to(rn, (tm, FF_CHUNK))

    for j in range(N_FF_CHUNKS):
        cs = slice(j * FF_CHUNK, (j + 1) * FF_CHUNK)
        hn = hn_s[...]
        a = _dot(hn, wa_ref[:, cs]) * rn_s[...]
        v = _dot(hn, wv_ref[:, cs]) * rn_s[...]
        cv = _conv_taps(a, carry_s[:, cs], cw_ref[:, cs])
        carry_s[:, cs] = a[tm - SUBLANES:tm]
        s_s[:, cs] = ((cv * _sigmoid(cv)) * v).astype(BF16)

    out_ref[0] = _rms_norm(h1_s[...] + _dot(s_s[...], wd_ref[...]), gf_ref[...])
    cout_ref[0] = carry_s[...]


def _ffn_small_kernel(y5m_ref, y5s_ref, yhm_ref, yhs_ref, hm_ref, hs_ref,
                      wglu_ref, bglu_ref, wout_ref, g2_ref, wa_ref, wv_ref, wd_ref, cw_ref,
                      cin_a_ref, cin_b_ref, gf_ref,
                      out_ref, a_out_ref, mcarry_ref,
                      wglu_bf_ref, wout_bf_ref, wa_bf_ref, wv_bf_ref, wd_bf_ref,
                      hn_s, h1_s, acc_s, *, n_meta):
    j = pl.program_id(0)

    @pl.when(j == 0)
    def _():
        wglu = wglu_ref[...].astype(BF16)
        wout = wout_ref[...].astype(BF16)
        wglu_bf_ref[...] = wglu
        wout_bf_ref[...] = wout
        rows = lambda m_ref, s_ref: jnp.concatenate([m_ref[0], s_ref[0]], axis=0)
        h1 = _mix_out(rows(y5m_ref, y5s_ref), rows(yhm_ref, yhs_ref), rows(hm_ref, hs_ref),
                      wglu, bglu_ref[...], wout)
        h1_s[...] = h1
        hn_s[...] = _rms_norm(h1, g2_ref[...]).astype(BF16)
        acc_s[...] = jnp.zeros_like(acc_s)

    wa = wa_ref[...].astype(BF16)
    wv = wv_ref[...].astype(BF16)
    wd = wd_ref[...].astype(BF16)
    wa_bf_ref[...] = wa
    wv_bf_ref[...] = wv
    wd_bf_ref[...] = wd
    hn = hn_s[...]
    a = _dot(hn, wa)
    v = _dot(hn, wv)
    cw = cw_ref[...]
    a_m, a_s = a[:n_meta], a[n_meta:]
    cv_m = _conv_taps(a_m, jnp.zeros((SUBLANES, FF_CHUNK), F32), cw)
    cv_s = cw[3:4] + cin_a_ref[...] * cw[0:1] + cin_b_ref[...] * cw[1:2] + a_s * cw[2:3]
    mcarry_ref[...] = a_m[n_meta - SUBLANES:]
    a_out_ref[...] = a_s
    cv = jnp.concatenate([cv_m, cv_s], axis=0)
    acc_s[...] += _dot(((cv * _sigmoid(cv)) * v).astype(BF16), wd)

    @pl.when(j == pl.num_programs(0) - 1)
    def _():
        out_ref[...] = _rms_norm((h1_s[...] + acc_s[...])[n_meta:], gf_ref[...])


def _ffn_small(y5m, y5s, yhm, yhs, hm, hs, wglu, bglu, wout, g2, wup, wd, cw, cin_a, cin_b, gf):
    n_meta, r = y5m.shape[1], y5s.shape[1]
    const = lambda shape: pl.BlockSpec(shape, lambda j: (0,) * len(shape))
    head = lambda rows, width: pl.BlockSpec((1, rows, width), lambda j: (0, 0, 0))
    ff_cols = lambda rows: pl.BlockSpec((rows, FF_CHUNK), lambda j: (0, j))
    sds = jax.ShapeDtypeStruct
    return pl.pallas_call(
        functools.partial(_ffn_small_kernel, n_meta=n_meta),
        grid=(N_FF_CHUNKS,),
        in_specs=[head(n_meta, S5_DIM), head(r, S5_DIM), head(n_meta, HG_DIM), head(r, HG_DIM),
                  head(n_meta, D_MODEL), head(r, D_MODEL),
                  const(wglu.shape), const(bglu.shape), const(wout.shape), const(g2.shape),
                  ff_cols(D_MODEL),
                  pl.BlockSpec((D_MODEL, FF_CHUNK), lambda j: (0, N_FF_CHUNKS + j)),
                  pl.BlockSpec((FF_CHUNK, D_MODEL), lambda j: (j, 0)),
                  ff_cols(SUBLANES), ff_cols(r), ff_cols(r), const(gf.shape)],
        out_specs=[const((r, D_MODEL)), ff_cols(r), ff_cols(SUBLANES),
                   const(wglu.shape), const(wout.shape), ff_cols(D_MODEL), ff_cols(D_MODEL),
                   pl.BlockSpec((FF_CHUNK, D_MODEL), lambda j: (j, 0))],
        out_shape=(sds((r, D_MODEL), F32), sds((r, D_FF), F32), sds((SUBLANES, D_FF), F32),
                   sds(wglu.shape, BF16), sds(wout.shape, BF16),
                   sds((D_MODEL, D_FF), BF16), sds((D_MODEL, D_FF), BF16), sds(wd.shape, BF16)),
        scratch_shapes=[pltpu.VMEM((n_meta + r, D_MODEL), BF16),
                        pltpu.VMEM((n_meta + r, D_MODEL), F32),
                        pltpu.VMEM((n_meta + r, D_MODEL), F32)],
        compiler_params=_params("arbitrary"),
        name="ffn_small",
    )(y5m, y5s, yhm, yhs, hm, hs, wglu, bglu, wout, g2, wup, wup, wd, cw, cin_a, cin_b, gf)


def _ffn(y5, yh, h, wglu, bglu, wout, g2, wa, wv, wd, cw, cin, gf, tm, n_tiles):
    n = h.shape[0]
    const = lambda shape: pl.BlockSpec(shape, lambda b, i: (0,) * len(shape),
                                       pipeline_mode=pl.Buffered(1))
    own = cin.shape[0] == n
    conv = lambda index: pl.BlockSpec((1, SUBLANES, D_FF), index)
    return pl.pallas_call(
        functools.partial(_ffn_kernel, tm=tm),
        grid=(n, n_tiles),
        in_specs=[pl.BlockSpec((1, tm, S5_DIM), lambda b, i: (b, i, 0)),
                  pl.BlockSpec((1, tm, HG_DIM), lambda b, i: (b, i, 0)),
                  pl.BlockSpec((1, tm, D_MODEL), lambda b, i: (b, i, 0)),
                  const(wglu.shape), const(bglu.shape), const(wout.shape), const(g2.shape),
                  const(wa.shape), const(wv.shape), const(wd.shape), const(cw.shape),
                  conv(lambda b, i: (b if own else 0, 0, 0)), const(gf.shape)],
        out_specs=[pl.BlockSpec((1, tm, D_MODEL), lambda b, i: (b, i, 0)),
                   conv(lambda b, i: (b, 0, 0))],
        out_shape=(jax.ShapeDtypeStruct((n, n_tiles * tm, D_MODEL), F32),
                   jax.ShapeDtypeStruct((n, SUBLANES, D_FF), F32)),
        scratch_shapes=[pltpu.VMEM((SUBLANES, D_FF), F32),
                        pltpu.VMEM((tm, D_MODEL), BF16),
                        pltpu.VMEM((tm, D_MODEL), F32),
                        pltpu.VMEM((tm, FF_CHUNK), F32),
                        pltpu.VMEM((tm, D_FF), BF16)],
        compiler_params=_params("arbitrary", "arbitrary"),
        name="mix_out_ffn",
    )(y5, yh, h, wglu, bglu, wout, g2, wa, wv, wd, cw, cin, gf)


def kernel(x_prompt, x_sample, state_s5_re, state_s5_im, state_hgrn, state_ffn_conv, meta_tokens, norm_mix_g, w_in, s5_lambda_re, s5_lambda_im, s5_log_dt, s5_b_re, s5_b_im, s5_c_re, s5_c_im, s5_d, s5_w_glu, s5_b_glu, hg_lower_bounds, hg_norm_g, w_out, norm_ffn_g, ffn_w_up, ffn_conv_w, ffn_conv_b, ffn_w_down, final_norm_g):
    nb, seq, _ = x_prompt.shape
    ns = x_sample.shape[0]
    li = 0

    a5, bb, lb = _prep(s5_lambda_re[li], s5_lambda_im[li], s5_log_dt[li],
                       s5_b_re[li], s5_b_im[li], hg_lower_bounds)
    a5 = a5.reshape(2, 1, S5_LANES)
    wb = _block_diag(bb)
    wc = _block_diag(jnp.stack([s5_c_re[li], -s5_c_im[li]]).transpose(0, 1, 3, 2))
    d5 = s5_d[li].reshape(1, S5_DIM)
    g1 = norm_mix_g[li].reshape(1, D_MODEL)
    g2 = norm_ffn_g[li].reshape(1, D_MODEL)
    gf = final_norm_g.reshape(1, D_MODEL)
    ng = hg_norm_g[li].reshape(1, HG_DIM)
    bglu = s5_b_glu[li].reshape(1, S5_DIM)
    w_in_b = w_in[li].astype(BF16)
    cw = jnp.concatenate([ffn_conv_w[li], ffn_conv_b[li][None],
                          jnp.zeros((SUBLANES - CONV_W - 1, D_FF), F32)], axis=0)
    s5w = (a5, wb, wc, d5)

    meta = jnp.zeros((1, HG_CHUNK, D_MODEL), F32).at[0, :N_META].set(meta_tokens)
    z5 = jnp.zeros((1, S5_LANES), F32)
    zh = jnp.zeros((1, HG_HEADS, HG_HEAD_DIM, HG_HEAD_DIM), F32)
    u_tm, zf, qig = _inproj(meta, g1, w_in_b, HG_CHUNK, 1)
    y5m, m5r, m5i = _s5(u_tm, z5, z5, *s5w, 1, N_META, 1)
    yhm, mhg = _hgrn(zf, qig, lb, ng, zh, HG_CHUNK, 1, N_META)

    xs = x_sample.reshape(1, ns, D_MODEL)
    u_tm, zf, qig = _inproj(xs, g1, w_in_b, ns, 1)
    y5s, s5r, s5i = _s5(u_tm, state_s5_re[li].reshape(ns, S5_LANES), state_s5_im[li].reshape(ns, S5_LANES),
                        *s5w, ns, 1, 1)
    yhs, shg = _hgrn_step(zf, qig, lb, ng, state_hgrn[li], 16)

    buf = state_ffn_conv[li]
    y_sample, a_new, mconv, wglu, wout, wa, wv, wd = _ffn_small(
        y5m, y5s, yhm, yhs, meta, xs, s5_w_glu[li], bglu, w_out[li], g2,
        ffn_w_up[li], ffn_w_down[li], cw, buf[:, 0], buf[:, 1], gf)
    sconv = jnp.stack([buf[:, 1], a_new], axis=1)

    ti, tm, tt, th = 512, 512, 128, 1024
    u_tm, zf, qig = _inproj(x_prompt, g1, w_in_b, ti, seq // ti)
    y5, p5r, p5i = _s5(u_tm, m5r, m5i, *s5w, nb, tt, seq // tt)
    yh, phg = _hgrn(zf, qig, lb, ng, mhg, th, seq // th, th)
    y_prompt, pconv = _ffn(y5, yh, x_prompt, wglu, bglu, wout, g2, wa, wv, wd, cw, mconv[None], gf,
                           tm, seq // tm)

    st5 = lambda t: t.reshape(1, -1, S5_GROUPS, S5_STATE)
    pconv = pconv[:, SUBLANES - (CONV_W - 1):, :]
    return (y_prompt, y_sample.reshape(ns, 1, D_MODEL),
            st5(p5r), st5(p5i), phg[None], pconv[None],
            st5(s5r), st5(s5i), shg[None], sconv[None])
```

```python
import functools

import jax
import jax.numpy as jnp
import numpy as np
from jax import lax
from jax.experimental import pallas as pl
from jax.experimental.pallas import tpu as pltpu

F32 = jnp.float32
BF16 = jnp.bfloat16

D_MODEL = 1024
N_META = 16
S5_DIM = 512
S5_GROUP = 16
S5_GROUPS = 32
S5_STATE = 64
S5_LANES = S5_GROUPS * S5_STATE
HG_DIM = 512
HG_HEAD_DIM = 128
HG_HEADS = 4
HG_CHUNK = 64
D_FF = 2816
CONV_W = 3
EPS = 1e-6

S5_HALVES = 2
S5_HALF_CH = S5_DIM // S5_HALVES
S5_HALF_ST = S5_LANES // S5_HALVES
SCAN_LANES = 512
FF_CHUNK = 256
N_FF_CHUNKS = D_FF // FF_CHUNK
SUBLANES = 8
LANES = 128
VMEM_LIMIT = 56 * 1024 * 1024


def _sigmoid(x):
    return 0.5 + 0.5 * jnp.tanh(0.5 * x)


def _rms_norm(x, g):
    ms = jnp.mean(x * x, axis=-1, keepdims=True)
    return x * lax.rsqrt(ms + EPS) * g


def _dot(a, b):
    return jnp.dot(a, b, preferred_element_type=F32)


def _params(*sem):
    return pltpu.CompilerParams(dimension_semantics=sem, vmem_limit_bytes=VMEM_LIMIT)


def _prep_kernel(lam_ref, dt_ref, bt_ref, hlb_ref, a_ref, bb_ref, lb_ref):
    lam_re = lam_ref[0]
    lam_im = lam_ref[1]
    dt = jnp.exp(dt_ref[...])
    mag = jnp.exp(lam_re * dt)
    ar = mag * jnp.cos(lam_im * dt)
    ai = mag * jnp.sin(lam_im * dt)
    nr = ar - 1.0
    den = lam_re * lam_re + lam_im * lam_im
    cr = (nr * lam_re + ai * lam_im) / den
    ci = (ai * lam_re - nr * lam_im) / den
    a_ref[0] = ar
    a_ref[1] = ai
    bt_re = bt_ref[0]
    bt_im = bt_ref[1]
    bb_ref[0] = cr * bt_re - ci * bt_im
    bb_ref[1] = cr * bt_im + ci * bt_re
    hlb = hlb_ref[...]
    e = jnp.exp(hlb - jnp.max(hlb, axis=0, keepdims=True))
    lb_ref[...] = e[0:1] / jnp.sum(e, axis=0, keepdims=True)


def _prep(lam_re, lam_im, log_dt, b_re, b_im, hlb):
    g, p = lam_re.shape
    sds = jax.ShapeDtypeStruct
    return pl.pallas_call(
        _prep_kernel,
        out_shape=(sds((2, g, 1, p), F32), sds((2, g, S5_GROUP, p), F32), sds((1, HG_DIM), F32)),
        name="param_prep",
    )(jnp.stack([lam_re, lam_im]).reshape(2, g, 1, p),
      jnp.broadcast_to(log_dt.reshape(g, 1, 1), (g, 1, p)),
      jnp.stack([b_re, b_im]).transpose(0, 1, 3, 2), hlb)


def _block_diag(blocks):
    two, g, r, c = blocks.shape
    per = g // S5_HALVES
    eye = np.eye(per, dtype=np.float32)
    out = jnp.einsum("ajgrc,gh->ajgrhc", blocks.reshape(two, S5_HALVES, per, r, c), eye)
    return out.reshape(two, S5_HALVES, per * r, per * c).astype(BF16)


def _inproj_kernel(x_ref, g_ref, w_ref, u_ref, f_ref, qig_ref, *, n, tm):
    x = x_ref[0]
    hn = (x * g_ref[...]).astype(BF16)
    rn = lax.rsqrt(jnp.mean(x * x, axis=-1, keepdims=True) + EPS)
    rn = jnp.broadcast_to(rn, (tm, HG_DIM))
    col = lambda j: _dot(hn, w_ref[:, j * HG_DIM:(j + 1) * HG_DIM]) * rn
    b = pl.program_id(1)
    u = col(0)
    for l in range(S5_DIM // LANES):
        ul = u[:, l * LANES:(l + 1) * LANES]
        if n == 1:
            u_ref[l] = ul
        else:
            u_ref[l, pl.ds(b, tm, stride=n), :] = ul
    f_ref[0] = col(2)
    for j, src in enumerate((1, 3, 4)):
        qig_ref[0, :, j * HG_DIM:(j + 1) * HG_DIM] = col(src).astype(BF16)


def _inproj(x, g, w, tm, n_tiles):
    n, l, d = x.shape
    cols = w.shape[1]
    rows = tm * n_tiles
    const = lambda shape: pl.BlockSpec(shape, lambda i, b: (0,) * len(shape),
                                       pipeline_mode=pl.Buffered(1))
    return pl.pallas_call(
        functools.partial(_inproj_kernel, n=n, tm=tm),
        grid=(n_tiles, n),
        in_specs=[pl.BlockSpec((1, tm, d), lambda i, b: (b, i, 0)),
                  const((1, d)), const((d, cols))],
        out_specs=[pl.BlockSpec((S5_DIM // LANES, tm * n, LANES), lambda i, b: (0, i, 0)),
                   pl.BlockSpec((1, tm, HG_DIM), lambda i, b: (b, i, 0)),
                   pl.BlockSpec((1, tm, 3 * HG_DIM), lambda i, b: (b, i, 0))],
        out_shape=(jax.ShapeDtypeStruct((S5_DIM // LANES, rows * n, LANES), F32),
                   jax.ShapeDtypeStruct((n, rows, HG_DIM), F32),
                   jax.ShapeDtypeStruct((n, rows, 3 * HG_DIM), BF16)),
        compiler_params=_params("arbitrary", "arbitrary"),
        name="inproj",
    )(x, g, w)


def _gelu_tanh(y):
    return 0.5 * y * (1.0 + jnp.tanh(0.7978845608028654 * (y + 0.044715 * (y * y * y))))


def _s5_tile(step, u_ref, h0r_ref, h0i_ref, a_ref, wb_ref, wc_ref, d_ref,
             y_ref, hr_out_ref, hi_out_ref, xr_s, xi_s, hr_s, hi_s, y_s, *, n, tt, interleaved):
    @pl.when(step == 0)
    def _():
        hr_s[...] = jnp.broadcast_to(h0r_ref[...], hr_s.shape)
        hi_s[...] = jnp.broadcast_to(h0i_ref[...], hi_s.shape)

    slabs = S5_HALF_CH // LANES
    load_u = lambda j: jnp.concatenate([u_ref[j * slabs + l] for l in range(slabs)], axis=1)
    for j in range(S5_HALVES):
        st = slice(j * S5_HALF_ST, (j + 1) * S5_HALF_ST)
        ub = load_u(j).astype(BF16)
        xr_s[:, st] = _dot(ub, wb_ref[0, j])
        xi_s[:, st] = _dot(ub, wb_ref[1, j])
    for c in range(S5_LANES // SCAN_LANES):
        loc = slice(c * SCAN_LANES, (c + 1) * SCAN_LANES)
        ar = jnp.broadcast_to(a_ref[0, :, loc], (n, SCAN_LANES))
        ai = jnp.broadcast_to(a_ref[1, :, loc], (n, SCAN_LANES))

        def step(t, carry, ar=ar, ai=ai, loc=loc):
            hr, hi = carry
            r = 0 if tt == 1 else pl.multiple_of(t * n, n)
            nhr = ar * hr - ai * hi + xr_s[pl.ds(r, n), loc]
            nhi = ar * hi + ai * hr + xi_s[pl.ds(r, n), loc]
            xr_s[pl.ds(r, n), loc] = nhr
            xi_s[pl.ds(r, n), loc] = nhi
            return nhr, nhi

        carry = (hr_s[:, loc], hi_s[:, loc])
        if tt == 1:
            carry = step(0, carry)
        else:
            carry = lax.fori_loop(0, tt, step, carry, unroll=True)
        hr_s[:, loc] = carry[0]
        hi_s[:, loc] = carry[1]
    for j in range(S5_HALVES):
        ch = slice(j * S5_HALF_CH, (j + 1) * S5_HALF_CH)
        st = slice(j * S5_HALF_ST, (j + 1) * S5_HALF_ST)
        y = (_dot(xr_s[:, st].astype(BF16), wc_ref[0, j]) + _dot(xi_s[:, st].astype(BF16), wc_ref[1, j])
             + d_ref[:, ch] * load_u(j))
        if interleaved:
            for l in range(slabs):
                y_s[j * slabs + l] = y[:, l * LANES:(l + 1) * LANES]
        else:
            y_ref[0, :, ch] = y
    if interleaved:
        for b in range(n):
            for l in range(S5_DIM // LANES):
                y_ref[b, :, l * LANES:(l + 1) * LANES] = y_s[l, pl.ds(b, tt, stride=n), :]
    hr_out_ref[...] = hr_s[...]
    hi_out_ref[...] = hi_s[...]


def _s5_kernel(*refs, **static):
    _s5_tile(pl.program_id(0), *refs, **static)


def _s5_scratch(n, rows, interleaved):
    return [pltpu.VMEM((rows, S5_LANES), F32), pltpu.VMEM((rows, S5_LANES), F32),
            pltpu.VMEM((n, S5_LANES), F32), pltpu.VMEM((n, S5_LANES), F32),
            pltpu.VMEM((S5_DIM // LANES, rows if interleaved else SUBLANES, LANES), F32)]


def _s5(u4, h0r, h0i, a, wb, wc, d, n, tt, n_tiles):
    rows = tt * n
    interleaved = n > 1 and tt > 1
    const = lambda shape: pl.BlockSpec(shape, lambda i: (0,) * len(shape),
                                       pipeline_mode=pl.Buffered(1))
    state = pl.BlockSpec((n, S5_LANES), lambda i: (0, 0))
    y_block = (n, tt, S5_DIM) if interleaved else (1, rows, S5_DIM)
    y_shape = (n, tt * n_tiles, S5_DIM) if interleaved else (1, rows * n_tiles, S5_DIM)
    return pl.pallas_call(
        functools.partial(_s5_kernel, n=n, tt=tt, interleaved=interleaved),
        grid=(n_tiles,),
        in_specs=[pl.BlockSpec((S5_DIM // LANES, rows, LANES), lambda i: (0, i, 0)),
                  const(h0r.shape), const(h0i.shape), const(a.shape), const(wb.shape), const(wc.shape),
                  const((1, S5_DIM))],
        out_specs=[pl.BlockSpec(y_block, lambda i: (0, i, 0)), state, state],
        out_shape=(jax.ShapeDtypeStruct(y_shape, F32),
                   jax.ShapeDtypeStruct((n, S5_LANES), F32),
                   jax.ShapeDtypeStruct((n, S5_LANES), F32)),
        scratch_shapes=_s5_scratch(n, rows, interleaved),
        compiler_params=_params("arbitrary"),
        name="s5_scan",
    )(u4, h0r, h0i, a, wb, wc, d)


def _hgrn_gate_out(o, g, ng):
    parts = []
    for h in range(HG_HEADS):
        oh = o[:, h * HG_HEAD_DIM:(h + 1) * HG_HEAD_DIM]
        ms = jnp.mean(oh * oh, axis=-1, keepdims=True)
        parts.append(oh * lax.rsqrt(ms + EPS))
    return jnp.concatenate(parts, axis=-1) * ng * (g * _sigmoid(g))


def _hgrn_tile(row, q_ref, f_ref, i_ref, g_ref, lb_ref, ng_ref, tri_ref, y_ref, st_s, o_s, *, th, valid):
    c = HG_CHUNK
    lb = lb_ref[...]
    f = lb + (1.0 - lb) * _sigmoid(f_ref[row])
    lc = jnp.log(f)
    k = 1.0 - f
    q = q_ref[row]
    if valid < th:
        live = lax.broadcasted_iota(jnp.int32, (th, 1), 0) < valid
        lc = jnp.where(live, lc, 0.0)
        k = jnp.where(live, k, 0.0)
        q = jnp.where(live, q.astype(F32), 0.0).astype(BF16)
    tri = tri_ref[...]
    tr = tri.shape[0]
    lc_hi = lc.astype(BF16)
    lc_lo = (lc - lc_hi.astype(F32)).astype(BF16)
    b = jnp.concatenate([_dot(tri, lc_hi[r:r + tr]) + _dot(tri, lc_lo[r:r + tr])
                         for r in range(0, th, tr)], axis=0)
    eb = jnp.exp(b)
    qd = q * eb.astype(BF16)
    kd = k / eb
    kdb = kd.astype(BF16)
    vb = i_ref[row]
    causal = (lax.broadcasted_iota(jnp.int32, (c, c), 1) <= lax.broadcasted_iota(jnp.int32, (c, c), 0))
    nt = (((1,), (1,)), ((), ()))
    n_chunks = th // c
    rows = [slice(cc * c, (cc + 1) * c) for cc in range(n_chunks)]
    lanes = [slice(h * HG_HEAD_DIM, (h + 1) * HG_HEAD_DIM) for h in range(HG_HEADS)]
    dec = [jnp.exp(b[cc * c + c - 1:cc * c + c, :]) for cc in range(n_chunks)]
    kdec = [(kd[rows[cc]] * dec[cc]).astype(BF16) for cc in range(n_chunks)]
    att = [[jnp.where(causal, lax.dot_general(qd[rows[cc], ls], kdb[rows[cc], ls], nt,
                                              preferred_element_type=F32), 0.0).astype(BF16)
            for ls in lanes] for cc in range(n_chunks)]
    upd = [[_dot(vb[rows[cc], ls].T, kdec[cc][:, ls])
            for ls in lanes] for cc in range(n_chunks)]
    st_in = [[None] * HG_HEADS for _ in range(n_chunks)]
    for h, ls in enumerate(lanes):
        st = st_s[row, h]
        for cc in range(n_chunks):
            st_in[cc][h] = st.astype(BF16)
            st = dec[cc][:, ls] * st + upd[cc][h]
        st_s[row, h] = st
    for cc in range(n_chunks):
        for h, ls in enumerate(lanes):
            o_s[row, rows[cc], ls] = (
                lax.dot_general(qd[rows[cc], ls], st_in[cc][h], nt, preferred_element_type=F32)
                + _dot(att[cc][h], vb[rows[cc], ls]))
    y_ref[row] = _hgrn_gate_out(o_s[row], g_ref[row].astype(F32), ng_ref[...])


def _hgrn_load_state(s0_ref, st_s):
    for r in range(st_s.shape[0]):
        for h in range(HG_HEADS):
            st_s[r, h] = s0_ref[r if s0_ref.shape[0] > 1 else 0, h].T


def _hgrn_store_state(s_out_ref, st_s):
    for r in range(st_s.shape[0]):
        for h in range(HG_HEADS):
            s_out_ref[r, h] = st_s[r, h].T


def _hgrn_kernel(q_ref, f_ref, i_ref, g_ref, lb_ref, ng_ref, tri_ref, s0_ref,
                 y_ref, s_out_ref, st_s, o_s, *, th, valid):
    i = pl.program_id(1)
    pl.when(i == 0)(lambda: _hgrn_load_state(s0_ref, st_s))
    _hgrn_tile(0, q_ref, f_ref, i_ref, g_ref, lb_ref, ng_ref, tri_ref, y_ref, st_s, o_s,
               th=th, valid=valid)
    pl.when(i == pl.num_programs(1) - 1)(lambda: _hgrn_store_state(s_out_ref, st_s))


def _hgrn_tri(tr):
    idx = np.arange(tr)
    tri = (idx[:, None] // HG_CHUNK == idx[None, :] // HG_CHUNK) & (idx[None, :] <= idx[:, None])
    return jnp.asarray(tri, dtype=BF16)


def _hgrn(zf, qig, lb, ng, s0, th, n_tiles, valid):
    n = zf.shape[0]
    own = s0.shape[0] == n
    tr = min(th, 256)
    col = lambda j: pl.BlockSpec((1, th, HG_DIM), lambda b, i, j=j: (b, i, j))
    const = lambda shape: pl.BlockSpec(shape, lambda b, i: (0,) * len(shape),
                                       pipeline_mode=pl.Buffered(1))
    st = pl.BlockSpec((1, HG_HEADS, HG_HEAD_DIM, HG_HEAD_DIM), lambda b, i: (b, 0, 0, 0))
    st_in = pl.BlockSpec((1, HG_HEADS, HG_HEAD_DIM, HG_HEAD_DIM),
                         lambda b, i: (b if own else 0, 0, 0, 0))
    return pl.pallas_call(
        functools.partial(_hgrn_kernel, th=th, valid=valid),
        grid=(n, n_tiles),
        in_specs=[col(0), col(0), col(1), col(2), const((1, HG_DIM)), const((1, HG_DIM)),
                  const((tr, tr)), st_in],
        out_specs=[pl.BlockSpec((1, th, HG_DIM), lambda b, i: (b, i, 0)), st],
        out_shape=(jax.ShapeDtypeStruct((n, n_tiles * th, HG_DIM), F32),
                   jax.ShapeDtypeStruct((n, HG_HEADS, HG_HEAD_DIM, HG_HEAD_DIM), F32)),
        scratch_shapes=[pltpu.VMEM((1, HG_HEADS, HG_HEAD_DIM, HG_HEAD_DIM), F32),
                        pltpu.VMEM((1, th, HG_DIM), F32)],
        compiler_params=_params("arbitrary", "arbitrary"),
        name="hgrn_chunks",
    )(qig, zf, qig, qig, lb, ng, _hgrn_tri(tr), s0)


def _hgrn_step_kernel(q_ref, f_ref, i_ref, g_ref, lb_ref, ng_ref, s0_ref,
                      y_ref, s_out_ref, o_s, *, sb):
    lb = lb_ref[...]
    f = lb + (1.0 - lb) * _sigmoid(f_ref[0])
    q = q_ref[0].astype(F32)
    v = i_ref[0].astype(F32)
    pad = jnp.zeros((HG_HEAD_DIM - sb, HG_HEAD_DIM), F32)
    sq = (HG_HEAD_DIM, HG_HEAD_DIM)
    for h in range(HG_HEADS):
        ls = slice(h * HG_HEAD_DIM, (h + 1) * HG_HEAD_DIM)
        fcols = jnp.concatenate([f[:, ls], pad], axis=0).T
        for s in range(sb):
            fc = jnp.broadcast_to(fcols[:, s:s + 1], sq)
            sn = fc * s0_ref[s, h] + (1.0 - fc) * v[s:s + 1, ls]
            s_out_ref[s, h] = sn
            o_s[s:s + 1, ls] = _dot(q[s:s + 1, ls].astype(BF16), sn.astype(BF16))
    y_ref[0] = _hgrn_gate_out(o_s[...], g_ref[0].astype(F32), ng_ref[...])


def _hgrn_step(zf, qig, lb, ng, s0, sb):
    r = zf.shape[1]
    col = lambda j: pl.BlockSpec((1, sb, HG_DIM), lambda i, j=j: (0, i, j))
    vec = pl.BlockSpec((1, HG_DIM), lambda i: (0, 0))
    st = pl.BlockSpec((sb, HG_HEADS, HG_HEAD_DIM, HG_HEAD_DIM), lambda i: (i, 0, 0, 0))
    return pl.pallas_call(
        functools.partial(_hgrn_step_kernel, sb=sb),
        grid=(r // sb,),
        in_specs=[col(0), col(0), col(1), col(2), vec, vec, st],
        out_specs=[pl.BlockSpec((1, sb, HG_DIM), lambda i: (0, i, 0)), st],
        out_shape=(jax.ShapeDtypeStruct((1, r, HG_DIM), F32),
                   jax.ShapeDtypeStruct(s0.shape, F32)),
        scratch_shapes=[pltpu.VMEM((sb, HG_DIM), F32)],
        compiler_params=_params("arbitrary"),
        name="hgrn_step",
    )(qig, zf, qig, qig, lb, ng, s0)


def _mix_out(y5_raw, yh, h, wglu, bglu, wout):
    y5p = _gelu_tanh(y5_raw)
    y5 = y5p * _sigmoid(_dot(y5p.astype(BF16), wglu) + bglu)
    ymix = jnp.concatenate([y5, yh], axis=-1).astype(BF16)
    return h + _dot(ymix, wout)


def _conv_taps(a, prev8, cw):
    rows = a.shape[0]
    cv = cw[3:4] + pltpu.roll(a, 2, 0) * cw[0:1] + pltpu.roll(a, 1, 0) * cw[1:2] + a * cw[2:3]
    d = prev8 - a[rows - SUBLANES:rows]
    rid = lax.broadcasted_iota(jnp.int32, (SUBLANES, 1), 0)
    fix = (jnp.where(rid < 1, pltpu.roll(d, 1, 0), 0.0) * cw[1:2]
           + jnp.where(rid < 2, pltpu.roll(d, 2, 0), 0.0) * cw[0:1])
    return jnp.concatenate([cv[:SUBLANES] + fix, cv[SUBLANES:]], axis=0)


def _ffn_kernel(y5_ref, yh_ref, h_ref, wglu_ref, bglu_ref, wout_ref, g2_ref,
                wa_ref, wv_ref, wd_ref, cw_ref, cin_ref, gf_ref,
                out_ref, cout_ref, carry_s, hn_s, h1_s, rn_s, s_s, *, tm):
    @pl.when(pl.program_id(1) == 0)
    def _():
        carry_s[...] = cin_ref[0]

    h1 = _mix_out(y5_ref[0], yh_ref[0], h_ref[0], wglu_ref[...], bglu_ref[...], wout_ref[...])
    h1_s[...] = h1
    hn_s[...] = (h1 * g2_ref[...]).astype(BF16)
    rn = lax.rsqrt(jnp.mean(h1 * h1, axis=-1, keepdims=True) + EPS)
    rn_s[...] = jnp.broadcast_to(rn, (tm, FF_CHUNK))

    for j in range(N_FF_CHUNKS):
        cs = slice(j * FF_CHUNK, (j + 1) * FF_CHUNK)
        hn = hn_s[...]
        a = _dot(hn, wa_ref[:, cs]) * rn_s[...]
        v = _dot(hn, wv_ref[:, cs]) * rn_s[...]
        cv = _conv_taps(a, carry_s[:, cs], cw_ref[:, cs])
        carry_s[:, cs] = a[tm - SUBLANES:tm]
        s_s[:, cs] = ((cv * _sigmoid(cv)) * v).astype(BF16)

    out_ref[0] = _rms_norm(h1_s[...] + _dot(s_s[...], wd_ref[...]), gf_ref[...])
    cout_ref[0] = carry_s[...]


def _ffn_small_kernel(y5m_ref, y5s_ref, yhm_ref, yhs_ref, hm_ref, hs_ref,
                      wglu_ref, bglu_ref, wout_ref, g2_ref, wa_ref, wv_ref, wd_ref, cw_ref,
                      cin_a_ref, cin_b_ref, gf_ref,
                      out_ref, a_out_ref, mcarry_ref,
                      wglu_bf_ref, wout_bf_ref, wa_bf_ref, wv_bf_ref, wd_bf_ref,
                      hn_s, h1_s, acc_s, *, n_meta):
    j = pl.program_id(0)

    @pl.when(j == 0)
    def _():
        wglu = wglu_ref[...].astype(BF16)
        wout = wout_ref[...].astype(BF16)
        wglu_bf_ref[...] = wglu
        wout_bf_ref[...] = wout
        rows = lambda m_ref, s_ref: jnp.concatenate([m_ref[0], s_ref[0]], axis=0)
        h1 = _mix_out(rows(y5m_ref, y5s_ref), rows(yhm_ref, yhs_ref), rows(hm_ref, hs_ref),
                      wglu, bglu_ref[...], wout)
        h1_s[...] = h1
        hn_s[...] = _rms_norm(h1, g2_ref[...]).astype(BF16)
        acc_s[...] = jnp.zeros_like(acc_s)

    wa = wa_ref[...].astype(BF16)
    wv = wv_ref[...].astype(BF16)
    wd = wd_ref[...].astype(BF16)
    wa_bf_ref[...] = wa
    wv_bf_ref[...] = wv
    wd_bf_ref[...] = wd
    hn = hn_s[...]
    a = _dot(hn, wa)
    v = _dot(hn, wv)
    cw = cw_ref[...]
    a_m, a_s = a[:n_meta], a[n_meta:]
    cv_m = _conv_taps(a_m, jnp.zeros((SUBLANES, FF_CHUNK), F32), cw)
    cv_s = cw[3:4] + cin_a_ref[...] * cw[0:1] + cin_b_ref[...] * cw[1:2] + a_s * cw[2:3]
    mcarry_ref[...] = a_m[n_meta - SUBLANES:]
    a_out_ref[...] = a_s
    cv = jnp.concatenate([cv_m, cv_s], axis=0)
    acc_s[...] += _dot(((cv * _sigmoid(cv)) * v).astype(BF16), wd)

    @pl.when(j == pl.num_programs(0) - 1)
    def _():
        out_ref[...] = _rms_norm((h1_s[...] + acc_s[...])[n_meta:], gf_ref[...])


def _ffn_small(y5m, y5s, yhm, yhs, hm, hs, wglu, bglu, wout, g2, wup, wd, cw, cin_a, cin_b, gf):
    n_meta, r = y5m.shape[1], y5s.shape[1]
    const = lambda shape: pl.BlockSpec(shape, lambda j: (0,) * len(shape))
    head = lambda rows, width: pl.BlockSpec((1, rows, width), lambda j: (0, 0, 0))
    ff_cols = lambda rows: pl.BlockSpec((rows, FF_CHUNK), lambda j: (0, j))
    sds = jax.ShapeDtypeStruct
    return pl.pallas_call(
        functools.partial(_ffn_small_kernel, n_meta=n_meta),
        grid=(N_FF_CHUNKS,),
        in_specs=[head(n_meta, S5_DIM), head(r, S5_DIM), head(n_meta, HG_DIM), head(r, HG_DIM),
                  head(n_meta, D_MODEL), head(r, D_MODEL),
                  const(wglu.shape), const(bglu.shape), const(wout.shape), const(g2.shape),
                  ff_cols(D_MODEL),
                  pl.BlockSpec((D_MODEL, FF_CHUNK), lambda j: (0, N_FF_CHUNKS + j)),
                  pl.BlockSpec((FF_CHUNK, D_MODEL), lambda j: (j, 0)),
                  ff_cols(SUBLANES), ff_cols(r), ff_cols(r), const(gf.shape)],
        out_specs=[const((r, D_MODEL)), ff_cols(r), ff_cols(SUBLANES),
                   const(wglu.shape), const(wout.shape), ff_cols(D_MODEL), ff_cols(D_MODEL),
                   pl.BlockSpec((FF_CHUNK, D_MODEL), lambda j: (j, 0))],
        out_shape=(sds((r, D_MODEL), F32), sds((r, D_FF), F32), sds((SUBLANES, D_FF), F32),
                   sds(wglu.shape, BF16), sds(wout.shape, BF16),
                   sds((D_MODEL, D_FF), BF16), sds((D_MODEL, D_FF), BF16), sds(wd.shape, BF16)),
        scratch_shapes=[pltpu.VMEM((n_meta + r, D_MODEL), BF16),
                        pltpu.VMEM((n_meta + r, D_MODEL), F32),
                        pltpu.VMEM((n_meta + r, D_MODEL), F32)],
        compiler_params=_params("arbitrary"),
        name="ffn_small",
    )(y5m, y5s, yhm, yhs, hm, hs, wglu, bglu, wout, g2, wup, wup, wd, cw, cin_a, cin_b, gf)


def _ffn(y5, yh, h, wglu, bglu, wout, g2, wa, wv, wd, cw, cin, gf, tm, n_tiles):
    n = h.shape[0]
    const = lambda shape: pl.BlockSpec(shape, lambda b, i: (0,) * len(shape),
                                       pipeline_mode=pl.Buffered(1))
    own = cin.shape[0] == n
    conv = lambda index: pl.BlockSpec((1, SUBLANES, D_FF), index)
    return pl.pallas_call(
        functools.partial(_ffn_kernel, tm=tm),
        grid=(n, n_tiles),
        in_specs=[pl.BlockSpec((1, tm, S5_DIM), lambda b, i: (b, i, 0)),
                  pl.BlockSpec((1, tm, HG_DIM), lambda b, i: (b, i, 0)),
                  pl.BlockSpec((1, tm, D_MODEL), lambda b, i: (b, i, 0)),
                  const(wglu.shape), const(bglu.shape), const(wout.shape), const(g2.shape),
                  const(wa.shape), const(wv.shape), const(wd.shape), const(cw.shape),
                  conv(lambda b, i: (b if own else 0, 0, 0)), const(gf.shape)],
        out_specs=[pl.BlockSpec((1, tm, D_MODEL), lambda b, i: (b, i, 0)),
                   conv(lambda b, i: (b, 0, 0))],
        out_shape=(jax.ShapeDtypeStruct((n, n_tiles * tm, D_MODEL), F32),
                   jax.ShapeDtypeStruct((n, SUBLANES, D_FF), F32)),
        scratch_shapes=[pltpu.VMEM((SUBLANES, D_FF), F32),
                        pltpu.VMEM((tm, D_MODEL), BF16),
                        pltpu.VMEM((tm, D_MODEL), F32),
                        pltpu.VMEM((tm, FF_CHUNK), F32),
                        pltpu.VMEM((tm, D_FF), BF16)],
        compiler_params=_params("arbitrary", "arbitrary"),
        name="mix_out_ffn",
    )(y5, yh, h, wglu, bglu, wout, g2, wa, wv, wd, cw, cin, gf)


def kernel(x_prompt, x_sample, state_s5_re, state_s5_im, state_hgrn, state_ffn_conv, meta_tokens, norm_mix_g, w_in, s5_lambda_re, s5_lambda_im, s5_log_dt, s5_b_re, s5_b_im, s5_c_re, s5_c_im, s5_d, s5_w_glu, s5_b_glu, hg_lower_bounds, hg_norm_g, w_out, norm_ffn_g, ffn_w_up, ffn_conv_w, ffn_conv_b, ffn_w_down, final_norm_g):
    nb, seq, _ = x_prompt.shape
    ns = x_sample.shape[0]
    li = 0

    a5, bb, lb = _prep(s5_lambda_re[li], s5_lambda_im[li], s5_log_dt[li],
                       s5_b_re[li], s5_b_im[li], hg_lower_bounds)
    a5 = a5.reshape(2, 1, S5_LANES)
    wb = _block_diag(bb)
    wc = _block_diag(jnp.stack([s5_c_re[li], -s5_c_im[li]]).transpose(0, 1, 3, 2))
    d5 = s5_d[li].reshape(1, S5_DIM)
    g1 = norm_mix_g[li].reshape(1, D_MODEL)
    g2 = norm_ffn_g[li].reshape(1, D_MODEL)
    gf = final_norm_g.reshape(1, D_MODEL)
    ng = hg_norm_g[li].reshape(1, HG_DIM)
    bglu = s5_b_glu[li].reshape(1, S5_DIM)
    w_in_b = w_in[li].astype(BF16)
    cw = jnp.concatenate([ffn_conv_w[li], ffn_conv_b[li][None],
                          jnp.zeros((SUBLANES - CONV_W - 1, D_FF), F32)], axis=0)
    s5w = (a5, wb, wc, d5)

    meta = jnp.zeros((1, HG_CHUNK, D_MODEL), F32).at[0, :N_META].set(meta_tokens)
    z5 = jnp.zeros((1, S5_LANES), F32)
    zh = jnp.zeros((1, HG_HEADS, HG_HEAD_DIM, HG_HEAD_DIM), F32)
    u_tm, zf, qig = _inproj(meta, g1, w_in_b, HG_CHUNK, 1)
    y5m, m5r, m5i = _s5(u_tm, z5, z5, *s5w, 1, N_META, 1)
    yhm, mhg = _hgrn(zf, qig, lb, ng, zh, HG_CHUNK, 1, N_META)

    xs = x_sample.reshape(1, ns, D_MODEL)
    u_tm, zf, qig = _inproj(xs, g1, w_in_b, ns, 1)
    y5s, s5r, s5i = _s5(u_tm, state_s5_re[li].reshape(ns, S5_LANES), state_s5_im[li].reshape(ns, S5_LANES),
                        *s5w, ns, 1, 1)
    yhs, shg = _hgrn_step(zf, qig, lb, ng, state_hgrn[li], 16)

    buf = state_ffn_conv[li]
    y_sample, a_new, mconv, wglu, wout, wa, wv, wd = _ffn_small(
        y5m, y5s, yhm, yhs, meta, xs, s5_w_glu[li], bglu, w_out[li], g2,
        ffn_w_up[li], ffn_w_down[li], cw, buf[:, 0], buf[:, 1], gf)
    sconv = jnp.stack([buf[:, 1], a_new], axis=1)

    ti, tm, tt, th = 512, 512, 128, 1024
    u_tm, zf, qig = _inproj(x_prompt, g1, w_in_b, ti, seq // ti)
    y5, p5r, p5i = _s5(u_tm, m5r, m5i, *s5w, nb, tt, seq // tt)
    yh, phg = _hgrn(zf, qig, lb, ng, mhg, th, seq // th, th)
    y_prompt, pconv = _ffn(y5, yh, x_prompt, wglu, bglu, wout, g2, wa, wv, wd, cw, mconv[None], gf,
                           tm, seq // tm)

    st5 = lambda t: t.reshape(1, -1, S5_GROUPS, S5_STATE)
    pconv = pconv[:, SUBLANES - (CONV_W - 1):, :]
    return (y_prompt, y_sample.reshape(ns, 1, D_MODEL),
            st5(p5r), st5(p5i), phg[None], pconv[None],
            st5(s5r), st5(s5i), shg[None], sconv[None])
```

```python
import functools

import jax
import jax.numpy as jnp
import numpy as np
from jax import lax
from jax.experimental import pallas as pl
from jax.experimental.pallas import tpu as pltpu

F32 = jnp.float32
BF16 = jnp.bfloat16

D_MODEL = 1024
N_META = 16
S5_DIM = 512
S5_GROUP = 16
S5_GROUPS = 32
S5_STATE = 64
S5_LANES = S5_GROUPS * S5_STATE
HG_DIM = 512
HG_HEAD_DIM = 128
HG_HEADS = 4
HG_CHUNK = 64
D_FF = 2816
CONV_W = 3
EPS = 1e-6

S5_HALVES = 2
S5_HALF_CH = S5_DIM // S5_HALVES
S5_HALF_ST = S5_LANES // S5_HALVES
SCAN_LANES = 512
FF_CHUNK = 256
N_FF_CHUNKS = D_FF // FF_CHUNK
SUBLANES = 8
LANES = 128
VMEM_LIMIT = 56 * 1024 * 1024


def _sigmoid(x):
    return 0.5 + 0.5 * jnp.tanh(0.5 * x)


def _rms_norm(x, g):
    ms = jnp.mean(x * x, axis=-1, keepdims=True)
    return x * lax.rsqrt(ms + EPS) * g


def _dot(a, b):
    return jnp.dot(a, b, preferred_element_type=F32)


def _params(*sem):
    return pltpu.CompilerParams(dimension_semantics=sem, vmem_limit_bytes=VMEM_LIMIT)


def _prep_kernel(lam_ref, dt_ref, bt_ref, hlb_ref, a_ref, bb_ref, lb_ref):
    lam_re = lam_ref[0]
    lam_im = lam_ref[1]
    dt = jnp.exp(dt_ref[...])
    mag = jnp.exp(lam_re * dt)
    ar = mag * jnp.cos(lam_im * dt)
    ai = mag * jnp.sin(lam_im * dt)
    nr = ar - 1.0
    den = lam_re * lam_re + lam_im * lam_im
    cr = (nr * lam_re + ai * lam_im) / den
    ci = (ai * lam_re - nr * lam_im) / den
    a_ref[0] = ar
    a_ref[1] = ai
    bt_re = bt_ref[0]
    bt_im = bt_ref[1]
    bb_ref[0] = cr * bt_re - ci * bt_im
    bb_ref[1] = cr * bt_im + ci * bt_re
    hlb = hlb_ref[...]
    e = jnp.exp(hlb - jnp.max(hlb, axis=0, keepdims=True))
    lb_ref[...] = e[0:1] / jnp.sum(e, axis=0, keepdims=True)


def _prep(lam_re, lam_im, log_dt, b_re, b_im, hlb):
    g, p = lam_re.shape
    sds = jax.ShapeDtypeStruct
    return pl.pallas_call(
        _prep_kernel,
        out_shape=(sds((2, g, 1, p), F32), sds((2, g, S5_GROUP, p), F32), sds((1, HG_DIM), F32)),
        name="param_prep",
    )(jnp.stack([lam_re, lam_im]).reshape(2, g, 1, p),
      jnp.broadcast_to(log_dt.reshape(g, 1, 1), (g, 1, p)),
      jnp.stack([b_re, b_im]).transpose(0, 1, 3, 2), hlb)


def _block_diag(blocks):
    two, g, r, c = blocks.shape
    per = g // S5_HALVES
    eye = np.eye(per, dtype=np.float32)
    out = jnp.einsum("ajgrc,gh->ajgrhc", blocks.reshape(two, S5_HALVES, per, r, c), eye)
    return out.reshape(two, S5_HALVES, per * r, per * c).astype(BF16)


def _inproj_kernel(x_ref, g_ref, w_ref, u_ref, f_ref, qig_ref, *wb_ref, n, tm):
    x = x_ref[0]
    hn = (x * g_ref[...]).astype(BF16)
    rn = lax.rsqrt(jnp.mean(x * x, axis=-1, keepdims=True) + EPS)
    rn = jnp.broadcast_to(rn, (tm, HG_DIM))
    if wb_ref:
        wb_ref[0][...] = w_ref[...].astype(BF16)
        w_ref = wb_ref[0]
    col = lambda j: _dot(hn, w_ref[:, j * HG_DIM:(j + 1) * HG_DIM]) * rn
    b = pl.program_id(1)
    u = col(0)
    for l in range(S5_DIM // LANES):
        ul = u[:, l * LANES:(l + 1) * LANES]
        if n == 1:
            u_ref[l] = ul
        else:
            u_ref[l, pl.ds(b, tm, stride=n), :] = ul
    f_ref[0] = col(2)
    for j, src in enumerate((1, 3, 4)):
        qig_ref[0, :, j * HG_DIM:(j + 1) * HG_DIM] = col(src).astype(BF16)


def _inproj(x, g, w, tm, n_tiles, emit_bf16=False):
    n, l, d = x.shape
    cols = w.shape[1]
    rows = tm * n_tiles
    const = lambda shape: pl.BlockSpec(shape, lambda i, b: (0,) * len(shape),
                                       pipeline_mode=pl.Buffered(1))
    return pl.pallas_call(
        functools.partial(_inproj_kernel, n=n, tm=tm),
        grid=(n_tiles, n),
        in_specs=[pl.BlockSpec((1, tm, d), lambda i, b: (b, i, 0)),
                  const((1, d)), const((d, cols))],
        out_specs=[pl.BlockSpec((S5_DIM // LANES, tm * n, LANES), lambda i, b: (0, i, 0)),
                   pl.BlockSpec((1, tm, HG_DIM), lambda i, b: (b, i, 0)),
                   pl.BlockSpec((1, tm, 3 * HG_DIM), lambda i, b: (b, i, 0))]
                  + [pl.BlockSpec((d, cols), lambda i, b: (0, 0))] * emit_bf16,
        out_shape=(jax.ShapeDtypeStruct((S5_DIM // LANES, rows * n, LANES), F32),
                   jax.ShapeDtypeStruct((n, rows, HG_DIM), F32),
                   jax.ShapeDtypeStruct((n, rows, 3 * HG_DIM), BF16))
                  + (jax.ShapeDtypeStruct((d, cols), BF16),) * emit_bf16,
        compiler_params=_params("arbitrary", "arbitrary"),
        name="inproj",
    )(x, g, w)


def _gelu_tanh(y):
    return 0.5 * y * (1.0 + jnp.tanh(0.7978845608028654 * (y + 0.044715 * (y * y * y))))


def _s5_tile(step, u_ref, h0r_ref, h0i_ref, a_ref, wb_ref, wc_ref, d_ref,
             y_ref, hr_out_ref, hi_out_ref, xr_s, xi_s, hr_s, hi_s, y_s, *, n, tt, interleaved):
    @pl.when(step == 0)
    def _():
        hr_s[...] = jnp.broadcast_to(h0r_ref[...], hr_s.shape)
        hi_s[...] = jnp.broadcast_to(h0i_ref[...], hi_s.shape)

    slabs = S5_HALF_CH // LANES
    load_u = lambda j: jnp.concatenate([u_ref[j * slabs + l] for l in range(slabs)], axis=1)
    for j in range(S5_HALVES):
        st = slice(j * S5_HALF_ST, (j + 1) * S5_HALF_ST)
        ub = load_u(j).astype(BF16)
        xr_s[:, st] = _dot(ub, wb_ref[0, j])
        xi_s[:, st] = _dot(ub, wb_ref[1, j])
    for c in range(S5_LANES // SCAN_LANES):
        loc = slice(c * SCAN_LANES, (c + 1) * SCAN_LANES)
        ar = jnp.broadcast_to(a_ref[0, :, loc], (n, SCAN_LANES))
        ai = jnp.broadcast_to(a_ref[1, :, loc], (n, SCAN_LANES))

        def step(t, carry, ar=ar, ai=ai, loc=loc):
            hr, hi = carry
            r = 0 if tt == 1 else pl.multiple_of(t * n, n)
            nhr = ar * hr - ai * hi + xr_s[pl.ds(r, n), loc]
            nhi = ar * hi + ai * hr + xi_s[pl.ds(r, n), loc]
            xr_s[pl.ds(r, n), loc] = nhr
            xi_s[pl.ds(r, n), loc] = nhi
            return nhr, nhi

        carry = (hr_s[:, loc], hi_s[:, loc])
        if tt == 1:
            carry = step(0, carry)
        else:
            carry = lax.fori_loop(0, tt, step, carry, unroll=True)
        hr_s[:, loc] = carry[0]
        hi_s[:, loc] = carry[1]
    for j in range(S5_HALVES):
        ch = slice(j * S5_HALF_CH, (j + 1) * S5_HALF_CH)
        st = slice(j * S5_HALF_ST, (j + 1) * S5_HALF_ST)
        y = (_dot(xr_s[:, st].astype(BF16), wc_ref[0, j]) + _dot(xi_s[:, st].astype(BF16), wc_ref[1, j])
             + d_ref[:, ch] * load_u(j))
        if interleaved:
            for l in range(slabs):
                y_s[j * slabs + l] = y[:, l * LANES:(l + 1) * LANES]
        else:
            y_ref[0, :, ch] = y
    if interleaved:
        for b in range(n):
            for l in range(S5_DIM // LANES):
                y_ref[b, :, l * LANES:(l + 1) * LANES] = y_s[l, pl.ds(b, tt, stride=n), :]
    hr_out_ref[...] = hr_s[...]
    hi_out_ref[...] = hi_s[...]


def _s5_kernel(*refs, **static):
    _s5_tile(pl.program_id(0), *refs, **static)


def _s5_scratch(n, rows, interleaved):
    return [pltpu.VMEM((rows, S5_LANES), F32), pltpu.VMEM((rows, S5_LANES), F32),
            pltpu.VMEM((n, S5_LANES), F32), pltpu.VMEM((n, S5_LANES), F32),
            pltpu.VMEM((S5_DIM // LANES, rows if interleaved else SUBLANES, LANES), F32)]


def _s5(u4, h0r, h0i, a, wb, wc, d, n, tt, n_tiles):
    rows = tt * n
    interleaved = n > 1 and tt > 1
    const = lambda shape: pl.BlockSpec(shape, lambda i: (0,) * len(shape),
                                       pipeline_mode=pl.Buffered(1))
    state = pl.BlockSpec((n, S5_LANES), lambda i: (0, 0))
    y_block = (n, tt, S5_DIM) if interleaved else (1, rows, S5_DIM)
    y_shape = (n, tt * n_tiles, S5_DIM) if interleaved else (1, rows * n_tiles, S5_DIM)
    return pl.pallas_call(
        functools.partial(_s5_kernel, n=n, tt=tt, interleaved=interleaved),
        grid=(n_tiles,),
        in_specs=[pl.BlockSpec((S5_DIM // LANES, rows, LANES), lambda i: (0, i, 0)),
                  const(h0r.shape), const(h0i.shape), const(a.shape), const(wb.shape), const(wc.shape),
                  const((1, S5_DIM))],
        out_specs=[pl.BlockSpec(y_block, lambda i: (0, i, 0)), state, state],
        out_shape=(jax.ShapeDtypeStruct(y_shape, F32),
                   jax.ShapeDtypeStruct((n, S5_LANES), F32),
                   jax.ShapeDtypeStruct((n, S5_LANES), F32)),
        scratch_shapes=_s5_scratch(n, rows, interleaved),
        compiler_params=_params("arbitrary"),
        name="s5_scan",
    )(u4, h0r, h0i, a, wb, wc, d)


def _hgrn_gate_out(o, g, ng):
    parts = []
    for h in range(HG_HEADS):
        oh = o[:, h * HG_HEAD_DIM:(h + 1) * HG_HEAD_DIM]
        ms = jnp.mean(oh * oh, axis=-1, keepdims=True)
        parts.append(oh * lax.rsqrt(ms + EPS))
    return jnp.concatenate(parts, axis=-1) * ng * (g * _sigmoid(g))


def _hgrn_tile(row, q_ref, f_ref, i_ref, g_ref, lb_ref, ng_ref, tri_ref, y_ref, st_s, o_s, *, th, valid):
    c = HG_CHUNK
    lb = lb_ref[...]
    f = lb + (1.0 - lb) * _sigmoid(f_ref[row])
    lc = jnp.log(f)
    k = 1.0 - f
    q = q_ref[row]
    if valid < th:
        live = lax.broadcasted_iota(jnp.int32, (th, 1), 0) < valid
        lc = jnp.where(live, lc, 0.0)
        k = jnp.where(live, k, 0.0)
        q = jnp.where(live, q.astype(F32), 0.0).astype(BF16)
    tri = tri_ref[...]
    tr = tri.shape[0]
    lc_hi = lc.astype(BF16)
    lc_lo = (lc - lc_hi.astype(F32)).astype(BF16)
    b = jnp.concatenate([_dot(tri, lc_hi[r:r + tr]) + _dot(tri, lc_lo[r:r + tr])
                         for r in range(0, th, tr)], axis=0)
    eb = jnp.exp(b)
    qd = q * eb.astype(BF16)
    kd = k / eb
    kdb = kd.astype(BF16)
    vb = i_ref[row]
    causal = (lax.broadcasted_iota(jnp.int32, (c, c), 1) <= lax.broadcasted_iota(jnp.int32, (c, c), 0))
    nt = (((1,), (1,)), ((), ()))
    n_chunks = th // c
    rows = [slice(cc * c, (cc + 1) * c) for cc in range(n_chunks)]
    lanes = [slice(h * HG_HEAD_DIM, (h + 1) * HG_HEAD_DIM) for h in range(HG_HEADS)]
    dec = [jnp.exp(b[cc * c + c - 1:cc * c + c, :]) for cc in range(n_chunks)]
    kdec = [(kd[rows[cc]] * dec[cc]).astype(BF16) for cc in range(n_chunks)]
    att = [[jnp.where(causal, lax.dot_general(qd[rows[cc], ls], kdb[rows[cc], ls], nt,
                                              preferred_element_type=F32), 0.0).astype(BF16)
            for ls in lanes] for cc in range(n_chunks)]
    upd = [[_dot(vb[rows[cc], ls].T, kdec[cc][:, ls])
            for ls in lanes] for cc in range(n_chunks)]
    st_in = [[None] * HG_HEADS for _ in range(n_chunks)]
    for h, ls in enumerate(lanes):
        st = st_s[row, h]
        for cc in range(n_chunks):
            st_in[cc][h] = st.astype(BF16)
            st = dec[cc][:, ls] * st + upd[cc][h]
        st_s[row, h] = st
    for cc in range(n_chunks):
        for h, ls in enumerate(lanes):
            o_s[row, rows[cc], ls] = (
                lax.dot_general(qd[rows[cc], ls], st_in[cc][h], nt, preferred_element_type=F32)
                + _dot(att[cc][h], vb[rows[cc], ls]))
    y_ref[row] = _hgrn_gate_out(o_s[row], g_ref[row].astype(F32), ng_ref[...])


def _hgrn_load_state(s0_ref, st_s):
    for r in range(st_s.shape[0]):
        for h in range(HG_HEADS):
            st_s[r, h] = s0_ref[r if s0_ref.shape[0] > 1 else 0, h].T


def _hgrn_store_state(s_out_ref, st_s):
    for r in range(st_s.shape[0]):
        for h in range(HG_HEADS):
            s_out_ref[r, h] = st_s[r, h].T


def _hgrn_kernel(q_ref, f_ref, i_ref, g_ref, lb_ref, ng_ref, tri_ref, s0_ref,
                 y_ref, s_out_ref, st_s, o_s, *, th, valid):
    i = pl.program_id(1)
    pl.when(i == 0)(lambda: _hgrn_load_state(s0_ref, st_s))
    _hgrn_tile(0, q_ref, f_ref, i_ref, g_ref, lb_ref, ng_ref, tri_ref, y_ref, st_s, o_s,
               th=th, valid=valid)
    pl.when(i == pl.num_programs(1) - 1)(lambda: _hgrn_store_state(s_out_ref, st_s))


def _hgrn_tri(tr):
    idx = np.arange(tr)
    tri = (idx[:, None] // HG_CHUNK == idx[None, :] // HG_CHUNK) & (idx[None, :] <= idx[:, None])
    return jnp.asarray(tri, dtype=BF16)


def _hgrn(zf, qig, lb, ng, s0, th, n_tiles, valid):
    n = zf.shape[0]
    own = s0.shape[0] == n
    tr = min(th, 256)
    col = lambda j: pl.BlockSpec((1, th, HG_DIM), lambda b, i, j=j: (b, i, j))
    const = lambda shape: pl.BlockSpec(shape, lambda b, i: (0,) * len(shape),
                                       pipeline_mode=pl.Buffered(1))
    st = pl.BlockSpec((1, HG_HEADS, HG_HEAD_DIM, HG_HEAD_DIM), lambda b, i: (b, 0, 0, 0))
    st_in = pl.BlockSpec((1, HG_HEADS, HG_HEAD_DIM, HG_HEAD_DIM),
                         lambda b, i: (b if own else 0, 0, 0, 0))
    return pl.pallas_call(
        functools.partial(_hgrn_kernel, th=th, valid=valid),
        grid=(n, n_tiles),
        in_specs=[col(0), col(0), col(1), col(2), const((1, HG_DIM)), const((1, HG_DIM)),
                  const((tr, tr)), st_in],
        out_specs=[pl.BlockSpec((1, th, HG_DIM), lambda b, i: (b, i, 0)), st],
        out_shape=(jax.ShapeDtypeStruct((n, n_tiles * th, HG_DIM), F32),
                   jax.ShapeDtypeStruct((n, HG_HEADS, HG_HEAD_DIM, HG_HEAD_DIM), F32)),
        scratch_shapes=[pltpu.VMEM((1, HG_HEADS, HG_HEAD_DIM, HG_HEAD_DIM), F32),
                        pltpu.VMEM((1, th, HG_DIM), F32)],
        compiler_params=_params("arbitrary", "arbitrary"),
        name="hgrn_chunks",
    )(qig, zf, qig, qig, lb, ng, _hgrn_tri(tr), s0)


def _hgrn_step_kernel(q_ref, f_ref, i_ref, g_ref, lb_ref, ng_ref, s0_ref,
                      y_ref, s_out_ref, o_s, *, sb):
    lb = lb_ref[...]
    f = lb + (1.0 - lb) * _sigmoid(f_ref[0])
    q = q_ref[0].astype(F32)
    v = i_ref[0].astype(F32)
    pad = jnp.zeros((HG_HEAD_DIM - sb, HG_HEAD_DIM), F32)
    sq = (HG_HEAD_DIM, HG_HEAD_DIM)
    for h in range(HG_HEADS):
        ls = slice(h * HG_HEAD_DIM, (h + 1) * HG_HEAD_DIM)
        fcols = jnp.concatenate([f[:, ls], pad], axis=0).T
        for s in range(sb):
            fc = jnp.broadcast_to(fcols[:, s:s + 1], sq)
            sn = fc * s0_ref[s, h] + (1.0 - fc) * v[s:s + 1, ls]
            s_out_ref[s, h] = sn
            o_s[s:s + 1, ls] = _dot(q[s:s + 1, ls].astype(BF16), sn.astype(BF16))
    y_ref[0] = _hgrn_gate_out(o_s[...], g_ref[0].astype(F32), ng_ref[...])


def _hgrn_step(zf, qig, lb, ng, s0, sb):
    r = zf.shape[1]
    col = lambda j: pl.BlockSpec((1, sb, HG_DIM), lambda i, j=j: (0, i, j))
    vec = pl.BlockSpec((1, HG_DIM), lambda i: (0, 0))
    st = pl.BlockSpec((sb, HG_HEADS, HG_HEAD_DIM, HG_HEAD_DIM), lambda i: (i, 0, 0, 0))
    return pl.pallas_call(
        functools.partial(_hgrn_step_kernel, sb=sb),
        grid=(r // sb,),
        in_specs=[col(0), col(0), col(1), col(2), vec, vec, st],
        out_specs=[pl.BlockSpec((1, sb, HG_DIM), lambda i: (0, i, 0)), st],
        out_shape=(jax.ShapeDtypeStruct((1, r, HG_DIM), F32),
                   jax.ShapeDtypeStruct(s0.shape, F32)),
        scratch_shapes=[pltpu.VMEM((sb, HG_DIM), F32)],
        compiler_params=_params("arbitrary"),
        name="hgrn_step",
    )(qig, zf, qig, qig, lb, ng, s0)


def _mix_out(y5_raw, yh, h, wglu, bglu, wout):
    y5p = _gelu_tanh(y5_raw)
    y5 = y5p * _sigmoid(_dot(y5p.astype(BF16), wglu) + bglu)
    ymix = jnp.concatenate([y5, yh], axis=-1).astype(BF16)
    return h + _dot(ymix, wout)


def _conv_taps(a, prev8, cw):
    rows = a.shape[0]
    cv = cw[3:4] + pltpu.roll(a, 2, 0) * cw[0:1] + pltpu.roll(a, 1, 0) * cw[1:2] + a * cw[2:3]
    d = prev8 - a[rows - SUBLANES:rows]
    rid = lax.broadcasted_iota(jnp.int32, (SUBLANES, 1), 0)
    fix = (jnp.where(rid < 1, pltpu.roll(d, 1, 0), 0.0) * cw[1:2]
           + jnp.where(rid < 2, pltpu.roll(d, 2, 0), 0.0) * cw[0:1])
    return jnp.concatenate([cv[:SUBLANES] + fix, cv[SUBLANES:]], axis=0)


def _ffn_kernel(y5_ref, yh_ref, h_ref, wglu_ref, bglu_ref, wout_ref, g2_ref,
                wa_ref, wv_ref, wd_ref, cw_ref, cin_ref, gf_ref,
                out_ref, cout_ref, carry_s, hn_s, h1_s, rn_s, s_s, *, tm):
    @pl.when(pl.program_id(1) == 0)
    def _():
        carry_s[...] = cin_ref[0]

    h1 = _mix_out(y5_ref[0], yh_ref[0], h_ref[0], wglu_ref[...], bglu_ref[...], wout_ref[...])
    h1_s[...] = h1
    hn_s[...] = (h1 * g2_ref[...]).astype(BF16)
    rn = lax.rsqrt(jnp.mean(h1 * h1, axis=-1, keepdims=True) + EPS)
    rn_s[...] = jnp.broadcast_to(rn, (tm, FF_CHUNK))

    for j in range(N_FF_CHUNKS):
        cs = slice(j * FF_CHUNK, (j + 1) * FF_CHUNK)
        hn = hn_s[...]
        a = _dot(hn, wa_ref[:, cs]) * rn_s[...]
        v = _dot(hn, wv_ref[:, cs]) * rn_s[...]
        cv = _conv_taps(a, carry_s[:, cs], cw_ref[:, cs])
        carry_s[:, cs] = a[tm - SUBLANES:tm]
        s_s[:, cs] = ((cv * _sigmoid(cv)) * v).astype(BF16)

    out_ref[0] = _rms_norm(h1_s[...] + _dot(s_s[...], wd_ref[...]), gf_ref[...])
    cout_ref[0] = carry_s[...]


def _ffn_small_kernel(y5m_ref, y5s_ref, yhm_ref, yhs_ref, hm_ref, hs_ref,
                      wglu_ref, bglu_ref, wout_ref, g2_ref, wa_ref, wv_ref, wd_ref, cw_ref,
                      cin_a_ref, cin_b_ref, gf_ref,
                      out_ref, a_out_ref, mcarry_ref,
                      wglu_bf_ref, wout_bf_ref, wa_bf_ref, wv_bf_ref, wd_bf_ref,
                      hn_s, h1_s, acc_s, *, n_meta):
    j = pl.program_id(0)

    @pl.when(j == 0)
    def _():
        wglu = wglu_ref[...].astype(BF16)
        wout = wout_ref[...].astype(BF16)
        wglu_bf_ref[...] = wglu
        wout_bf_ref[...] = wout
        rows = lambda m_ref, s_ref: jnp.concatenate([m_ref[0], s_ref[0]], axis=0)
        h1 = _mix_out(rows(y5m_ref, y5s_ref), rows(yhm_ref, yhs_ref), rows(hm_ref, hs_ref),
                      wglu, bglu_ref[...], wout)
        h1_s[...] = h1
        hn_s[...] = _rms_norm(h1, g2_ref[...]).astype(BF16)
        acc_s[...] = jnp.zeros_like(acc_s)

    wa = wa_ref[...].astype(BF16)
    wv = wv_ref[...].astype(BF16)
    wd = wd_ref[...].astype(BF16)
    wa_bf_ref[...] = wa
    wv_bf_ref[...] = wv
    wd_bf_ref[...] = wd
    hn = hn_s[...]
    a = _dot(hn, wa)
    v = _dot(hn, wv)
    cw = cw_ref[...]
    a_m, a_s = a[:n_meta], a[n_meta:]
    cv_m = _conv_taps(a_m, jnp.zeros((SUBLANES, FF_CHUNK), F32), cw)
    cv_s = cw[3:4] + cin_a_ref[...] * cw[0:1] + cin_b_ref[...] * cw[1:2] + a_s * cw[2:3]
    mcarry_ref[...] = a_m[n_meta - SUBLANES:]
    a_out_ref[...] = a_s
    cv = jnp.concatenate([cv_m, cv_s], axis=0)
    acc_s[...] += _dot(((cv * _sigmoid(cv)) * v).astype(BF16), wd)

    @pl.when(j == pl.num_programs(0) - 1)
    def _():
        out_ref[...] = _rms_norm((h1_s[...] + acc_s[...])[n_meta:], gf_ref[...])


def _ffn_small(y5m, y5s, yhm, yhs, hm, hs, wglu, bglu, wout, g2, wup, wd, cw, cin_a, cin_b, gf):
    n_meta, r = y5m.shape[1], y5s.shape[1]
    const = lambda shape: pl.BlockSpec(shape, lambda j: (0,) * len(shape))
    head = lambda rows, width: pl.BlockSpec((1, rows, width), lambda j: (0, 0, 0))
    ff_cols = lambda rows: pl.BlockSpec((rows, FF_CHUNK), lambda j: (0, j))
    sds = jax.ShapeDtypeStruct
    return pl.pallas_call(
        functools.partial(_ffn_small_kernel, n_meta=n_meta),
        grid=(N_FF_CHUNKS,),
        in_specs=[head(n_meta, S5_DIM), head(r, S5_DIM), head(n_meta, HG_DIM), head(r, HG_DIM),
                  head(n_meta, D_MODEL), head(r, D_MODEL),
                  const(wglu.shape), const(bglu.shape), const(wout.shape), const(g2.shape),
                  ff_cols(D_MODEL),
                  pl.BlockSpec((D_MODEL, FF_CHUNK), lambda j: (0, N_FF_CHUNKS + j)),
                  pl.BlockSpec((FF_CHUNK, D_MODEL), lambda j: (j, 0)),
                  ff_cols(SUBLANES), ff_cols(r), ff_cols(r), const(gf.shape)],
        out_specs=[const((r, D_MODEL)), ff_cols(r), ff_cols(SUBLANES),
                   const(wglu.shape), const(wout.shape), ff_cols(D_MODEL), ff_cols(D_MODEL),
                   pl.BlockSpec((FF_CHUNK, D_MODEL), lambda j: (j, 0))],
        out_shape=(sds((r, D_MODEL), F32), sds((r, D_FF), F32), sds((SUBLANES, D_FF), F32),
                   sds(wglu.shape, BF16), sds(wout.shape, BF16),
                   sds((D_MODEL, D_FF), BF16), sds((D_MODEL, D_FF), BF16), sds(wd.shape, BF16)),
        scratch_shapes=[pltpu.VMEM((n_meta + r, D_MODEL), BF16),
                        pltpu.VMEM((n_meta + r, D_MODEL), F32),
                        pltpu.VMEM((n_meta + r, D_MODEL), F32)],
        compiler_params=_params("arbitrary"),
        name="ffn_small",
    )(y5m, y5s, yhm, yhs, hm, hs, wglu, bglu, wout, g2, wup, wup, wd, cw, cin_a, cin_b, gf)


def _ffn(y5, yh, h, wglu, bglu, wout, g2, wa, wv, wd, cw, cin, gf, tm, n_tiles):
    n = h.shape[0]
    const = lambda shape: pl.BlockSpec(shape, lambda b, i: (0,) * len(shape),
                                       pipeline_mode=pl.Buffered(1))
    own = cin.shape[0] == n
    conv = lambda index: pl.BlockSpec((1, SUBLANES, D_FF), index)
    return pl.pallas_call(
        functools.partial(_ffn_kernel, tm=tm),
        grid=(n, n_tiles),
        in_specs=[pl.BlockSpec((1, tm, S5_DIM), lambda b, i: (b, i, 0)),
                  pl.BlockSpec((1, tm, HG_DIM), lambda b, i: (b, i, 0)),
                  pl.BlockSpec((1, tm, D_MODEL), lambda b, i: (b, i, 0)),
                  const(wglu.shape), const(bglu.shape), const(wout.shape), const(g2.shape),
                  const(wa.shape), const(wv.shape), const(wd.shape), const(cw.shape),
                  conv(lambda b, i: (b if own else 0, 0, 0)), const(gf.shape)],
        out_specs=[pl.BlockSpec((1, tm, D_MODEL), lambda b, i: (b, i, 0)),
                   conv(lambda b, i: (b, 0, 0))],
        out_shape=(jax.ShapeDtypeStruct((n, n_tiles * tm, D_MODEL), F32),
                   jax.ShapeDtypeStruct((n, SUBLANES, D_FF), F32)),
        scratch_shapes=[pltpu.VMEM((SUBLANES, D_FF), F32),
                        pltpu.VMEM((tm, D_MODEL), BF16),
                        pltpu.VMEM((tm, D_MODEL), F32),
                        pltpu.VMEM((tm, FF_CHUNK), F32),
                        pltpu.VMEM((tm, D_FF), BF16)],
        compiler_params=_params("arbitrary", "arbitrary"),
        name="mix_out_ffn",
    )(y5, yh, h, wglu, bglu, wout, g2, wa, wv, wd, cw, cin, gf)


def kernel(x_prompt, x_sample, state_s5_re, state_s5_im, state_hgrn, state_ffn_conv, meta_tokens, norm_mix_g, w_in, s5_lambda_re, s5_lambda_im, s5_log_dt, s5_b_re, s5_b_im, s5_c_re, s5_c_im, s5_d, s5_w_glu, s5_b_glu, hg_lower_bounds, hg_norm_g, w_out, norm_ffn_g, ffn_w_up, ffn_conv_w, ffn_conv_b, ffn_w_down, final_norm_g):
    nb, seq, _ = x_prompt.shape
    ns = x_sample.shape[0]
    li = 0

    a5, bb, lb = _prep(s5_lambda_re[li], s5_lambda_im[li], s5_log_dt[li],
                       s5_b_re[li], s5_b_im[li], hg_lower_bounds)
    a5 = a5.reshape(2, 1, S5_LANES)
    wb = _block_diag(bb)
    wc = _block_diag(jnp.stack([s5_c_re[li], -s5_c_im[li]]).transpose(0, 1, 3, 2))
    d5 = s5_d[li].reshape(1, S5_DIM)
    g1 = norm_mix_g[li].reshape(1, D_MODEL)
    g2 = norm_ffn_g[li].reshape(1, D_MODEL)
    gf = final_norm_g.reshape(1, D_MODEL)
    ng = hg_norm_g[li].reshape(1, HG_DIM)
    bglu = s5_b_glu[li].reshape(1, S5_DIM)
    cw = jnp.concatenate([ffn_conv_w[li], ffn_conv_b[li][None],
                          jnp.zeros((SUBLANES - CONV_W - 1, D_FF), F32)], axis=0)
    s5w = (a5, wb, wc, d5)

    xs = x_sample.reshape(1, ns, D_MODEL)
    u_tm, zf, qig, w_in_b = _inproj(xs, g1, w_in[li], ns, 1, emit_bf16=True)
    y5s, s5r, s5i = _s5(u_tm, state_s5_re[li].reshape(ns, S5_LANES), state_s5_im[li].reshape(ns, S5_LANES),
                        *s5w, ns, 1, 1)
    yhs, shg = _hgrn_step(zf, qig, lb, ng, state_hgrn[li], 16)

    meta = jnp.zeros((1, HG_CHUNK, D_MODEL), F32).at[0, :N_META].set(meta_tokens)
    z5 = jnp.zeros((1, S5_LANES), F32)
    zh = jnp.zeros((1, HG_HEADS, HG_HEAD_DIM, HG_HEAD_DIM), F32)
    u_tm, zf, qig = _inproj(meta, g1, w_in_b, HG_CHUNK, 1)
    y5m, m5r, m5i = _s5(u_tm, z5, z5, *s5w, 1, N_META, 1)
    yhm, mhg = _hgrn(zf, qig, lb, ng, zh, HG_CHUNK, 1, N_META)

    buf = state_ffn_conv[li]
    y_sample, a_new, mconv, wglu, wout, wa, wv, wd = _ffn_small(
        y5m, y5s, yhm, yhs, meta, xs, s5_w_glu[li], bglu, w_out[li], g2,
        ffn_w_up[li], ffn_w_down[li], cw, buf[:, 0], buf[:, 1], gf)
    sconv = jnp.stack([buf[:, 1], a_new], axis=1)

    ti, tm, tt, th = 512, 512, 128, 1024
    u_tm, zf, qig = _inproj(x_prompt, g1, w_in_b, ti, seq // ti)
    y5, p5r, p5i = _s5(u_tm, m5r, m5i, *s5w, nb, tt, seq // tt)
    yh, phg = _hgrn(zf, qig, lb, ng, mhg, th, seq // th, th)
    y_prompt, pconv = _ffn(y5, yh, x_prompt, wglu, bglu, wout, g2, wa, wv, wd, cw, mconv[None], gf,
                           tm, seq // tm)

    st5 = lambda t: t.reshape(1, -1, S5_GROUPS, S5_STATE)
    pconv = pconv[:, SUBLANES - (CONV_W - 1):, :]
    return (y_prompt, y_sample.reshape(ns, 1, D_MODEL),
            st5(p5r), st5(p5i), phg[None], pconv[None],
            st5(s5r), st5(s5i), shg[None], sconv[None])
```

```python
import functools

import jax
import jax.numpy as jnp
import numpy as np
from jax import lax
from jax.experimental import pallas as pl
from jax.experimental.pallas import tpu as pltpu

F32 = jnp.float32
BF16 = jnp.bfloat16

D_MODEL = 1024
N_META = 16
S5_DIM = 512
S5_GROUP = 16
S5_GROUPS = 32
S5_STATE = 64
S5_LANES = S5_GROUPS * S5_STATE
HG_DIM = 512
HG_HEAD_DIM = 128
HG_HEADS = 4
HG_CHUNK = 64
D_FF = 2816
CONV_W = 3
EPS = 1e-6

S5_HALVES = 2
S5_HALF_CH = S5_DIM // S5_HALVES
S5_HALF_ST = S5_LANES // S5_HALVES
SCAN_LANES = 512
FF_CHUNK = 256
N_FF_CHUNKS = D_FF // FF_CHUNK
SUBLANES = 8
LANES = 128
VMEM_LIMIT = 56 * 1024 * 1024


def _sigmoid(x):
    return 0.5 + 0.5 * jnp.tanh(0.5 * x)


def _rms_norm(x, g):
    ms = jnp.mean(x * x, axis=-1, keepdims=True)
    return x * lax.rsqrt(ms + EPS) * g


def _dot(a, b):
    return jnp.dot(a, b, preferred_element_type=F32)


def _params(*sem):
    return pltpu.CompilerParams(dimension_semantics=sem, vmem_limit_bytes=VMEM_LIMIT)


def _prep_kernel(lam_ref, dt_ref, bt_ref, hlb_ref, a_ref, bb_ref, lb_ref):
    lam_re = lam_ref[0]
    lam_im = lam_ref[1]
    dt = jnp.exp(dt_ref[...])
    mag = jnp.exp(lam_re * dt)
    ar = mag * jnp.cos(lam_im * dt)
    ai = mag * jnp.sin(lam_im * dt)
    nr = ar - 1.0
    den = lam_re * lam_re + lam_im * lam_im
    cr = (nr * lam_re + ai * lam_im) / den
    ci = (ai * lam_re - nr * lam_im) / den
    a_ref[0] = ar
    a_ref[1] = ai
    bt_re = bt_ref[0]
    bt_im = bt_ref[1]
    bb_ref[0] = cr * bt_re - ci * bt_im
    bb_ref[1] = cr * bt_im + ci * bt_re
    hlb = hlb_ref[...]
    e = jnp.exp(hlb - jnp.max(hlb, axis=0, keepdims=True))
    lb_ref[...] = e[0:1] / jnp.sum(e, axis=0, keepdims=True)


def _prep(lam_re, lam_im, log_dt, b_re, b_im, hlb):
    g, p = lam_re.shape
    sds = jax.ShapeDtypeStruct
    return pl.pallas_call(
        _prep_kernel,
        out_shape=(sds((2, g, 1, p), F32), sds((2, g, S5_GROUP, p), F32), sds((1, HG_DIM), F32)),
        name="param_prep",
    )(jnp.stack([lam_re, lam_im]).reshape(2, g, 1, p),
      jnp.broadcast_to(log_dt.reshape(g, 1, 1), (g, 1, p)),
      jnp.stack([b_re, b_im]).transpose(0, 1, 3, 2), hlb)


def _block_diag(blocks):
    two, g, r, c = blocks.shape
    per = g // S5_HALVES
    eye = np.eye(per, dtype=np.float32)
    out = jnp.einsum("ajgrc,gh->ajgrhc", blocks.reshape(two, S5_HALVES, per, r, c), eye)
    return out.reshape(two, S5_HALVES, per * r, per * c).astype(BF16)


def _inproj_kernel(x_ref, g_ref, w_ref, u_ref, f_ref, qig_ref, *wb_ref, n, tm):
    x = x_ref[0]
    hn = (x * g_ref[...]).astype(BF16)
    rn = lax.rsqrt(jnp.mean(x * x, axis=-1, keepdims=True) + EPS)
    rn = jnp.broadcast_to(rn, (tm, HG_DIM))
    if wb_ref:
        wb_ref[0][...] = w_ref[...].astype(BF16)
        w_ref = wb_ref[0]
    col = lambda j: _dot(hn, w_ref[:, j * HG_DIM:(j + 1) * HG_DIM]) * rn
    b = pl.program_id(1)
    u = col(0)
    for l in range(S5_DIM // LANES):
        ul = u[:, l * LANES:(l + 1) * LANES]
        if n == 1:
            u_ref[l] = ul
        else:
            u_ref[l, pl.ds(b, tm, stride=n), :] = ul
    f_ref[0] = col(2)
    for j, src in enumerate((1, 3, 4)):
        qig_ref[0, :, j * HG_DIM:(j + 1) * HG_DIM] = col(src).astype(BF16)


def _inproj(x, g, w, tm, n_tiles, emit_bf16=False):
    n, l, d = x.shape
    cols = w.shape[1]
    rows = tm * n_tiles
    const = lambda shape: pl.BlockSpec(shape, lambda i, b: (0,) * len(shape),
                                       pipeline_mode=pl.Buffered(1))
    return pl.pallas_call(
        functools.partial(_inproj_kernel, n=n, tm=tm),
        grid=(n_tiles, n),
        in_specs=[pl.BlockSpec((1, tm, d), lambda i, b: (b, i, 0)),
                  const((1, d)), const((d, cols))],
        out_specs=[pl.BlockSpec((S5_DIM // LANES, tm * n, LANES), lambda i, b: (0, i, 0)),
                   pl.BlockSpec((1, tm, HG_DIM), lambda i, b: (b, i, 0)),
                   pl.BlockSpec((1, tm, 3 * HG_DIM), lambda i, b: (b, i, 0))]
                  + [pl.BlockSpec((d, cols), lambda i, b: (0, 0))] * emit_bf16,
        out_shape=(jax.ShapeDtypeStruct((S5_DIM // LANES, rows * n, LANES), F32),
                   jax.ShapeDtypeStruct((n, rows, HG_DIM), F32),
                   jax.ShapeDtypeStruct((n, rows, 3 * HG_DIM), BF16))
                  + (jax.ShapeDtypeStruct((d, cols), BF16),) * emit_bf16,
        compiler_params=_params("arbitrary", "arbitrary"),
        name="inproj",
    )(x, g, w)


def _gelu_tanh(y):
    return 0.5 * y * (1.0 + jnp.tanh(0.7978845608028654 * (y + 0.044715 * (y * y * y))))


def _s5_tile(step, u_ref, h0r_ref, h0i_ref, a_ref, wb_ref, wc_ref, d_ref,
             y_ref, hr_out_ref, hi_out_ref, xr_s, xi_s, hr_s, hi_s, y_s, *, n, tt, interleaved):
    @pl.when(step == 0)
    def _():
        hr_s[...] = jnp.broadcast_to(h0r_ref[...], hr_s.shape)
        hi_s[...] = jnp.broadcast_to(h0i_ref[...], hi_s.shape)

    slabs = S5_HALF_CH // LANES
    load_u = lambda j: jnp.concatenate([u_ref[j * slabs + l] for l in range(slabs)], axis=1)
    for j in range(S5_HALVES):
        st = slice(j * S5_HALF_ST, (j + 1) * S5_HALF_ST)
        ub = load_u(j).astype(BF16)
        xr_s[:, st] = _dot(ub, wb_ref[0, j])
        xi_s[:, st] = _dot(ub, wb_ref[1, j])
    for c in range(S5_LANES // SCAN_LANES):
        loc = slice(c * SCAN_LANES, (c + 1) * SCAN_LANES)
        ar = jnp.broadcast_to(a_ref[0, :, loc], (n, SCAN_LANES))
        ai = jnp.broadcast_to(a_ref[1, :, loc], (n, SCAN_LANES))

        def step(t, carry, ar=ar, ai=ai, loc=loc):
            hr, hi = carry
            r = 0 if tt == 1 else pl.multiple_of(t * n, n)
            nhr = ar * hr - ai * hi + xr_s[pl.ds(r, n), loc]
            nhi = ar * hi + ai * hr + xi_s[pl.ds(r, n), loc]
            xr_s[pl.ds(r, n), loc] = nhr
            xi_s[pl.ds(r, n), loc] = nhi
            return nhr, nhi

        carry = (hr_s[:, loc], hi_s[:, loc])
        if tt == 1:
            carry = step(0, carry)
        else:
            carry = lax.fori_loop(0, tt, step, carry, unroll=True)
        hr_s[:, loc] = carry[0]
        hi_s[:, loc] = carry[1]
    for j in range(S5_HALVES):
        ch = slice(j * S5_HALF_CH, (j + 1) * S5_HALF_CH)
        st = slice(j * S5_HALF_ST, (j + 1) * S5_HALF_ST)
        y = (_dot(xr_s[:, st].astype(BF16), wc_ref[0, j]) + _dot(xi_s[:, st].astype(BF16), wc_ref[1, j])
             + d_ref[:, ch] * load_u(j))
        if interleaved:
            for l in range(slabs):
                y_s[j * slabs + l] = y[:, l * LANES:(l + 1) * LANES]
        else:
            y_ref[0, :, ch] = y
    if interleaved:
        for b in range(n):
            for l in range(S5_DIM // LANES):
                y_ref[b, :, l * LANES:(l + 1) * LANES] = y_s[l, pl.ds(b, tt, stride=n), :]
    hr_out_ref[...] = hr_s[...]
    hi_out_ref[...] = hi_s[...]


def _s5_kernel(*refs, **static):
    _s5_tile(pl.program_id(0), *refs, **static)


def _s5_scratch(n, rows, interleaved):
    return [pltpu.VMEM((rows, S5_LANES), F32), pltpu.VMEM((rows, S5_LANES), F32),
            pltpu.VMEM((n, S5_LANES), F32), pltpu.VMEM((n, S5_LANES), F32),
            pltpu.VMEM((S5_DIM // LANES, rows if interleaved else SUBLANES, LANES), F32)]


def _s5(u4, h0r, h0i, a, wb, wc, d, n, tt, n_tiles):
    rows = tt * n
    interleaved = n > 1 and tt > 1
    const = lambda shape: pl.BlockSpec(shape, lambda i: (0,) * len(shape),
                                       pipeline_mode=pl.Buffered(1))
    state = pl.BlockSpec((n, S5_LANES), lambda i: (0, 0))
    y_block = (n, tt, S5_DIM) if interleaved else (1, rows, S5_DIM)
    y_shape = (n, tt * n_tiles, S5_DIM) if interleaved else (1, rows * n_tiles, S5_DIM)
    return pl.pallas_call(
        functools.partial(_s5_kernel, n=n, tt=tt, interleaved=interleaved),
        grid=(n_tiles,),
        in_specs=[pl.BlockSpec((S5_DIM // LANES, rows, LANES), lambda i: (0, i, 0)),
                  const(h0r.shape), const(h0i.shape), const(a.shape), const(wb.shape), const(wc.shape),
                  const((1, S5_DIM))],
        out_specs=[pl.BlockSpec(y_block, lambda i: (0, i, 0)), state, state],
        out_shape=(jax.ShapeDtypeStruct(y_shape, F32),
                   jax.ShapeDtypeStruct((n, S5_LANES), F32),
                   jax.ShapeDtypeStruct((n, S5_LANES), F32)),
        scratch_shapes=_s5_scratch(n, rows, interleaved),
        compiler_params=_params("arbitrary"),
        name="s5_scan",
    )(u4, h0r, h0i, a, wb, wc, d)


def _hgrn_gate_out(o, g, ng):
    parts = []
    for h in range(HG_HEADS):
        oh = o[:, h * HG_HEAD_DIM:(h + 1) * HG_HEAD_DIM]
        ms = jnp.mean(oh * oh, axis=-1, keepdims=True)
        parts.append(oh * lax.rsqrt(ms + EPS))
    return jnp.concatenate(parts, axis=-1) * ng * (g * _sigmoid(g))


def _hgrn_tile(row, q_ref, f_ref, i_ref, g_ref, lb_ref, ng_ref, tri_ref, y_ref, st_s, o_s, *, th, valid):
    c = HG_CHUNK
    lb = lb_ref[...]
    f = lb + (1.0 - lb) * _sigmoid(f_ref[row])
    lc = jnp.log(f)
    k = 1.0 - f
    q = q_ref[row]
    if valid < th:
        live = lax.broadcasted_iota(jnp.int32, (th, 1), 0) < valid
        lc = jnp.where(live, lc, 0.0)
        k = jnp.where(live, k, 0.0)
        q = jnp.where(live, q.astype(F32), 0.0).astype(BF16)
    tri = tri_ref[...]
    tr = tri.shape[0]
    lc_hi = lc.astype(BF16)
    lc_lo = (lc - lc_hi.astype(F32)).astype(BF16)
    b = jnp.concatenate([_dot(tri, lc_hi[r:r + tr]) + _dot(tri, lc_lo[r:r + tr])
                         for r in range(0, th, tr)], axis=0)
    eb = jnp.exp(b)
    qd = q * eb.astype(BF16)
    kd = k / eb
    kdb = kd.astype(BF16)
    vb = i_ref[row]
    causal = (lax.broadcasted_iota(jnp.int32, (c, c), 1) <= lax.broadcasted_iota(jnp.int32, (c, c), 0))
    nt = (((1,), (1,)), ((), ()))
    n_chunks = th // c
    rows = [slice(cc * c, (cc + 1) * c) for cc in range(n_chunks)]
    lanes = [slice(h * HG_HEAD_DIM, (h + 1) * HG_HEAD_DIM) for h in range(HG_HEADS)]
    dec = [jnp.exp(b[cc * c + c - 1:cc * c + c, :]) for cc in range(n_chunks)]
    kdec = [(kd[rows[cc]] * dec[cc]).astype(BF16) for cc in range(n_chunks)]
    att = [[jnp.where(causal, lax.dot_general(qd[rows[cc], ls], kdb[rows[cc], ls], nt,
                                              preferred_element_type=F32), 0.0).astype(BF16)
            for ls in lanes] for cc in range(n_chunks)]
    upd = [[_dot(vb[rows[cc], ls].T, kdec[cc][:, ls])
            for ls in lanes] for cc in range(n_chunks)]
    st_in = [[None] * HG_HEADS for _ in range(n_chunks)]
    for h, ls in enumerate(lanes):
        st = st_s[row, h]
        for cc in range(n_chunks):
            st_in[cc][h] = st.astype(BF16)
            st = dec[cc][:, ls] * st + upd[cc][h]
        st_s[row, h] = st
    for cc in range(n_chunks):
        for h, ls in enumerate(lanes):
            o_s[row, rows[cc], ls] = (
                lax.dot_general(qd[rows[cc], ls], st_in[cc][h], nt, preferred_element_type=F32)
                + _dot(att[cc][h], vb[rows[cc], ls]))
    y_ref[row] = _hgrn_gate_out(o_s[row], g_ref[row].astype(F32), ng_ref[...])


def _hgrn_load_state(s0_ref, st_s):
    for r in range(st_s.shape[0]):
        for h in range(HG_HEADS):
            st_s[r, h] = s0_ref[r if s0_ref.shape[0] > 1 else 0, h].T


def _hgrn_store_state(s_out_ref, st_s):
    for r in range(st_s.shape[0]):
        for h in range(HG_HEADS):
            s_out_ref[r, h] = st_s[r, h].T


def _hgrn_kernel(q_ref, f_ref, i_ref, g_ref, lb_ref, ng_ref, tri_ref, s0_ref,
                 y_ref, s_out_ref, st_s, o_s, *, th, valid):
    i = pl.program_id(1)
    pl.when(i == 0)(lambda: _hgrn_load_state(s0_ref, st_s))
    _hgrn_tile(0, q_ref, f_ref, i_ref, g_ref, lb_ref, ng_ref, tri_ref, y_ref, st_s, o_s,
               th=th, valid=valid)
    pl.when(i == pl.num_programs(1) - 1)(lambda: _hgrn_store_state(s_out_ref, st_s))


def _hgrn_tri(tr):
    idx = np.arange(tr)
    tri = (idx[:, None] // HG_CHUNK == idx[None, :] // HG_CHUNK) & (idx[None, :] <= idx[:, None])
    return jnp.asarray(tri, dtype=BF16)


def _hgrn(zf, qig, lb, ng, s0, th, n_tiles, valid):
    n = zf.shape[0]
    own = s0.shape[0] == n
    tr = min(th, 256)
    col = lambda j: pl.BlockSpec((1, th, HG_DIM), lambda b, i, j=j: (b, i, j))
    const = lambda shape: pl.BlockSpec(shape, lambda b, i: (0,) * len(shape),
                                       pipeline_mode=pl.Buffered(1))
    st = pl.BlockSpec((1, HG_HEADS, HG_HEAD_DIM, HG_HEAD_DIM), lambda b, i: (b, 0, 0, 0))
    st_in = pl.BlockSpec((1, HG_HEADS, HG_HEAD_DIM, HG_HEAD_DIM),
                         lambda b, i: (b if own else 0, 0, 0, 0))
    return pl.pallas_call(
        functools.partial(_hgrn_kernel, th=th, valid=valid),
        grid=(n, n_tiles),
        in_specs=[col(0), col(0), col(1), col(2), const((1, HG_DIM)), const((1, HG_DIM)),
                  const((tr, tr)), st_in],
        out_specs=[pl.BlockSpec((1, th, HG_DIM), lambda b, i: (b, i, 0)), st],
        out_shape=(jax.ShapeDtypeStruct((n, n_tiles * th, HG_DIM), F32),
                   jax.ShapeDtypeStruct((n, HG_HEADS, HG_HEAD_DIM, HG_HEAD_DIM), F32)),
        scratch_shapes=[pltpu.VMEM((1, HG_HEADS, HG_HEAD_DIM, HG_HEAD_DIM), F32),
                        pltpu.VMEM((1, th, HG_DIM), F32)],
        compiler_params=_params("arbitrary", "arbitrary"),
        name="hgrn_chunks",
    )(qig, zf, qig, qig, lb, ng, _hgrn_tri(tr), s0)


def _hgrn_step_kernel(q_ref, f_ref, i_ref, g_ref, lb_ref, ng_ref, s0_ref,
                      y_ref, s_out_ref, o_s, *, sb):
    lb = lb_ref[...]
    f = lb + (1.0 - lb) * _sigmoid(f_ref[0])
    q = q_ref[0].astype(F32)
    v = i_ref[0].astype(F32)
    pad = jnp.zeros((HG_HEAD_DIM - sb, HG_HEAD_DIM), F32)
    sq = (HG_HEAD_DIM, HG_HEAD_DIM)
    for h in range(HG_HEADS):
        ls = slice(h * HG_HEAD_DIM, (h + 1) * HG_HEAD_DIM)
        fcols = jnp.concatenate([f[:, ls], pad], axis=0).T
        for s in range(sb):
            fc = jnp.broadcast_to(fcols[:, s:s + 1], sq)
            sn = fc * s0_ref[s, h] + (1.0 - fc) * v[s:s + 1, ls]
            s_out_ref[s, h] = sn
            o_s[s:s + 1, ls] = _dot(q[s:s + 1, ls].astype(BF16), sn.astype(BF16))
    y_ref[0] = _hgrn_gate_out(o_s[...], g_ref[0].astype(F32), ng_ref[...])


def _hgrn_step(zf, qig, lb, ng, s0, sb):
    r = zf.shape[1]
    col = lambda j: pl.BlockSpec((1, sb, HG_DIM), lambda i, j=j: (0, i, j))
    vec = pl.BlockSpec((1, HG_DIM), lambda i: (0, 0))
    st = pl.BlockSpec((sb, HG_HEADS, HG_HEAD_DIM, HG_HEAD_DIM), lambda i: (i, 0, 0, 0))
    return pl.pallas_call(
        functools.partial(_hgrn_step_kernel, sb=sb),
        grid=(r // sb,),
        in_specs=[col(0), col(0), col(1), col(2), vec, vec, st],
        out_specs=[pl.BlockSpec((1, sb, HG_DIM), lambda i: (0, i, 0)), st],
        out_shape=(jax.ShapeDtypeStruct((1, r, HG_DIM), F32),
                   jax.ShapeDtypeStruct(s0.shape, F32)),
        scratch_shapes=[pltpu.VMEM((sb, HG_DIM), F32)],
        compiler_params=_params("arbitrary"),
        name="hgrn_step",
    )(qig, zf, qig, qig, lb, ng, s0)


def _mix_out(y5_raw, yh, h, wglu, bglu, wout):
    y5p = _gelu_tanh(y5_raw)
    y5 = y5p * _sigmoid(_dot(y5p.astype(BF16), wglu) + bglu)
    ymix = jnp.concatenate([y5, yh], axis=-1).astype(BF16)
    return h + _dot(ymix, wout)


def _conv_taps(a, prev8, cw):
    rows = a.shape[0]
    cv = cw[3:4] + pltpu.roll(a, 2, 0) * cw[0:1] + pltpu.roll(a, 1, 0) * cw[1:2] + a * cw[2:3]
    d = prev8 - a[rows - SUBLANES:rows]
    rid = lax.broadcasted_iota(jnp.int32, (SUBLANES, 1), 0)
    fix = (jnp.where(rid < 1, pltpu.roll(d, 1, 0), 0.0) * cw[1:2]
           + jnp.where(rid < 2, pltpu.roll(d, 2, 0), 0.0) * cw[0:1])
    return jnp.concatenate([cv[:SUBLANES] + fix, cv[SUBLANES:]], axis=0)


def _ffn_kernel(y5_ref, yh_ref, h_ref, wglu_ref, bglu_ref, wout_ref, g2_ref,
                wa_ref, wv_ref, wd_ref, cw_ref, cin_ref, gf_ref,
                out_ref, cout_ref, carry_s, hn_s, h1_s, rn_s, s_s, *, tm):
    @pl.when(pl.program_id(1) == 0)
    def _():
        carry_s[...] = cin_ref[0]

    h1 = _mix_out(y5_ref[0], yh_ref[0], h_ref[0], wglu_ref[...], bglu_ref[...], wout_ref[...])
    h1_s[...] = h1
    hn_s[...] = (h1 * g2_ref[...]).astype(BF16)
    rn = lax.rsqrt(jnp.mean(h1 * h1, axis=-1, keepdims=True) + EPS)
    rn_s[...] = jnp.broadcast_to(rn, (tm, FF_CHUNK))

    for j in range(N_FF_CHUNKS):
        cs = slice(j * FF_CHUNK, (j + 1) * FF_CHUNK)
        hn = hn_s[...]
        a = _dot(hn, wa_ref[:, cs]) * rn_s[...]
        v = _dot(hn, wv_ref[:, cs]) * rn_s[...]
        cv = _conv_taps(a, carry_s[:, cs], cw_ref[:, cs])
        carry_s[:, cs] = a[tm - SUBLANES:tm]
        s_s[:, cs] = ((cv * _sigmoid(cv)) * v).astype(BF16)

    out_ref[0] = _rms_norm(h1_s[...] + _dot(s_s[...], wd_ref[...]), gf_ref[...])
    cout_ref[0] = carry_s[...]


def _ffn_small_kernel(y5m_ref, y5s_ref, yhm_ref, yhs_ref, hm_ref, hs_ref,
                      wglu_ref, bglu_ref, wout_ref, g2_ref, wa_ref, wv_ref, wd_ref, cw_ref,
                      cin_a_ref, cin_b_ref, gf_ref,
                      out_ref, a_out_ref, mcarry_ref,
                      wglu_bf_ref, wout_bf_ref, wa_bf_ref, wv_bf_ref, wd_bf_ref,
                      hn_s, h1_s, acc_s, *, n_meta):
    j = pl.program_id(0)

    @pl.when(j == 0)
    def _():
        wglu = wglu_ref[...].astype(BF16)
        wout = wout_ref[...].astype(BF16)
        wglu_bf_ref[...] = wglu
        wout_bf_ref[...] = wout
        rows = lambda m_ref, s_ref: jnp.concatenate([m_ref[0], s_ref[0]], axis=0)
        h1 = _mix_out(rows(y5m_ref, y5s_ref), rows(yhm_ref, yhs_ref), rows(hm_ref, hs_ref),
                      wglu, bglu_ref[...], wout)
        h1_s[...] = h1
        hn_s[...] = _rms_norm(h1, g2_ref[...]).astype(BF16)
        acc_s[...] = jnp.zeros_like(acc_s)

    wa = wa_ref[...].astype(BF16)
    wv = wv_ref[...].astype(BF16)
    wd = wd_ref[...].astype(BF16)
    wa_bf_ref[...] = wa
    wv_bf_ref[...] = wv
    wd_bf_ref[...] = wd
    hn = hn_s[...]
    a = _dot(hn, wa)
    v = _dot(hn, wv)
    cw = cw_ref[...]
    a_m, a_s = a[:n_meta], a[n_meta:]
    cv_m = _conv_taps(a_m, jnp.zeros((SUBLANES, FF_CHUNK), F32), cw)
    cv_s = cw[3:4] + cin_a_ref[...] * cw[0:1] + cin_b_ref[...] * cw[1:2] + a_s * cw[2:3]
    mcarry_ref[...] = a_m[n_meta - SUBLANES:]
    a_out_ref[...] = a_s
    cv = jnp.concatenate([cv_m, cv_s], axis=0)
    acc_s[...] += _dot(((cv * _sigmoid(cv)) * v).astype(BF16), wd)

    @pl.when(j == pl.num_programs(0) - 1)
    def _():
        out_ref[...] = _rms_norm((h1_s[...] + acc_s[...])[n_meta:], gf_ref[...])


def _ffn_small(y5m, y5s, yhm, yhs, hm, hs, wglu, bglu, wout, g2, wup, wd, cw, cin_a, cin_b, gf):
    n_meta, r = y5m.shape[1], y5s.shape[1]
    const = lambda shape: pl.BlockSpec(shape, lambda j: (0,) * len(shape))
    head = lambda rows, width: pl.BlockSpec((1, rows, width), lambda j: (0, 0, 0))
    ff_cols = lambda rows: pl.BlockSpec((rows, FF_CHUNK), lambda j: (0, j))
    sds = jax.ShapeDtypeStruct
    return pl.pallas_call(
        functools.partial(_ffn_small_kernel, n_meta=n_meta),
        grid=(N_FF_CHUNKS,),
        in_specs=[head(n_meta, S5_DIM), head(r, S5_DIM), head(n_meta, HG_DIM), head(r, HG_DIM),
                  head(n_meta, D_MODEL), head(r, D_MODEL),
                  const(wglu.shape), const(bglu.shape), const(wout.shape), const(g2.shape),
                  ff_cols(D_MODEL),
                  pl.BlockSpec((D_MODEL, FF_CHUNK), lambda j: (0, N_FF_CHUNKS + j)),
                  pl.BlockSpec((FF_CHUNK, D_MODEL), lambda j: (j, 0)),
                  ff_cols(SUBLANES), ff_cols(r), ff_cols(r), const(gf.shape)],
        out_specs=[const((r, D_MODEL)), ff_cols(r), ff_cols(SUBLANES),
                   const(wglu.shape), const(wout.shape), ff_cols(D_MODEL), ff_cols(D_MODEL),
                   pl.BlockSpec((FF_CHUNK, D_MODEL), lambda j: (j, 0))],
        out_shape=(sds((r, D_MODEL), F32), sds((r, D_FF), F32), sds((SUBLANES, D_FF), F32),
                   sds(wglu.shape, BF16), sds(wout.shape, BF16),
                   sds((D_MODEL, D_FF), BF16), sds((D_MODEL, D_FF), BF16), sds(wd.shape, BF16)),
        scratch_shapes=[pltpu.VMEM((n_meta + r, D_MODEL), BF16),
                        pltpu.VMEM((n_meta + r, D_MODEL), F32),
                        pltpu.VMEM((n_meta + r, D_MODEL), F32)],
        compiler_params=_params("arbitrary"),
        name="ffn_small",
    )(y5m, y5s, yhm, yhs, hm, hs, wglu, bglu, wout, g2, wup, wup, wd, cw, cin_a, cin_b, gf)


def _ffn(y5, yh, h, wglu, bglu, wout, g2, wa, wv, wd, cw, cin, gf, tm, n_tiles):
    n = h.shape[0]
    const = lambda shape: pl.BlockSpec(shape, lambda b, i: (0,) * len(shape),
                                       pipeline_mode=pl.Buffered(1))
    own = cin.shape[0] == n
    conv = lambda index: pl.BlockSpec((1, SUBLANES, D_FF), index)
    return pl.pallas_call(
        functools.partial(_ffn_kernel, tm=tm),
        grid=(n, n_tiles),
        in_specs=[pl.BlockSpec((1, tm, S5_DIM), lambda b, i: (b, i, 0)),
                  pl.BlockSpec((1, tm, HG_DIM), lambda b, i: (b, i, 0)),
                  pl.BlockSpec((1, tm, D_MODEL), lambda b, i: (b, i, 0)),
                  const(wglu.shape), const(bglu.shape), const(wout.shape), const(g2.shape),
                  const(wa.shape), const(wv.shape), const(wd.shape), const(cw.shape),
                  conv(lambda b, i: (b if own else 0, 0, 0)), const(gf.shape)],
        out_specs=[pl.BlockSpec((1, tm, D_MODEL), lambda b, i: (b, i, 0)),
                   conv(lambda b, i: (b, 0, 0))],
        out_shape=(jax.ShapeDtypeStruct((n, n_tiles * tm, D_MODEL), F32),
                   jax.ShapeDtypeStruct((n, SUBLANES, D_FF), F32)),
        scratch_shapes=[pltpu.VMEM((SUBLANES, D_FF), F32),
                        pltpu.VMEM((tm, D_MODEL), BF16),
                        pltpu.VMEM((tm, D_MODEL), F32),
                        pltpu.VMEM((tm, FF_CHUNK), F32),
                        pltpu.VMEM((tm, D_FF), BF16)],
        compiler_params=_params("arbitrary", "arbitrary"),
        name="mix_out_ffn",
    )(y5, yh, h, wglu, bglu, wout, g2, wa, wv, wd, cw, cin, gf)


def kernel(x_prompt, x_sample, state_s5_re, state_s5_im, state_hgrn, state_ffn_conv, meta_tokens, norm_mix_g, w_in, s5_lambda_re, s5_lambda_im, s5_log_dt, s5_b_re, s5_b_im, s5_c_re, s5_c_im, s5_d, s5_w_glu, s5_b_glu, hg_lower_bounds, hg_norm_g, w_out, norm_ffn_g, ffn_w_up, ffn_conv_w, ffn_conv_b, ffn_w_down, final_norm_g):
    nb, seq, _ = x_prompt.shape
    ns = x_sample.shape[0]
    li = 0

    a5, bb, lb = _prep(s5_lambda_re[li], s5_lambda_im[li], s5_log_dt[li],
                       s5_b_re[li], s5_b_im[li], hg_lower_bounds)
    a5 = a5.reshape(2, 1, S5_LANES)
    wb = _block_diag(bb)
    wc = _block_diag(jnp.stack([s5_c_re[li], -s5_c_im[li]]).transpose(0, 1, 3, 2))
    d5 = s5_d[li].reshape(1, S5_DIM)
    g1 = norm_mix_g[li].reshape(1, D_MODEL)
    g2 = norm_ffn_g[li].reshape(1, D_MODEL)
    gf = final_norm_g.reshape(1, D_MODEL)
    ng = hg_norm_g[li].reshape(1, HG_DIM)
    bglu = s5_b_glu[li].reshape(1, S5_DIM)
    cw = jnp.concatenate([ffn_conv_w[li], ffn_conv_b[li][None],
                          jnp.zeros((SUBLANES - CONV_W - 1, D_FF), F32)], axis=0)
    s5w = (a5, wb, wc, d5)

    xs = x_sample.reshape(1, ns, D_MODEL)
    u_tm, zf, qig, w_in_b = _inproj(xs, g1, w_in[li], ns, 1, emit_bf16=True)
    y5s, s5r, s5i = _s5(u_tm, state_s5_re[li].reshape(ns, S5_LANES), state_s5_im[li].reshape(ns, S5_LANES),
                        *s5w, ns, 1, 1)
    yhs, shg = _hgrn_step(zf, qig, lb, ng, state_hgrn[li], 16)

    meta = jnp.zeros((1, HG_CHUNK, D_MODEL), F32).at[0, :N_META].set(meta_tokens)
    z5 = jnp.zeros((1, S5_LANES), F32)
    zh = jnp.zeros((1, HG_HEADS, HG_HEAD_DIM, HG_HEAD_DIM), F32)
    u_tm, zf, qig = _inproj(meta, g1, w_in_b, HG_CHUNK, 1)
    y5m, m5r, m5i = _s5(u_tm, z5, z5, *s5w, 1, N_META, 1)
    yhm, mhg = _hgrn(zf, qig, lb, ng, zh, HG_CHUNK, 1, N_META)

    buf = state_ffn_conv[li]
    y_sample, a_new, mconv, wglu, wout, wa, wv, wd = _ffn_small(
        y5m, y5s, yhm, yhs, meta, xs, s5_w_glu[li], bglu, w_out[li], g2,
        ffn_w_up[li], ffn_w_down[li], cw, buf[:, 0], buf[:, 1], gf)
    sconv = jnp.stack([buf[:, 1], a_new], axis=1)

    ti, tm, tt, th = 512, 512, 128, 2048
    u_tm, zf, qig = _inproj(x_prompt, g1, w_in_b, ti, seq // ti)
    y5, p5r, p5i = _s5(u_tm, m5r, m5i, *s5w, nb, tt, seq // tt)
    yh, phg = _hgrn(zf, qig, lb, ng, mhg, th, seq // th, th)
    y_prompt, pconv = _ffn(y5, yh, x_prompt, wglu, bglu, wout, g2, wa, wv, wd, cw, mconv[None], gf,
                           tm, seq // tm)

    st5 = lambda t: t.reshape(1, -1, S5_GROUPS, S5_STATE)
    pconv = pconv[:, SUBLANES - (CONV_W - 1):, :]
    return (y_prompt, y_sample.reshape(ns, 1, D_MODEL),
            st5(p5r), st5(p5i), phg[None], pconv[None],
            st5(s5r), st5(s5i), shg[None], sconv[None])
```

```python
import functools

import jax
import jax.numpy as jnp
import numpy as np
from jax import lax
from jax.experimental import pallas as pl
from jax.experimental.pallas import tpu as pltpu

F32 = jnp.float32
BF16 = jnp.bfloat16

D_MODEL = 1024
N_META = 16
S5_DIM = 512
S5_GROUP = 16
S5_GROUPS = 32
S5_STATE = 64
S5_LANES = S5_GROUPS * S5_STATE
HG_DIM = 512
HG_HEAD_DIM = 128
HG_HEADS = 4
HG_CHUNK = 64
D_FF = 2816
CONV_W = 3
EPS = 1e-6

S5_HALVES = 2
S5_HALF_CH = S5_DIM // S5_HALVES
S5_HALF_ST = S5_LANES // S5_HALVES
SCAN_LANES = 512
FF_CHUNK = 256
N_FF_CHUNKS = D_FF // FF_CHUNK
SUBLANES = 8
LANES = 128
VMEM_LIMIT = 56 * 1024 * 1024
IN_TILE = 512
S5_TILE = 128
FFN_TILE = 512
STEP_ROWS = 16


def _sigmoid(x):
    return 0.5 + 0.5 * jnp.tanh(0.5 * x)


def _rms_norm(x, g):
    ms = jnp.mean(x * x, axis=-1, keepdims=True)
    return x * lax.rsqrt(ms + EPS) * g


def _dot(a, b):
    return jnp.dot(a, b, preferred_element_type=F32)


def _params(*sem):
    return pltpu.CompilerParams(dimension_semantics=sem, vmem_limit_bytes=VMEM_LIMIT)


def _prep_kernel(lam_ref, dt_ref, bt_ref, hlb_ref, a_ref, bb_ref, lb_ref):
    lam_re = lam_ref[0]
    lam_im = lam_ref[1]
    dt = jnp.exp(dt_ref[...])
    mag = jnp.exp(lam_re * dt)
    ar = mag * jnp.cos(lam_im * dt)
    ai = mag * jnp.sin(lam_im * dt)
    nr = ar - 1.0
    den = lam_re * lam_re + lam_im * lam_im
    cr = (nr * lam_re + ai * lam_im) / den
    ci = (ai * lam_re - nr * lam_im) / den
    a_ref[0] = ar
    a_ref[1] = ai
    bt_re = bt_ref[0]
    bt_im = bt_ref[1]
    bb_ref[0] = cr * bt_re - ci * bt_im
    bb_ref[1] = cr * bt_im + ci * bt_re
    hlb = hlb_ref[...]
    e = jnp.exp(hlb - jnp.max(hlb, axis=0, keepdims=True))
    lb_ref[...] = e[0:1] / jnp.sum(e, axis=0, keepdims=True)


def _prep(lam_re, lam_im, log_dt, b_re, b_im, hlb):
    g, p = lam_re.shape
    sds = jax.ShapeDtypeStruct
    return pl.pallas_call(
        _prep_kernel,
        out_shape=(sds((2, g, 1, p), F32), sds((2, g, S5_GROUP, p), F32), sds((1, HG_DIM), F32)),
        name="param_prep",
    )(jnp.stack([lam_re, lam_im]).reshape(2, g, 1, p),
      jnp.broadcast_to(log_dt.reshape(g, 1, 1), (g, 1, p)),
      jnp.stack([b_re, b_im]).transpose(0, 1, 3, 2), hlb)


def _block_diag(blocks):
    two, g, r, c = blocks.shape
    per = g // S5_HALVES
    eye = np.eye(per, dtype=np.float32)
    out = jnp.einsum("ajgrc,gh->ajgrhc", blocks.reshape(two, S5_HALVES, per, r, c), eye)
    return out.reshape(two, S5_HALVES, per * r, per * c).astype(BF16)


def _inproj_kernel(x_ref, g_ref, w_ref, u_ref, f_ref, qig_ref, *wb_ref, n, tm):
    x = x_ref[0]
    hn = (x * g_ref[...]).astype(BF16)
    rn = lax.rsqrt(jnp.mean(x * x, axis=-1, keepdims=True) + EPS)
    rn = jnp.broadcast_to(rn, (tm, HG_DIM))
    if wb_ref:
        wb_ref[0][...] = w_ref[...].astype(BF16)
        w_ref = wb_ref[0]
    col = lambda j: _dot(hn, w_ref[:, j * HG_DIM:(j + 1) * HG_DIM]) * rn
    b = pl.program_id(1)
    u = col(0)
    for l in range(S5_DIM // LANES):
        ul = u[:, l * LANES:(l + 1) * LANES]
        if n == 1:
            u_ref[l] = ul
        else:
            u_ref[l, pl.ds(b, tm, stride=n), :] = ul
    f_ref[0] = col(2)
    for j, src in enumerate((1, 3, 4)):
        qig_ref[0, :, j * HG_DIM:(j + 1) * HG_DIM] = col(src).astype(BF16)


def _inproj(x, g, w, tm, n_tiles, emit_bf16=False):
    n, l, d = x.shape
    cols = w.shape[1]
    rows = tm * n_tiles
    const = lambda shape: pl.BlockSpec(shape, lambda i, b: (0,) * len(shape),
                                       pipeline_mode=pl.Buffered(1))
    return pl.pallas_call(
        functools.partial(_inproj_kernel, n=n, tm=tm),
        grid=(n_tiles, n),
        in_specs=[pl.BlockSpec((1, tm, d), lambda i, b: (b, i, 0)),
                  const((1, d)), const((d, cols))],
        out_specs=[pl.BlockSpec((S5_DIM // LANES, tm * n, LANES), lambda i, b: (0, i, 0)),
                   pl.BlockSpec((1, tm, HG_DIM), lambda i, b: (b, i, 0)),
                   pl.BlockSpec((1, tm, 3 * HG_DIM), lambda i, b: (b, i, 0))]
                  + [pl.BlockSpec((d, cols), lambda i, b: (0, 0))] * emit_bf16,
        out_shape=(jax.ShapeDtypeStruct((S5_DIM // LANES, rows * n, LANES), F32),
                   jax.ShapeDtypeStruct((n, rows, HG_DIM), F32),
                   jax.ShapeDtypeStruct((n, rows, 3 * HG_DIM), BF16))
                  + (jax.ShapeDtypeStruct((d, cols), BF16),) * emit_bf16,
        compiler_params=_params("arbitrary", "arbitrary"),
        name="inproj",
    )(x, g, w)


def _gelu_tanh(y):
    return 0.5 * y * (1.0 + jnp.tanh(0.7978845608028654 * (y + 0.044715 * (y * y * y))))


def _s5_kernel(u_ref, h0r_ref, h0i_ref, a_ref, wb_ref, wc_ref, d_ref,
               y_ref, hr_out_ref, hi_out_ref, xr_s, xi_s, hr_s, hi_s, y_s, *, n, tt, interleaved):
    @pl.when(pl.program_id(0) == 0)
    def _():
        hr_s[...] = jnp.broadcast_to(h0r_ref[...], hr_s.shape)
        hi_s[...] = jnp.broadcast_to(h0i_ref[...], hi_s.shape)

    slabs = S5_HALF_CH // LANES
    load_u = lambda j: jnp.concatenate([u_ref[j * slabs + l] for l in range(slabs)], axis=1)
    for j in range(S5_HALVES):
        st = slice(j * S5_HALF_ST, (j + 1) * S5_HALF_ST)
        ub = load_u(j).astype(BF16)
        xr_s[:, st] = _dot(ub, wb_ref[0, j])
        xi_s[:, st] = _dot(ub, wb_ref[1, j])
    for c in range(S5_LANES // SCAN_LANES):
        loc = slice(c * SCAN_LANES, (c + 1) * SCAN_LANES)
        ar = jnp.broadcast_to(a_ref[0, :, loc], (n, SCAN_LANES))
        ai = jnp.broadcast_to(a_ref[1, :, loc], (n, SCAN_LANES))

        def step(t, carry, ar=ar, ai=ai, loc=loc):
            hr, hi = carry
            r = 0 if tt == 1 else pl.multiple_of(t * n, n)
            nhr = ar * hr - ai * hi + xr_s[pl.ds(r, n), loc]
            nhi = ar * hi + ai * hr + xi_s[pl.ds(r, n), loc]
            xr_s[pl.ds(r, n), loc] = nhr
            xi_s[pl.ds(r, n), loc] = nhi
            return nhr, nhi

        carry = (hr_s[:, loc], hi_s[:, loc])
        if tt == 1:
            carry = step(0, carry)
        else:
            carry = lax.fori_loop(0, tt, step, carry, unroll=True)
        hr_s[:, loc] = carry[0]
        hi_s[:, loc] = carry[1]
    for j in range(S5_HALVES):
        ch = slice(j * S5_HALF_CH, (j + 1) * S5_HALF_CH)
        st = slice(j * S5_HALF_ST, (j + 1) * S5_HALF_ST)
        y = (_dot(xr_s[:, st].astype(BF16), wc_ref[0, j]) + _dot(xi_s[:, st].astype(BF16), wc_ref[1, j])
             + d_ref[:, ch] * load_u(j))
        if interleaved:
            for l in range(slabs):
                y_s[j * slabs + l] = y[:, l * LANES:(l + 1) * LANES]
        else:
            y_ref[0, :, ch] = y
    if interleaved:
        for b in range(n):
            for l in range(S5_DIM // LANES):
                y_ref[b, :, l * LANES:(l + 1) * LANES] = y_s[l, pl.ds(b, tt, stride=n), :]
    hr_out_ref[...] = hr_s[...]
    hi_out_ref[...] = hi_s[...]


def _s5(u4, h0r, h0i, a, wb, wc, d, n, tt, n_tiles):
    rows = tt * n
    interleaved = n > 1 and tt > 1
    const = lambda shape: pl.BlockSpec(shape, lambda i: (0,) * len(shape),
                                       pipeline_mode=pl.Buffered(1))
    state = pl.BlockSpec((n, S5_LANES), lambda i: (0, 0))
    y_block = (n, tt, S5_DIM) if interleaved else (1, rows, S5_DIM)
    y_shape = (n, tt * n_tiles, S5_DIM) if interleaved else (1, rows * n_tiles, S5_DIM)
    return pl.pallas_call(
        functools.partial(_s5_kernel, n=n, tt=tt, interleaved=interleaved),
        grid=(n_tiles,),
        in_specs=[pl.BlockSpec((S5_DIM // LANES, rows, LANES), lambda i: (0, i, 0)),
                  const(h0r.shape), const(h0i.shape), const(a.shape), const(wb.shape), const(wc.shape),
                  const((1, S5_DIM))],
        out_specs=[pl.BlockSpec(y_block, lambda i: (0, i, 0)), state, state],
        out_shape=(jax.ShapeDtypeStruct(y_shape, F32),
                   jax.ShapeDtypeStruct((n, S5_LANES), F32),
                   jax.ShapeDtypeStruct((n, S5_LANES), F32)),
        scratch_shapes=[pltpu.VMEM((rows, S5_LANES), F32), pltpu.VMEM((rows, S5_LANES), F32),
                        pltpu.VMEM((n, S5_LANES), F32), pltpu.VMEM((n, S5_LANES), F32),
                        pltpu.VMEM((S5_DIM // LANES, rows if interleaved else SUBLANES, LANES), F32)],
        compiler_params=_params("arbitrary"),
        name="s5_scan",
    )(u4, h0r, h0i, a, wb, wc, d)


def _hgrn_gate_out(o, g, ng):
    parts = []
    for h in range(HG_HEADS):
        oh = o[:, h * HG_HEAD_DIM:(h + 1) * HG_HEAD_DIM]
        ms = jnp.mean(oh * oh, axis=-1, keepdims=True)
        parts.append(oh * lax.rsqrt(ms + EPS))
    return jnp.concatenate(parts, axis=-1) * ng * (g * _sigmoid(g))


def _hgrn_kernel(q_ref, f_ref, i_ref, g_ref, lb_ref, ng_ref, tri_ref, s0_ref,
                 y_ref, s_out_ref, st_s, o_s, *, th, valid):
    i = pl.program_id(1)
    c = HG_CHUNK

    @pl.when(i == 0)
    def _():
        for h in range(HG_HEADS):
            st_s[h] = s0_ref[0, h].T

    lb = lb_ref[...]
    f = lb + (1.0 - lb) * _sigmoid(f_ref[0])
    lc = jnp.log(f)
    k = 1.0 - f
    q = q_ref[0]
    if valid < th:
        live = lax.broadcasted_iota(jnp.int32, (th, 1), 0) < valid
        lc = jnp.where(live, lc, 0.0)
        k = jnp.where(live, k, 0.0)
        q = jnp.where(live, q.astype(F32), 0.0).astype(BF16)
    tri = tri_ref[...]
    tr = tri.shape[0]
    lc_hi = lc.astype(BF16)
    lc_lo = (lc - lc_hi.astype(F32)).astype(BF16)
    b = jnp.concatenate([_dot(tri, lc_hi[r:r + tr]) + _dot(tri, lc_lo[r:r + tr])
                         for r in range(0, th, tr)], axis=0)
    qd = q * jnp.exp(b).astype(BF16)
    kd = k * jnp.exp(-b)
    kdb = kd.astype(BF16)
    vb = i_ref[0]
    causal = (lax.broadcasted_iota(jnp.int32, (c, c), 1) <= lax.broadcasted_iota(jnp.int32, (c, c), 0))
    nt = (((1,), (1,)), ((), ()))
    n_chunks = th // c
    rows = [slice(cc * c, (cc + 1) * c) for cc in range(n_chunks)]
    lanes = [slice(h * HG_HEAD_DIM, (h + 1) * HG_HEAD_DIM) for h in range(HG_HEADS)]
    dec = [jnp.exp(b[cc * c + c - 1:cc * c + c, :]) for cc in range(n_chunks)]
    kdec = [(kd[rows[cc]] * dec[cc]).astype(BF16) for cc in range(n_chunks)]
    att = [[jnp.where(causal, lax.dot_general(qd[rows[cc], ls], kdb[rows[cc], ls], nt,
                                              preferred_element_type=F32), 0.0).astype(BF16)
            for ls in lanes] for cc in range(n_chunks)]
    upd = [[_dot(vb[rows[cc], ls].T, kdec[cc][:, ls])
            for ls in lanes] for cc in range(n_chunks)]
    st_in = [[None] * HG_HEADS for _ in range(n_chunks)]
    for h, ls in enumerate(lanes):
        st = st_s[h]
        for cc in range(n_chunks):
            st_in[cc][h] = st.astype(BF16)
            st = dec[cc][:, ls] * st + upd[cc][h]
        st_s[h] = st
    for cc in range(n_chunks):
        for h, ls in enumerate(lanes):
            o_s[rows[cc], ls] = (
                lax.dot_general(qd[rows[cc], ls], st_in[cc][h], nt, preferred_element_type=F32)
                + _dot(att[cc][h], vb[rows[cc], ls]))
    y_ref[0] = _hgrn_gate_out(o_s[...], g_ref[0].astype(F32), ng_ref[...])

    @pl.when(i == pl.num_programs(1) - 1)
    def _():
        for h in range(HG_HEADS):
            s_out_ref[0, h] = st_s[h].T


def _hgrn_tri(tr):
    idx = np.arange(tr)
    tri = (idx[:, None] // HG_CHUNK == idx[None, :] // HG_CHUNK) & (idx[None, :] <= idx[:, None])
    return jnp.asarray(tri, dtype=BF16)


def _hgrn(zf, qig, lb, ng, s0, th, n_tiles, valid):
    n = zf.shape[0]
    own = s0.shape[0] == n
    tr = min(th, 256)
    col = lambda j: pl.BlockSpec((1, th, HG_DIM), lambda b, i, j=j: (b, i, j))
    const = lambda shape: pl.BlockSpec(shape, lambda b, i: (0,) * len(shape),
                                       pipeline_mode=pl.Buffered(1))
    st = pl.BlockSpec((1, HG_HEADS, HG_HEAD_DIM, HG_HEAD_DIM), lambda b, i: (b, 0, 0, 0))
    st_in = pl.BlockSpec((1, HG_HEADS, HG_HEAD_DIM, HG_HEAD_DIM),
                         lambda b, i: (b if own else 0, 0, 0, 0))
    return pl.pallas_call(
        functools.partial(_hgrn_kernel, th=th, valid=valid),
        grid=(n, n_tiles),
        in_specs=[col(0), col(0), col(1), col(2), const((1, HG_DIM)), const((1, HG_DIM)),
                  const((tr, tr)), st_in],
        out_specs=[pl.BlockSpec((1, th, HG_DIM), lambda b, i: (b, i, 0)), st],
        out_shape=(jax.ShapeDtypeStruct((n, n_tiles * th, HG_DIM), F32),
                   jax.ShapeDtypeStruct((n, HG_HEADS, HG_HEAD_DIM, HG_HEAD_DIM), F32)),
        scratch_shapes=[pltpu.VMEM((HG_HEADS, HG_HEAD_DIM, HG_HEAD_DIM), F32),
                        pltpu.VMEM((th, HG_DIM), F32)],
        compiler_params=_params("arbitrary", "arbitrary"),
        name="hgrn_chunks",
    )(qig, zf, qig, qig, lb, ng, _hgrn_tri(tr), s0)


def _hgrn_step_kernel(q_ref, f_ref, i_ref, g_ref, lb_ref, ng_ref, s0_ref,
                      y_ref, s_out_ref, o_s, *, sb):
    lb = lb_ref[...]
    f = lb + (1.0 - lb) * _sigmoid(f_ref[0])
    q = q_ref[0].astype(F32)
    v = i_ref[0].astype(F32)
    pad = jnp.zeros((HG_HEAD_DIM - sb, HG_HEAD_DIM), F32)
    sq = (HG_HEAD_DIM, HG_HEAD_DIM)
    for h in range(HG_HEADS):
        ls = slice(h * HG_HEAD_DIM, (h + 1) * HG_HEAD_DIM)
        fcols = jnp.concatenate([f[:, ls], pad], axis=0).T
        for s in range(sb):
            fc = jnp.broadcast_to(fcols[:, s:s + 1], sq)
            sn = fc * s0_ref[s, h] + (1.0 - fc) * v[s:s + 1, ls]
            s_out_ref[s, h] = sn
            o_s[s:s + 1, ls] = _dot(q[s:s + 1, ls].astype(BF16), sn.astype(BF16))
    y_ref[0] = _hgrn_gate_out(o_s[...], g_ref[0].astype(F32), ng_ref[...])


def _hgrn_step(zf, qig, lb, ng, s0, sb):
    r = zf.shape[1]
    col = lambda j: pl.BlockSpec((1, sb, HG_DIM), lambda i, j=j: (0, i, j))
    vec = pl.BlockSpec((1, HG_DIM), lambda i: (0, 0))
    st = pl.BlockSpec((sb, HG_HEADS, HG_HEAD_DIM, HG_HEAD_DIM), lambda i: (i, 0, 0, 0))
    return pl.pallas_call(
        functools.partial(_hgrn_step_kernel, sb=sb),
        grid=(r // sb,),
        in_specs=[col(0), col(0), col(1), col(2), vec, vec, st],
        out_specs=[pl.BlockSpec((1, sb, HG_DIM), lambda i: (0, i, 0)), st],
        out_shape=(jax.ShapeDtypeStruct((1, r, HG_DIM), F32),
                   jax.ShapeDtypeStruct(s0.shape, F32)),
        scratch_shapes=[pltpu.VMEM((sb, HG_DIM), F32)],
        compiler_params=_params("arbitrary"),
        name="hgrn_step",
    )(qig, zf, qig, qig, lb, ng, s0)


def _mix_out(y5_raw, yh, h, wglu, bglu, wout):
    y5p = _gelu_tanh(y5_raw)
    y5 = y5p * _sigmoid(_dot(y5p.astype(BF16), wglu) + bglu)
    ymix = jnp.concatenate([y5, yh], axis=-1).astype(BF16)
    return h + _dot(ymix, wout)


def _conv_taps(a, prev8, cw):
    rows = a.shape[0]
    cv = cw[3:4] + pltpu.roll(a, 2, 0) * cw[0:1] + pltpu.roll(a, 1, 0) * cw[1:2] + a * cw[2:3]
    d = prev8 - a[rows - SUBLANES:rows]
    rid = lax.broadcasted_iota(jnp.int32, (SUBLANES, 1), 0)
    fix = (jnp.where(rid < 1, pltpu.roll(d, 1, 0), 0.0) * cw[1:2]
           + jnp.where(rid < 2, pltpu.roll(d, 2, 0), 0.0) * cw[0:1])
    return jnp.concatenate([cv[:SUBLANES] + fix, cv[SUBLANES:]], axis=0)


def _ffn_kernel(y5_ref, yh_ref, h_ref, wglu_ref, bglu_ref, wout_ref, g2_ref,
                wa_ref, wv_ref, wd_ref, cw_ref, cin_ref, gf_ref,
                out_ref, cout_ref, carry_s, hn_s, h1_s, rn_s, s_s, *, tm):
    @pl.when(pl.program_id(1) == 0)
    def _():
        carry_s[...] = cin_ref[0]

    h1 = _mix_out(y5_ref[0], yh_ref[0], h_ref[0], wglu_ref[...], bglu_ref[...], wout_ref[...])
    h1_s[...] = h1
    hn_s[...] = (h1 * g2_ref[...]).astype(BF16)
    rn = lax.rsqrt(jnp.mean(h1 * h1, axis=-1, keepdims=True) + EPS)
    rn_s[...] = jnp.broadcast_to(rn, (tm, FF_CHUNK))

    for j in range(N_FF_CHUNKS):
        cs = slice(j * FF_CHUNK, (j + 1) * FF_CHUNK)
        hn = hn_s[...]
        a = _dot(hn, wa_ref[:, cs]) * rn_s[...]
        v = _dot(hn, wv_ref[:, cs]) * rn_s[...]
        cv = _conv_taps(a, carry_s[:, cs], cw_ref[:, cs])
        carry_s[:, cs] = a[tm - SUBLANES:tm]
        s_s[:, cs] = ((cv * _sigmoid(cv)) * v).astype(BF16)

    out_ref[0] = _rms_norm(h1_s[...] + _dot(s_s[...], wd_ref[...]), gf_ref[...])
    cout_ref[0] = carry_s[...]


def _ffn_small_kernel(y5m_ref, y5s_ref, yhm_ref, yhs_ref, hm_ref, hs_ref,
                      wglu_ref, bglu_ref, wout_ref, g2_ref, wa_ref, wv_ref, wd_ref, cw_ref,
                      cin_a_ref, cin_b_ref, gf_ref,
                      out_ref, a_out_ref, mcarry_ref,
                      wglu_bf_ref, wout_bf_ref, wa_bf_ref, wv_bf_ref, wd_bf_ref,
                      hn_s, h1_s, acc_s, *, n_meta):
    j = pl.program_id(0)

    @pl.when(j == 0)
    def _():
        wglu = wglu_ref[...].astype(BF16)
        wout = wout_ref[...].astype(BF16)
        wglu_bf_ref[...] = wglu
        wout_bf_ref[...] = wout
        rows = lambda m_ref, s_ref: jnp.concatenate([m_ref[0], s_ref[0]], axis=0)
        h1 = _mix_out(rows(y5m_ref, y5s_ref), rows(yhm_ref, yhs_ref), rows(hm_ref, hs_ref),
                      wglu, bglu_ref[...], wout)
        h1_s[...] = h1
        hn_s[...] = _rms_norm(h1, g2_ref[...]).astype(BF16)
        acc_s[...] = jnp.zeros_like(acc_s)

    wa = wa_ref[...].astype(BF16)
    wv = wv_ref[...].astype(BF16)
    wd = wd_ref[...].astype(BF16)
    wa_bf_ref[...] = wa
    wv_bf_ref[...] = wv
    wd_bf_ref[...] = wd
    hn = hn_s[...]
    a = _dot(hn, wa)
    v = _dot(hn, wv)
    cw = cw_ref[...]
    a_m, a_s = a[:n_meta], a[n_meta:]
    cv_m = _conv_taps(a_m, jnp.zeros((SUBLANES, FF_CHUNK), F32), cw)
    cv_s = cw[3:4] + cin_a_ref[...] * cw[0:1] + cin_b_ref[...] * cw[1:2] + a_s * cw[2:3]
    mcarry_ref[...] = a_m[n_meta - SUBLANES:]
    a_out_ref[...] = a_s
    cv = jnp.concatenate([cv_m, cv_s], axis=0)
    acc_s[...] += _dot(((cv * _sigmoid(cv)) * v).astype(BF16), wd)

    @pl.when(j == pl.num_programs(0) - 1)
    def _():
        out_ref[...] = _rms_norm((h1_s[...] + acc_s[...])[n_meta:], gf_ref[...])


def _ffn_small(y5m, y5s, yhm, yhs, hm, hs, wglu, bglu, wout, g2, wup, wd, cw, cin_a, cin_b, gf):
    n_meta, r = y5m.shape[1], y5s.shape[1]
    const = lambda shape: pl.BlockSpec(shape, lambda j: (0,) * len(shape))
    head = lambda rows, width: pl.BlockSpec((1, rows, width), lambda j: (0, 0, 0))
    ff_cols = lambda rows: pl.BlockSpec((rows, FF_CHUNK), lambda j: (0, j))
    sds = jax.ShapeDtypeStruct
    return pl.pallas_call(
        functools.partial(_ffn_small_kernel, n_meta=n_meta),
        grid=(N_FF_CHUNKS,),
        in_specs=[head(n_meta, S5_DIM), head(r, S5_DIM), head(n_meta, HG_DIM), head(r, HG_DIM),
                  head(n_meta, D_MODEL), head(r, D_MODEL),
                  const(wglu.shape), const(bglu.shape), const(wout.shape), const(g2.shape),
                  ff_cols(D_MODEL),
                  pl.BlockSpec((D_MODEL, FF_CHUNK), lambda j: (0, N_FF_CHUNKS + j)),
                  pl.BlockSpec((FF_CHUNK, D_MODEL), lambda j: (j, 0)),
                  ff_cols(SUBLANES), ff_cols(r), ff_cols(r), const(gf.shape)],
        out_specs=[const((r, D_MODEL)), ff_cols(r), ff_cols(SUBLANES),
                   const(wglu.shape), const(wout.shape), ff_cols(D_MODEL), ff_cols(D_MODEL),
                   pl.BlockSpec((FF_CHUNK, D_MODEL), lambda j: (j, 0))],
        out_shape=(sds((r, D_MODEL), F32), sds((r, D_FF), F32), sds((SUBLANES, D_FF), F32),
                   sds(wglu.shape, BF16), sds(wout.shape, BF16),
                   sds((D_MODEL, D_FF), BF16), sds((D_MODEL, D_FF), BF16), sds(wd.shape, BF16)),
        scratch_shapes=[pltpu.VMEM((n_meta + r, D_MODEL), BF16),
                        pltpu.VMEM((n_meta + r, D_MODEL), F32),
                        pltpu.VMEM((n_meta + r, D_MODEL), F32)],
        compiler_params=_params("arbitrary"),
        name="ffn_small",
    )(y5m, y5s, yhm, yhs, hm, hs, wglu, bglu, wout, g2, wup, wup, wd, cw, cin_a, cin_b, gf)


def _ffn(y5, yh, h, wglu, bglu, wout, g2, wa, wv, wd, cw, cin, gf, tm, n_tiles):
    n = h.shape[0]
    const = lambda shape: pl.BlockSpec(shape, lambda b, i: (0,) * len(shape),
                                       pipeline_mode=pl.Buffered(1))
    own = cin.shape[0] == n
    conv = lambda index: pl.BlockSpec((1, SUBLANES, D_FF), index)
    return pl.pallas_call(
        functools.partial(_ffn_kernel, tm=tm),
        grid=(n, n_tiles),
        in_specs=[pl.BlockSpec((1, tm, S5_DIM), lambda b, i: (b, i, 0)),
                  pl.BlockSpec((1, tm, HG_DIM), lambda b, i: (b, i, 0)),
                  pl.BlockSpec((1, tm, D_MODEL), lambda b, i: (b, i, 0)),
                  const(wglu.shape), const(bglu.shape), const(wout.shape), const(g2.shape),
                  const(wa.shape), const(wv.shape), const(wd.shape), const(cw.shape),
                  conv(lambda b, i: (b if own else 0, 0, 0)), const(gf.shape)],
        out_specs=[pl.BlockSpec((1, tm, D_MODEL), lambda b, i: (b, i, 0)),
                   conv(lambda b, i: (b, 0, 0))],
        out_shape=(jax.ShapeDtypeStruct((n, n_tiles * tm, D_MODEL), F32),
                   jax.ShapeDtypeStruct((n, SUBLANES, D_FF), F32)),
        scratch_shapes=[pltpu.VMEM((SUBLANES, D_FF), F32),
                        pltpu.VMEM((tm, D_MODEL), BF16),
                        pltpu.VMEM((tm, D_MODEL), F32),
                        pltpu.VMEM((tm, FF_CHUNK), F32),
                        pltpu.VMEM((tm, D_FF), BF16)],
        compiler_params=_params("arbitrary", "arbitrary"),
        name="mix_out_ffn",
    )(y5, yh, h, wglu, bglu, wout, g2, wa, wv, wd, cw, cin, gf)


def kernel(x_prompt, x_sample, state_s5_re, state_s5_im, state_hgrn, state_ffn_conv, meta_tokens, norm_mix_g, w_in, s5_lambda_re, s5_lambda_im, s5_log_dt, s5_b_re, s5_b_im, s5_c_re, s5_c_im, s5_d, s5_w_glu, s5_b_glu, hg_lower_bounds, hg_norm_g, w_out, norm_ffn_g, ffn_w_up, ffn_conv_w, ffn_conv_b, ffn_w_down, final_norm_g):
    nb, seq, _ = x_prompt.shape
    ns = x_sample.shape[0]
    li = 0

    a5, bb, lb = _prep(s5_lambda_re[li], s5_lambda_im[li], s5_log_dt[li],
                       s5_b_re[li], s5_b_im[li], hg_lower_bounds)
    a5 = a5.reshape(2, 1, S5_LANES)
    wb = _block_diag(bb)
    wc = _block_diag(jnp.stack([s5_c_re[li], -s5_c_im[li]]).transpose(0, 1, 3, 2))
    d5 = s5_d[li].reshape(1, S5_DIM)
    g1 = norm_mix_g[li].reshape(1, D_MODEL)
    g2 = norm_ffn_g[li].reshape(1, D_MODEL)
    gf = final_norm_g.reshape(1, D_MODEL)
    ng = hg_norm_g[li].reshape(1, HG_DIM)
    bglu = s5_b_glu[li].reshape(1, S5_DIM)
    cw = jnp.concatenate([ffn_conv_w[li], ffn_conv_b[li][None],
                          jnp.zeros((SUBLANES - CONV_W - 1, D_FF), F32)], axis=0)
    s5w = (a5, wb, wc, d5)

    xs = x_sample.reshape(1, ns, D_MODEL)
    u_tm, zf, qig, w_in_b = _inproj(xs, g1, w_in[li], ns, 1, emit_bf16=True)
    y5s, s5r, s5i = _s5(u_tm, state_s5_re[li].reshape(ns, S5_LANES), state_s5_im[li].reshape(ns, S5_LANES),
                        *s5w, ns, 1, 1)
    yhs, shg = _hgrn_step(zf, qig, lb, ng, state_hgrn[li], STEP_ROWS)

    meta = jnp.zeros((1, HG_CHUNK, D_MODEL), F32).at[0, :N_META].set(meta_tokens)
    z5 = jnp.zeros((1, S5_LANES), F32)
    zh = jnp.zeros((1, HG_HEADS, HG_HEAD_DIM, HG_HEAD_DIM), F32)
    u_tm, zf, qig = _inproj(meta, g1, w_in_b, HG_CHUNK, 1)
    y5m, m5r, m5i = _s5(u_tm, z5, z5, *s5w, 1, N_META, 1)
    yhm, mhg = _hgrn(zf, qig, lb, ng, zh, HG_CHUNK, 1, N_META)

    buf = state_ffn_conv[li]
    y_sample, a_new, mconv, wglu, wout, wa, wv, wd = _ffn_small(
        y5m, y5s, yhm, yhs, meta, xs, s5_w_glu[li], bglu, w_out[li], g2,
        ffn_w_up[li], ffn_w_down[li], cw, buf[:, 0], buf[:, 1], gf)
    sconv = jnp.stack([buf[:, 1], a_new], axis=1)

    u_tm, zf, qig = _inproj(x_prompt, g1, w_in_b, IN_TILE, seq // IN_TILE)
    y5, p5r, p5i = _s5(u_tm, m5r, m5i, *s5w, nb, S5_TILE, seq // S5_TILE)
    yh, phg = _hgrn(zf, qig, lb, ng, mhg, seq, 1, seq)
    y_prompt, pconv = _ffn(y5, yh, x_prompt, wglu, bglu, wout, g2, wa, wv, wd, cw, mconv[None], gf,
                           FFN_TILE, seq // FFN_TILE)

    st5 = lambda t: t.reshape(1, -1, S5_GROUPS, S5_STATE)
    pconv = pconv[:, SUBLANES - (CONV_W - 1):, :]
    return (y_prompt, y_sample.reshape(ns, 1, D_MODEL),
            st5(p5r), st5(p5i), phg[None], pconv[None],
            st5(s5r), st5(s5i), shg[None], sconv[None])
```

```python
import functools

import jax
import jax.numpy as jnp
import numpy as np
from jax import lax
from jax.experimental import pallas as pl
from jax.experimental.pallas import tpu as pltpu

F32 = jnp.float32
BF16 = jnp.bfloat16

D_MODEL = 1024
N_META = 16
S5_DIM = 512
S5_GROUP = 16
S5_GROUPS = 32
S5_STATE = 64
S5_LANES = S5_GROUPS * S5_STATE
HG_DIM = 512
HG_HEAD_DIM = 128
HG_HEADS = 4
HG_CHUNK = 64
D_FF = 2816
CONV_W = 3
EPS = 1e-6

S5_HALVES = 2
S5_HALF_CH = S5_DIM // S5_HALVES
S5_HALF_ST = S5_LANES // S5_HALVES
SCAN_LANES = 512
FF_CHUNK = 256
N_FF_CHUNKS = D_FF // FF_CHUNK
SUBLANES = 8
LANES = 128
VMEM_LIMIT = 56 * 1024 * 1024
IN_TILE = 512
S5_TILE = 128
FFN_TILE = 512
STEP_ROWS = 16


def _sigmoid(x):
    return 0.5 + 0.5 * jnp.tanh(0.5 * x)


def _rms_norm(x, g):
    ms = jnp.mean(x * x, axis=-1, keepdims=True)
    return x * lax.rsqrt(ms + EPS) * g


def _dot(a, b):
    return jnp.dot(a, b, preferred_element_type=F32)


def _params(*sem):
    return pltpu.CompilerParams(dimension_semantics=sem, vmem_limit_bytes=VMEM_LIMIT)


def _prep_kernel(lam_ref, dt_ref, bt_ref, hlb_ref, a_ref, bb_ref, lb_ref):
    lam_re = lam_ref[0]
    lam_im = lam_ref[1]
    dt = jnp.exp(dt_ref[...])
    mag = jnp.exp(lam_re * dt)
    ar = mag * jnp.cos(lam_im * dt)
    ai = mag * jnp.sin(lam_im * dt)
    nr = ar - 1.0
    den = lam_re * lam_re + lam_im * lam_im
    cr = (nr * lam_re + ai * lam_im) / den
    ci = (ai * lam_re - nr * lam_im) / den
    a_ref[0] = ar
    a_ref[1] = ai
    bt_re = bt_ref[0]
    bt_im = bt_ref[1]
    bb_ref[0] = cr * bt_re - ci * bt_im
    bb_ref[1] = cr * bt_im + ci * bt_re
    hlb = hlb_ref[...]
    e = jnp.exp(hlb - jnp.max(hlb, axis=0, keepdims=True))
    lb_ref[...] = e[0:1] / jnp.sum(e, axis=0, keepdims=True)


def _prep(lam_re, lam_im, log_dt, b_re, b_im, hlb):
    g, p = lam_re.shape
    sds = jax.ShapeDtypeStruct
    return pl.pallas_call(
        _prep_kernel,
        out_shape=(sds((2, g, 1, p), F32), sds((2, g, S5_GROUP, p), F32), sds((1, HG_DIM), F32)),
        name="param_prep",
    )(jnp.stack([lam_re, lam_im]).reshape(2, g, 1, p),
      jnp.broadcast_to(log_dt.reshape(g, 1, 1), (g, 1, p)),
      jnp.stack([b_re, b_im]).transpose(0, 1, 3, 2), hlb)


def _block_diag(blocks):
    two, g, r, c = blocks.shape
    per = g // S5_HALVES
    eye = np.eye(per, dtype=np.float32)
    out = jnp.einsum("ajgrc,gh->ajgrhc", blocks.reshape(two, S5_HALVES, per, r, c), eye)
    return out.reshape(two, S5_HALVES, per * r, per * c).astype(BF16)


def _inproj_kernel(x_ref, g_ref, w_ref, u_ref, f_ref, qig_ref, *wb_ref, n, tm):
    x = x_ref[0]
    hn = (x * g_ref[...]).astype(BF16)
    rn = lax.rsqrt(jnp.mean(x * x, axis=-1, keepdims=True) + EPS)
    rn = jnp.broadcast_to(rn, (tm, HG_DIM))
    if wb_ref:
        wb_ref[0][...] = w_ref[...].astype(BF16)
        w_ref = wb_ref[0]
    col = lambda j: _dot(hn, w_ref[:, j * HG_DIM:(j + 1) * HG_DIM]) * rn
    b = pl.program_id(1)
    u = col(0)
    for l in range(S5_DIM // LANES):
        ul = u[:, l * LANES:(l + 1) * LANES]
        if n == 1:
            u_ref[l] = ul
        else:
            u_ref[l, pl.ds(b, tm, stride=n), :] = ul
    f_ref[0] = col(2)
    for j, src in enumerate((1, 3, 4)):
        qig_ref[0, :, j * HG_DIM:(j + 1) * HG_DIM] = col(src).astype(BF16)


def _inproj(x, g, w, tm, n_tiles, emit_bf16=False):
    n, l, d = x.shape
    cols = w.shape[1]
    rows = tm * n_tiles
    const = lambda shape: pl.BlockSpec(shape, lambda i, b: (0,) * len(shape),
                                       pipeline_mode=pl.Buffered(1))
    return pl.pallas_call(
        functools.partial(_inproj_kernel, n=n, tm=tm),
        grid=(n_tiles, n),
        in_specs=[pl.BlockSpec((1, tm, d), lambda i, b: (b, i, 0)),
                  const((1, d)), const((d, cols))],
        out_specs=[pl.BlockSpec((S5_DIM // LANES, tm * n, LANES), lambda i, b: (0, i, 0)),
                   pl.BlockSpec((1, tm, HG_DIM), lambda i, b: (b, i, 0)),
                   pl.BlockSpec((1, tm, 3 * HG_DIM), lambda i, b: (b, i, 0))]
                  + [pl.BlockSpec((d, cols), lambda i, b: (0, 0))] * emit_bf16,
        out_shape=(jax.ShapeDtypeStruct((S5_DIM // LANES, rows * n, LANES), F32),
                   jax.ShapeDtypeStruct((n, rows, HG_DIM), F32),
                   jax.ShapeDtypeStruct((n, rows, 3 * HG_DIM), BF16))
                  + (jax.ShapeDtypeStruct((d, cols), BF16),) * emit_bf16,
        compiler_params=_params("arbitrary", "arbitrary"),
        name="inproj",
    )(x, g, w)


def _gelu_tanh(y):
    return 0.5 * y * (1.0 + jnp.tanh(0.7978845608028654 * (y + 0.044715 * (y * y * y))))


def _s5_kernel(u_ref, h0r_ref, h0i_ref, a_ref, wb_ref, wc_ref, d_ref,
               y_ref, hr_out_ref, hi_out_ref, xr_s, xi_s, hr_s, hi_s, y_s, *, n, tt, interleaved):
    @pl.when(pl.program_id(0) == 0)
    def _():
        hr_s[...] = jnp.broadcast_to(h0r_ref[...], hr_s.shape)
        hi_s[...] = jnp.broadcast_to(h0i_ref[...], hi_s.shape)

    slabs = S5_HALF_CH // LANES
    load_u = lambda j: jnp.concatenate([u_ref[j * slabs + l] for l in range(slabs)], axis=1)
    for j in range(S5_HALVES):
        st = slice(j * S5_HALF_ST, (j + 1) * S5_HALF_ST)
        ub = load_u(j).astype(BF16)
        xr_s[:, st] = _dot(ub, wb_ref[0, j])
        xi_s[:, st] = _dot(ub, wb_ref[1, j])
    for c in range(S5_LANES // SCAN_LANES):
        loc = slice(c * SCAN_LANES, (c + 1) * SCAN_LANES)
        ar = jnp.broadcast_to(a_ref[0, :, loc], (n, SCAN_LANES))
        ai = jnp.broadcast_to(a_ref[1, :, loc], (n, SCAN_LANES))

        def step(t, carry, ar=ar, ai=ai, loc=loc):
            hr, hi = carry
            r = 0 if tt == 1 else pl.multiple_of(t * n, n)
            nhr = ar * hr - ai * hi + xr_s[pl.ds(r, n), loc]
            nhi = ar * hi + ai * hr + xi_s[pl.ds(r, n), loc]
            xr_s[pl.ds(r, n), loc] = nhr
            xi_s[pl.ds(r, n), loc] = nhi
            return nhr, nhi

        carry = (hr_s[:, loc], hi_s[:, loc])
        if tt == 1:
            carry = step(0, carry)
        else:
            carry = lax.fori_loop(0, tt, step, carry, unroll=True)
        hr_s[:, loc] = carry[0]
        hi_s[:, loc] = carry[1]
    for j in range(S5_HALVES):
        ch = slice(j * S5_HALF_CH, (j + 1) * S5_HALF_CH)
        st = slice(j * S5_HALF_ST, (j + 1) * S5_HALF_ST)
        y = (_dot(xr_s[:, st].astype(BF16), wc_ref[0, j]) + _dot(xi_s[:, st].astype(BF16), wc_ref[1, j])
             + d_ref[:, ch] * load_u(j))
        if interleaved:
            for l in range(slabs):
                y_s[j * slabs + l] = y[:, l * LANES:(l + 1) * LANES]
        else:
            y_ref[0, :, ch] = y
    if interleaved:
        for b in range(n):
            for l in range(S5_DIM // LANES):
                y_ref[b, :, l * LANES:(l + 1) * LANES] = y_s[l, pl.ds(b, tt, stride=n), :]
    hr_out_ref[...] = hr_s[...]
    hi_out_ref[...] = hi_s[...]


def _s5(u4, h0r, h0i, a, wb, wc, d, n, tt, n_tiles, tile0=0):
    rows = tt * n
    interleaved = n > 1 and tt > 1
    const = lambda shape: pl.BlockSpec(shape, lambda i: (0,) * len(shape),
                                       pipeline_mode=pl.Buffered(1))
    state = pl.BlockSpec((n, S5_LANES), lambda i: (0, 0))
    y_block = (n, tt, S5_DIM) if interleaved else (1, rows, S5_DIM)
    y_shape = (n, tt * n_tiles, S5_DIM) if interleaved else (1, rows * n_tiles, S5_DIM)
    return pl.pallas_call(
        functools.partial(_s5_kernel, n=n, tt=tt, interleaved=interleaved),
        grid=(n_tiles,),
        in_specs=[pl.BlockSpec((S5_DIM // LANES, rows, LANES), lambda i: (0, tile0 + i, 0)),
                  const(h0r.shape), const(h0i.shape), const(a.shape), const(wb.shape), const(wc.shape),
                  const((1, S5_DIM))],
        out_specs=[pl.BlockSpec(y_block, lambda i: (0, i, 0)), state, state],
        out_shape=(jax.ShapeDtypeStruct(y_shape, F32),
                   jax.ShapeDtypeStruct((n, S5_LANES), F32),
                   jax.ShapeDtypeStruct((n, S5_LANES), F32)),
        scratch_shapes=[pltpu.VMEM((rows, S5_LANES), F32), pltpu.VMEM((rows, S5_LANES), F32),
                        pltpu.VMEM((n, S5_LANES), F32), pltpu.VMEM((n, S5_LANES), F32),
                        pltpu.VMEM((S5_DIM // LANES, rows if interleaved else SUBLANES, LANES), F32)],
        compiler_params=_params("arbitrary"),
        name="s5_scan",
    )(u4, h0r, h0i, a, wb, wc, d)


def _hgrn_gate_out(o, g, ng):
    parts = []
    for h in range(HG_HEADS):
        oh = o[:, h * HG_HEAD_DIM:(h + 1) * HG_HEAD_DIM]
        ms = jnp.mean(oh * oh, axis=-1, keepdims=True)
        parts.append(oh * lax.rsqrt(ms + EPS))
    return jnp.concatenate(parts, axis=-1) * ng * (g * _sigmoid(g))


def _hgrn_kernel(q_ref, f_ref, i_ref, g_ref, lb_ref, ng_ref, tri_ref, s0_ref,
                 y_ref, s_out_ref, st_s, o_s, *, th, valid):
    i = pl.program_id(1)
    c = HG_CHUNK

    @pl.when(i == 0)
    def _():
        for h in range(HG_HEADS):
            st_s[h] = s0_ref[0, h].T

    lb = lb_ref[...]
    f = lb + (1.0 - lb) * _sigmoid(f_ref[0])
    lc = jnp.log(f)
    k = 1.0 - f
    q = q_ref[0]
    if valid < th:
        live = lax.broadcasted_iota(jnp.int32, (th, 1), 0) < valid
        lc = jnp.where(live, lc, 0.0)
        k = jnp.where(live, k, 0.0)
        q = jnp.where(live, q.astype(F32), 0.0).astype(BF16)
    tri = tri_ref[...]
    tr = tri.shape[0]
    lc_hi = lc.astype(BF16)
    lc_lo = (lc - lc_hi.astype(F32)).astype(BF16)
    b = jnp.concatenate([_dot(tri, lc_hi[r:r + tr]) + _dot(tri, lc_lo[r:r + tr])
                         for r in range(0, th, tr)], axis=0)
    qd = q * jnp.exp(b).astype(BF16)
    kd = k * jnp.exp(-b)
    kdb = kd.astype(BF16)
    vb = i_ref[0]
    causal = (lax.broadcasted_iota(jnp.int32, (c, c), 1) <= lax.broadcasted_iota(jnp.int32, (c, c), 0))
    nt = (((1,), (1,)), ((), ()))
    n_chunks = th // c
    rows = [slice(cc * c, (cc + 1) * c) for cc in range(n_chunks)]
    lanes = [slice(h * HG_HEAD_DIM, (h + 1) * HG_HEAD_DIM) for h in range(HG_HEADS)]
    dec = [jnp.exp(b[cc * c + c - 1:cc * c + c, :]) for cc in range(n_chunks)]
    kdec = [(kd[rows[cc]] * dec[cc]).astype(BF16) for cc in range(n_chunks)]
    att = [[jnp.where(causal, lax.dot_general(qd[rows[cc], ls], kdb[rows[cc], ls], nt,
                                              preferred_element_type=F32), 0.0).astype(BF16)
            for ls in lanes] for cc in range(n_chunks)]
    upd = [[_dot(vb[rows[cc], ls].T, kdec[cc][:, ls])
            for ls in lanes] for cc in range(n_chunks)]
    st_in = [[None] * HG_HEADS for _ in range(n_chunks)]
    for h, ls in enumerate(lanes):
        st = st_s[h]
        for cc in range(n_chunks):
            st_in[cc][h] = st.astype(BF16)
            st = dec[cc][:, ls] * st + upd[cc][h]
        st_s[h] = st
    for cc in range(n_chunks):
        for h, ls in enumerate(lanes):
            o_s[rows[cc], ls] = (
                lax.dot_general(qd[rows[cc], ls], st_in[cc][h], nt, preferred_element_type=F32)
                + _dot(att[cc][h], vb[rows[cc], ls]))
    y_ref[0] = _hgrn_gate_out(o_s[...], g_ref[0].astype(F32), ng_ref[...])

    @pl.when(i == pl.num_programs(1) - 1)
    def _():
        for h in range(HG_HEADS):
            s_out_ref[0, h] = st_s[h].T


def _hgrn_tri(tr):
    idx = np.arange(tr)
    tri = (idx[:, None] // HG_CHUNK == idx[None, :] // HG_CHUNK) & (idx[None, :] <= idx[:, None])
    return jnp.asarray(tri, dtype=BF16)


def _hgrn(zf, qig, lb, ng, s0, th, n_tiles, valid, tile0=0):
    n = zf.shape[0]
    own = s0.shape[0] == n
    tr = min(th, 256)
    col = lambda j: pl.BlockSpec((1, th, HG_DIM), lambda b, i, j=j: (b, tile0 + i, j))
    const = lambda shape: pl.BlockSpec(shape, lambda b, i: (0,) * len(shape),
                                       pipeline_mode=pl.Buffered(1))
    st = pl.BlockSpec((1, HG_HEADS, HG_HEAD_DIM, HG_HEAD_DIM), lambda b, i: (b, 0, 0, 0))
    st_in = pl.BlockSpec((1, HG_HEADS, HG_HEAD_DIM, HG_HEAD_DIM),
                         lambda b, i: (b if own else 0, 0, 0, 0))
    return pl.pallas_call(
        functools.partial(_hgrn_kernel, th=th, valid=valid),
        grid=(n, n_tiles),
        in_specs=[col(0), col(0), col(1), col(2), const((1, HG_DIM)), const((1, HG_DIM)),
                  const((tr, tr)), st_in],
        out_specs=[pl.BlockSpec((1, th, HG_DIM), lambda b, i: (b, i, 0)), st],
        out_shape=(jax.ShapeDtypeStruct((n, n_tiles * th, HG_DIM), F32),
                   jax.ShapeDtypeStruct((n, HG_HEADS, HG_HEAD_DIM, HG_HEAD_DIM), F32)),
        scratch_shapes=[pltpu.VMEM((HG_HEADS, HG_HEAD_DIM, HG_HEAD_DIM), F32),
                        pltpu.VMEM((th, HG_DIM), F32)],
        compiler_params=_params("arbitrary", "arbitrary"),
        name="hgrn_chunks",
    )(qig, zf, qig, qig, lb, ng, _hgrn_tri(tr), s0)


def _hgrn_step_kernel(q_ref, f_ref, i_ref, g_ref, lb_ref, ng_ref, s0_ref,
                      y_ref, s_out_ref, o_s, *, sb):
    lb = lb_ref[...]
    f = lb + (1.0 - lb) * _sigmoid(f_ref[0])
    q = q_ref[0].astype(F32)
    v = i_ref[0].astype(F32)
    pad = jnp.zeros((HG_HEAD_DIM - sb, HG_HEAD_DIM), F32)
    sq = (HG_HEAD_DIM, HG_HEAD_DIM)
    for h in range(HG_HEADS):
        ls = slice(h * HG_HEAD_DIM, (h + 1) * HG_HEAD_DIM)
        fcols = jnp.concatenate([f[:, ls], pad], axis=0).T
        for s in range(sb):
            fc = jnp.broadcast_to(fcols[:, s:s + 1], sq)
            sn = fc * s0_ref[s, h] + (1.0 - fc) * v[s:s + 1, ls]
            s_out_ref[s, h] = sn
            o_s[s:s + 1, ls] = _dot(q[s:s + 1, ls].astype(BF16), sn.astype(BF16))
    y_ref[0] = _hgrn_gate_out(o_s[...], g_ref[0].astype(F32), ng_ref[...])


def _hgrn_step(zf, qig, lb, ng, s0, sb):
    r = s0.shape[0]
    col = lambda j: pl.BlockSpec((1, sb, HG_DIM), lambda i, j=j: (0, i, j))
    vec = pl.BlockSpec((1, HG_DIM), lambda i: (0, 0))
    st = pl.BlockSpec((sb, HG_HEADS, HG_HEAD_DIM, HG_HEAD_DIM), lambda i: (i, 0, 0, 0))
    return pl.pallas_call(
        functools.partial(_hgrn_step_kernel, sb=sb),
        grid=(r // sb,),
        in_specs=[col(0), col(0), col(1), col(2), vec, vec, st],
        out_specs=[pl.BlockSpec((1, sb, HG_DIM), lambda i: (0, i, 0)), st],
        out_shape=(jax.ShapeDtypeStruct((1, r, HG_DIM), F32),
                   jax.ShapeDtypeStruct(s0.shape, F32)),
        scratch_shapes=[pltpu.VMEM((sb, HG_DIM), F32)],
        compiler_params=_params("arbitrary"),
        name="hgrn_step",
    )(qig, zf, qig, qig, lb, ng, s0)


def _mix_out(y5_raw, yh, h, wglu, bglu, wout):
    y5p = _gelu_tanh(y5_raw)
    y5 = y5p * _sigmoid(_dot(y5p.astype(BF16), wglu) + bglu)
    ymix = jnp.concatenate([y5, yh], axis=-1).astype(BF16)
    return h + _dot(ymix, wout)


def _conv_taps(a, prev8, cw):
    rows = a.shape[0]
    cv = cw[3:4] + pltpu.roll(a, 2, 0) * cw[0:1] + pltpu.roll(a, 1, 0) * cw[1:2] + a * cw[2:3]
    d = prev8 - a[rows - SUBLANES:rows]
    rid = lax.broadcasted_iota(jnp.int32, (SUBLANES, 1), 0)
    fix = (jnp.where(rid < 1, pltpu.roll(d, 1, 0), 0.0) * cw[1:2]
           + jnp.where(rid < 2, pltpu.roll(d, 2, 0), 0.0) * cw[0:1])
    return jnp.concatenate([cv[:SUBLANES] + fix, cv[SUBLANES:]], axis=0)


def _ffn_kernel(y5_ref, yh_ref, h_ref, wglu_ref, bglu_ref, wout_ref, g2_ref,
                wa_ref, wv_ref, wd_ref, cw_ref, cin_ref, gf_ref,
                out_ref, cout_ref, carry_s, hn_s, h1_s, rn_s, s_s, *, tm):
    @pl.when(pl.program_id(1) == 0)
    def _():
        carry_s[...] = cin_ref[0]

    h1 = _mix_out(y5_ref[0], yh_ref[0], h_ref[0], wglu_ref[...], bglu_ref[...], wout_ref[...])
    h1_s[...] = h1
    hn_s[...] = (h1 * g2_ref[...]).astype(BF16)
    rn = lax.rsqrt(jnp.mean(h1 * h1, axis=-1, keepdims=True) + EPS)
    rn_s[...] = jnp.broadcast_to(rn, (tm, FF_CHUNK))

    for j in range(N_FF_CHUNKS):
        cs = slice(j * FF_CHUNK, (j + 1) * FF_CHUNK)
        hn = hn_s[...]
        a = _dot(hn, wa_ref[:, cs]) * rn_s[...]
        v = _dot(hn, wv_ref[:, cs]) * rn_s[...]
        cv = _conv_taps(a, carry_s[:, cs], cw_ref[:, cs])
        carry_s[:, cs] = a[tm - SUBLANES:tm]
        s_s[:, cs] = ((cv * _sigmoid(cv)) * v).astype(BF16)

    out_ref[0] = _rms_norm(h1_s[...] + _dot(s_s[...], wd_ref[...]), gf_ref[...])
    cout_ref[0] = carry_s[...]


def _ffn_small_kernel(y5m_ref, y5s_ref, yhm_ref, yhs_ref, hm_ref, hs_ref,
                      wglu_ref, bglu_ref, wout_ref, g2_ref, wa_ref, wv_ref, wd_ref, cw_ref,
                      cin_a_ref, cin_b_ref, gf_ref,
                      out_ref, a_out_ref, mcarry_ref,
                      wglu_bf_ref, wout_bf_ref, wa_bf_ref, wv_bf_ref, wd_bf_ref,
                      hn_s, h1_s, acc_s, *, n_meta):
    j = pl.program_id(0)

    @pl.when(j == 0)
    def _():
        wglu = wglu_ref[...].astype(BF16)
        wout = wout_ref[...].astype(BF16)
        wglu_bf_ref[...] = wglu
        wout_bf_ref[...] = wout
        rows = lambda m_ref, s_ref: jnp.concatenate([m_ref[0], s_ref[0]], axis=0)
        h1 = _mix_out(rows(y5m_ref, y5s_ref), rows(yhm_ref, yhs_ref), rows(hm_ref, hs_ref),
                      wglu, bglu_ref[...], wout)
        h1_s[...] = h1
        hn_s[...] = _rms_norm(h1, g2_ref[...]).astype(BF16)
        acc_s[...] = jnp.zeros_like(acc_s)

    wa = wa_ref[...].astype(BF16)
    wv = wv_ref[...].astype(BF16)
    wd = wd_ref[...].astype(BF16)
    wa_bf_ref[...] = wa
    wv_bf_ref[...] = wv
    wd_bf_ref[...] = wd
    hn = hn_s[...]
    a = _dot(hn, wa)
    v = _dot(hn, wv)
    cw = cw_ref[...]
    a_m, a_s = a[:n_meta], a[n_meta:]
    cv_m = _conv_taps(a_m, jnp.zeros((SUBLANES, FF_CHUNK), F32), cw)
    cv_s = cw[3:4] + cin_a_ref[...] * cw[0:1] + cin_b_ref[...] * cw[1:2] + a_s * cw[2:3]
    mcarry_ref[...] = a_m[n_meta - SUBLANES:]
    a_out_ref[...] = a_s
    cv = jnp.concatenate([cv_m, cv_s], axis=0)
    acc_s[...] += _dot(((cv * _sigmoid(cv)) * v).astype(BF16), wd)

    @pl.when(j == pl.num_programs(0) - 1)
    def _():
        out_ref[...] = _rms_norm((h1_s[...] + acc_s[...])[n_meta:], gf_ref[...])


def _ffn_small(y5m, y5s, yhm, yhs, hm, hs, wglu, bglu, wout, g2, wup, wd, cw, cin_a, cin_b, gf,
               hm_tile=0):
    n_meta, r = y5m.shape[1], y5s.shape[1]
    const = lambda shape: pl.BlockSpec(shape, lambda j: (0,) * len(shape))
    head = lambda rows, width: pl.BlockSpec((1, rows, width), lambda j: (0, 0, 0))
    ff_cols = lambda rows: pl.BlockSpec((rows, FF_CHUNK), lambda j: (0, j))
    sds = jax.ShapeDtypeStruct
    return pl.pallas_call(
        functools.partial(_ffn_small_kernel, n_meta=n_meta),
        grid=(N_FF_CHUNKS,),
        in_specs=[head(n_meta, S5_DIM), head(r, S5_DIM), head(n_meta, HG_DIM), head(r, HG_DIM),
                  pl.BlockSpec((1, n_meta, D_MODEL), lambda j: (0, hm_tile, 0)), head(r, D_MODEL),
                  const(wglu.shape), const(bglu.shape), const(wout.shape), const(g2.shape),
                  ff_cols(D_MODEL),
                  pl.BlockSpec((D_MODEL, FF_CHUNK), lambda j: (0, N_FF_CHUNKS + j)),
                  pl.BlockSpec((FF_CHUNK, D_MODEL), lambda j: (j, 0)),
                  ff_cols(SUBLANES), ff_cols(r), ff_cols(r), const(gf.shape)],
        out_specs=[const((r, D_MODEL)), ff_cols(r), ff_cols(SUBLANES),
                   const(wglu.shape), const(wout.shape), ff_cols(D_MODEL), ff_cols(D_MODEL),
                   pl.BlockSpec((FF_CHUNK, D_MODEL), lambda j: (j, 0))],
        out_shape=(sds((r, D_MODEL), F32), sds((r, D_FF), F32), sds((SUBLANES, D_FF), F32),
                   sds(wglu.shape, BF16), sds(wout.shape, BF16),
                   sds((D_MODEL, D_FF), BF16), sds((D_MODEL, D_FF), BF16), sds(wd.shape, BF16)),
        scratch_shapes=[pltpu.VMEM((n_meta + r, D_MODEL), BF16),
                        pltpu.VMEM((n_meta + r, D_MODEL), F32),
                        pltpu.VMEM((n_meta + r, D_MODEL), F32)],
        compiler_params=_params("arbitrary"),
        name="ffn_small",
    )(y5m, y5s, yhm, yhs, hm, hs, wglu, bglu, wout, g2, wup, wup, wd, cw, cin_a, cin_b, gf)


def _ffn(y5, yh, h, wglu, bglu, wout, g2, wa, wv, wd, cw, cin, gf, tm, n_tiles):
    n = h.shape[0]
    const = lambda shape: pl.BlockSpec(shape, lambda b, i: (0,) * len(shape),
                                       pipeline_mode=pl.Buffered(1))
    own = cin.shape[0] == n
    conv = lambda index: pl.BlockSpec((1, SUBLANES, D_FF), index)
    return pl.pallas_call(
        functools.partial(_ffn_kernel, tm=tm),
        grid=(n, n_tiles),
        in_specs=[pl.BlockSpec((1, tm, S5_DIM), lambda b, i: (b, i, 0)),
                  pl.BlockSpec((1, tm, HG_DIM), lambda b, i: (b, i, 0)),
                  pl.BlockSpec((1, tm, D_MODEL), lambda b, i: (b, i, 0)),
                  const(wglu.shape), const(bglu.shape), const(wout.shape), const(g2.shape),
                  const(wa.shape), const(wv.shape), const(wd.shape), const(cw.shape),
                  conv(lambda b, i: (b if own else 0, 0, 0)), const(gf.shape)],
        out_specs=[pl.BlockSpec((1, tm, D_MODEL), lambda b, i: (b, i, 0)),
                   conv(lambda b, i: (b, 0, 0))],
        out_shape=(jax.ShapeDtypeStruct((n, n_tiles * tm, D_MODEL), F32),
                   jax.ShapeDtypeStruct((n, SUBLANES, D_FF), F32)),
        scratch_shapes=[pltpu.VMEM((SUBLANES, D_FF), F32),
                        pltpu.VMEM((tm, D_MODEL), BF16),
                        pltpu.VMEM((tm, D_MODEL), F32),
                        pltpu.VMEM((tm, FF_CHUNK), F32),
                        pltpu.VMEM((tm, D_FF), BF16)],
        compiler_params=_params("arbitrary", "arbitrary"),
        name="mix_out_ffn",
    )(y5, yh, h, wglu, bglu, wout, g2, wa, wv, wd, cw, cin, gf)


def kernel(x_prompt, x_sample, state_s5_re, state_s5_im, state_hgrn, state_ffn_conv, meta_tokens, norm_mix_g, w_in, s5_lambda_re, s5_lambda_im, s5_log_dt, s5_b_re, s5_b_im, s5_c_re, s5_c_im, s5_d, s5_w_glu, s5_b_glu, hg_lower_bounds, hg_norm_g, w_out, norm_ffn_g, ffn_w_up, ffn_conv_w, ffn_conv_b, ffn_w_down, final_norm_g):
    nb, seq, _ = x_prompt.shape
    ns = x_sample.shape[0]
    li = 0

    a5, bb, lb = _prep(s5_lambda_re[li], s5_lambda_im[li], s5_log_dt[li],
                       s5_b_re[li], s5_b_im[li], hg_lower_bounds)
    a5 = a5.reshape(2, 1, S5_LANES)
    wb = _block_diag(bb)
    wc = _block_diag(jnp.stack([s5_c_re[li], -s5_c_im[li]]).transpose(0, 1, 3, 2))
    d5 = s5_d[li].reshape(1, S5_DIM)
    g1 = norm_mix_g[li].reshape(1, D_MODEL)
    g2 = norm_ffn_g[li].reshape(1, D_MODEL)
    gf = final_norm_g.reshape(1, D_MODEL)
    ng = hg_norm_g[li].reshape(1, HG_DIM)
    bglu = s5_b_glu[li].reshape(1, S5_DIM)
    cw = jnp.concatenate([ffn_conv_w[li], ffn_conv_b[li][None],
                          jnp.zeros((SUBLANES - CONV_W - 1, D_FF), F32)], axis=0)
    s5w = (a5, wb, wc, d5)

    xs = jnp.concatenate([x_sample.reshape(ns, D_MODEL), meta_tokens,
                          jnp.zeros((HG_CHUNK - N_META, D_MODEL), F32)])[None]
    u_tm, zf, qig, w_in_b = _inproj(xs, g1, w_in[li], ns + HG_CHUNK, 1, emit_bf16=True)

    y5s, s5r, s5i = _s5(u_tm, state_s5_re[li].reshape(ns, S5_LANES), state_s5_im[li].reshape(ns, S5_LANES),
                        *s5w, ns, 1, 1)
    yhs, shg = _hgrn_step(zf, qig, lb, ng, state_hgrn[li], STEP_ROWS)

    z5 = jnp.zeros((1, S5_LANES), F32)
    zh = jnp.zeros((1, HG_HEADS, HG_HEAD_DIM, HG_HEAD_DIM), F32)
    y5m, m5r, m5i = _s5(u_tm, z5, z5, *s5w, 1, N_META, 1, tile0=ns // N_META)
    yhm, mhg = _hgrn(zf, qig, lb, ng, zh, HG_CHUNK, 1, N_META, tile0=ns // HG_CHUNK)

    buf = state_ffn_conv[li]
    y_sample, a_new, mconv, wglu, wout, wa, wv, wd = _ffn_small(
        y5m, y5s, yhm, yhs, xs, xs, s5_w_glu[li], bglu, w_out[li], g2,
        ffn_w_up[li], ffn_w_down[li], cw, buf[:, 0], buf[:, 1], gf, hm_tile=ns // N_META)
    sconv = jnp.stack([buf[:, 1], a_new], axis=1)

    u_tm, zf, qig = _inproj(x_prompt, g1, w_in_b, IN_TILE, seq // IN_TILE)
    y5, p5r, p5i = _s5(u_tm, m5r, m5i, *s5w, nb, S5_TILE, seq // S5_TILE)
    yh, phg = _hgrn(zf, qig, lb, ng, mhg, seq, 1, seq)
    y_prompt, pconv = _ffn(y5, yh, x_prompt, wglu, bglu, wout, g2, wa, wv, wd, cw, mconv[None], gf,
                           FFN_TILE, seq // FFN_TILE)

    st5 = lambda t: t.reshape(1, -1, S5_GROUPS, S5_STATE)
    pconv = pconv[:, SUBLANES - (CONV_W - 1):, :]
    return (y_prompt, y_sample.reshape(ns, 1, D_MODEL),
            st5(p5r), st5(p5i), phg[None], pconv[None],
            st5(s5r), st5(s5i), shg[None], sconv[None])
```

```python
import functools

import jax
import jax.numpy as jnp
import numpy as np
from jax import lax
from jax.experimental import pallas as pl
from jax.experimental.pallas import tpu as pltpu

F32 = jnp.float32
BF16 = jnp.bfloat16

D_MODEL = 1024
N_META = 16
S5_DIM = 512
S5_GROUP = 16
S5_GROUPS = 32
S5_STATE = 64
S5_LANES = S5_GROUPS * S5_STATE
HG_DIM = 512
HG_HEAD_DIM = 128
HG_HEADS = 4
HG_CHUNK = 64
D_FF = 2816
CONV_W = 3
EPS = 1e-6

S5_HALVES = 2
S5_HALF_CH = S5_DIM // S5_HALVES
S5_HALF_ST = S5_LANES // S5_HALVES
SCAN_LANES = 512
FF_CHUNK = 256
N_FF_CHUNKS = D_FF // FF_CHUNK
SUBLANES = 8
LANES = 128
VMEM_LIMIT = 56 * 1024 * 1024
IN_TILE = 512
IN_SEQS = 2
S5_TILE = 128
FFN_TILE = 512
STEP_ROWS = 16


def _sigmoid(x):
    return 0.5 + 0.5 * jnp.tanh(0.5 * x)


def _rms_norm(x, g):
    ms = jnp.mean(x * x, axis=-1, keepdims=True)
    return x * lax.rsqrt(ms + EPS) * g


def _dot(a, b):
    return jnp.dot(a, b, preferred_element_type=F32)


def _params(*sem):
    return pltpu.CompilerParams(dimension_semantics=sem, vmem_limit_bytes=VMEM_LIMIT)


def _prep_kernel(lam_ref, dt_ref, bt_ref, hlb_ref, a_ref, bb_ref, lb_ref):
    lam_re = lam_ref[0]
    lam_im = lam_ref[1]
    dt = jnp.exp(dt_ref[...])
    mag = jnp.exp(lam_re * dt)
    ar = mag * jnp.cos(lam_im * dt)
    ai = mag * jnp.sin(lam_im * dt)
    nr = ar - 1.0
    den = lam_re * lam_re + lam_im * lam_im
    cr = (nr * lam_re + ai * lam_im) / den
    ci = (ai * lam_re - nr * lam_im) / den
    a_ref[0] = ar
    a_ref[1] = ai
    bt_re = bt_ref[0]
    bt_im = bt_ref[1]
    bb_ref[0] = cr * bt_re - ci * bt_im
    bb_ref[1] = cr * bt_im + ci * bt_re
    hlb = hlb_ref[...]
    e = jnp.exp(hlb - jnp.max(hlb, axis=0, keepdims=True))
    lb_ref[...] = e[0:1] / jnp.sum(e, axis=0, keepdims=True)


def _prep(lam_re, lam_im, log_dt, b_re, b_im, hlb):
    g, p = lam_re.shape
    sds = jax.ShapeDtypeStruct
    return pl.pallas_call(
        _prep_kernel,
        out_shape=(sds((2, g, 1, p), F32), sds((2, g, S5_GROUP, p), F32), sds((1, HG_DIM), F32)),
        name="param_prep",
    )(jnp.stack([lam_re, lam_im]).reshape(2, g, 1, p),
      jnp.broadcast_to(log_dt.reshape(g, 1, 1), (g, 1, p)),
      jnp.stack([b_re, b_im]).transpose(0, 1, 3, 2), hlb)


def _block_diag(blocks):
    two, g, r, c = blocks.shape
    per = g // S5_HALVES
    eye = np.eye(per, dtype=np.float32)
    out = jnp.einsum("ajgrc,gh->ajgrhc", blocks.reshape(two, S5_HALVES, per, r, c), eye)
    return out.reshape(two, S5_HALVES, per * r, per * c).astype(BF16)


def _inproj_kernel(x_ref, g_ref, w_ref, u_ref, f_ref, qig_ref, *wb_ref, n, nb, tm):
    x = x_ref[...].reshape(nb * tm, D_MODEL)
    hn = (x * g_ref[...]).astype(BF16)
    rn = lax.rsqrt(jnp.mean(x * x, axis=-1, keepdims=True) + EPS)
    rn = jnp.broadcast_to(rn, (nb * tm, HG_DIM))
    if wb_ref:
        wb_ref[0][...] = w_ref[...].astype(BF16)
        w_ref = wb_ref[0]
    col = lambda j: _dot(hn, w_ref[:, j * HG_DIM:(j + 1) * HG_DIM]) * rn
    b0 = pl.program_id(1) * nb
    u = col(0)
    for l in range(S5_DIM // LANES):
        for r in range(nb):
            ul = u[r * tm:(r + 1) * tm, l * LANES:(l + 1) * LANES]
            if n == 1:
                u_ref[l] = ul
            else:
                u_ref[l, pl.ds(b0 + r, tm, stride=n), :] = ul
    f_ref[...] = col(2).reshape(nb, tm, HG_DIM)
    for j, src in enumerate((1, 3, 4)):
        qig_ref[:, :, j * HG_DIM:(j + 1) * HG_DIM] = col(src).astype(BF16).reshape(nb, tm, HG_DIM)


def _inproj(x, g, w, tm, n_tiles, nb=1, emit_bf16=False):
    n, l, d = x.shape
    cols = w.shape[1]
    rows = tm * n_tiles
    const = lambda shape: pl.BlockSpec(shape, lambda i, b: (0,) * len(shape),
                                       pipeline_mode=pl.Buffered(1))
    return pl.pallas_call(
        functools.partial(_inproj_kernel, n=n, nb=nb, tm=tm),
        grid=(n_tiles, n // nb),
        in_specs=[pl.BlockSpec((nb, tm, d), lambda i, b: (b, i, 0)),
                  const((1, d)), const((d, cols))],
        out_specs=[pl.BlockSpec((S5_DIM // LANES, tm * n, LANES), lambda i, b: (0, i, 0)),
                   pl.BlockSpec((nb, tm, HG_DIM), lambda i, b: (b, i, 0)),
                   pl.BlockSpec((nb, tm, 3 * HG_DIM), lambda i, b: (b, i, 0))]
                  + [pl.BlockSpec((d, cols), lambda i, b: (0, 0))] * emit_bf16,
        out_shape=(jax.ShapeDtypeStruct((S5_DIM // LANES, rows * n, LANES), F32),
                   jax.ShapeDtypeStruct((n, rows, HG_DIM), F32),
                   jax.ShapeDtypeStruct((n, rows, 3 * HG_DIM), BF16))
                  + (jax.ShapeDtypeStruct((d, cols), BF16),) * emit_bf16,
        compiler_params=_params("arbitrary", "arbitrary"),
        name="inproj",
    )(x, g, w)


def _gelu_tanh(y):
    return 0.5 * y * (1.0 + jnp.tanh(0.7978845608028654 * (y + 0.044715 * (y * y * y))))


def _s5_kernel(u_ref, h0r_ref, h0i_ref, a_ref, wb_ref, wc_ref, d_ref,
               y_ref, hr_out_ref, hi_out_ref, xr_s, xi_s, hr_s, hi_s, y_s, *, n, tt, interleaved):
    @pl.when(pl.program_id(0) == 0)
    def _():
        hr_s[...] = jnp.broadcast_to(h0r_ref[...], hr_s.shape)
        hi_s[...] = jnp.broadcast_to(h0i_ref[...], hi_s.shape)

    slabs = S5_HALF_CH // LANES
    load_u = lambda j: jnp.concatenate([u_ref[j * slabs + l] for l in range(slabs)], axis=1)
    for j in range(S5_HALVES):
        st = slice(j * S5_HALF_ST, (j + 1) * S5_HALF_ST)
        ub = load_u(j).astype(BF16)
        xr_s[:, st] = _dot(ub, wb_ref[0, j])
        xi_s[:, st] = _dot(ub, wb_ref[1, j])
    for c in range(S5_LANES // SCAN_LANES):
        loc = slice(c * SCAN_LANES, (c + 1) * SCAN_LANES)
        ar = jnp.broadcast_to(a_ref[0, :, loc], (n, SCAN_LANES))
        ai = jnp.broadcast_to(a_ref[1, :, loc], (n, SCAN_LANES))

        def step(t, carry, ar=ar, ai=ai, loc=loc):
            hr, hi = carry
            r = 0 if tt == 1 else pl.multiple_of(t * n, n)
            nhr = ar * hr - ai * hi + xr_s[pl.ds(r, n), loc]
            nhi = ar * hi + ai * hr + xi_s[pl.ds(r, n), loc]
            xr_s[pl.ds(r, n), loc] = nhr
            xi_s[pl.ds(r, n), loc] = nhi
            return nhr, nhi

        carry = (hr_s[:, loc], hi_s[:, loc])
        if tt == 1:
            carry = step(0, carry)
        else:
            carry = lax.fori_loop(0, tt, step, carry, unroll=True)
        hr_s[:, loc] = carry[0]
        hi_s[:, loc] = carry[1]
    for j in range(S5_HALVES):
        ch = slice(j * S5_HALF_CH, (j + 1) * S5_HALF_CH)
        st = slice(j * S5_HALF_ST, (j + 1) * S5_HALF_ST)
        y = (_dot(xr_s[:, st].astype(BF16), wc_ref[0, j]) + _dot(xi_s[:, st].astype(BF16), wc_ref[1, j])
             + d_ref[:, ch] * load_u(j))
        if interleaved:
            for l in range(slabs):
                y_s[j * slabs + l] = y[:, l * LANES:(l + 1) * LANES]
        else:
            y_ref[0, :, ch] = y
    if interleaved:
        for b in range(n):
            for l in range(S5_DIM // LANES):
                y_ref[b, :, l * LANES:(l + 1) * LANES] = y_s[l, pl.ds(b, tt, stride=n), :]
    hr_out_ref[...] = hr_s[...]
    hi_out_ref[...] = hi_s[...]


def _s5(u4, h0r, h0i, a, wb, wc, d, n, tt, n_tiles, tile0=0):
    rows = tt * n
    interleaved = n > 1 and tt > 1
    const = lambda shape: pl.BlockSpec(shape, lambda i: (0,) * len(shape),
                                       pipeline_mode=pl.Buffered(1))
    state = pl.BlockSpec((n, S5_LANES), lambda i: (0, 0))
    y_block = (n, tt, S5_DIM) if interleaved else (1, rows, S5_DIM)
    y_shape = (n, tt * n_tiles, S5_DIM) if interleaved else (1, rows * n_tiles, S5_DIM)
    return pl.pallas_call(
        functools.partial(_s5_kernel, n=n, tt=tt, interleaved=interleaved),
        grid=(n_tiles,),
        in_specs=[pl.BlockSpec((S5_DIM // LANES, rows, LANES), lambda i: (0, tile0 + i, 0)),
                  const(h0r.shape), const(h0i.shape), const(a.shape), const(wb.shape), const(wc.shape),
                  const((1, S5_DIM))],
        out_specs=[pl.BlockSpec(y_block, lambda i: (0, i, 0)), state, state],
        out_shape=(jax.ShapeDtypeStruct(y_shape, F32),
                   jax.ShapeDtypeStruct((n, S5_LANES), F32),
                   jax.ShapeDtypeStruct((n, S5_LANES), F32)),
        scratch_shapes=[pltpu.VMEM((rows, S5_LANES), F32), pltpu.VMEM((rows, S5_LANES), F32),
                        pltpu.VMEM((n, S5_LANES), F32), pltpu.VMEM((n, S5_LANES), F32),
                        pltpu.VMEM((S5_DIM // LANES, rows if interleaved else SUBLANES, LANES), F32)],
        compiler_params=_params("arbitrary"),
        name="s5_scan",
    )(u4, h0r, h0i, a, wb, wc, d)


def _hgrn_gate_out(o, g, ng):
    parts = []
    for h in range(HG_HEADS):
        oh = o[:, h * HG_HEAD_DIM:(h + 1) * HG_HEAD_DIM]
        ms = jnp.mean(oh * oh, axis=-1, keepdims=True)
        parts.append(oh * lax.rsqrt(ms + EPS))
    return jnp.concatenate(parts, axis=-1) * ng * (g * _sigmoid(g))


def _hgrn_kernel(q_ref, f_ref, i_ref, g_ref, lb_ref, ng_ref, tri_ref, s0_ref,
                 y_ref, s_out_ref, st_s, o_s, *, th, valid):
    i = pl.program_id(1)
    c = HG_CHUNK

    @pl.when(i == 0)
    def _():
        for h in range(HG_HEADS):
            st_s[h] = s0_ref[0, h].T

    lb = lb_ref[...]
    f = lb + (1.0 - lb) * _sigmoid(f_ref[0])
    lc = jnp.log(f)
    k = 1.0 - f
    q = q_ref[0]
    if valid < th:
        live = lax.broadcasted_iota(jnp.int32, (th, 1), 0) < valid
        lc = jnp.where(live, lc, 0.0)
        k = jnp.where(live, k, 0.0)
        q = jnp.where(live, q.astype(F32), 0.0).astype(BF16)
    tri = tri_ref[...]
    tr = tri.shape[0]
    lc_hi = lc.astype(BF16)
    lc_lo = (lc - lc_hi.astype(F32)).astype(BF16)
    b = jnp.concatenate([_dot(tri, lc_hi[r:r + tr]) + _dot(tri, lc_lo[r:r + tr])
                         for r in range(0, th, tr)], axis=0)
    qd = q * jnp.exp(b).astype(BF16)
    kd = k * jnp.exp(-b)
    kdb = kd.astype(BF16)
    vb = i_ref[0]
    causal = (lax.broadcasted_iota(jnp.int32, (c, c), 1) <= lax.broadcasted_iota(jnp.int32, (c, c), 0))
    nt = (((1,), (1,)), ((), ()))
    n_chunks = th // c
    rows = [slice(cc * c, (cc + 1) * c) for cc in range(n_chunks)]
    lanes = [slice(h * HG_HEAD_DIM, (h + 1) * HG_HEAD_DIM) for h in range(HG_HEADS)]
    dec = [jnp.exp(b[cc * c + c - 1:cc * c + c, :]) for cc in range(n_chunks)]
    kdec = [(kd[rows[cc]] * dec[cc]).astype(BF16) for cc in range(n_chunks)]
    att = [[jnp.where(causal, lax.dot_general(qd[rows[cc], ls], kdb[rows[cc], ls], nt,
                                              preferred_element_type=F32), 0.0).astype(BF16)
            for ls in lanes] for cc in range(n_chunks)]
    upd = [[_dot(vb[rows[cc], ls].T, kdec[cc][:, ls])
            for ls in lanes] for cc in range(n_chunks)]
    st_in = [[None] * HG_HEADS for _ in range(n_chunks)]
    for h, ls in enumerate(lanes):
        st = st_s[h]
        for cc in range(n_chunks):
            st_in[cc][h] = st.astype(BF16)
            st = dec[cc][:, ls] * st + upd[cc][h]
        st_s[h] = st
    for cc in range(n_chunks):
        for h, ls in enumerate(lanes):
            o_s[rows[cc], ls] = (
                lax.dot_general(qd[rows[cc], ls], st_in[cc][h], nt, preferred_element_type=F32)
                + _dot(att[cc][h], vb[rows[cc], ls]))
    y_ref[0] = _hgrn_gate_out(o_s[...], g_ref[0].astype(F32), ng_ref[...])

    @pl.when(i == pl.num_programs(1) - 1)
    def _():
        for h in range(HG_HEADS):
            s_out_ref[0, h] = st_s[h].T


def _hgrn_tri(tr):
    idx = np.arange(tr)
    tri = (idx[:, None] // HG_CHUNK == idx[None, :] // HG_CHUNK) & (idx[None, :] <= idx[:, None])
    return jnp.asarray(tri, dtype=BF16)


def _hgrn(zf, qig, lb, ng, s0, th, n_tiles, valid, tile0=0):
    n = zf.shape[0]
    own = s0.shape[0] == n
    tr = min(th, 256)
    col = lambda j: pl.BlockSpec((1, th, HG_DIM), lambda b, i, j=j: (b, tile0 + i, j))
    const = lambda shape: pl.BlockSpec(shape, lambda b, i: (0,) * len(shape),
                                       pipeline_mode=pl.Buffered(1))
    st = pl.BlockSpec((1, HG_HEADS, HG_HEAD_DIM, HG_HEAD_DIM), lambda b, i: (b, 0, 0, 0))
    st_in = pl.BlockSpec((1, HG_HEADS, HG_HEAD_DIM, HG_HEAD_DIM),
                         lambda b, i: (b if own else 0, 0, 0, 0))
    return pl.pallas_call(
        functools.partial(_hgrn_kernel, th=th, valid=valid),
        grid=(n, n_tiles),
        in_specs=[col(0), col(0), col(1), col(2), const((1, HG_DIM)), const((1, HG_DIM)),
                  const((tr, tr)), st_in],
        out_specs=[pl.BlockSpec((1, th, HG_DIM), lambda b, i: (b, i, 0)), st],
        out_shape=(jax.ShapeDtypeStruct((n, n_tiles * th, HG_DIM), F32),
                   jax.ShapeDtypeStruct((n, HG_HEADS, HG_HEAD_DIM, HG_HEAD_DIM), F32)),
        scratch_shapes=[pltpu.VMEM((HG_HEADS, HG_HEAD_DIM, HG_HEAD_DIM), F32),
                        pltpu.VMEM((th, HG_DIM), F32)],
        compiler_params=_params("arbitrary", "arbitrary"),
        name="hgrn_chunks",
    )(qig, zf, qig, qig, lb, ng, _hgrn_tri(tr), s0)


def _hgrn_step_kernel(q_ref, f_ref, i_ref, g_ref, lb_ref, ng_ref, s0_ref,
                      y_ref, s_out_ref, o_s, *, sb):
    lb = lb_ref[...]
    f = lb + (1.0 - lb) * _sigmoid(f_ref[0])
    q = q_ref[0].astype(F32)
    v = i_ref[0].astype(F32)
    pad = jnp.zeros((HG_HEAD_DIM - sb, HG_HEAD_DIM), F32)
    sq = (HG_HEAD_DIM, HG_HEAD_DIM)
    for h in range(HG_HEADS):
        ls = slice(h * HG_HEAD_DIM, (h + 1) * HG_HEAD_DIM)
        fcols = jnp.concatenate([f[:, ls], pad], axis=0).T
        for s in range(sb):
            fc = jnp.broadcast_to(fcols[:, s:s + 1], sq)
            sn = fc * s0_ref[s, h] + (1.0 - fc) * v[s:s + 1, ls]
            s_out_ref[s, h] = sn
            o_s[s:s + 1, ls] = _dot(q[s:s + 1, ls].astype(BF16), sn.astype(BF16))
    y_ref[0] = _hgrn_gate_out(o_s[...], g_ref[0].astype(F32), ng_ref[...])


def _hgrn_step(zf, qig, lb, ng, s0, sb):
    r = s0.shape[0]
    col = lambda j: pl.BlockSpec((1, sb, HG_DIM), lambda i, j=j: (0, i, j))
    vec = pl.BlockSpec((1, HG_DIM), lambda i: (0, 0))
    st = pl.BlockSpec((sb, HG_HEADS, HG_HEAD_DIM, HG_HEAD_DIM), lambda i: (i, 0, 0, 0))
    return pl.pallas_call(
        functools.partial(_hgrn_step_kernel, sb=sb),
        grid=(r // sb,),
        in_specs=[col(0), col(0), col(1), col(2), vec, vec, st],
        out_specs=[pl.BlockSpec((1, sb, HG_DIM), lambda i: (0, i, 0)), st],
        out_shape=(jax.ShapeDtypeStruct((1, r, HG_DIM), F32),
                   jax.ShapeDtypeStruct(s0.shape, F32)),
        scratch_shapes=[pltpu.VMEM((sb, HG_DIM), F32)],
        compiler_params=_params("arbitrary"),
        name="hgrn_step",
    )(qig, zf, qig, qig, lb, ng, s0)


def _mix_out(y5_raw, yh, h, wglu, bglu, wout):
    y5p = _gelu_tanh(y5_raw)
    y5 = y5p * _sigmoid(_dot(y5p.astype(BF16), wglu) + bglu)
    ymix = jnp.concatenate([y5, yh], axis=-1).astype(BF16)
    return h + _dot(ymix, wout)


def _conv_taps(a, prev8, cw):
    rows = a.shape[0]
    cv = cw[3:4] + pltpu.roll(a, 2, 0) * cw[0:1] + pltpu.roll(a, 1, 0) * cw[1:2] + a * cw[2:3]
    d = prev8 - a[rows - SUBLANES:rows]
    rid = lax.broadcasted_iota(jnp.int32, (SUBLANES, 1), 0)
    fix = (jnp.where(rid < 1, pltpu.roll(d, 1, 0), 0.0) * cw[1:2]
           + jnp.where(rid < 2, pltpu.roll(d, 2, 0), 0.0) * cw[0:1])
    return jnp.concatenate([cv[:SUBLANES] + fix, cv[SUBLANES:]], axis=0)


def _ffn_kernel(y5_ref, yh_ref, h_ref, wglu_ref, bglu_ref, wout_ref, g2_ref,
                wa_ref, wv_ref, wd_ref, cw_ref, cin_ref, gf_ref,
                out_ref, cout_ref, carry_s, hn_s, h1_s, rn_s, s_s, *, tm):
    @pl.when(pl.program_id(1) == 0)
    def _():
        carry_s[...] = cin_ref[0]

    h1 = _mix_out(y5_ref[0], yh_ref[0], h_ref[0], wglu_ref[...], bglu_ref[...], wout_ref[...])
    h1_s[...] = h1
    hn_s[...] = (h1 * g2_ref[...]).astype(BF16)
    rn = lax.rsqrt(jnp.mean(h1 * h1, axis=-1, keepdims=True) + EPS)
    rn_s[...] = jnp.broadcast_to(rn, (tm, FF_CHUNK))

    for j in range(N_FF_CHUNKS):
        cs = slice(j * FF_CHUNK, (j + 1) * FF_CHUNK)
        hn = hn_s[...]
        a = _dot(hn, wa_ref[:, cs]) * rn_s[...]
        v = _dot(hn, wv_ref[:, cs]) * rn_s[...]
        cv = _conv_taps(a, carry_s[:, cs], cw_ref[:, cs])
        carry_s[:, cs] = a[tm - SUBLANES:tm]
        s_s[:, cs] = ((cv * _sigmoid(cv)) * v).astype(BF16)

    out_ref[0] = _rms_norm(h1_s[...] + _dot(s_s[...], wd_ref[...]), gf_ref[...])
    cout_ref[0] = carry_s[...]


def _ffn_small_kernel(y5m_ref, y5s_ref, yhm_ref, yhs_ref, hm_ref, hs_ref,
                      wglu_ref, bglu_ref, wout_ref, g2_ref, wa_ref, wv_ref, wd_ref, cw_ref,
                      cin_a_ref, cin_b_ref, gf_ref,
                      out_ref, a_out_ref, mcarry_ref,
                      wglu_bf_ref, wout_bf_ref, wa_bf_ref, wv_bf_ref, wd_bf_ref,
                      hn_s, h1_s, acc_s, *, n_meta):
    j = pl.program_id(0)

    @pl.when(j == 0)
    def _():
        wglu = wglu_ref[...].astype(BF16)
        wout = wout_ref[...].astype(BF16)
        wglu_bf_ref[...] = wglu
        wout_bf_ref[...] = wout
        rows = lambda m_ref, s_ref: jnp.concatenate([m_ref[0], s_ref[0]], axis=0)
        h1 = _mix_out(rows(y5m_ref, y5s_ref), rows(yhm_ref, yhs_ref), rows(hm_ref, hs_ref),
                      wglu, bglu_ref[...], wout)
        h1_s[...] = h1
        hn_s[...] = _rms_norm(h1, g2_ref[...]).astype(BF16)
        acc_s[...] = jnp.zeros_like(acc_s)

    wa = wa_ref[...].astype(BF16)
    wv = wv_ref[...].astype(BF16)
    wd = wd_ref[...].astype(BF16)
    wa_bf_ref[...] = wa
    wv_bf_ref[...] = wv
    wd_bf_ref[...] = wd
    hn = hn_s[...]
    a = _dot(hn, wa)
    v = _dot(hn, wv)
    cw = cw_ref[...]
    a_m, a_s = a[:n_meta], a[n_meta:]
    cv_m = _conv_taps(a_m, jnp.zeros((SUBLANES, FF_CHUNK), F32), cw)
    cv_s = cw[3:4] + cin_a_ref[...] * cw[0:1] + cin_b_ref[...] * cw[1:2] + a_s * cw[2:3]
    mcarry_ref[...] = a_m[n_meta - SUBLANES:]
    a_out_ref[...] = a_s
    cv = jnp.concatenate([cv_m, cv_s], axis=0)
    acc_s[...] += _dot(((cv * _sigmoid(cv)) * v).astype(BF16), wd)

    @pl.when(j == pl.num_programs(0) - 1)
    def _():
        out_ref[...] = _rms_norm((h1_s[...] + acc_s[...])[n_meta:], gf_ref[...])


def _ffn_small(y5m, y5s, yhm, yhs, hm, hs, wglu, bglu, wout, g2, wup, wd, cw, cin_a, cin_b, gf,
               hm_tile=0):
    n_meta, r = y5m.shape[1], y5s.shape[1]
    const = lambda shape: pl.BlockSpec(shape, lambda j: (0,) * len(shape))
    head = lambda rows, width: pl.BlockSpec((1, rows, width), lambda j: (0, 0, 0))
    ff_cols = lambda rows: pl.BlockSpec((rows, FF_CHUNK), lambda j: (0, j))
    sds = jax.ShapeDtypeStruct
    return pl.pallas_call(
        functools.partial(_ffn_small_kernel, n_meta=n_meta),
        grid=(N_FF_CHUNKS,),
        in_specs=[head(n_meta, S5_DIM), head(r, S5_DIM), head(n_meta, HG_DIM), head(r, HG_DIM),
                  pl.BlockSpec((1, n_meta, D_MODEL), lambda j: (0, hm_tile, 0)), head(r, D_MODEL),
                  const(wglu.shape), const(bglu.shape), const(wout.shape), const(g2.shape),
                  ff_cols(D_MODEL),
                  pl.BlockSpec((D_MODEL, FF_CHUNK), lambda j: (0, N_FF_CHUNKS + j)),
                  pl.BlockSpec((FF_CHUNK, D_MODEL), lambda j: (j, 0)),
                  ff_cols(SUBLANES), ff_cols(r), ff_cols(r), const(gf.shape)],
        out_specs=[const((r, D_MODEL)), ff_cols(r), ff_cols(SUBLANES),
                   const(wglu.shape), const(wout.shape), ff_cols(D_MODEL), ff_cols(D_MODEL),
                   pl.BlockSpec((FF_CHUNK, D_MODEL), lambda j: (j, 0))],
        out_shape=(sds((r, D_MODEL), F32), sds((r, D_FF), F32), sds((SUBLANES, D_FF), F32),
                   sds(wglu.shape, BF16), sds(wout.shape, BF16),
                   sds((D_MODEL, D_FF), BF16), sds((D_MODEL, D_FF), BF16), sds(wd.shape, BF16)),
        scratch_shapes=[pltpu.VMEM((n_meta + r, D_MODEL), BF16),
                        pltpu.VMEM((n_meta + r, D_MODEL), F32),
                        pltpu.VMEM((n_meta + r, D_MODEL), F32)],
        compiler_params=_params("arbitrary"),
        name="ffn_small",
    )(y5m, y5s, yhm, yhs, hm, hs, wglu, bglu, wout, g2, wup, wup, wd, cw, cin_a, cin_b, gf)


def _ffn(y5, yh, h, wglu, bglu, wout, g2, wa, wv, wd, cw, cin, gf, tm, n_tiles):
    n = h.shape[0]
    const = lambda shape: pl.BlockSpec(shape, lambda b, i: (0,) * len(shape),
                                       pipeline_mode=pl.Buffered(1))
    own = cin.shape[0] == n
    conv = lambda index: pl.BlockSpec((1, SUBLANES, D_FF), index)
    return pl.pallas_call(
        functools.partial(_ffn_kernel, tm=tm),
        grid=(n, n_tiles),
        in_specs=[pl.BlockSpec((1, tm, S5_DIM), lambda b, i: (b, i, 0)),
                  pl.BlockSpec((1, tm, HG_DIM), lambda b, i: (b, i, 0)),
                  pl.BlockSpec((1, tm, D_MODEL), lambda b, i: (b, i, 0)),
                  const(wglu.shape), const(bglu.shape), const(wout.shape), const(g2.shape),
                  const(wa.shape), const(wv.shape), const(wd.shape), const(cw.shape),
                  conv(lambda b, i: (b if own else 0, 0, 0)), const(gf.shape)],
        out_specs=[pl.BlockSpec((1, tm, D_MODEL), lambda b, i: (b, i, 0)),
                   conv(lambda b, i: (b, 0, 0))],
        out_shape=(jax.ShapeDtypeStruct((n, n_tiles * tm, D_MODEL), F32),
                   jax.ShapeDtypeStruct((n, SUBLANES, D_FF), F32)),
        scratch_shapes=[pltpu.VMEM((SUBLANES, D_FF), F32),
                        pltpu.VMEM((tm, D_MODEL), BF16),
                        pltpu.VMEM((tm, D_MODEL), F32),
                        pltpu.VMEM((tm, FF_CHUNK), F32),
                        pltpu.VMEM((tm, D_FF), BF16)],
        compiler_params=_params("arbitrary", "arbitrary"),
        name="mix_out_ffn",
    )(y5, yh, h, wglu, bglu, wout, g2, wa, wv, wd, cw, cin, gf)


def kernel(x_prompt, x_sample, state_s5_re, state_s5_im, state_hgrn, state_ffn_conv, meta_tokens, norm_mix_g, w_in, s5_lambda_re, s5_lambda_im, s5_log_dt, s5_b_re, s5_b_im, s5_c_re, s5_c_im, s5_d, s5_w_glu, s5_b_glu, hg_lower_bounds, hg_norm_g, w_out, norm_ffn_g, ffn_w_up, ffn_conv_w, ffn_conv_b, ffn_w_down, final_norm_g):
    nb, seq, _ = x_prompt.shape
    ns = x_sample.shape[0]
    li = 0

    a5, bb, lb = _prep(s5_lambda_re[li], s5_lambda_im[li], s5_log_dt[li],
                       s5_b_re[li], s5_b_im[li], hg_lower_bounds)
    a5 = a5.reshape(2, 1, S5_LANES)
    wb = _block_diag(bb)
    wc = _block_diag(jnp.stack([s5_c_re[li], -s5_c_im[li]]).transpose(0, 1, 3, 2))
    d5 = s5_d[li].reshape(1, S5_DIM)
    g1 = norm_mix_g[li].reshape(1, D_MODEL)
    g2 = norm_ffn_g[li].reshape(1, D_MODEL)
    gf = final_norm_g.reshape(1, D_MODEL)
    ng = hg_norm_g[li].reshape(1, HG_DIM)
    bglu = s5_b_glu[li].reshape(1, S5_DIM)
    cw = jnp.concatenate([ffn_conv_w[li], ffn_conv_b[li][None],
                          jnp.zeros((SUBLANES - CONV_W - 1, D_FF), F32)], axis=0)
    s5w = (a5, wb, wc, d5)

    xs = jnp.concatenate([x_sample.reshape(ns, D_MODEL), meta_tokens,
                          jnp.zeros((HG_CHUNK - N_META, D_MODEL), F32)])[None]
    u_tm, zf, qig, w_in_b = _inproj(xs, g1, w_in[li], ns + HG_CHUNK, 1, emit_bf16=True)

    y5s, s5r, s5i = _s5(u_tm, state_s5_re[li].reshape(ns, S5_LANES), state_s5_im[li].reshape(ns, S5_LANES),
                        *s5w, ns, 1, 1)
    yhs, shg = _hgrn_step(zf, qig, lb, ng, state_hgrn[li], STEP_ROWS)

    z5 = jnp.zeros((1, S5_LANES), F32)
    zh = jnp.zeros((1, HG_HEADS, HG_HEAD_DIM, HG_HEAD_DIM), F32)
    y5m, m5r, m5i = _s5(u_tm, z5, z5, *s5w, 1, N_META, 1, tile0=ns // N_META)
    yhm, mhg = _hgrn(zf, qig, lb, ng, zh, HG_CHUNK, 1, N_META, tile0=ns // HG_CHUNK)

    buf = state_ffn_conv[li]
    y_sample, a_new, mconv, wglu, wout, wa, wv, wd = _ffn_small(
        y5m, y5s, yhm, yhs, xs, xs, s5_w_glu[li], bglu, w_out[li], g2,
        ffn_w_up[li], ffn_w_down[li], cw, buf[:, 0], buf[:, 1], gf, hm_tile=ns // N_META)
    sconv = jnp.stack([buf[:, 1], a_new], axis=1)

    u_tm, zf, qig = _inproj(x_prompt, g1, w_in_b, IN_TILE, seq // IN_TILE, nb=IN_SEQS)
    y5, p5r, p5i = _s5(u_tm, m5r, m5i, *s5w, nb, S5_TILE, seq // S5_TILE)
    yh, phg = _hgrn(zf, qig, lb, ng, mhg, seq, 1, seq)
    y_prompt, pconv = _ffn(y5, yh, x_prompt, wglu, bglu, wout, g2, wa, wv, wd, cw, mconv[None], gf,
                           FFN_TILE, seq // FFN_TILE)

    st5 = lambda t: t.reshape(1, -1, S5_GROUPS, S5_STATE)
    pconv = pconv[:, SUBLANES - (CONV_W - 1):, :]
    return (y_prompt, y_sample.reshape(ns, 1, D_MODEL),
            st5(p5r), st5(p5i), phg[None], pconv[None],
            st5(s5r), st5(s5i), shg[None], sconv[None])
```

```python
import functools

import jax
import jax.numpy as jnp
import numpy as np
from jax import lax
from jax.experimental import pallas as pl
from jax.experimental.pallas import tpu as pltpu

F32 = jnp.float32
BF16 = jnp.bfloat16

D_MODEL = 1024
N_META = 16
S5_DIM = 512
S5_GROUP = 16
S5_GROUPS = 32
S5_STATE = 64
S5_LANES = S5_GROUPS * S5_STATE
HG_DIM = 512
HG_HEAD_DIM = 128
HG_HEADS = 4
HG_CHUNK = 64
D_FF = 2816
CONV_W = 3
EPS = 1e-6

S5_HALVES = 2
S5_HALF_CH = S5_DIM // S5_HALVES
S5_HALF_ST = S5_LANES // S5_HALVES
SCAN_LANES = 512
FF_CHUNK = 256
N_FF_CHUNKS = D_FF // FF_CHUNK
SUBLANES = 8
LANES = 128
VMEM_LIMIT = 56 * 1024 * 1024
IN_TILE = 512
IN_SEQS = 2
S5_TILE = 128
FFN_TILE = 1024
STEP_ROWS = 16


def _sigmoid(x):
    return 0.5 + 0.5 * jnp.tanh(0.5 * x)


def _rms_norm(x, g):
    ms = jnp.mean(x * x, axis=-1, keepdims=True)
    return x * lax.rsqrt(ms + EPS) * g


def _dot(a, b):
    return jnp.dot(a, b, preferred_element_type=F32)


def _params(*sem):
    return pltpu.CompilerParams(dimension_semantics=sem, vmem_limit_bytes=VMEM_LIMIT)


def _prep_kernel(lam_ref, dt_ref, bt_ref, hlb_ref, a_ref, bb_ref, lb_ref):
    lam_re = lam_ref[0]
    lam_im = lam_ref[1]
    dt = jnp.exp(dt_ref[...])
    mag = jnp.exp(lam_re * dt)
    ar = mag * jnp.cos(lam_im * dt)
    ai = mag * jnp.sin(lam_im * dt)
    nr = ar - 1.0
    den = lam_re * lam_re + lam_im * lam_im
    cr = (nr * lam_re + ai * lam_im) / den
    ci = (ai * lam_re - nr * lam_im) / den
    a_ref[0] = ar
    a_ref[1] = ai
    bt_re = bt_ref[0]
    bt_im = bt_ref[1]
    bb_ref[0] = cr * bt_re - ci * bt_im
    bb_ref[1] = cr * bt_im + ci * bt_re
    hlb = hlb_ref[...]
    e = jnp.exp(hlb - jnp.max(hlb, axis=0, keepdims=True))
    lb_ref[...] = e[0:1] / jnp.sum(e, axis=0, keepdims=True)


def _prep(lam_re, lam_im, log_dt, b_re, b_im, hlb):
    g, p = lam_re.shape
    sds = jax.ShapeDtypeStruct
    return pl.pallas_call(
        _prep_kernel,
        out_shape=(sds((2, g, 1, p), F32), sds((2, g, S5_GROUP, p), F32), sds((1, HG_DIM), F32)),
        name="param_prep",
    )(jnp.stack([lam_re, lam_im]).reshape(2, g, 1, p),
      jnp.broadcast_to(log_dt.reshape(g, 1, 1), (g, 1, p)),
      jnp.stack([b_re, b_im]).transpose(0, 1, 3, 2), hlb)


def _block_diag(blocks):
    two, g, r, c = blocks.shape
    per = g // S5_HALVES
    eye = np.eye(per, dtype=np.float32)
    out = jnp.einsum("ajgrc,gh->ajgrhc", blocks.reshape(two, S5_HALVES, per, r, c), eye)
    return out.reshape(two, S5_HALVES, per * r, per * c).astype(BF16)


def _inproj_kernel(x_ref, g_ref, w_ref, u_ref, f_ref, qig_ref, *wb_ref, n, nb, tm):
    x = x_ref[...].reshape(nb * tm, D_MODEL)
    hn = (x * g_ref[...]).astype(BF16)
    rn = lax.rsqrt(jnp.mean(x * x, axis=-1, keepdims=True) + EPS)
    rn = jnp.broadcast_to(rn, (nb * tm, HG_DIM))
    if wb_ref:
        wb_ref[0][...] = w_ref[...].astype(BF16)
        w_ref = wb_ref[0]
    col = lambda j: _dot(hn, w_ref[:, j * HG_DIM:(j + 1) * HG_DIM]) * rn
    b0 = pl.program_id(1) * nb
    u = col(0)
    for l in range(S5_DIM // LANES):
        for r in range(nb):
            ul = u[r * tm:(r + 1) * tm, l * LANES:(l + 1) * LANES]
            if n == 1:
                u_ref[l] = ul
            else:
                u_ref[l, pl.ds(b0 + r, tm, stride=n), :] = ul
    f_ref[...] = col(2).reshape(nb, tm, HG_DIM)
    for j, src in enumerate((1, 3, 4)):
        qig_ref[:, :, j * HG_DIM:(j + 1) * HG_DIM] = col(src).astype(BF16).reshape(nb, tm, HG_DIM)


def _inproj(x, g, w, tm, n_tiles, nb=1, emit_bf16=False):
    n, l, d = x.shape
    cols = w.shape[1]
    rows = tm * n_tiles
    const = lambda shape: pl.BlockSpec(shape, lambda i, b: (0,) * len(shape),
                                       pipeline_mode=pl.Buffered(1))
    return pl.pallas_call(
        functools.partial(_inproj_kernel, n=n, nb=nb, tm=tm),
        grid=(n_tiles, n // nb),
        in_specs=[pl.BlockSpec((nb, tm, d), lambda i, b: (b, i, 0)),
                  const((1, d)), const((d, cols))],
        out_specs=[pl.BlockSpec((S5_DIM // LANES, tm * n, LANES), lambda i, b: (0, i, 0)),
                   pl.BlockSpec((nb, tm, HG_DIM), lambda i, b: (b, i, 0)),
                   pl.BlockSpec((nb, tm, 3 * HG_DIM), lambda i, b: (b, i, 0))]
                  + [pl.BlockSpec((d, cols), lambda i, b: (0, 0))] * emit_bf16,
        out_shape=(jax.ShapeDtypeStruct((S5_DIM // LANES, rows * n, LANES), F32),
                   jax.ShapeDtypeStruct((n, rows, HG_DIM), F32),
                   jax.ShapeDtypeStruct((n, rows, 3 * HG_DIM), BF16))
                  + (jax.ShapeDtypeStruct((d, cols), BF16),) * emit_bf16,
        compiler_params=_params("arbitrary", "arbitrary"),
        name="inproj",
    )(x, g, w)


def _gelu_tanh(y):
    return 0.5 * y * (1.0 + jnp.tanh(0.7978845608028654 * (y + 0.044715 * (y * y * y))))


def _s5_kernel(u_ref, h0r_ref, h0i_ref, a_ref, wb_ref, wc_ref, d_ref,
               y_ref, hr_out_ref, hi_out_ref, xr_s, xi_s, hr_s, hi_s, y_s, *, n, tt, interleaved):
    @pl.when(pl.program_id(0) == 0)
    def _():
        hr_s[...] = jnp.broadcast_to(h0r_ref[...], hr_s.shape)
        hi_s[...] = jnp.broadcast_to(h0i_ref[...], hi_s.shape)

    slabs = S5_HALF_CH // LANES
    load_u = lambda j: jnp.concatenate([u_ref[j * slabs + l] for l in range(slabs)], axis=1)
    for j in range(S5_HALVES):
        st = slice(j * S5_HALF_ST, (j + 1) * S5_HALF_ST)
        ub = load_u(j).astype(BF16)
        xr_s[:, st] = _dot(ub, wb_ref[0, j])
        xi_s[:, st] = _dot(ub, wb_ref[1, j])
    for c in range(S5_LANES // SCAN_LANES):
        loc = slice(c * SCAN_LANES, (c + 1) * SCAN_LANES)
        ar = jnp.broadcast_to(a_ref[0, :, loc], (n, SCAN_LANES))
        ai = jnp.broadcast_to(a_ref[1, :, loc], (n, SCAN_LANES))

        def step(t, carry, ar=ar, ai=ai, loc=loc):
            hr, hi = carry
            r = 0 if tt == 1 else pl.multiple_of(t * n, n)
            nhr = ar * hr - ai * hi + xr_s[pl.ds(r, n), loc]
            nhi = ar * hi + ai * hr + xi_s[pl.ds(r, n), loc]
            xr_s[pl.ds(r, n), loc] = nhr
            xi_s[pl.ds(r, n), loc] = nhi
            return nhr, nhi

        carry = (hr_s[:, loc], hi_s[:, loc])
        if tt == 1:
            carry = step(0, carry)
        else:
            carry = lax.fori_loop(0, tt, step, carry, unroll=True)
        hr_s[:, loc] = carry[0]
        hi_s[:, loc] = carry[1]
    for j in range(S5_HALVES):
        ch = slice(j * S5_HALF_CH, (j + 1) * S5_HALF_CH)
        st = slice(j * S5_HALF_ST, (j + 1) * S5_HALF_ST)
        y = (_dot(xr_s[:, st].astype(BF16), wc_ref[0, j]) + _dot(xi_s[:, st].astype(BF16), wc_ref[1, j])
             + d_ref[:, ch] * load_u(j))
        if interleaved:
            for l in range(slabs):
                y_s[j * slabs + l] = y[:, l * LANES:(l + 1) * LANES]
        else:
            y_ref[0, :, ch] = y
    if interleaved:
        for b in range(n):
            for l in range(S5_DIM // LANES):
                y_ref[b, :, l * LANES:(l + 1) * LANES] = y_s[l, pl.ds(b, tt, stride=n), :]
    hr_out_ref[...] = hr_s[...]
    hi_out_ref[...] = hi_s[...]


def _s5(u4, h0r, h0i, a, wb, wc, d, n, tt, n_tiles, tile0=0):
    rows = tt * n
    interleaved = n > 1 and tt > 1
    const = lambda shape: pl.BlockSpec(shape, lambda i: (0,) * len(shape),
                                       pipeline_mode=pl.Buffered(1))
    state = pl.BlockSpec((n, S5_LANES), lambda i: (0, 0))
    y_block = (n, tt, S5_DIM) if interleaved else (1, rows, S5_DIM)
    y_shape = (n, tt * n_tiles, S5_DIM) if interleaved else (1, rows * n_tiles, S5_DIM)
    return pl.pallas_call(
        functools.partial(_s5_kernel, n=n, tt=tt, interleaved=interleaved),
        grid=(n_tiles,),
        in_specs=[pl.BlockSpec((S5_DIM // LANES, rows, LANES), lambda i: (0, tile0 + i, 0)),
                  const(h0r.shape), const(h0i.shape), const(a.shape), const(wb.shape), const(wc.shape),
                  const((1, S5_DIM))],
        out_specs=[pl.BlockSpec(y_block, lambda i: (0, i, 0)), state, state],
        out_shape=(jax.ShapeDtypeStruct(y_shape, F32),
                   jax.ShapeDtypeStruct((n, S5_LANES), F32),
                   jax.ShapeDtypeStruct((n, S5_LANES), F32)),
        scratch_shapes=[pltpu.VMEM((rows, S5_LANES), F32), pltpu.VMEM((rows, S5_LANES), F32),
                        pltpu.VMEM((n, S5_LANES), F32), pltpu.VMEM((n, S5_LANES), F32),
                        pltpu.VMEM((S5_DIM // LANES, rows if interleaved else SUBLANES, LANES), F32)],
        compiler_params=_params("arbitrary"),
        name="s5_scan",
    )(u4, h0r, h0i, a, wb, wc, d)


def _hgrn_gate_out(o, g, ng):
    parts = []
    for h in range(HG_HEADS):
        oh = o[:, h * HG_HEAD_DIM:(h + 1) * HG_HEAD_DIM]
        ms = jnp.mean(oh * oh, axis=-1, keepdims=True)
        parts.append(oh * lax.rsqrt(ms + EPS))
    return jnp.concatenate(parts, axis=-1) * ng * (g * _sigmoid(g))


def _hgrn_kernel(q_ref, f_ref, i_ref, g_ref, lb_ref, ng_ref, tri_ref, s0_ref,
                 y_ref, s_out_ref, st_s, o_s, *, th, valid):
    i = pl.program_id(1)
    c = HG_CHUNK

    @pl.when(i == 0)
    def _():
        for h in range(HG_HEADS):
            st_s[h] = s0_ref[0, h].T

    lb = lb_ref[...]
    f = lb + (1.0 - lb) * _sigmoid(f_ref[0])
    lc = jnp.log(f)
    k = 1.0 - f
    q = q_ref[0]
    if valid < th:
        live = lax.broadcasted_iota(jnp.int32, (th, 1), 0) < valid
        lc = jnp.where(live, lc, 0.0)
        k = jnp.where(live, k, 0.0)
        q = jnp.where(live, q.astype(F32), 0.0).astype(BF16)
    tri = tri_ref[...]
    tr = tri.shape[0]
    lc_hi = lc.astype(BF16)
    lc_lo = (lc - lc_hi.astype(F32)).astype(BF16)
    b = jnp.concatenate([_dot(tri, lc_hi[r:r + tr]) + _dot(tri, lc_lo[r:r + tr])
                         for r in range(0, th, tr)], axis=0)
    qd = q * jnp.exp(b).astype(BF16)
    kd = k * jnp.exp(-b)
    kdb = kd.astype(BF16)
    vb = i_ref[0]
    causal = (lax.broadcasted_iota(jnp.int32, (c, c), 1) <= lax.broadcasted_iota(jnp.int32, (c, c), 0))
    nt = (((1,), (1,)), ((), ()))
    n_chunks = th // c
    rows = [slice(cc * c, (cc + 1) * c) for cc in range(n_chunks)]
    lanes = [slice(h * HG_HEAD_DIM, (h + 1) * HG_HEAD_DIM) for h in range(HG_HEADS)]
    dec = [jnp.exp(b[cc * c + c - 1:cc * c + c, :]) for cc in range(n_chunks)]
    kdec = [(kd[rows[cc]] * dec[cc]).astype(BF16) for cc in range(n_chunks)]
    att = [[jnp.where(causal, lax.dot_general(qd[rows[cc], ls], kdb[rows[cc], ls], nt,
                                              preferred_element_type=F32), 0.0).astype(BF16)
            for ls in lanes] for cc in range(n_chunks)]
    upd = [[_dot(vb[rows[cc], ls].T, kdec[cc][:, ls])
            for ls in lanes] for cc in range(n_chunks)]
    st_in = [[None] * HG_HEADS for _ in range(n_chunks)]
    for h, ls in enumerate(lanes):
        st = st_s[h]
        for cc in range(n_chunks):
            st_in[cc][h] = st.astype(BF16)
            st = dec[cc][:, ls] * st + upd[cc][h]
        st_s[h] = st
    for cc in range(n_chunks):
        for h, ls in enumerate(lanes):
            o_s[rows[cc], ls] = (
                lax.dot_general(qd[rows[cc], ls], st_in[cc][h], nt, preferred_element_type=F32)
                + _dot(att[cc][h], vb[rows[cc], ls]))
    y_ref[0] = _hgrn_gate_out(o_s[...], g_ref[0].astype(F32), ng_ref[...])

    @pl.when(i == pl.num_programs(1) - 1)
    def _():
        for h in range(HG_HEADS):
            s_out_ref[0, h] = st_s[h].T


def _hgrn_tri(tr):
    idx = np.arange(tr)
    tri = (idx[:, None] // HG_CHUNK == idx[None, :] // HG_CHUNK) & (idx[None, :] <= idx[:, None])
    return jnp.asarray(tri, dtype=BF16)


def _hgrn(zf, qig, lb, ng, s0, th, n_tiles, valid, tile0=0):
    n = zf.shape[0]
    own = s0.shape[0] == n
    tr = min(th, 256)
    col = lambda j: pl.BlockSpec((1, th, HG_DIM), lambda b, i, j=j: (b, tile0 + i, j))
    const = lambda shape: pl.BlockSpec(shape, lambda b, i: (0,) * len(shape),
                                       pipeline_mode=pl.Buffered(1))
    st = pl.BlockSpec((1, HG_HEADS, HG_HEAD_DIM, HG_HEAD_DIM), lambda b, i: (b, 0, 0, 0))
    st_in = pl.BlockSpec((1, HG_HEADS, HG_HEAD_DIM, HG_HEAD_DIM),
                         lambda b, i: (b if own else 0, 0, 0, 0))
    return pl.pallas_call(
        functools.partial(_hgrn_kernel, th=th, valid=valid),
        grid=(n, n_tiles),
        in_specs=[col(0), col(0), col(1), col(2), const((1, HG_DIM)), const((1, HG_DIM)),
                  const((tr, tr)), st_in],
        out_specs=[pl.BlockSpec((1, th, HG_DIM), lambda b, i: (b, i, 0)), st],
        out_shape=(jax.ShapeDtypeStruct((n, n_tiles * th, HG_DIM), F32),
                   jax.ShapeDtypeStruct((n, HG_HEADS, HG_HEAD_DIM, HG_HEAD_DIM), F32)),
        scratch_shapes=[pltpu.VMEM((HG_HEADS, HG_HEAD_DIM, HG_HEAD_DIM), F32),
                        pltpu.VMEM((th, HG_DIM), F32)],
        compiler_params=_params("arbitrary", "arbitrary"),
        name="hgrn_chunks",
    )(qig, zf, qig, qig, lb, ng, _hgrn_tri(tr), s0)


def _hgrn_step_kernel(q_ref, f_ref, i_ref, g_ref, lb_ref, ng_ref, s0_ref,
                      y_ref, s_out_ref, o_s, *, sb):
    lb = lb_ref[...]
    f = lb + (1.0 - lb) * _sigmoid(f_ref[0])
    q = q_ref[0].astype(F32)
    v = i_ref[0].astype(F32)
    pad = jnp.zeros((HG_HEAD_DIM - sb, HG_HEAD_DIM), F32)
    sq = (HG_HEAD_DIM, HG_HEAD_DIM)
    for h in range(HG_HEADS):
        ls = slice(h * HG_HEAD_DIM, (h + 1) * HG_HEAD_DIM)
        fcols = jnp.concatenate([f[:, ls], pad], axis=0).T
        for s in range(sb):
            fc = jnp.broadcast_to(fcols[:, s:s + 1], sq)
            sn = fc * s0_ref[s, h] + (1.0 - fc) * v[s:s + 1, ls]
            s_out_ref[s, h] = sn
            o_s[s:s + 1, ls] = _dot(q[s:s + 1, ls].astype(BF16), sn.astype(BF16))
    y_ref[0] = _hgrn_gate_out(o_s[...], g_ref[0].astype(F32), ng_ref[...])


def _hgrn_step(zf, qig, lb, ng, s0, sb):
    r = s0.shape[0]
    col = lambda j: pl.BlockSpec((1, sb, HG_DIM), lambda i, j=j: (0, i, j))
    vec = pl.BlockSpec((1, HG_DIM), lambda i: (0, 0))
    st = pl.BlockSpec((sb, HG_HEADS, HG_HEAD_DIM, HG_HEAD_DIM), lambda i: (i, 0, 0, 0))
    return pl.pallas_call(
        functools.partial(_hgrn_step_kernel, sb=sb),
        grid=(r // sb,),
        in_specs=[col(0), col(0), col(1), col(2), vec, vec, st],
        out_specs=[pl.BlockSpec((1, sb, HG_DIM), lambda i: (0, i, 0)), st],
        out_shape=(jax.ShapeDtypeStruct((1, r, HG_DIM), F32),
                   jax.ShapeDtypeStruct(s0.shape, F32)),
        scratch_shapes=[pltpu.VMEM((sb, HG_DIM), F32)],
        compiler_params=_params("arbitrary"),
        name="hgrn_step",
    )(qig, zf, qig, qig, lb, ng, s0)


def _mix_out(y5_raw, yh, h, wglu, bglu, wout):
    y5p = _gelu_tanh(y5_raw)
    y5 = y5p * _sigmoid(_dot(y5p.astype(BF16), wglu) + bglu)
    ymix = jnp.concatenate([y5, yh], axis=-1).astype(BF16)
    return h + _dot(ymix, wout)


def _conv_taps(a, prev8, cw):
    rows = a.shape[0]
    cv = cw[3:4] + pltpu.roll(a, 2, 0) * cw[0:1] + pltpu.roll(a, 1, 0) * cw[1:2] + a * cw[2:3]
    d = prev8 - a[rows - SUBLANES:rows]
    rid = lax.broadcasted_iota(jnp.int32, (SUBLANES, 1), 0)
    fix = (jnp.where(rid < 1, pltpu.roll(d, 1, 0), 0.0) * cw[1:2]
           + jnp.where(rid < 2, pltpu.roll(d, 2, 0), 0.0) * cw[0:1])
    return jnp.concatenate([cv[:SUBLANES] + fix, cv[SUBLANES:]], axis=0)


def _ffn_kernel(y5_ref, yh_ref, h_ref, wglu_ref, bglu_ref, wout_ref, g2_ref,
                wa_ref, wv_ref, wd_ref, cw_ref, cin_ref, gf_ref,
                out_ref, cout_ref, carry_s, hn_s, h1_s, rn_s, s_s, *, tm):
    @pl.when(pl.program_id(1) == 0)
    def _():
        carry_s[...] = cin_ref[0]

    h1 = _mix_out(y5_ref[0], yh_ref[0], h_ref[0], wglu_ref[...], bglu_ref[...], wout_ref[...])
    h1_s[...] = h1
    hn_s[...] = (h1 * g2_ref[...]).astype(BF16)
    rn = lax.rsqrt(jnp.mean(h1 * h1, axis=-1, keepdims=True) + EPS)
    rn_s[...] = jnp.broadcast_to(rn, (tm, FF_CHUNK))

    for j in range(N_FF_CHUNKS):
        cs = slice(j * FF_CHUNK, (j + 1) * FF_CHUNK)
        hn = hn_s[...]
        a = _dot(hn, wa_ref[:, cs]) * rn_s[...]
        v = _dot(hn, wv_ref[:, cs]) * rn_s[...]
        cv = _conv_taps(a, carry_s[:, cs], cw_ref[:, cs])
        carry_s[:, cs] = a[tm - SUBLANES:tm]
        s_s[:, cs] = ((cv * _sigmoid(cv)) * v).astype(BF16)

    out_ref[0] = _rms_norm(h1_s[...] + _dot(s_s[...], wd_ref[...]), gf_ref[...])
    cout_ref[0] = carry_s[...]


def _ffn_small_kernel(y5m_ref, y5s_ref, yhm_ref, yhs_ref, hm_ref, hs_ref,
                      wglu_ref, bglu_ref, wout_ref, g2_ref, wa_ref, wv_ref, wd_ref, cw_ref,
                      cin_a_ref, cin_b_ref, gf_ref,
                      out_ref, a_out_ref, mcarry_ref,
                      wglu_bf_ref, wout_bf_ref, wa_bf_ref, wv_bf_ref, wd_bf_ref,
                      hn_s, h1_s, acc_s, *, n_meta):
    j = pl.program_id(0)

    @pl.when(j == 0)
    def _():
        wglu = wglu_ref[...].astype(BF16)
        wout = wout_ref[...].astype(BF16)
        wglu_bf_ref[...] = wglu
        wout_bf_ref[...] = wout
        rows = lambda m_ref, s_ref: jnp.concatenate([m_ref[0], s_ref[0]], axis=0)
        h1 = _mix_out(rows(y5m_ref, y5s_ref), rows(yhm_ref, yhs_ref), rows(hm_ref, hs_ref),
                      wglu, bglu_ref[...], wout)
        h1_s[...] = h1
        hn_s[...] = _rms_norm(h1, g2_ref[...]).astype(BF16)
        acc_s[...] = jnp.zeros_like(acc_s)

    wa = wa_ref[...].astype(BF16)
    wv = wv_ref[...].astype(BF16)
    wd = wd_ref[...].astype(BF16)
    wa_bf_ref[...] = wa
    wv_bf_ref[...] = wv
    wd_bf_ref[...] = wd
    hn = hn_s[...]
    a = _dot(hn, wa)
    v = _dot(hn, wv)
    cw = cw_ref[...]
    a_m, a_s = a[:n_meta], a[n_meta:]
    cv_m = _conv_taps(a_m, jnp.zeros((SUBLANES, FF_CHUNK), F32), cw)
    cv_s = cw[3:4] + cin_a_ref[...] * cw[0:1] + cin_b_ref[...] * cw[1:2] + a_s * cw[2:3]
    mcarry_ref[...] = a_m[n_meta - SUBLANES:]
    a_out_ref[...] = a_s
    cv = jnp.concatenate([cv_m, cv_s], axis=0)
    acc_s[...] += _dot(((cv * _sigmoid(cv)) * v).astype(BF16), wd)

    @pl.when(j == pl.num_programs(0) - 1)
    def _():
        out_ref[...] = _rms_norm((h1_s[...] + acc_s[...])[n_meta:], gf_ref[...])


def _ffn_small(y5m, y5s, yhm, yhs, hm, hs, wglu, bglu, wout, g2, wup, wd, cw, cin_a, cin_b, gf,
               hm_tile=0):
    n_meta, r = y5m.shape[1], y5s.shape[1]
    const = lambda shape: pl.BlockSpec(shape, lambda j: (0,) * len(shape))
    head = lambda rows, width: pl.BlockSpec((1, rows, width), lambda j: (0, 0, 0))
    ff_cols = lambda rows: pl.BlockSpec((rows, FF_CHUNK), lambda j: (0, j))
    sds = jax.ShapeDtypeStruct
    return pl.pallas_call(
        functools.partial(_ffn_small_kernel, n_meta=n_meta),
        grid=(N_FF_CHUNKS,),
        in_specs=[head(n_meta, S5_DIM), head(r, S5_DIM), head(n_meta, HG_DIM), head(r, HG_DIM),
                  pl.BlockSpec((1, n_meta, D_MODEL), lambda j: (0, hm_tile, 0)), head(r, D_MODEL),
                  const(wglu.shape), const(bglu.shape), const(wout.shape), const(g2.shape),
                  ff_cols(D_MODEL),
                  pl.BlockSpec((D_MODEL, FF_CHUNK), lambda j: (0, N_FF_CHUNKS + j)),
                  pl.BlockSpec((FF_CHUNK, D_MODEL), lambda j: (j, 0)),
                  ff_cols(SUBLANES), ff_cols(r), ff_cols(r), const(gf.shape)],
        out_specs=[const((r, D_MODEL)), ff_cols(r), ff_cols(SUBLANES),
                   const(wglu.shape), const(wout.shape), ff_cols(D_MODEL), ff_cols(D_MODEL),
                   pl.BlockSpec((FF_CHUNK, D_MODEL), lambda j: (j, 0))],
        out_shape=(sds((r, D_MODEL), F32), sds((r, D_FF), F32), sds((SUBLANES, D_FF), F32),
                   sds(wglu.shape, BF16), sds(wout.shape, BF16),
                   sds((D_MODEL, D_FF), BF16), sds((D_MODEL, D_FF), BF16), sds(wd.shape, BF16)),
        scratch_shapes=[pltpu.VMEM((n_meta + r, D_MODEL), BF16),
                        pltpu.VMEM((n_meta + r, D_MODEL), F32),
                        pltpu.VMEM((n_meta + r, D_MODEL), F32)],
        compiler_params=_params("arbitrary"),
        name="ffn_small",
    )(y5m, y5s, yhm, yhs, hm, hs, wglu, bglu, wout, g2, wup, wup, wd, cw, cin_a, cin_b, gf)


def _ffn(y5, yh, h, wglu, bglu, wout, g2, wa, wv, wd, cw, cin, gf, tm, n_tiles):
    n = h.shape[0]
    const = lambda shape: pl.BlockSpec(shape, lambda b, i: (0,) * len(shape),
                                       pipeline_mode=pl.Buffered(1))
    own = cin.shape[0] == n
    conv = lambda index: pl.BlockSpec((1, SUBLANES, D_FF), index)
    return pl.pallas_call(
        functools.partial(_ffn_kernel, tm=tm),
        grid=(n, n_tiles),
        in_specs=[pl.BlockSpec((1, tm, S5_DIM), lambda b, i: (b, i, 0)),
                  pl.BlockSpec((1, tm, HG_DIM), lambda b, i: (b, i, 0)),
                  pl.BlockSpec((1, tm, D_MODEL), lambda b, i: (b, i, 0)),
                  const(wglu.shape), const(bglu.shape), const(wout.shape), const(g2.shape),
                  const(wa.shape), const(wv.shape), const(wd.shape), const(cw.shape),
                  conv(lambda b, i: (b if own else 0, 0, 0)), const(gf.shape)],
        out_specs=[pl.BlockSpec((1, tm, D_MODEL), lambda b, i: (b, i, 0)),
                   conv(lambda b, i: (b, 0, 0))],
        out_shape=(jax.ShapeDtypeStruct((n, n_tiles * tm, D_MODEL), F32),
                   jax.ShapeDtypeStruct((n, SUBLANES, D_FF), F32)),
        scratch_shapes=[pltpu.VMEM((SUBLANES, D_FF), F32),
                        pltpu.VMEM((tm, D_MODEL), BF16),
                        pltpu.VMEM((tm, D_MODEL), F32),
                        pltpu.VMEM((tm, FF_CHUNK), F32),
                        pltpu.VMEM((tm, D_FF), BF16)],
        compiler_params=_params("arbitrary", "arbitrary"),
        name="mix_out_ffn",
    )(y5, yh, h, wglu, bglu, wout, g2, wa, wv, wd, cw, cin, gf)


def kernel(x_prompt, x_sample, state_s5_re, state_s5_im, state_hgrn, state_ffn_conv, meta_tokens, norm_mix_g, w_in, s5_lambda_re, s5_lambda_im, s5_log_dt, s5_b_re, s5_b_im, s5_c_re, s5_c_im, s5_d, s5_w_glu, s5_b_glu, hg_lower_bounds, hg_norm_g, w_out, norm_ffn_g, ffn_w_up, ffn_conv_w, ffn_conv_b, ffn_w_down, final_norm_g):
    nb, seq, _ = x_prompt.shape
    ns = x_sample.shape[0]
    li = 0

    a5, bb, lb = _prep(s5_lambda_re[li], s5_lambda_im[li], s5_log_dt[li],
                       s5_b_re[li], s5_b_im[li], hg_lower_bounds)
    a5 = a5.reshape(2, 1, S5_LANES)
    wb = _block_diag(bb)
    wc = _block_diag(jnp.stack([s5_c_re[li], -s5_c_im[li]]).transpose(0, 1, 3, 2))
    d5 = s5_d[li].reshape(1, S5_DIM)
    g1 = norm_mix_g[li].reshape(1, D_MODEL)
    g2 = norm_ffn_g[li].reshape(1, D_MODEL)
    gf = final_norm_g.reshape(1, D_MODEL)
    ng = hg_norm_g[li].reshape(1, HG_DIM)
    bglu = s5_b_glu[li].reshape(1, S5_DIM)
    cw = jnp.concatenate([ffn_conv_w[li], ffn_conv_b[li][None],
                          jnp.zeros((SUBLANES - CONV_W - 1, D_FF), F32)], axis=0)
    s5w = (a5, wb, wc, d5)

    xs = jnp.concatenate([x_sample.reshape(ns, D_MODEL), meta_tokens,
                          jnp.zeros((HG_CHUNK - N_META, D_MODEL), F32)])[None]
    u_tm, zf, qig, w_in_b = _inproj(xs, g1, w_in[li], ns + HG_CHUNK, 1, emit_bf16=True)

    y5s, s5r, s5i = _s5(u_tm, state_s5_re[li].reshape(ns, S5_LANES), state_s5_im[li].reshape(ns, S5_LANES),
                        *s5w, ns, 1, 1)
    yhs, shg = _hgrn_step(zf, qig, lb, ng, state_hgrn[li], STEP_ROWS)

    z5 = jnp.zeros((1, S5_LANES), F32)
    zh = jnp.zeros((1, HG_HEADS, HG_HEAD_DIM, HG_HEAD_DIM), F32)
    y5m, m5r, m5i = _s5(u_tm, z5, z5, *s5w, 1, N_META, 1, tile0=ns // N_META)
    yhm, mhg = _hgrn(zf, qig, lb, ng, zh, HG_CHUNK, 1, N_META, tile0=ns // HG_CHUNK)

    buf = state_ffn_conv[li]
    y_sample, a_new, mconv, wglu, wout, wa, wv, wd = _ffn_small(
        y5m, y5s, yhm, yhs, xs, xs, s5_w_glu[li], bglu, w_out[li], g2,
        ffn_w_up[li], ffn_w_down[li], cw, buf[:, 0], buf[:, 1], gf, hm_tile=ns // N_META)
    sconv = jnp.stack([buf[:, 1], a_new], axis=1)

    u_tm, zf, qig = _inproj(x_prompt, g1, w_in_b, IN_TILE, seq // IN_TILE, nb=IN_SEQS)
    y5, p5r, p5i = _s5(u_tm, m5r, m5i, *s5w, nb, S5_TILE, seq // S5_TILE)
    yh, phg = _hgrn(zf, qig, lb, ng, mhg, seq, 1, seq)
    y_prompt, pconv = _ffn(y5, yh, x_prompt, wglu, bglu, wout, g2, wa, wv, wd, cw, mconv[None], gf,
                           FFN_TILE, seq // FFN_TILE)

    st5 = lambda t: t.reshape(1, -1, S5_GROUPS, S5_STATE)
    pconv = pconv[:, SUBLANES - (CONV_W - 1):, :]
    return (y_prompt, y_sample.reshape(ns, 1, D_MODEL),
            st5(p5r), st5(p5i), phg[None], pconv[None],
            st5(s5r), st5(s5i), shg[None], sconv[None])
```

```python
import functools

import jax
import jax.numpy as jnp
import numpy as np
from jax import lax
from jax.experimental import pallas as pl
from jax.experimental.pallas import tpu as pltpu

F32 = jnp.float32
BF16 = jnp.bfloat16

D_MODEL = 1024
N_META = 16
S5_DIM = 512
S5_GROUP = 16
S5_GROUPS = 32
S5_STATE = 64
S5_LANES = S5_GROUPS * S5_STATE
HG_DIM = 512
HG_HEAD_DIM = 128
HG_HEADS = 4
HG_CHUNK = 64
D_FF = 2816
CONV_W = 3
EPS = 1e-6

S5_HALVES = 2
S5_HALF_CH = S5_DIM // S5_HALVES
S5_HALF_ST = S5_LANES // S5_HALVES
SCAN_LANES = 512
FF_CHUNK = 256
N_FF_CHUNKS = D_FF // FF_CHUNK
SUBLANES = 8
LANES = 128
VMEM_LIMIT = 56 * 1024 * 1024
IN_TILE = 512
IN_SEQS = 2
S5_TILE = 128
FFN_TILE = 1024
STEP_ROWS = 16


def _sigmoid(x):
    return 0.5 + 0.5 * jnp.tanh(0.5 * x)


def _rms_norm(x, g):
    ms = jnp.mean(x * x, axis=-1, keepdims=True)
    return x * lax.rsqrt(ms + EPS) * g


def _dot(a, b):
    return jnp.dot(a, b, preferred_element_type=F32)


def _params(*sem):
    return pltpu.CompilerParams(dimension_semantics=sem, vmem_limit_bytes=VMEM_LIMIT)


def _prep_kernel(lam_ref, dt_ref, bt_ref, hlb_ref, a_ref, bb_ref, lb_ref):
    lam_re = lam_ref[0]
    lam_im = lam_ref[1]
    dt = jnp.exp(dt_ref[...])
    mag = jnp.exp(lam_re * dt)
    ar = mag * jnp.cos(lam_im * dt)
    ai = mag * jnp.sin(lam_im * dt)
    nr = ar - 1.0
    den = lam_re * lam_re + lam_im * lam_im
    cr = (nr * lam_re + ai * lam_im) / den
    ci = (ai * lam_re - nr * lam_im) / den
    a_ref[0] = ar
    a_ref[1] = ai
    bt_re = bt_ref[0]
    bt_im = bt_ref[1]
    bb_ref[0] = cr * bt_re - ci * bt_im
    bb_ref[1] = cr * bt_im + ci * bt_re
    hlb = hlb_ref[...]
    e = jnp.exp(hlb - jnp.max(hlb, axis=0, keepdims=True))
    lb_ref[...] = e[0:1] / jnp.sum(e, axis=0, keepdims=True)


def _prep(lam_re, lam_im, log_dt, b_re, b_im, hlb):
    g, p = lam_re.shape
    sds = jax.ShapeDtypeStruct
    return pl.pallas_call(
        _prep_kernel,
        out_shape=(sds((2, g, 1, p), F32), sds((2, g, S5_GROUP, p), F32), sds((1, HG_DIM), F32)),
        name="param_prep",
    )(jnp.stack([lam_re, lam_im]).reshape(2, g, 1, p),
      jnp.broadcast_to(log_dt.reshape(g, 1, 1), (g, 1, p)),
      jnp.stack([b_re, b_im]).transpose(0, 1, 3, 2), hlb)


def _block_diag(blocks):
    two, g, r, c = blocks.shape
    per = g // S5_HALVES
    eye = np.eye(per, dtype=np.float32)
    out = jnp.einsum("ajgrc,gh->ajgrhc", blocks.reshape(two, S5_HALVES, per, r, c), eye)
    return out.reshape(two, S5_HALVES, per * r, per * c).astype(BF16)


def _inproj_kernel(x_ref, g_ref, w_ref, u_ref, f_ref, qig_ref, *wb_ref, n, nb, tm):
    x = x_ref[...].reshape(nb * tm, D_MODEL)
    hn = (x * g_ref[...]).astype(BF16)
    rn = lax.rsqrt(jnp.mean(x * x, axis=-1, keepdims=True) + EPS)
    rn = jnp.broadcast_to(rn, (nb * tm, HG_DIM))
    if wb_ref:
        wb_ref[0][...] = w_ref[...].astype(BF16)
        w_ref = wb_ref[0]
    col = lambda j: _dot(hn, w_ref[:, j * HG_DIM:(j + 1) * HG_DIM]) * rn
    b0 = pl.program_id(1) * nb
    u = col(0)
    for l in range(S5_DIM // LANES):
        for r in range(nb):
            ul = u[r * tm:(r + 1) * tm, l * LANES:(l + 1) * LANES]
            if n == 1:
                u_ref[l] = ul
            else:
                u_ref[l, pl.ds(b0 + r, tm, stride=n), :] = ul
    f_ref[...] = col(2).reshape(nb, tm, HG_DIM)
    for j, src in enumerate((1, 3, 4)):
        qig_ref[:, :, j * HG_DIM:(j + 1) * HG_DIM] = col(src).astype(BF16).reshape(nb, tm, HG_DIM)


def _inproj(x, g, w, tm, n_tiles, nb=1, emit_bf16=False):
    n, l, d = x.shape
    cols = w.shape[1]
    rows = tm * n_tiles
    const = lambda shape: pl.BlockSpec(shape, lambda i, b: (0,) * len(shape),
                                       pipeline_mode=pl.Buffered(1))
    return pl.pallas_call(
        functools.partial(_inproj_kernel, n=n, nb=nb, tm=tm),
        grid=(n_tiles, n // nb),
        in_specs=[pl.BlockSpec((nb, tm, d), lambda i, b: (b, i, 0)),
                  const((1, d)), const((d, cols))],
        out_specs=[pl.BlockSpec((S5_DIM // LANES, tm * n, LANES), lambda i, b: (0, i, 0)),
                   pl.BlockSpec((nb, tm, HG_DIM), lambda i, b: (b, i, 0)),
                   pl.BlockSpec((nb, tm, 3 * HG_DIM), lambda i, b: (b, i, 0))]
                  + [pl.BlockSpec((d, cols), lambda i, b: (0, 0))] * emit_bf16,
        out_shape=(jax.ShapeDtypeStruct((S5_DIM // LANES, rows * n, LANES), F32),
                   jax.ShapeDtypeStruct((n, rows, HG_DIM), F32),
                   jax.ShapeDtypeStruct((n, rows, 3 * HG_DIM), BF16))
                  + (jax.ShapeDtypeStruct((d, cols), BF16),) * emit_bf16,
        compiler_params=_params("arbitrary", "arbitrary"),
        name="inproj",
    )(x, g, w)


def _gelu_tanh(y):
    return 0.5 * y * (1.0 + jnp.tanh(0.7978845608028654 * (y + 0.044715 * (y * y * y))))


def _s5_kernel(u_ref, h0r_ref, h0i_ref, a_ref, wb_ref, wc_ref, d_ref,
               y_ref, hr_out_ref, hi_out_ref, xr_s, xi_s, hr_s, hi_s, y_s, *, n, tt, interleaved):
    @pl.when(pl.program_id(0) == 0)
    def _():
        hr_s[...] = jnp.broadcast_to(h0r_ref[...], hr_s.shape)
        hi_s[...] = jnp.broadcast_to(h0i_ref[...], hi_s.shape)

    slabs = S5_HALF_CH // LANES
    load_u = lambda j: jnp.concatenate([u_ref[j * slabs + l] for l in range(slabs)], axis=1)
    for j in range(S5_HALVES):
        st = slice(j * S5_HALF_ST, (j + 1) * S5_HALF_ST)
        ub = load_u(j).astype(BF16)
        xr_s[:, st] = _dot(ub, wb_ref[0, j])
        xi_s[:, st] = _dot(ub, wb_ref[1, j])
    for c in range(S5_LANES // SCAN_LANES):
        loc = slice(c * SCAN_LANES, (c + 1) * SCAN_LANES)
        ar = jnp.broadcast_to(a_ref[0, :, loc], (n, SCAN_LANES))
        ai = jnp.broadcast_to(a_ref[1, :, loc], (n, SCAN_LANES))

        def step(t, carry, ar=ar, ai=ai, loc=loc):
            hr, hi = carry
            r = 0 if tt == 1 else pl.multiple_of(t * n, n)
            nhr = ar * hr - ai * hi + xr_s[pl.ds(r, n), loc]
            nhi = ar * hi + ai * hr + xi_s[pl.ds(r, n), loc]
            xr_s[pl.ds(r, n), loc] = nhr
            xi_s[pl.ds(r, n), loc] = nhi
            return nhr, nhi

        carry = (hr_s[:, loc], hi_s[:, loc])
        if tt == 1:
            carry = step(0, carry)
        else:
            carry = lax.fori_loop(0, tt, step, carry, unroll=True)
        hr_s[:, loc] = carry[0]
        hi_s[:, loc] = carry[1]
    for j in range(S5_HALVES):
        ch = slice(j * S5_HALF_CH, (j + 1) * S5_HALF_CH)
        st = slice(j * S5_HALF_ST, (j + 1) * S5_HALF_ST)
        y = (_dot(xr_s[:, st].astype(BF16), wc_ref[0, j]) + _dot(xi_s[:, st].astype(BF16), wc_ref[1, j])
             + d_ref[:, ch] * load_u(j))
        if interleaved:
            for l in range(slabs):
                y_s[j * slabs + l] = y[:, l * LANES:(l + 1) * LANES]
        else:
            y_ref[0, :, ch] = y
    if interleaved:
        for b in range(n):
            for l in range(S5_DIM // LANES):
                y_ref[b, :, l * LANES:(l + 1) * LANES] = y_s[l, pl.ds(b, tt, stride=n), :]
    hr_out_ref[...] = hr_s[...]
    hi_out_ref[...] = hi_s[...]


def _s5(u4, h0r, h0i, a, wb, wc, d, n, tt, n_tiles, tile0=0):
    rows = tt * n
    interleaved = n > 1 and tt > 1
    const = lambda shape: pl.BlockSpec(shape, lambda i: (0,) * len(shape),
                                       pipeline_mode=pl.Buffered(1))
    state = pl.BlockSpec((n, S5_LANES), lambda i: (0, 0))
    y_block = (n, tt, S5_DIM) if interleaved else (1, rows, S5_DIM)
    y_shape = (n, tt * n_tiles, S5_DIM) if interleaved else (1, rows * n_tiles, S5_DIM)
    return pl.pallas_call(
        functools.partial(_s5_kernel, n=n, tt=tt, interleaved=interleaved),
        grid=(n_tiles,),
        in_specs=[pl.BlockSpec((S5_DIM // LANES, rows, LANES), lambda i: (0, tile0 + i, 0)),
                  const(h0r.shape), const(h0i.shape), const(a.shape), const(wb.shape), const(wc.shape),
                  const((1, S5_DIM))],
        out_specs=[pl.BlockSpec(y_block, lambda i: (0, i, 0)), state, state],
        out_shape=(jax.ShapeDtypeStruct(y_shape, F32),
                   jax.ShapeDtypeStruct((n, S5_LANES), F32),
                   jax.ShapeDtypeStruct((n, S5_LANES), F32)),
        scratch_shapes=[pltpu.VMEM((rows, S5_LANES), F32), pltpu.VMEM((rows, S5_LANES), F32),
                        pltpu.VMEM((n, S5_LANES), F32), pltpu.VMEM((n, S5_LANES), F32),
                        pltpu.VMEM((S5_DIM // LANES, rows if interleaved else SUBLANES, LANES), F32)],
        compiler_params=_params("arbitrary"),
        name="s5_scan",
    )(u4, h0r, h0i, a, wb, wc, d)


def _hgrn_gate_out(o, g, ng):
    parts = []
    for h in range(HG_HEADS):
        oh = o[:, h * HG_HEAD_DIM:(h + 1) * HG_HEAD_DIM]
        ms = jnp.mean(oh * oh, axis=-1, keepdims=True)
        parts.append(oh * lax.rsqrt(ms + EPS))
    return jnp.concatenate(parts, axis=-1) * ng * (g * _sigmoid(g))


def _hgrn_kernel(q_ref, f_ref, i_ref, g_ref, lb_ref, ng_ref, tri_ref, s0_ref,
                 y_ref, s_out_ref, st_s, o_s, *, th, valid):
    i = pl.program_id(1)
    c = HG_CHUNK

    @pl.when(i == 0)
    def _():
        for h in range(HG_HEADS):
            st_s[h] = s0_ref[0, h].T

    lb = lb_ref[...]
    f = lb + (1.0 - lb) * _sigmoid(f_ref[0])
    lc = jnp.log(f)
    k = 1.0 - f
    q = q_ref[0]
    if valid < th:
        live = lax.broadcasted_iota(jnp.int32, (th, 1), 0) < valid
        lc = jnp.where(live, lc, 0.0)
        k = jnp.where(live, k, 0.0)
        q = jnp.where(live, q.astype(F32), 0.0).astype(BF16)
    tri = tri_ref[...]
    tr = tri.shape[0]
    lc_hi = lc.astype(BF16)
    lc_lo = (lc - lc_hi.astype(F32)).astype(BF16)
    b = jnp.concatenate([_dot(tri, lc_hi[r:r + tr]) + _dot(tri, lc_lo[r:r + tr])
                         for r in range(0, th, tr)], axis=0)
    qd = q * jnp.exp(b).astype(BF16)
    kd = k * jnp.exp(-b)
    kdb = kd.astype(BF16)
    vb = i_ref[0]
    causal = (lax.broadcasted_iota(jnp.int32, (c, c), 1) <= lax.broadcasted_iota(jnp.int32, (c, c), 0))
    nt = (((1,), (1,)), ((), ()))
    n_chunks = th // c
    rows = [slice(cc * c, (cc + 1) * c) for cc in range(n_chunks)]
    lanes = [slice(h * HG_HEAD_DIM, (h + 1) * HG_HEAD_DIM) for h in range(HG_HEADS)]
    dec = [jnp.exp(b[cc * c + c - 1:cc * c + c, :]) for cc in range(n_chunks)]
    kdec = [(kd[rows[cc]] * dec[cc]).astype(BF16) for cc in range(n_chunks)]
    att = [[jnp.where(causal, lax.dot_general(qd[rows[cc], ls], kdb[rows[cc], ls], nt,
                                              preferred_element_type=F32), 0.0).astype(BF16)
            for ls in lanes] for cc in range(n_chunks)]
    upd = [[_dot(vb[rows[cc], ls].T, kdec[cc][:, ls])
            for ls in lanes] for cc in range(n_chunks)]
    st_in = [[None] * HG_HEADS for _ in range(n_chunks)]
    for h, ls in enumerate(lanes):
        st = st_s[h]
        for cc in range(n_chunks):
            st_in[cc][h] = st.astype(BF16)
            st = dec[cc][:, ls] * st + upd[cc][h]
        st_s[h] = st
    for cc in range(n_chunks):
        for h, ls in enumerate(lanes):
            o_s[rows[cc], ls] = (
                lax.dot_general(qd[rows[cc], ls], st_in[cc][h], nt, preferred_element_type=F32)
                + _dot(att[cc][h], vb[rows[cc], ls]))
    y_ref[0] = _hgrn_gate_out(o_s[...], g_ref[0].astype(F32), ng_ref[...])

    @pl.when(i == pl.num_programs(1) - 1)
    def _():
        for h in range(HG_HEADS):
            s_out_ref[0, h] = st_s[h].T


def _hgrn_tri(tr):
    idx = np.arange(tr)
    tri = (idx[:, None] // HG_CHUNK == idx[None, :] // HG_CHUNK) & (idx[None, :] <= idx[:, None])
    return jnp.asarray(tri, dtype=BF16)


def _hgrn(zf, qig, lb, ng, s0, th, n_tiles, valid, tile0=0):
    n = zf.shape[0]
    own = s0.shape[0] == n
    tr = min(th, 256)
    col = lambda j: pl.BlockSpec((1, th, HG_DIM), lambda b, i, j=j: (b, tile0 + i, j))
    const = lambda shape: pl.BlockSpec(shape, lambda b, i: (0,) * len(shape),
                                       pipeline_mode=pl.Buffered(1))
    st = pl.BlockSpec((1, HG_HEADS, HG_HEAD_DIM, HG_HEAD_DIM), lambda b, i: (b, 0, 0, 0))
    st_in = pl.BlockSpec((1, HG_HEADS, HG_HEAD_DIM, HG_HEAD_DIM),
                         lambda b, i: (b if own else 0, 0, 0, 0))
    return pl.pallas_call(
        functools.partial(_hgrn_kernel, th=th, valid=valid),
        grid=(n, n_tiles),
        in_specs=[col(0), col(0), col(1), col(2), const((1, HG_DIM)), const((1, HG_DIM)),
                  const((tr, tr)), st_in],
        out_specs=[pl.BlockSpec((1, th, HG_DIM), lambda b, i: (b, i, 0)), st],
        out_shape=(jax.ShapeDtypeStruct((n, n_tiles * th, HG_DIM), F32),
                   jax.ShapeDtypeStruct((n, HG_HEADS, HG_HEAD_DIM, HG_HEAD_DIM), F32)),
        scratch_shapes=[pltpu.VMEM((HG_HEADS, HG_HEAD_DIM, HG_HEAD_DIM), F32),
                        pltpu.VMEM((th, HG_DIM), F32)],
        compiler_params=_params("arbitrary", "arbitrary"),
        name="hgrn_chunks",
    )(qig, zf, qig, qig, lb, ng, _hgrn_tri(tr), s0)


def _hgrn_step_kernel(q_ref, f_ref, i_ref, g_ref, lb_ref, ng_ref, s0_ref,
                      y_ref, s_out_ref, o_s, *, sb):
    lb = lb_ref[...]
    f = lb + (1.0 - lb) * _sigmoid(f_ref[0])
    q = q_ref[0].astype(F32)
    v = i_ref[0].astype(F32)
    pad = jnp.zeros((HG_HEAD_DIM - sb, HG_HEAD_DIM), F32)
    sq = (HG_HEAD_DIM, HG_HEAD_DIM)
    for h in range(HG_HEADS):
        ls = slice(h * HG_HEAD_DIM, (h + 1) * HG_HEAD_DIM)
        fcols = jnp.concatenate([f[:, ls], pad], axis=0).T
        for s in range(sb):
            fc = jnp.broadcast_to(fcols[:, s:s + 1], sq)
            sn = fc * s0_ref[s, h] + (1.0 - fc) * v[s:s + 1, ls]
            s_out_ref[s, h] = sn
            o_s[s:s + 1, ls] = _dot(q[s:s + 1, ls].astype(BF16), sn.astype(BF16))
    y_ref[0] = _hgrn_gate_out(o_s[...], g_ref[0].astype(F32), ng_ref[...])


def _hgrn_step(zf, qig, lb, ng, s0, sb):
    r = s0.shape[0]
    col = lambda j: pl.BlockSpec((1, sb, HG_DIM), lambda i, j=j: (0, i, j))
    vec = pl.BlockSpec((1, HG_DIM), lambda i: (0, 0))
    st = pl.BlockSpec((sb, HG_HEADS, HG_HEAD_DIM, HG_HEAD_DIM), lambda i: (i, 0, 0, 0))
    return pl.pallas_call(
        functools.partial(_hgrn_step_kernel, sb=sb),
        grid=(r // sb,),
        in_specs=[col(0), col(0), col(1), col(2), vec, vec, st],
        out_specs=[pl.BlockSpec((1, sb, HG_DIM), lambda i: (0, i, 0)), st],
        out_shape=(jax.ShapeDtypeStruct((1, r, HG_DIM), F32),
                   jax.ShapeDtypeStruct(s0.shape, F32)),
        scratch_shapes=[pltpu.VMEM((sb, HG_DIM), F32)],
        compiler_params=_params("arbitrary"),
        name="hgrn_step",
    )(qig, zf, qig, qig, lb, ng, s0)


def _mix_out(y5_raw, yh, h, wglu, bglu, wout):
    y5p = _gelu_tanh(y5_raw)
    y5 = y5p * _sigmoid(_dot(y5p.astype(BF16), wglu) + bglu)
    ymix = jnp.concatenate([y5, yh], axis=-1).astype(BF16)
    return h + _dot(ymix, wout)


def _conv_taps(a, prev8, cw):
    rows = a.shape[0]
    cv = cw[3:4] + pltpu.roll(a, 2, 0) * cw[0:1] + pltpu.roll(a, 1, 0) * cw[1:2] + a * cw[2:3]
    d = prev8 - a[rows - SUBLANES:rows]
    rid = lax.broadcasted_iota(jnp.int32, (SUBLANES, 1), 0)
    fix = (jnp.where(rid < 1, pltpu.roll(d, 1, 0), 0.0) * cw[1:2]
           + jnp.where(rid < 2, pltpu.roll(d, 2, 0), 0.0) * cw[0:1])
    return jnp.concatenate([cv[:SUBLANES] + fix, cv[SUBLANES:]], axis=0)


def _ffn_kernel(y5_ref, yh_ref, h_ref, wglu_ref, bglu_ref, wout_ref, g2_ref,
                wa_ref, wv_ref, wd_ref, cw_ref, cin_ref, gf_ref,
                out_ref, cout_ref, carry_s, hn_s, h1_s, rn_s, s_s, *, tm):
    @pl.when(pl.program_id(1) == 0)
    def _():
        carry_s[...] = cin_ref[0]

    h1 = _mix_out(y5_ref[0], yh_ref[0], h_ref[0], wglu_ref[...], bglu_ref[...], wout_ref[...])
    h1_s[...] = h1
    hn_s[...] = (h1 * g2_ref[...]).astype(BF16)
    rn = lax.rsqrt(jnp.mean(h1 * h1, axis=-1, keepdims=True) + EPS)
    rn_s[...] = jnp.broadcast_to(rn, (tm, FF_CHUNK))

    for j in range(N_FF_CHUNKS):
        cs = slice(j * FF_CHUNK, (j + 1) * FF_CHUNK)
        hn = hn_s[...]
        a = _dot(hn, wa_ref[:, cs]) * rn_s[...]
        v = _dot(hn, wv_ref[:, cs]) * rn_s[...]
        cv = _conv_taps(a, carry_s[:, cs], cw_ref[:, cs])
        carry_s[:, cs] = a[tm - SUBLANES:tm]
        s_s[:, cs] = ((cv * _sigmoid(cv)) * v).astype(BF16)

    out_ref[0] = _rms_norm(h1_s[...] + _dot(s_s[...], wd_ref[...]), gf_ref[...])
    cout_ref[0] = carry_s[...]


def _ffn_small_kernel(y5m_ref, y5s_ref, yhm_ref, yhs_ref, hm_ref, hs_ref,
                      wglu_ref, bglu_ref, wout_ref, g2_ref, wa_ref, wv_ref, wd_ref, cw_ref,
                      cin_a_ref, cin_b_ref, gf_ref,
                      out_ref, a_out_ref, mcarry_ref,
                      wglu_bf_ref, wout_bf_ref, wa_bf_ref, wv_bf_ref, wd_bf_ref,
                      hn_s, h1_s, acc_s, *, n_meta):
    j = pl.program_id(0)

    @pl.when(j == 0)
    def _():
        wglu = wglu_ref[...].astype(BF16)
        wout = wout_ref[...].astype(BF16)
        wglu_bf_ref[...] = wglu
        wout_bf_ref[...] = wout
        rows = lambda m_ref, s_ref: jnp.concatenate([m_ref[0], s_ref[0]], axis=0)
        h1 = _mix_out(rows(y5m_ref, y5s_ref), rows(yhm_ref, yhs_ref), rows(hm_ref, hs_ref),
                      wglu, bglu_ref[...], wout)
        h1_s[...] = h1
        hn_s[...] = _rms_norm(h1, g2_ref[...]).astype(BF16)
        acc_s[...] = jnp.zeros_like(acc_s)

    wa = wa_ref[...].astype(BF16)
    wv = wv_ref[...].astype(BF16)
    wd = wd_ref[...].astype(BF16)
    wa_bf_ref[...] = wa
    wv_bf_ref[...] = wv
    wd_bf_ref[...] = wd
    hn = hn_s[...]
    a = _dot(hn, wa)
    v = _dot(hn, wv)
    cw = cw_ref[...]
    a_m, a_s = a[:n_meta], a[n_meta:]
    cv_m = _conv_taps(a_m, jnp.zeros((SUBLANES, FF_CHUNK), F32), cw)
    cv_s = cw[3:4] + cin_a_ref[...] * cw[0:1] + cin_b_ref[...] * cw[1:2] + a_s * cw[2:3]
    mcarry_ref[...] = a_m[n_meta - SUBLANES:]
    a_out_ref[...] = a_s
    cv = jnp.concatenate([cv_m, cv_s], axis=0)
    acc_s[...] += _dot(((cv * _sigmoid(cv)) * v).astype(BF16), wd)

    @pl.when(j == pl.num_programs(0) - 1)
    def _():
        out_ref[...] = _rms_norm((h1_s[...] + acc_s[...])[n_meta:], gf_ref[...])


def _ffn_small(y5m, y5s, yhm, yhs, hm, hs, wglu, bglu, wout, g2, wup, wd, cw, cin, gf, hm_tile=0):
    n_meta, r = y5m.shape[1], y5s.shape[1]
    const = lambda shape: pl.BlockSpec(shape, lambda j: (0,) * len(shape))
    head = lambda rows, width: pl.BlockSpec((1, rows, width), lambda j: (0, 0, 0))
    ff_cols = lambda rows: pl.BlockSpec((rows, FF_CHUNK), lambda j: (0, j))
    sds = jax.ShapeDtypeStruct
    return pl.pallas_call(
        functools.partial(_ffn_small_kernel, n_meta=n_meta),
        grid=(N_FF_CHUNKS,),
        in_specs=[head(n_meta, S5_DIM), head(r, S5_DIM), head(n_meta, HG_DIM), head(r, HG_DIM),
                  pl.BlockSpec((1, n_meta, D_MODEL), lambda j: (0, hm_tile, 0)), head(r, D_MODEL),
                  const(wglu.shape), const(bglu.shape), const(wout.shape), const(g2.shape),
                  ff_cols(D_MODEL),
                  pl.BlockSpec((D_MODEL, FF_CHUNK), lambda j: (0, N_FF_CHUNKS + j)),
                  pl.BlockSpec((FF_CHUNK, D_MODEL), lambda j: (j, 0)),
                  ff_cols(SUBLANES), ff_cols(r),
                  pl.BlockSpec((r, FF_CHUNK), lambda j: (0, N_FF_CHUNKS + j)), const(gf.shape)],
        out_specs=[const((r, D_MODEL)), ff_cols(r), ff_cols(SUBLANES),
                   const(wglu.shape), const(wout.shape), ff_cols(D_MODEL), ff_cols(D_MODEL),
                   pl.BlockSpec((FF_CHUNK, D_MODEL), lambda j: (j, 0))],
        out_shape=(sds((r, D_MODEL), F32), sds((r, D_FF), F32), sds((SUBLANES, D_FF), F32),
                   sds(wglu.shape, BF16), sds(wout.shape, BF16),
                   sds((D_MODEL, D_FF), BF16), sds((D_MODEL, D_FF), BF16), sds(wd.shape, BF16)),
        scratch_shapes=[pltpu.VMEM((n_meta + r, D_MODEL), BF16),
                        pltpu.VMEM((n_meta + r, D_MODEL), F32),
                        pltpu.VMEM((n_meta + r, D_MODEL), F32)],
        compiler_params=_params("arbitrary"),
        name="ffn_small",
    )(y5m, y5s, yhm, yhs, hm, hs, wglu, bglu, wout, g2, wup, wup, wd, cw, cin, cin, gf)


def _ffn(y5, yh, h, wglu, bglu, wout, g2, wa, wv, wd, cw, cin, gf, tm, n_tiles):
    n = h.shape[0]
    const = lambda shape: pl.BlockSpec(shape, lambda b, i: (0,) * len(shape),
                                       pipeline_mode=pl.Buffered(1))
    own = cin.shape[0] == n
    conv = lambda index: pl.BlockSpec((1, SUBLANES, D_FF), index)
    return pl.pallas_call(
        functools.partial(_ffn_kernel, tm=tm),
        grid=(n, n_tiles),
        in_specs=[pl.BlockSpec((1, tm, S5_DIM), lambda b, i: (b, i, 0)),
                  pl.BlockSpec((1, tm, HG_DIM), lambda b, i: (b, i, 0)),
                  pl.BlockSpec((1, tm, D_MODEL), lambda b, i: (b, i, 0)),
                  const(wglu.shape), const(bglu.shape), const(wout.shape), const(g2.shape),
                  const(wa.shape), const(wv.shape), const(wd.shape), const(cw.shape),
                  conv(lambda b, i: (b if own else 0, 0, 0)), const(gf.shape)],
        out_specs=[pl.BlockSpec((1, tm, D_MODEL), lambda b, i: (b, i, 0)),
                   conv(lambda b, i: (b, 0, 0))],
        out_shape=(jax.ShapeDtypeStruct((n, n_tiles * tm, D_MODEL), F32),
                   jax.ShapeDtypeStruct((n, SUBLANES, D_FF), F32)),
        scratch_shapes=[pltpu.VMEM((SUBLANES, D_FF), F32),
                        pltpu.VMEM((tm, D_MODEL), BF16),
                        pltpu.VMEM((tm, D_MODEL), F32),
                        pltpu.VMEM((tm, FF_CHUNK), F32),
                        pltpu.VMEM((tm, D_FF), BF16)],
        compiler_params=_params("arbitrary", "arbitrary"),
        name="mix_out_ffn",
    )(y5, yh, h, wglu, bglu, wout, g2, wa, wv, wd, cw, cin, gf)


def kernel(x_prompt, x_sample, state_s5_re, state_s5_im, state_hgrn, state_ffn_conv, meta_tokens, norm_mix_g, w_in, s5_lambda_re, s5_lambda_im, s5_log_dt, s5_b_re, s5_b_im, s5_c_re, s5_c_im, s5_d, s5_w_glu, s5_b_glu, hg_lower_bounds, hg_norm_g, w_out, norm_ffn_g, ffn_w_up, ffn_conv_w, ffn_conv_b, ffn_w_down, final_norm_g):
    nb, seq, _ = x_prompt.shape
    ns = x_sample.shape[0]
    li = 0

    a5, bb, lb = _prep(s5_lambda_re[li], s5_lambda_im[li], s5_log_dt[li],
                       s5_b_re[li], s5_b_im[li], hg_lower_bounds)
    a5 = a5.reshape(2, 1, S5_LANES)
    wb = _block_diag(bb)
    wc = _block_diag(jnp.stack([s5_c_re[li], -s5_c_im[li]]).transpose(0, 1, 3, 2))
    d5 = s5_d[li].reshape(1, S5_DIM)
    g1 = norm_mix_g[li].reshape(1, D_MODEL)
    g2 = norm_ffn_g[li].reshape(1, D_MODEL)
    gf = final_norm_g.reshape(1, D_MODEL)
    ng = hg_norm_g[li].reshape(1, HG_DIM)
    bglu = s5_b_glu[li].reshape(1, S5_DIM)
    cw = jnp.concatenate([ffn_conv_w[li], ffn_conv_b[li][None],
                          jnp.zeros((SUBLANES - CONV_W - 1, D_FF), F32)], axis=0)
    s5w = (a5, wb, wc, d5)

    xs = jnp.concatenate([x_sample.reshape(ns, D_MODEL), meta_tokens,
                          jnp.zeros((HG_CHUNK - N_META, D_MODEL), F32)])[None]
    u_tm, zf, qig, w_in_b = _inproj(xs, g1, w_in[li], ns + HG_CHUNK, 1, emit_bf16=True)

    y5s, s5r, s5i = _s5(u_tm, state_s5_re[li].reshape(ns, S5_LANES), state_s5_im[li].reshape(ns, S5_LANES),
                        *s5w, ns, 1, 1)
    yhs, shg = _hgrn_step(zf, qig, lb, ng, state_hgrn[li], STEP_ROWS)

    z5 = jnp.zeros((1, S5_LANES), F32)
    zh = jnp.zeros((1, HG_HEADS, HG_HEAD_DIM, HG_HEAD_DIM), F32)
    y5m, m5r, m5i = _s5(u_tm, z5, z5, *s5w, 1, N_META, 1, tile0=ns // N_META)
    yhm, mhg = _hgrn(zf, qig, lb, ng, zh, HG_CHUNK, 1, N_META, tile0=ns // HG_CHUNK)

    buf = state_ffn_conv[li].reshape(ns, (CONV_W - 1) * D_FF)
    y_sample, a_new, mconv, wglu, wout, wa, wv, wd = _ffn_small(
        y5m, y5s, yhm, yhs, xs, xs, s5_w_glu[li], bglu, w_out[li], g2,
        ffn_w_up[li], ffn_w_down[li], cw, buf, gf, hm_tile=ns // N_META)
    sconv = jnp.stack([buf[:, D_FF:], a_new], axis=1)

    u_tm, zf, qig = _inproj(x_prompt, g1, w_in_b, IN_TILE, seq // IN_TILE, nb=IN_SEQS)
    y5, p5r, p5i = _s5(u_tm, m5r, m5i, *s5w, nb, S5_TILE, seq // S5_TILE)
    yh, phg = _hgrn(zf, qig, lb, ng, mhg, seq, 1, seq)
    y_prompt, pconv = _ffn(y5, yh, x_prompt, wglu, bglu, wout, g2, wa, wv, wd, cw, mconv[None], gf,
                           FFN_TILE, seq // FFN_TILE)

    st5 = lambda t: t.reshape(1, -1, S5_GROUPS, S5_STATE)
    pconv = pconv[:, SUBLANES - (CONV_W - 1):, :]
    return (y_prompt, y_sample.reshape(ns, 1, D_MODEL),
            st5(p5r), st5(p5i), phg[None], pconv[None],
            st5(s5r), st5(s5i), shg[None], sconv[None])
```

```python
import functools

import jax
import jax.numpy as jnp
import numpy as np
from jax import lax
from jax.experimental import pallas as pl
from jax.experimental.pallas import tpu as pltpu

F32 = jnp.float32
BF16 = jnp.bfloat16

D_MODEL = 1024
N_META = 16
S5_DIM = 512
S5_GROUP = 16
S5_GROUPS = 32
S5_STATE = 64
S5_LANES = S5_GROUPS * S5_STATE
HG_DIM = 512
HG_HEAD_DIM = 128
HG_HEADS = 4
HG_CHUNK = 64
D_FF = 2816
CONV_W = 3
EPS = 1e-6

S5_HALVES = 2
S5_HALF_CH = S5_DIM // S5_HALVES
S5_HALF_ST = S5_LANES // S5_HALVES
SCAN_LANES = 512
FF_CHUNK = 256
N_FF_CHUNKS = D_FF // FF_CHUNK
SUBLANES = 8
LANES = 128
VMEM_LIMIT = 56 * 1024 * 1024
IN_TILE = 512
IN_SEQS = 2
S5_TILE = 128
FFN_TILE = 1024
STEP_ROWS = 16


def _sigmoid(x):
    return 0.5 + 0.5 * jnp.tanh(0.5 * x)


def _rms_norm(x, g):
    ms = jnp.mean(x * x, axis=-1, keepdims=True)
    return x * lax.rsqrt(ms + EPS) * g


def _dot(a, b):
    return jnp.dot(a, b, preferred_element_type=F32)


def _params(*sem):
    return pltpu.CompilerParams(dimension_semantics=sem, vmem_limit_bytes=VMEM_LIMIT)


def _prep_kernel(lam_ref, dt_ref, bt_ref, hlb_ref, a_ref, bb_ref, lb_ref):
    lam_re = lam_ref[0]
    lam_im = lam_ref[1]
    dt = jnp.exp(dt_ref[...])
    mag = jnp.exp(lam_re * dt)
    ar = mag * jnp.cos(lam_im * dt)
    ai = mag * jnp.sin(lam_im * dt)
    nr = ar - 1.0
    den = lam_re * lam_re + lam_im * lam_im
    cr = (nr * lam_re + ai * lam_im) / den
    ci = (ai * lam_re - nr * lam_im) / den
    a_ref[0] = ar
    a_ref[1] = ai
    bt_re = bt_ref[0]
    bt_im = bt_ref[1]
    bb_ref[0] = cr * bt_re - ci * bt_im
    bb_ref[1] = cr * bt_im + ci * bt_re
    hlb = hlb_ref[...]
    e = jnp.exp(hlb - jnp.max(hlb, axis=0, keepdims=True))
    lb_ref[...] = e[0:1] / jnp.sum(e, axis=0, keepdims=True)


def _prep(lam_re, lam_im, log_dt, b_re, b_im, hlb):
    g, p = lam_re.shape
    sds = jax.ShapeDtypeStruct
    return pl.pallas_call(
        _prep_kernel,
        out_shape=(sds((2, g, 1, p), F32), sds((2, g, S5_GROUP, p), F32), sds((1, HG_DIM), F32)),
        name="param_prep",
    )(jnp.stack([lam_re, lam_im]).reshape(2, g, 1, p),
      jnp.broadcast_to(log_dt.reshape(g, 1, 1), (g, 1, p)),
      jnp.stack([b_re, b_im]).transpose(0, 1, 3, 2), hlb)


def _block_diag(blocks):
    two, g, r, c = blocks.shape
    per = g // S5_HALVES
    rows = blocks.reshape(two, S5_HALVES, per * r, 1, c)
    tiled = jnp.broadcast_to(rows, (two, S5_HALVES, per * r, per, c)).reshape(two, S5_HALVES, per * r, per * c)
    on_diag = (np.arange(per * r)[:, None] // r) == (np.arange(per * c)[None, :] // c)
    return jnp.where(on_diag, tiled, 0.0).astype(BF16)


def _inproj_kernel(x_ref, g_ref, w_ref, u_ref, f_ref, qig_ref, *wb_ref, n, nb, tm):
    x = x_ref[...].reshape(nb * tm, D_MODEL)
    hn = (x * g_ref[...]).astype(BF16)
    rn = lax.rsqrt(jnp.mean(x * x, axis=-1, keepdims=True) + EPS)
    rn = jnp.broadcast_to(rn, (nb * tm, HG_DIM))
    if wb_ref:
        wb_ref[0][...] = w_ref[...].astype(BF16)
        w_ref = wb_ref[0]
    col = lambda j: _dot(hn, w_ref[:, j * HG_DIM:(j + 1) * HG_DIM]) * rn
    b0 = pl.program_id(1) * nb
    u = col(0)
    for l in range(S5_DIM // LANES):
        for r in range(nb):
            ul = u[r * tm:(r + 1) * tm, l * LANES:(l + 1) * LANES]
            if n == 1:
                u_ref[l] = ul
            else:
                u_ref[l, pl.ds(b0 + r, tm, stride=n), :] = ul
    f_ref[...] = col(2).reshape(nb, tm, HG_DIM)
    for j, src in enumerate((1, 3, 4)):
        qig_ref[:, :, j * HG_DIM:(j + 1) * HG_DIM] = col(src).astype(BF16).reshape(nb, tm, HG_DIM)


def _inproj(x, g, w, tm, n_tiles, nb=1, emit_bf16=False):
    n, l, d = x.shape
    cols = w.shape[1]
    rows = tm * n_tiles
    const = lambda shape: pl.BlockSpec(shape, lambda i, b: (0,) * len(shape),
                                       pipeline_mode=pl.Buffered(1))
    return pl.pallas_call(
        functools.partial(_inproj_kernel, n=n, nb=nb, tm=tm),
        grid=(n_tiles, n // nb),
        in_specs=[pl.BlockSpec((nb, tm, d), lambda i, b: (b, i, 0)),
                  const((1, d)), const((d, cols))],
        out_specs=[pl.BlockSpec((S5_DIM // LANES, tm * n, LANES), lambda i, b: (0, i, 0)),
                   pl.BlockSpec((nb, tm, HG_DIM), lambda i, b: (b, i, 0)),
                   pl.BlockSpec((nb, tm, 3 * HG_DIM), lambda i, b: (b, i, 0))]
                  + [pl.BlockSpec((d, cols), lambda i, b: (0, 0))] * emit_bf16,
        out_shape=(jax.ShapeDtypeStruct((S5_DIM // LANES, rows * n, LANES), F32),
                   jax.ShapeDtypeStruct((n, rows, HG_DIM), F32),
                   jax.ShapeDtypeStruct((n, rows, 3 * HG_DIM), BF16))
                  + (jax.ShapeDtypeStruct((d, cols), BF16),) * emit_bf16,
        compiler_params=_params("arbitrary", "arbitrary"),
        name="inproj",
    )(x, g, w)


def _gelu_tanh(y):
    return 0.5 * y * (1.0 + jnp.tanh(0.7978845608028654 * (y + 0.044715 * (y * y * y))))


def _s5_kernel(u_ref, h0r_ref, h0i_ref, a_ref, wb_ref, wc_ref, d_ref,
               y_ref, hr_out_ref, hi_out_ref, xr_s, xi_s, hr_s, hi_s, y_s, *, n, tt, interleaved):
    @pl.when(pl.program_id(0) == 0)
    def _():
        hr_s[...] = jnp.broadcast_to(h0r_ref[...], hr_s.shape)
        hi_s[...] = jnp.broadcast_to(h0i_ref[...], hi_s.shape)

    slabs = S5_HALF_CH // LANES
    load_u = lambda j: jnp.concatenate([u_ref[j * slabs + l] for l in range(slabs)], axis=1)
    for j in range(S5_HALVES):
        st = slice(j * S5_HALF_ST, (j + 1) * S5_HALF_ST)
        ub = load_u(j).astype(BF16)
        xr_s[:, st] = _dot(ub, wb_ref[0, j])
        xi_s[:, st] = _dot(ub, wb_ref[1, j])
    for c in range(S5_LANES // SCAN_LANES):
        loc = slice(c * SCAN_LANES, (c + 1) * SCAN_LANES)
        ar = jnp.broadcast_to(a_ref[0, :, loc], (n, SCAN_LANES))
        ai = jnp.broadcast_to(a_ref[1, :, loc], (n, SCAN_LANES))

        def step(t, carry, ar=ar, ai=ai, loc=loc):
            hr, hi = carry
            r = 0 if tt == 1 else pl.multiple_of(t * n, n)
            nhr = ar * hr - ai * hi + xr_s[pl.ds(r, n), loc]
            nhi = ar * hi + ai * hr + xi_s[pl.ds(r, n), loc]
            xr_s[pl.ds(r, n), loc] = nhr
            xi_s[pl.ds(r, n), loc] = nhi
            return nhr, nhi

        carry = (hr_s[:, loc], hi_s[:, loc])
        if tt == 1:
            carry = step(0, carry)
        else:
            carry = lax.fori_loop(0, tt, step, carry, unroll=True)
        hr_s[:, loc] = carry[0]
        hi_s[:, loc] = carry[1]
    for j in range(S5_HALVES):
        ch = slice(j * S5_HALF_CH, (j + 1) * S5_HALF_CH)
        st = slice(j * S5_HALF_ST, (j + 1) * S5_HALF_ST)
        y = (_dot(xr_s[:, st].astype(BF16), wc_ref[0, j]) + _dot(xi_s[:, st].astype(BF16), wc_ref[1, j])
             + d_ref[:, ch] * load_u(j))
        if interleaved:
            for l in range(slabs):
                y_s[j * slabs + l] = y[:, l * LANES:(l + 1) * LANES]
        else:
            y_ref[0, :, ch] = y
    if interleaved:
        for b in range(n):
            for l in range(S5_DIM // LANES):
                y_ref[b, :, l * LANES:(l + 1) * LANES] = y_s[l, pl.ds(b, tt, stride=n), :]
    hr_out_ref[...] = hr_s[...]
    hi_out_ref[...] = hi_s[...]


def _s5(u4, h0r, h0i, a, wb, wc, d, n, tt, n_tiles, tile0=0):
    rows = tt * n
    interleaved = n > 1 and tt > 1
    const = lambda shape: pl.BlockSpec(shape, lambda i: (0,) * len(shape),
                                       pipeline_mode=pl.Buffered(1))
    state = pl.BlockSpec((n, S5_LANES), lambda i: (0, 0))
    y_block = (n, tt, S5_DIM) if interleaved else (1, rows, S5_DIM)
    y_shape = (n, tt * n_tiles, S5_DIM) if interleaved else (1, rows * n_tiles, S5_DIM)
    return pl.pallas_call(
        functools.partial(_s5_kernel, n=n, tt=tt, interleaved=interleaved),
        grid=(n_tiles,),
        in_specs=[pl.BlockSpec((S5_DIM // LANES, rows, LANES), lambda i: (0, tile0 + i, 0)),
                  const(h0r.shape), const(h0i.shape), const(a.shape), const(wb.shape), const(wc.shape),
                  const((1, S5_DIM))],
        out_specs=[pl.BlockSpec(y_block, lambda i: (0, i, 0)), state, state],
        out_shape=(jax.ShapeDtypeStruct(y_shape, F32),
                   jax.ShapeDtypeStruct((n, S5_LANES), F32),
                   jax.ShapeDtypeStruct((n, S5_LANES), F32)),
        scratch_shapes=[pltpu.VMEM((rows, S5_LANES), F32), pltpu.VMEM((rows, S5_LANES), F32),
                        pltpu.VMEM((n, S5_LANES), F32), pltpu.VMEM((n, S5_LANES), F32),
                        pltpu.VMEM((S5_DIM // LANES, rows if interleaved else SUBLANES, LANES), F32)],
        compiler_params=_params("arbitrary"),
        name="s5_scan",
    )(u4, h0r, h0i, a, wb, wc, d)


def _hgrn_gate_out(o, g, ng):
    parts = []
    for h in range(HG_HEADS):
        oh = o[:, h * HG_HEAD_DIM:(h + 1) * HG_HEAD_DIM]
        ms = jnp.mean(oh * oh, axis=-1, keepdims=True)
        parts.append(oh * lax.rsqrt(ms + EPS))
    return jnp.concatenate(parts, axis=-1) * ng * (g * _sigmoid(g))


def _hgrn_kernel(q_ref, f_ref, i_ref, g_ref, lb_ref, ng_ref, tri_ref, s0_ref,
                 y_ref, s_out_ref, st_s, o_s, *, th, valid):
    i = pl.program_id(1)
    c = HG_CHUNK

    @pl.when(i == 0)
    def _():
        for h in range(HG_HEADS):
            st_s[h] = s0_ref[0, h].T

    lb = lb_ref[...]
    f = lb + (1.0 - lb) * _sigmoid(f_ref[0])
    lc = jnp.log(f)
    k = 1.0 - f
    q = q_ref[0]
    if valid < th:
        live = lax.broadcasted_iota(jnp.int32, (th, 1), 0) < valid
        lc = jnp.where(live, lc, 0.0)
        k = jnp.where(live, k, 0.0)
        q = jnp.where(live, q.astype(F32), 0.0).astype(BF16)
    tri = tri_ref[...]
    tr = tri.shape[0]
    lc_hi = lc.astype(BF16)
    lc_lo = (lc - lc_hi.astype(F32)).astype(BF16)
    b = jnp.concatenate([_dot(tri, lc_hi[r:r + tr]) + _dot(tri, lc_lo[r:r + tr])
                         for r in range(0, th, tr)], axis=0)
    qd = q * jnp.exp(b).astype(BF16)
    kd = k * jnp.exp(-b)
    kdb = kd.astype(BF16)
    vb = i_ref[0]
    causal = (lax.broadcasted_iota(jnp.int32, (c, c), 1) <= lax.broadcasted_iota(jnp.int32, (c, c), 0))
    nt = (((1,), (1,)), ((), ()))
    n_chunks = th // c
    rows = [slice(cc * c, (cc + 1) * c) for cc in range(n_chunks)]
    lanes = [slice(h * HG_HEAD_DIM, (h + 1) * HG_HEAD_DIM) for h in range(HG_HEADS)]
    dec = [jnp.exp(b[cc * c + c - 1:cc * c + c, :]) for cc in range(n_chunks)]
    kdec = [(kd[rows[cc]] * dec[cc]).astype(BF16) for cc in range(n_chunks)]
    att = [[jnp.where(causal, lax.dot_general(qd[rows[cc], ls], kdb[rows[cc], ls], nt,
                                              preferred_element_type=F32), 0.0).astype(BF16)
            for ls in lanes] for cc in range(n_chunks)]
    upd = [[_dot(vb[rows[cc], ls].T, kdec[cc][:, ls])
            for ls in lanes] for cc in range(n_chunks)]
    st_in = [[None] * HG_HEADS for _ in range(n_chunks)]
    for h, ls in enumerate(lanes):
        st = st_s[h]
        for cc in range(n_chunks):
            st_in[cc][h] = st.astype(BF16)
            st = dec[cc][:, ls] * st + upd[cc][h]
        st_s[h] = st
    for cc in range(n_chunks):
        for h, ls in enumerate(lanes):
            o_s[rows[cc], ls] = (
                lax.dot_general(qd[rows[cc], ls], st_in[cc][h], nt, preferred_element_type=F32)
                + _dot(att[cc][h], vb[rows[cc], ls]))
    y_ref[0] = _hgrn_gate_out(o_s[...], g_ref[0].astype(F32), ng_ref[...])

    @pl.when(i == pl.num_programs(1) - 1)
    def _():
        for h in range(HG_HEADS):
            s_out_ref[0, h] = st_s[h].T


def _hgrn_tri(tr):
    idx = np.arange(tr)
    tri = (idx[:, None] // HG_CHUNK == idx[None, :] // HG_CHUNK) & (idx[None, :] <= idx[:, None])
    return jnp.asarray(tri, dtype=BF16)


def _hgrn(zf, qig, lb, ng, s0, th, n_tiles, valid, tile0=0):
    n = zf.shape[0]
    own = s0.shape[0] == n
    tr = min(th, 256)
    col = lambda j: pl.BlockSpec((1, th, HG_DIM), lambda b, i, j=j: (b, tile0 + i, j))
    const = lambda shape: pl.BlockSpec(shape, lambda b, i: (0,) * len(shape),
                                       pipeline_mode=pl.Buffered(1))
    st = pl.BlockSpec((1, HG_HEADS, HG_HEAD_DIM, HG_HEAD_DIM), lambda b, i: (b, 0, 0, 0))
    st_in = pl.BlockSpec((1, HG_HEADS, HG_HEAD_DIM, HG_HEAD_DIM),
                         lambda b, i: (b if own else 0, 0, 0, 0))
    return pl.pallas_call(
        functools.partial(_hgrn_kernel, th=th, valid=valid),
        grid=(n, n_tiles),
        in_specs=[col(0), col(0), col(1), col(2), const((1, HG_DIM)), const((1, HG_DIM)),
                  const((tr, tr)), st_in],
        out_specs=[pl.BlockSpec((1, th, HG_DIM), lambda b, i: (b, i, 0)), st],
        out_shape=(jax.ShapeDtypeStruct((n, n_tiles * th, HG_DIM), F32),
                   jax.ShapeDtypeStruct((n, HG_HEADS, HG_HEAD_DIM, HG_HEAD_DIM), F32)),
        scratch_shapes=[pltpu.VMEM((HG_HEADS, HG_HEAD_DIM, HG_HEAD_DIM), F32),
                        pltpu.VMEM((th, HG_DIM), F32)],
        compiler_params=_params("arbitrary", "arbitrary"),
        name="hgrn_chunks",
    )(qig, zf, qig, qig, lb, ng, _hgrn_tri(tr), s0)


def _hgrn_step_kernel(q_ref, f_ref, i_ref, g_ref, lb_ref, ng_ref, s0_ref,
                      y_ref, s_out_ref, o_s, *, sb):
    lb = lb_ref[...]
    f = lb + (1.0 - lb) * _sigmoid(f_ref[0])
    q = q_ref[0].astype(F32)
    v = i_ref[0].astype(F32)
    pad = jnp.zeros((HG_HEAD_DIM - sb, HG_HEAD_DIM), F32)
    sq = (HG_HEAD_DIM, HG_HEAD_DIM)
    for h in range(HG_HEADS):
        ls = slice(h * HG_HEAD_DIM, (h + 1) * HG_HEAD_DIM)
        fcols = jnp.concatenate([f[:, ls], pad], axis=0).T
        for s in range(sb):
            fc = jnp.broadcast_to(fcols[:, s:s + 1], sq)
            sn = fc * s0_ref[s, h] + (1.0 - fc) * v[s:s + 1, ls]
            s_out_ref[s, h] = sn
            o_s[s:s + 1, ls] = _dot(q[s:s + 1, ls].astype(BF16), sn.astype(BF16))
    y_ref[0] = _hgrn_gate_out(o_s[...], g_ref[0].astype(F32), ng_ref[...])


def _hgrn_step(zf, qig, lb, ng, s0, sb):
    r = s0.shape[0]
    col = lambda j: pl.BlockSpec((1, sb, HG_DIM), lambda i, j=j: (0, i, j))
    vec = pl.BlockSpec((1, HG_DIM), lambda i: (0, 0))
    st = pl.BlockSpec((sb, HG_HEADS, HG_HEAD_DIM, HG_HEAD_DIM), lambda i: (i, 0, 0, 0))
    return pl.pallas_call(
        functools.partial(_hgrn_step_kernel, sb=sb),
        grid=(r // sb,),
        in_specs=[col(0), col(0), col(1), col(2), vec, vec, st],
        out_specs=[pl.BlockSpec((1, sb, HG_DIM), lambda i: (0, i, 0)), st],
        out_shape=(jax.ShapeDtypeStruct((1, r, HG_DIM), F32),
                   jax.ShapeDtypeStruct(s0.shape, F32)),
        scratch_shapes=[pltpu.VMEM((sb, HG_DIM), F32)],
        compiler_params=_params("arbitrary"),
        name="hgrn_step",
    )(qig, zf, qig, qig, lb, ng, s0)


def _mix_out(y5_raw, yh, h, wglu, bglu, wout):
    y5p = _gelu_tanh(y5_raw)
    y5 = y5p * _sigmoid(_dot(y5p.astype(BF16), wglu) + bglu)
    ymix = jnp.concatenate([y5, yh], axis=-1).astype(BF16)
    return h + _dot(ymix, wout)


def _conv_taps(a, prev8, cw):
    rows = a.shape[0]
    cv = cw[3:4] + pltpu.roll(a, 2, 0) * cw[0:1] + pltpu.roll(a, 1, 0) * cw[1:2] + a * cw[2:3]
    d = prev8 - a[rows - SUBLANES:rows]
    rid = lax.broadcasted_iota(jnp.int32, (SUBLANES, 1), 0)
    fix = (jnp.where(rid < 1, pltpu.roll(d, 1, 0), 0.0) * cw[1:2]
           + jnp.where(rid < 2, pltpu.roll(d, 2, 0), 0.0) * cw[0:1])
    return jnp.concatenate([cv[:SUBLANES] + fix, cv[SUBLANES:]], axis=0)


def _ffn_kernel(y5_ref, yh_ref, h_ref, wglu_ref, bglu_ref, wout_ref, g2_ref,
                wa_ref, wv_ref, wd_ref, cw_ref, cin_ref, gf_ref,
                out_ref, cout_ref, carry_s, hn_s, h1_s, rn_s, s_s, *, tm):
    @pl.when(pl.program_id(1) == 0)
    def _():
        carry_s[...] = cin_ref[0]

    h1 = _mix_out(y5_ref[0], yh_ref[0], h_ref[0], wglu_ref[...], bglu_ref[...], wout_ref[...])
    h1_s[...] = h1
    hn_s[...] = (h1 * g2_ref[...]).astype(BF16)
    rn = lax.rsqrt(jnp.mean(h1 * h1, axis=-1, keepdims=True) + EPS)
    rn_s[...] = jnp.broadcast_to(rn, (tm, FF_CHUNK))

    for j in range(N_FF_CHUNKS):
        cs = slice(j * FF_CHUNK, (j + 1) * FF_CHUNK)
        hn = hn_s[...]
        a = _dot(hn, wa_ref[:, cs]) * rn_s[...]
        v = _dot(hn, wv_ref[:, cs]) * rn_s[...]
        cv = _conv_taps(a, carry_s[:, cs], cw_ref[:, cs])
        carry_s[:, cs] = a[tm - SUBLANES:tm]
        s_s[:, cs] = ((cv * _sigmoid(cv)) * v).astype(BF16)

    out_ref[0] = _rms_norm(h1_s[...] + _dot(s_s[...], wd_ref[...]), gf_ref[...])
    cout_ref[0] = carry_s[...]


def _ffn_small_kernel(y5m_ref, y5s_ref, yhm_ref, yhs_ref, hm_ref, hs_ref,
                      wglu_ref, bglu_ref, wout_ref, g2_ref, wa_ref, wv_ref, wd_ref, cw_ref,
                      cin_a_ref, cin_b_ref, gf_ref,
                      out_ref, a_out_ref, mcarry_ref,
                      wglu_bf_ref, wout_bf_ref, wa_bf_ref, wv_bf_ref, wd_bf_ref,
                      hn_s, h1_s, acc_s, *, n_meta):
    j = pl.program_id(0)

    @pl.when(j == 0)
    def _():
        wglu = wglu_ref[...].astype(BF16)
        wout = wout_ref[...].astype(BF16)
        wglu_bf_ref[...] = wglu
        wout_bf_ref[...] = wout
        rows = lambda m_ref, s_ref: jnp.concatenate([m_ref[0], s_ref[0]], axis=0)
        h1 = _mix_out(rows(y5m_ref, y5s_ref), rows(yhm_ref, yhs_ref), rows(hm_ref, hs_ref),
                      wglu, bglu_ref[...], wout)
        h1_s[...] = h1
        hn_s[...] = _rms_norm(h1, g2_ref[...]).astype(BF16)
        acc_s[...] = jnp.zeros_like(acc_s)

    wa = wa_ref[...].astype(BF16)
    wv = wv_ref[...].astype(BF16)
    wd = wd_ref[...].astype(BF16)
    wa_bf_ref[...] = wa
    wv_bf_ref[...] = wv
    wd_bf_ref[...] = wd
    hn = hn_s[...]
    a = _dot(hn, wa)
    v = _dot(hn, wv)
    cw = cw_ref[...]
    a_m, a_s = a[:n_meta], a[n_meta:]
    cv_m = _conv_taps(a_m, jnp.zeros((SUBLANES, FF_CHUNK), F32), cw)
    cv_s = cw[3:4] + cin_a_ref[...] * cw[0:1] + cin_b_ref[...] * cw[1:2] + a_s * cw[2:3]
    mcarry_ref[...] = a_m[n_meta - SUBLANES:]
    a_out_ref[...] = a_s
    cv = jnp.concatenate([cv_m, cv_s], axis=0)
    acc_s[...] += _dot(((cv * _sigmoid(cv)) * v).astype(BF16), wd)

    @pl.when(j == pl.num_programs(0) - 1)
    def _():
        out_ref[...] = _rms_norm((h1_s[...] + acc_s[...])[n_meta:], gf_ref[...])


def _ffn_small(y5m, y5s, yhm, yhs, hm, hs, wglu, bglu, wout, g2, wup, wd, cw, cin, gf, hm_tile=0):
    n_meta, r = y5m.shape[1], y5s.shape[1]
    const = lambda shape: pl.BlockSpec(shape, lambda j: (0,) * len(shape))
    head = lambda rows, width: pl.BlockSpec((1, rows, width), lambda j: (0, 0, 0))
    ff_cols = lambda rows: pl.BlockSpec((rows, FF_CHUNK), lambda j: (0, j))
    sds = jax.ShapeDtypeStruct
    return pl.pallas_call(
        functools.partial(_ffn_small_kernel, n_meta=n_meta),
        grid=(N_FF_CHUNKS,),
        in_specs=[head(n_meta, S5_DIM), head(r, S5_DIM), head(n_meta, HG_DIM), head(r, HG_DIM),
                  pl.BlockSpec((1, n_meta, D_MODEL), lambda j: (0, hm_tile, 0)), head(r, D_MODEL),
                  const(wglu.shape), const(bglu.shape), const(wout.shape), const(g2.shape),
                  ff_cols(D_MODEL),
                  pl.BlockSpec((D_MODEL, FF_CHUNK), lambda j: (0, N_FF_CHUNKS + j)),
                  pl.BlockSpec((FF_CHUNK, D_MODEL), lambda j: (j, 0)),
                  ff_cols(SUBLANES), ff_cols(r),
                  pl.BlockSpec((r, FF_CHUNK), lambda j: (0, N_FF_CHUNKS + j)), const(gf.shape)],
        out_specs=[const((r, D_MODEL)), ff_cols(r), ff_cols(SUBLANES),
                   const(wglu.shape), const(wout.shape), ff_cols(D_MODEL), ff_cols(D_MODEL),
                   pl.BlockSpec((FF_CHUNK, D_MODEL), lambda j: (j, 0))],
        out_shape=(sds((r, D_MODEL), F32), sds((r, D_FF), F32), sds((SUBLANES, D_FF), F32),
                   sds(wglu.shape, BF16), sds(wout.shape, BF16),
                   sds((D_MODEL, D_FF), BF16), sds((D_MODEL, D_FF), BF16), sds(wd.shape, BF16)),
        scratch_shapes=[pltpu.VMEM((n_meta + r, D_MODEL), BF16),
                        pltpu.VMEM((n_meta + r, D_MODEL), F32),
                        pltpu.VMEM((n_meta + r, D_MODEL), F32)],
        compiler_params=_params("arbitrary"),
        name="ffn_small",
    )(y5m, y5s, yhm, yhs, hm, hs, wglu, bglu, wout, g2, wup, wup, wd, cw, cin, cin, gf)


def _ffn(y5, yh, h, wglu, bglu, wout, g2, wa, wv, wd, cw, cin, gf, tm, n_tiles):
    n = h.shape[0]
    const = lambda shape: pl.BlockSpec(shape, lambda b, i: (0,) * len(shape),
                                       pipeline_mode=pl.Buffered(1))
    own = cin.shape[0] == n
    conv = lambda index: pl.BlockSpec((1, SUBLANES, D_FF), index)
    return pl.pallas_call(
        functools.partial(_ffn_kernel, tm=tm),
        grid=(n, n_tiles),
        in_specs=[pl.BlockSpec((1, tm, S5_DIM), lambda b, i: (b, i, 0)),
                  pl.BlockSpec((1, tm, HG_DIM), lambda b, i: (b, i, 0)),
                  pl.BlockSpec((1, tm, D_MODEL), lambda b, i: (b, i, 0)),
                  const(wglu.shape), const(bglu.shape), const(wout.shape), const(g2.shape),
                  const(wa.shape), const(wv.shape), const(wd.shape), const(cw.shape),
                  conv(lambda b, i: (b if own else 0, 0, 0)), const(gf.shape)],
        out_specs=[pl.BlockSpec((1, tm, D_MODEL), lambda b, i: (b, i, 0)),
                   conv(lambda b, i: (b, 0, 0))],
        out_shape=(jax.ShapeDtypeStruct((n, n_tiles * tm, D_MODEL), F32),
                   jax.ShapeDtypeStruct((n, SUBLANES, D_FF), F32)),
        scratch_shapes=[pltpu.VMEM((SUBLANES, D_FF), F32),
                        pltpu.VMEM((tm, D_MODEL), BF16),
                        pltpu.VMEM((tm, D_MODEL), F32),
                        pltpu.VMEM((tm, FF_CHUNK), F32),
                        pltpu.VMEM((tm, D_FF), BF16)],
        compiler_params=_params("arbitrary", "arbitrary"),
        name="mix_out_ffn",
    )(y5, yh, h, wglu, bglu, wout, g2, wa, wv, wd, cw, cin, gf)


def kernel(x_prompt, x_sample, state_s5_re, state_s5_im, state_hgrn, state_ffn_conv, meta_tokens, norm_mix_g, w_in, s5_lambda_re, s5_lambda_im, s5_log_dt, s5_b_re, s5_b_im, s5_c_re, s5_c_im, s5_d, s5_w_glu, s5_b_glu, hg_lower_bounds, hg_norm_g, w_out, norm_ffn_g, ffn_w_up, ffn_conv_w, ffn_conv_b, ffn_w_down, final_norm_g):
    nb, seq, _ = x_prompt.shape
    ns = x_sample.shape[0]
    li = 0

    a5, bb, lb = _prep(s5_lambda_re[li], s5_lambda_im[li], s5_log_dt[li],
                       s5_b_re[li], s5_b_im[li], hg_lower_bounds)
    a5 = a5.reshape(2, 1, S5_LANES)
    wb = _block_diag(bb)
    wc = _block_diag(jnp.stack([s5_c_re[li], -s5_c_im[li]]).transpose(0, 1, 3, 2))
    d5 = s5_d[li].reshape(1, S5_DIM)
    g1 = norm_mix_g[li].reshape(1, D_MODEL)
    g2 = norm_ffn_g[li].reshape(1, D_MODEL)
    gf = final_norm_g.reshape(1, D_MODEL)
    ng = hg_norm_g[li].reshape(1, HG_DIM)
    bglu = s5_b_glu[li].reshape(1, S5_DIM)
    cw = jnp.concatenate([ffn_conv_w[li], ffn_conv_b[li][None],
                          jnp.zeros((SUBLANES - CONV_W - 1, D_FF), F32)], axis=0)
    s5w = (a5, wb, wc, d5)

    xs = jnp.concatenate([x_sample.reshape(ns, D_MODEL), meta_tokens,
                          jnp.zeros((HG_CHUNK - N_META, D_MODEL), F32)])[None]
    u_tm, zf, qig, w_in_b = _inproj(xs, g1, w_in[li], ns + HG_CHUNK, 1, emit_bf16=True)

    y5s, s5r, s5i = _s5(u_tm, state_s5_re[li].reshape(ns, S5_LANES), state_s5_im[li].reshape(ns, S5_LANES),
                        *s5w, ns, 1, 1)
    yhs, shg = _hgrn_step(zf, qig, lb, ng, state_hgrn[li], STEP_ROWS)

    z5 = jnp.zeros((1, S5_LANES), F32)
    zh = jnp.zeros((1, HG_HEADS, HG_HEAD_DIM, HG_HEAD_DIM), F32)
    y5m, m5r, m5i = _s5(u_tm, z5, z5, *s5w, 1, N_META, 1, tile0=ns // N_META)
    yhm, mhg = _hgrn(zf, qig, lb, ng, zh, HG_CHUNK, 1, N_META, tile0=ns // HG_CHUNK)

    buf = state_ffn_conv[li].reshape(ns, (CONV_W - 1) * D_FF)
    y_sample, a_new, mconv, wglu, wout, wa, wv, wd = _ffn_small(
        y5m, y5s, yhm, yhs, xs, xs, s5_w_glu[li], bglu, w_out[li], g2,
        ffn_w_up[li], ffn_w_down[li], cw, buf, gf, hm_tile=ns // N_META)
    sconv = jnp.stack([buf[:, D_FF:], a_new], axis=1)

    u_tm, zf, qig = _inproj(x_prompt, g1, w_in_b, IN_TILE, seq // IN_TILE, nb=IN_SEQS)
    y5, p5r, p5i = _s5(u_tm, m5r, m5i, *s5w, nb, S5_TILE, seq // S5_TILE)
    yh, phg = _hgrn(zf, qig, lb, ng, mhg, seq, 1, seq)
    y_prompt, pconv = _ffn(y5, yh, x_prompt, wglu, bglu, wout, g2, wa, wv, wd, cw, mconv[None], gf,
                           FFN_TILE, seq // FFN_TILE)

    st5 = lambda t: t.reshape(1, -1, S5_GROUPS, S5_STATE)
    pconv = pconv[:, SUBLANES - (CONV_W - 1):, :]
    return (y_prompt, y_sample.reshape(ns, 1, D_MODEL),
            st5(p5r), st5(p5i), phg[None], pconv[None],
            st5(s5r), st5(s5i), shg[None], sconv[None])
```

```python
import functools

import jax
import jax.numpy as jnp
import numpy as np
from jax import lax
from jax.experimental import pallas as pl
from jax.experimental.pallas import tpu as pltpu

F32 = jnp.float32
BF16 = jnp.bfloat16

D_MODEL = 1024
N_META = 16
S5_DIM = 512
S5_GROUP = 16
S5_GROUPS = 32
S5_STATE = 64
S5_LANES = S5_GROUPS * S5_STATE
HG_DIM = 512
HG_HEAD_DIM = 128
HG_HEADS = 4
HG_CHUNK = 64
D_FF = 2816
CONV_W = 3
EPS = 1e-6

S5_HALVES = 2
S5_HALF_CH = S5_DIM // S5_HALVES
S5_HALF_ST = S5_LANES // S5_HALVES
SCAN_LANES = 512
FF_CHUNK = 256
N_FF_CHUNKS = D_FF // FF_CHUNK
SUBLANES = 8
LANES = 128
VMEM_LIMIT = 56 * 1024 * 1024
IN_TILE = 512
IN_SEQS = 2
S5_TILE = 128
FFN_TILE = 1024
STEP_ROWS = 32


def _sigmoid(x):
    return 0.5 + 0.5 * jnp.tanh(0.5 * x)


def _rms_norm(x, g):
    ms = jnp.mean(x * x, axis=-1, keepdims=True)
    return x * lax.rsqrt(ms + EPS) * g


def _dot(a, b):
    return jnp.dot(a, b, preferred_element_type=F32)


def _params(*sem):
    return pltpu.CompilerParams(dimension_semantics=sem, vmem_limit_bytes=VMEM_LIMIT)


def _prep_kernel(lam_re_ref, lam_im_ref, dt_ref, bt_ref, hlb_ref, a_ref, bb_ref, lb_ref):
    lam_re = lam_re_ref[...]
    lam_im = lam_im_ref[...]
    dt = jnp.exp(dt_ref[...])
    mag = jnp.exp(lam_re * dt)
    ar = mag * jnp.cos(lam_im * dt)
    ai = mag * jnp.sin(lam_im * dt)
    nr = ar - 1.0
    den = lam_re * lam_re + lam_im * lam_im
    cr = (nr * lam_re + ai * lam_im) / den
    ci = (ai * lam_re - nr * lam_im) / den
    a_ref[0] = ar
    a_ref[1] = ai
    bt_re = bt_ref[0]
    bt_im = bt_ref[1]
    bb_ref[0] = cr * bt_re - ci * bt_im
    bb_ref[1] = cr * bt_im + ci * bt_re
    hlb = hlb_ref[...]
    e = jnp.exp(hlb - jnp.max(hlb, axis=0, keepdims=True))
    lb_ref[...] = e[0:1] / jnp.sum(e, axis=0, keepdims=True)


def _prep(lam_re, lam_im, log_dt, b_re, b_im, hlb):
    g, p = lam_re.shape
    sds = jax.ShapeDtypeStruct
    return pl.pallas_call(
        _prep_kernel,
        out_shape=(sds((2, g, 1, p), F32), sds((2, g, S5_GROUP, p), F32), sds((1, HG_DIM), F32)),
        name="param_prep",
    )(lam_re.reshape(g, 1, p), lam_im.reshape(g, 1, p),
      jnp.broadcast_to(log_dt.reshape(g, 1, 1), (g, 1, p)),
      jnp.stack([b_re, b_im]).transpose(0, 1, 3, 2), hlb)


def _block_diag(blocks):
    two, g, r, c = blocks.shape
    per = g // S5_HALVES
    rows = blocks.reshape(two, S5_HALVES, per * r, 1, c)
    tiled = jnp.broadcast_to(rows, (two, S5_HALVES, per * r, per, c)).reshape(two, S5_HALVES, per * r, per * c)
    on_diag = (np.arange(per * r)[:, None] // r) == (np.arange(per * c)[None, :] // c)
    return jnp.where(on_diag, tiled, 0.0).astype(BF16)


def _inproj_kernel(x_ref, g_ref, w_ref, u_ref, f_ref, qig_ref, *wb_ref, n, nb, tm):
    x = x_ref[...].reshape(nb * tm, D_MODEL)
    hn = (x * g_ref[...]).astype(BF16)
    rn = lax.rsqrt(jnp.mean(x * x, axis=-1, keepdims=True) + EPS)
    rn = jnp.broadcast_to(rn, (nb * tm, HG_DIM))
    if wb_ref:
        wb_ref[0][...] = w_ref[...].astype(BF16)
        w_ref = wb_ref[0]
    col = lambda j: _dot(hn, w_ref[:, j * HG_DIM:(j + 1) * HG_DIM]) * rn
    b0 = pl.program_id(1) * nb
    u = col(0)
    for l in range(S5_DIM // LANES):
        for r in range(nb):
            ul = u[r * tm:(r + 1) * tm, l * LANES:(l + 1) * LANES]
            if n == 1:
                u_ref[l] = ul
            else:
                u_ref[l, pl.ds(b0 + r, tm, stride=n), :] = ul
    f_ref[...] = col(2).reshape(nb, tm, HG_DIM)
    for j, src in enumerate((1, 3, 4)):
        qig_ref[:, :, j * HG_DIM:(j + 1) * HG_DIM] = col(src).astype(BF16).reshape(nb, tm, HG_DIM)


def _inproj(x, g, w, tm, n_tiles, nb=1, emit_bf16=False):
    n, l, d = x.shape
    cols = w.shape[1]
    rows = tm * n_tiles
    const = lambda shape: pl.BlockSpec(shape, lambda i, b: (0,) * len(shape),
                                       pipeline_mode=pl.Buffered(1))
    return pl.pallas_call(
        functools.partial(_inproj_kernel, n=n, nb=nb, tm=tm),
        grid=(n_tiles, n // nb),
        in_specs=[pl.BlockSpec((nb, tm, d), lambda i, b: (b, i, 0)),
                  const((1, d)), const((d, cols))],
        out_specs=[pl.BlockSpec((S5_DIM // LANES, tm * n, LANES), lambda i, b: (0, i, 0)),
                   pl.BlockSpec((nb, tm, HG_DIM), lambda i, b: (b, i, 0)),
                   pl.BlockSpec((nb, tm, 3 * HG_DIM), lambda i, b: (b, i, 0))]
                  + [pl.BlockSpec((d, cols), lambda i, b: (0, 0))] * emit_bf16,
        out_shape=(jax.ShapeDtypeStruct((S5_DIM // LANES, rows * n, LANES), F32),
                   jax.ShapeDtypeStruct((n, rows, HG_DIM), F32),
                   jax.ShapeDtypeStruct((n, rows, 3 * HG_DIM), BF16))
                  + (jax.ShapeDtypeStruct((d, cols), BF16),) * emit_bf16,
        compiler_params=_params("arbitrary", "arbitrary"),
        name="inproj",
    )(x, g, w)


def _gelu_tanh(y):
    return 0.5 * y * (1.0 + jnp.tanh(0.7978845608028654 * (y + 0.044715 * (y * y * y))))


def _s5_kernel(u_ref, h0r_ref, h0i_ref, a_ref, wb_ref, wc_ref, d_ref,
               y_ref, hr_out_ref, hi_out_ref, xr_s, xi_s, hr_s, hi_s, y_s, *, n, tt, interleaved):
    @pl.when(pl.program_id(0) == 0)
    def _():
        hr_s[...] = jnp.broadcast_to(h0r_ref[...], hr_s.shape)
        hi_s[...] = jnp.broadcast_to(h0i_ref[...], hi_s.shape)

    slabs = S5_HALF_CH // LANES
    load_u = lambda j: jnp.concatenate([u_ref[j * slabs + l] for l in range(slabs)], axis=1)
    for j in range(S5_HALVES):
        st = slice(j * S5_HALF_ST, (j + 1) * S5_HALF_ST)
        ub = load_u(j).astype(BF16)
        xr_s[:, st] = _dot(ub, wb_ref[0, j])
        xi_s[:, st] = _dot(ub, wb_ref[1, j])
    for c in range(S5_LANES // SCAN_LANES):
        loc = slice(c * SCAN_LANES, (c + 1) * SCAN_LANES)
        ar = jnp.broadcast_to(a_ref[0, :, loc], (n, SCAN_LANES))
        ai = jnp.broadcast_to(a_ref[1, :, loc], (n, SCAN_LANES))

        def step(t, carry, ar=ar, ai=ai, loc=loc):
            hr, hi = carry
            r = 0 if tt == 1 else pl.multiple_of(t * n, n)
            nhr = ar * hr - ai * hi + xr_s[pl.ds(r, n), loc]
            nhi = ar * hi + ai * hr + xi_s[pl.ds(r, n), loc]
            xr_s[pl.ds(r, n), loc] = nhr
            xi_s[pl.ds(r, n), loc] = nhi
            return nhr, nhi

        carry = (hr_s[:, loc], hi_s[:, loc])
        if tt == 1:
            carry = step(0, carry)
        else:
            carry = lax.fori_loop(0, tt, step, carry, unroll=True)
        hr_s[:, loc] = carry[0]
        hi_s[:, loc] = carry[1]
    for j in range(S5_HALVES):
        ch = slice(j * S5_HALF_CH, (j + 1) * S5_HALF_CH)
        st = slice(j * S5_HALF_ST, (j + 1) * S5_HALF_ST)
        y = (_dot(xr_s[:, st].astype(BF16), wc_ref[0, j]) + _dot(xi_s[:, st].astype(BF16), wc_ref[1, j])
             + d_ref[:, ch] * load_u(j))
        if interleaved:
            for l in range(slabs):
                y_s[j * slabs + l] = y[:, l * LANES:(l + 1) * LANES]
        else:
            y_ref[0, :, ch] = y
    if interleaved:
        for b in range(n):
            for l in range(S5_DIM // LANES):
                y_ref[b, :, l * LANES:(l + 1) * LANES] = y_s[l, pl.ds(b, tt, stride=n), :]
    hr_out_ref[...] = hr_s[...]
    hi_out_ref[...] = hi_s[...]


def _s5(u4, h0r, h0i, a, wb, wc, d, n, tt, n_tiles, tile0=0):
    rows = tt * n
    interleaved = n > 1 and tt > 1
    const = lambda shape: pl.BlockSpec(shape, lambda i: (0,) * len(shape),
                                       pipeline_mode=pl.Buffered(1))
    state = pl.BlockSpec((n, S5_LANES), lambda i: (0, 0))
    y_block = (n, tt, S5_DIM) if interleaved else (1, rows, S5_DIM)
    y_shape = (n, tt * n_tiles, S5_DIM) if interleaved else (1, rows * n_tiles, S5_DIM)
    return pl.pallas_call(
        functools.partial(_s5_kernel, n=n, tt=tt, interleaved=interleaved),
        grid=(n_tiles,),
        in_specs=[pl.BlockSpec((S5_DIM // LANES, rows, LANES), lambda i: (0, tile0 + i, 0)),
                  const(h0r.shape), const(h0i.shape), const(a.shape), const(wb.shape), const(wc.shape),
                  const((1, S5_DIM))],
        out_specs=[pl.BlockSpec(y_block, lambda i: (0, i, 0)), state, state],
        out_shape=(jax.ShapeDtypeStruct(y_shape, F32),
                   jax.ShapeDtypeStruct((n, S5_LANES), F32),
                   jax.ShapeDtypeStruct((n, S5_LANES), F32)),
        scratch_shapes=[pltpu.VMEM((rows, S5_LANES), F32), pltpu.VMEM((rows, S5_LANES), F32),
                        pltpu.VMEM((n, S5_LANES), F32), pltpu.VMEM((n, S5_LANES), F32),
                        pltpu.VMEM((S5_DIM // LANES, rows if interleaved else SUBLANES, LANES), F32)],
        compiler_params=_params("arbitrary"),
        name="s5_scan",
    )(u4, h0r, h0i, a, wb, wc, d)


def _hgrn_gate_out(o, g, ng):
    parts = []
    for h in range(HG_HEADS):
        oh = o[:, h * HG_HEAD_DIM:(h + 1) * HG_HEAD_DIM]
        ms = jnp.mean(oh * oh, axis=-1, keepdims=True)
        parts.append(oh * lax.rsqrt(ms + EPS))
    return jnp.concatenate(parts, axis=-1) * ng * (g * _sigmoid(g))


def _hgrn_kernel(q_ref, f_ref, i_ref, g_ref, lb_ref, ng_ref, tri_ref, s0_ref,
                 y_ref, s_out_ref, st_s, o_s, *, th, valid):
    i = pl.program_id(1)
    c = HG_CHUNK

    @pl.when(i == 0)
    def _():
        for h in range(HG_HEADS):
            st_s[h] = s0_ref[0, h].T

    lb = lb_ref[...]
    f = lb + (1.0 - lb) * _sigmoid(f_ref[0])
    lc = jnp.log(f)
    k = 1.0 - f
    q = q_ref[0]
    if valid < th:
        live = lax.broadcasted_iota(jnp.int32, (th, 1), 0) < valid
        lc = jnp.where(live, lc, 0.0)
        k = jnp.where(live, k, 0.0)
        q = jnp.where(live, q.astype(F32), 0.0).astype(BF16)
    tri = tri_ref[...]
    tr = tri.shape[0]
    lc_hi = lc.astype(BF16)
    lc_lo = (lc - lc_hi.astype(F32)).astype(BF16)
    b = jnp.concatenate([_dot(tri, lc_hi[r:r + tr]) + _dot(tri, lc_lo[r:r + tr])
                         for r in range(0, th, tr)], axis=0)
    qd = q * jnp.exp(b).astype(BF16)
    kd = k * jnp.exp(-b)
    kdb = kd.astype(BF16)
    vb = i_ref[0]
    causal = (lax.broadcasted_iota(jnp.int32, (c, c), 1) <= lax.broadcasted_iota(jnp.int32, (c, c), 0))
    nt = (((1,), (1,)), ((), ()))
    n_chunks = th // c
    rows = [slice(cc * c, (cc + 1) * c) for cc in range(n_chunks)]
    lanes = [slice(h * HG_HEAD_DIM, (h + 1) * HG_HEAD_DIM) for h in range(HG_HEADS)]
    dec = [jnp.exp(b[cc * c + c - 1:cc * c + c, :]) for cc in range(n_chunks)]
    kdec = [(kd[rows[cc]] * dec[cc]).astype(BF16) for cc in range(n_chunks)]
    att = [[jnp.where(causal, lax.dot_general(qd[rows[cc], ls], kdb[rows[cc], ls], nt,
                                              preferred_element_type=F32), 0.0).astype(BF16)
            for ls in lanes] for cc in range(n_chunks)]
    upd = [[_dot(vb[rows[cc], ls].T, kdec[cc][:, ls])
            for ls in lanes] for cc in range(n_chunks)]
    st_in = [[None] * HG_HEADS for _ in range(n_chunks)]
    for h, ls in enumerate(lanes):
        st = st_s[h]
        for cc in range(n_chunks):
            st_in[cc][h] = st.astype(BF16)
            st = dec[cc][:, ls] * st + upd[cc][h]
        st_s[h] = st
    for cc in range(n_chunks):
        for h, ls in enumerate(lanes):
            o_s[rows[cc], ls] = (
                lax.dot_general(qd[rows[cc], ls], st_in[cc][h], nt, preferred_element_type=F32)
                + _dot(att[cc][h], vb[rows[cc], ls]))
    y_ref[0] = _hgrn_gate_out(o_s[...], g_ref[0].astype(F32), ng_ref[...])

    @pl.when(i == pl.num_programs(1) - 1)
    def _():
        for h in range(HG_HEADS):
            s_out_ref[0, h] = st_s[h].T


def _hgrn_tri(tr):
    idx = np.arange(tr)
    tri = (idx[:, None] // HG_CHUNK == idx[None, :] // HG_CHUNK) & (idx[None, :] <= idx[:, None])
    return jnp.asarray(tri, dtype=BF16)


def _hgrn(zf, qig, lb, ng, s0, th, n_tiles, valid, tile0=0):
    n = zf.shape[0]
    own = s0.shape[0] == n
    tr = min(th, 256)
    col = lambda j: pl.BlockSpec((1, th, HG_DIM), lambda b, i, j=j: (b, tile0 + i, j))
    const = lambda shape: pl.BlockSpec(shape, lambda b, i: (0,) * len(shape),
                                       pipeline_mode=pl.Buffered(1))
    st = pl.BlockSpec((1, HG_HEADS, HG_HEAD_DIM, HG_HEAD_DIM), lambda b, i: (b, 0, 0, 0))
    st_in = pl.BlockSpec((1, HG_HEADS, HG_HEAD_DIM, HG_HEAD_DIM),
                         lambda b, i: (b if own else 0, 0, 0, 0))
    return pl.pallas_call(
        functools.partial(_hgrn_kernel, th=th, valid=valid),
        grid=(n, n_tiles),
        in_specs=[col(0), col(0), col(1), col(2), const((1, HG_DIM)), const((1, HG_DIM)),
                  const((tr, tr)), st_in],
        out_specs=[pl.BlockSpec((1, th, HG_DIM), lambda b, i: (b, i, 0)), st],
        out_shape=(jax.ShapeDtypeStruct((n, n_tiles * th, HG_DIM), F32),
                   jax.ShapeDtypeStruct((n, HG_HEADS, HG_HEAD_DIM, HG_HEAD_DIM), F32)),
        scratch_shapes=[pltpu.VMEM((HG_HEADS, HG_HEAD_DIM, HG_HEAD_DIM), F32),
                        pltpu.VMEM((th, HG_DIM), F32)],
        compiler_params=_params("arbitrary", "arbitrary"),
        name="hgrn_chunks",
    )(qig, zf, qig, qig, lb, ng, _hgrn_tri(tr), s0)


def _hgrn_step_kernel(q_ref, f_ref, i_ref, g_ref, lb_ref, ng_ref, s0_ref,
                      y_ref, s_out_ref, o_s, *, sb):
    lb = lb_ref[...]
    f = lb + (1.0 - lb) * _sigmoid(f_ref[0])
    q = q_ref[0].astype(F32)
    v = i_ref[0].astype(F32)
    pad = jnp.zeros((HG_HEAD_DIM - sb, HG_HEAD_DIM), F32)
    sq = (HG_HEAD_DIM, HG_HEAD_DIM)
    for h in range(HG_HEADS):
        ls = slice(h * HG_HEAD_DIM, (h + 1) * HG_HEAD_DIM)
        fcols = jnp.concatenate([f[:, ls], pad], axis=0).T
        for s in range(sb):
            fc = jnp.broadcast_to(fcols[:, s:s + 1], sq)
            sn = fc * s0_ref[s, h] + (1.0 - fc) * v[s:s + 1, ls]
            s_out_ref[s, h] = sn
            o_s[s:s + 1, ls] = _dot(q[s:s + 1, ls].astype(BF16), sn.astype(BF16))
    y_ref[0] = _hgrn_gate_out(o_s[...], g_ref[0].astype(F32), ng_ref[...])


def _hgrn_step(zf, qig, lb, ng, s0, sb):
    r = s0.shape[0]
    col = lambda j: pl.BlockSpec((1, sb, HG_DIM), lambda i, j=j: (0, i, j))
    vec = pl.BlockSpec((1, HG_DIM), lambda i: (0, 0))
    st = pl.BlockSpec((sb, HG_HEADS, HG_HEAD_DIM, HG_HEAD_DIM), lambda i: (i, 0, 0, 0))
    return pl.pallas_call(
        functools.partial(_hgrn_step_kernel, sb=sb),
        grid=(r // sb,),
        in_specs=[col(0), col(0), col(1), col(2), vec, vec, st],
        out_specs=[pl.BlockSpec((1, sb, HG_DIM), lambda i: (0, i, 0)), st],
        out_shape=(jax.ShapeDtypeStruct((1, r, HG_DIM), F32),
                   jax.ShapeDtypeStruct(s0.shape, F32)),
        scratch_shapes=[pltpu.VMEM((sb, HG_DIM), F32)],
        compiler_params=_params("arbitrary"),
        name="hgrn_step",
    )(qig, zf, qig, qig, lb, ng, s0)


def _mix_out(y5_raw, yh, h, wglu, bglu, wout):
    y5p = _gelu_tanh(y5_raw)
    y5 = y5p * _sigmoid(_dot(y5p.astype(BF16), wglu) + bglu)
    ymix = jnp.concatenate([y5, yh], axis=-1).astype(BF16)
    return h + _dot(ymix, wout)


def _conv_taps(a, prev8, cw):
    rows = a.shape[0]
    cv = cw[3:4] + pltpu.roll(a, 2, 0) * cw[0:1] + pltpu.roll(a, 1, 0) * cw[1:2] + a * cw[2:3]
    d = prev8 - a[rows - SUBLANES:rows]
    rid = lax.broadcasted_iota(jnp.int32, (SUBLANES, 1), 0)
    fix = (jnp.where(rid < 1, pltpu.roll(d, 1, 0), 0.0) * cw[1:2]
           + jnp.where(rid < 2, pltpu.roll(d, 2, 0), 0.0) * cw[0:1])
    return jnp.concatenate([cv[:SUBLANES] + fix, cv[SUBLANES:]], axis=0)


def _ffn_kernel(y5_ref, yh_ref, h_ref, wglu_ref, bglu_ref, wout_ref, g2_ref,
                wa_ref, wv_ref, wd_ref, cw_ref, cin_ref, gf_ref,
                out_ref, cout_ref, carry_s, hn_s, h1_s, rn_s, s_s, *, tm):
    @pl.when(pl.program_id(1) == 0)
    def _():
        carry_s[...] = cin_ref[0]

    h1 = _mix_out(y5_ref[0], yh_ref[0], h_ref[0], wglu_ref[...], bglu_ref[...], wout_ref[...])
    h1_s[...] = h1
    hn_s[...] = (h1 * g2_ref[...]).astype(BF16)
    rn = lax.rsqrt(jnp.mean(h1 * h1, axis=-1, keepdims=True) + EPS)
    rn_s[...] = jnp.broadcast_to(rn, (tm, FF_CHUNK))

    for j in range(N_FF_CHUNKS):
        cs = slice(j * FF_CHUNK, (j + 1) * FF_CHUNK)
        hn = hn_s[...]
        a = _dot(hn, wa_ref[:, cs]) * rn_s[...]
        v = _dot(hn, wv_ref[:, cs]) * rn_s[...]
        cv = _conv_taps(a, carry_s[:, cs], cw_ref[:, cs])
        carry_s[:, cs] = a[tm - SUBLANES:tm]
        s_s[:, cs] = ((cv * _sigmoid(cv)) * v).astype(BF16)

    out_ref[0] = _rms_norm(h1_s[...] + _dot(s_s[...], wd_ref[...]), gf_ref[...])
    cout_ref[0] = carry_s[...]


def _ffn_small_kernel(y5m_ref, y5s_ref, yhm_ref, yhs_ref, hm_ref, hs_ref,
                      wglu_ref, bglu_ref, wout_ref, g2_ref, wa_ref, wv_ref, wd_ref, cw_ref,
                      cin_a_ref, cin_b_ref, gf_ref,
                      out_ref, a_out_ref, mcarry_ref,
                      wglu_bf_ref, wout_bf_ref, wa_bf_ref, wv_bf_ref, wd_bf_ref,
                      hn_s, h1_s, acc_s, *, n_meta):
    j = pl.program_id(0)

    @pl.when(j == 0)
    def _():
        wglu = wglu_ref[...].astype(BF16)
        wout = wout_ref[...].astype(BF16)
        wglu_bf_ref[...] = wglu
        wout_bf_ref[...] = wout
        rows = lambda m_ref, s_ref: jnp.concatenate([m_ref[0], s_ref[0]], axis=0)
        h1 = _mix_out(rows(y5m_ref, y5s_ref), rows(yhm_ref, yhs_ref), rows(hm_ref, hs_ref),
                      wglu, bglu_ref[...], wout)
        h1_s[...] = h1
        hn_s[...] = _rms_norm(h1, g2_ref[...]).astype(BF16)
        acc_s[...] = jnp.zeros_like(acc_s)

    wa = wa_ref[...].astype(BF16)
    wv = wv_ref[...].astype(BF16)
    wd = wd_ref[...].astype(BF16)
    wa_bf_ref[...] = wa
    wv_bf_ref[...] = wv
    wd_bf_ref[...] = wd
    hn = hn_s[...]
    a = _dot(hn, wa)
    v = _dot(hn, wv)
    cw = cw_ref[...]
    a_m, a_s = a[:n_meta], a[n_meta:]
    cv_m = _conv_taps(a_m, jnp.zeros((SUBLANES, FF_CHUNK), F32), cw)
    cv_s = cw[3:4] + cin_a_ref[...] * cw[0:1] + cin_b_ref[...] * cw[1:2] + a_s * cw[2:3]
    mcarry_ref[...] = a_m[n_meta - SUBLANES:]
    a_out_ref[...] = a_s
    cv = jnp.concatenate([cv_m, cv_s], axis=0)
    acc_s[...] += _dot(((cv * _sigmoid(cv)) * v).astype(BF16), wd)

    @pl.when(j == pl.num_programs(0) - 1)
    def _():
        out_ref[...] = _rms_norm((h1_s[...] + acc_s[...])[n_meta:], gf_ref[...])


def _ffn_small(y5m, y5s, yhm, yhs, hm, hs, wglu, bglu, wout, g2, wup, wd, cw, cin, gf, hm_tile=0):
    n_meta, r = y5m.shape[1], y5s.shape[1]
    const = lambda shape: pl.BlockSpec(shape, lambda j: (0,) * len(shape))
    head = lambda rows, width: pl.BlockSpec((1, rows, width), lambda j: (0, 0, 0))
    ff_cols = lambda rows: pl.BlockSpec((rows, FF_CHUNK), lambda j: (0, j))
    sds = jax.ShapeDtypeStruct
    return pl.pallas_call(
        functools.partial(_ffn_small_kernel, n_meta=n_meta),
        grid=(N_FF_CHUNKS,),
        in_specs=[head(n_meta, S5_DIM), head(r, S5_DIM), head(n_meta, HG_DIM), head(r, HG_DIM),
                  pl.BlockSpec((1, n_meta, D_MODEL), lambda j: (0, hm_tile, 0)), head(r, D_MODEL),
                  const(wglu.shape), const(bglu.shape), const(wout.shape), const(g2.shape),
                  ff_cols(D_MODEL),
                  pl.BlockSpec((D_MODEL, FF_CHUNK), lambda j: (0, N_FF_CHUNKS + j)),
                  pl.BlockSpec((FF_CHUNK, D_MODEL), lambda j: (j, 0)),
                  ff_cols(SUBLANES), ff_cols(r),
                  pl.BlockSpec((r, FF_CHUNK), lambda j: (0, N_FF_CHUNKS + j)), const(gf.shape)],
        out_specs=[const((r, D_MODEL)), ff_cols(r), ff_cols(SUBLANES),
                   const(wglu.shape), const(wout.shape), ff_cols(D_MODEL), ff_cols(D_MODEL),
                   pl.BlockSpec((FF_CHUNK, D_MODEL), lambda j: (j, 0))],
        out_shape=(sds((r, D_MODEL), F32), sds((r, D_FF), F32), sds((SUBLANES, D_FF), F32),
                   sds(wglu.shape, BF16), sds(wout.shape, BF16),
                   sds((D_MODEL, D_FF), BF16), sds((D_MODEL, D_FF), BF16), sds(wd.shape, BF16)),
        scratch_shapes=[pltpu.VMEM((n_meta + r, D_MODEL), BF16),
                        pltpu.VMEM((n_meta + r, D_MODEL), F32),
                        pltpu.VMEM((n_meta + r, D_MODEL), F32)],
        compiler_params=_params("arbitrary"),
        name="ffn_small",
    )(y5m, y5s, yhm, yhs, hm, hs, wglu, bglu, wout, g2, wup, wup, wd, cw, cin, cin, gf)


def _ffn(y5, yh, h, wglu, bglu, wout, g2, wa, wv, wd, cw, cin, gf, tm, n_tiles):
    n = h.shape[0]
    const = lambda shape: pl.BlockSpec(shape, lambda b, i: (0,) * len(shape),
                                       pipeline_mode=pl.Buffered(1))
    own = cin.shape[0] == n
    conv = lambda index: pl.BlockSpec((1, SUBLANES, D_FF), index)
    return pl.pallas_call(
        functools.partial(_ffn_kernel, tm=tm),
        grid=(n, n_tiles),
        in_specs=[pl.BlockSpec((1, tm, S5_DIM), lambda b, i: (b, i, 0)),
                  pl.BlockSpec((1, tm, HG_DIM), lambda b, i: (b, i, 0)),
                  pl.BlockSpec((1, tm, D_MODEL), lambda b, i: (b, i, 0)),
                  const(wglu.shape), const(bglu.shape), const(wout.shape), const(g2.shape),
                  const(wa.shape), const(wv.shape), const(wd.shape), const(cw.shape),
                  conv(lambda b, i: (b if own else 0, 0, 0)), const(gf.shape)],
        out_specs=[pl.BlockSpec((1, tm, D_MODEL), lambda b, i: (b, i, 0)),
                   conv(lambda b, i: (b, 0, 0))],
        out_shape=(jax.ShapeDtypeStruct((n, n_tiles * tm, D_MODEL), F32),
                   jax.ShapeDtypeStruct((n, SUBLANES, D_FF), F32)),
        scratch_shapes=[pltpu.VMEM((SUBLANES, D_FF), F32),
                        pltpu.VMEM((tm, D_MODEL), BF16),
                        pltpu.VMEM((tm, D_MODEL), F32),
                        pltpu.VMEM((tm, FF_CHUNK), F32),
                        pltpu.VMEM((tm, D_FF), BF16)],
        compiler_params=_params("arbitrary", "arbitrary"),
        name="mix_out_ffn",
    )(y5, yh, h, wglu, bglu, wout, g2, wa, wv, wd, cw, cin, gf)


def kernel(x_prompt, x_sample, state_s5_re, state_s5_im, state_hgrn, state_ffn_conv, meta_tokens, norm_mix_g, w_in, s5_lambda_re, s5_lambda_im, s5_log_dt, s5_b_re, s5_b_im, s5_c_re, s5_c_im, s5_d, s5_w_glu, s5_b_glu, hg_lower_bounds, hg_norm_g, w_out, norm_ffn_g, ffn_w_up, ffn_conv_w, ffn_conv_b, ffn_w_down, final_norm_g):
    nb, seq, _ = x_prompt.shape
    ns = x_sample.shape[0]
    li = 0

    a5, bb, lb = _prep(s5_lambda_re[li], s5_lambda_im[li], s5_log_dt[li],
                       s5_b_re[li], s5_b_im[li], hg_lower_bounds)
    a5 = a5.reshape(2, 1, S5_LANES)
    wb = _block_diag(bb)
    wc = _block_diag(jnp.stack([s5_c_re[li], -s5_c_im[li]]).transpose(0, 1, 3, 2))
    d5 = s5_d[li].reshape(1, S5_DIM)
    g1 = norm_mix_g[li].reshape(1, D_MODEL)
    g2 = norm_ffn_g[li].reshape(1, D_MODEL)
    gf = final_norm_g.reshape(1, D_MODEL)
    ng = hg_norm_g[li].reshape(1, HG_DIM)
    bglu = s5_b_glu[li].reshape(1, S5_DIM)
    cw = jnp.concatenate([ffn_conv_w[li], ffn_conv_b[li][None],
                          jnp.zeros((SUBLANES - CONV_W - 1, D_FF), F32)], axis=0)
    s5w = (a5, wb, wc, d5)

    xs = jnp.concatenate([x_sample.reshape(ns, D_MODEL), meta_tokens,
                          jnp.zeros((HG_CHUNK - N_META, D_MODEL), F32)])[None]
    u_tm, zf, qig, w_in_b = _inproj(xs, g1, w_in[li], ns + HG_CHUNK, 1, emit_bf16=True)

    y5s, s5r, s5i = _s5(u_tm, state_s5_re[li].reshape(ns, S5_LANES), state_s5_im[li].reshape(ns, S5_LANES),
                        *s5w, ns, 1, 1)
    yhs, shg = _hgrn_step(zf, qig, lb, ng, state_hgrn[li], STEP_ROWS)

    z5 = jnp.zeros((1, S5_LANES), F32)
    zh = jnp.zeros((1, HG_HEADS, HG_HEAD_DIM, HG_HEAD_DIM), F32)
    y5m, m5r, m5i = _s5(u_tm, z5, z5, *s5w, 1, N_META, 1, tile0=ns // N_META)
    yhm, mhg = _hgrn(zf, qig, lb, ng, zh, HG_CHUNK, 1, N_META, tile0=ns // HG_CHUNK)

    buf = state_ffn_conv[li].reshape(ns, (CONV_W - 1) * D_FF)
    y_sample, a_new, mconv, wglu, wout, wa, wv, wd = _ffn_small(
        y5m, y5s, yhm, yhs, xs, xs, s5_w_glu[li], bglu, w_out[li], g2,
        ffn_w_up[li], ffn_w_down[li], cw, buf, gf, hm_tile=ns // N_META)
    sconv = jnp.stack([buf[:, D_FF:], a_new], axis=1)

    u_tm, zf, qig = _inproj(x_prompt, g1, w_in_b, IN_TILE, seq // IN_TILE, nb=IN_SEQS)
    y5, p5r, p5i = _s5(u_tm, m5r, m5i, *s5w, nb, S5_TILE, seq // S5_TILE)
    yh, phg = _hgrn(zf, qig, lb, ng, mhg, seq, 1, seq)
    y_prompt, pconv = _ffn(y5, yh, x_prompt, wglu, bglu, wout, g2, wa, wv, wd, cw, mconv[None], gf,
                           FFN_TILE, seq // FFN_TILE)

    st5 = lambda t: t.reshape(1, -1, S5_GROUPS, S5_STATE)
    pconv = pconv[:, SUBLANES - (CONV_W - 1):, :]
    return (y_prompt, y_sample.reshape(ns, 1, D_MODEL),
            st5(p5r), st5(p5i), phg[None], pconv[None],
            st5(s5r), st5(s5i), shg[None], sconv[None])
```
